```python
import math
import jax, jax.numpy as jnp
from jax import lax
import numpy as np

D_MODEL = 2048
BATCH = 2
SEQ = 8192
DEPTH = 2

D_MIX = D_MODEL
HEAD_DIM = 64
BLOCK = 128
ROPE_THETA = 10000.0
A_HEADS = (3 * D_MIX) // (8 * HEAD_DIM)
A_KV_HEADS = A_HEADS // 3
A_GROUP = A_HEADS // A_KV_HEADS
A_WIDTH = A_HEADS * HEAD_DIM
WINDOW = 128
POOL_WINDOWS = (2, 4, 8, 16)
B_GROUPS = len(POOL_WINDOWS)
B_WIDTH = D_MIX // 4
B_GROUP_DIM = B_WIDTH // B_GROUPS
C_VDIM = 2 * HEAD_DIM
C_WIDTH = D_MIX - A_WIDTH - B_WIDTH
C_HEADS = C_WIDTH // C_VDIM
DIFF_EPS = 1e-5
IN_WIDTHS = (A_WIDTH, A_KV_HEADS * HEAD_DIM, A_KV_HEADS * HEAD_DIM, B_WIDTH, C_WIDTH, C_WIDTH, C_WIDTH)
IN_COLS = sum(IN_WIDTHS)
N_GROUPS = 4
EXPERTS_PER_GROUP = 8
N_EXPERTS = N_GROUPS * EXPERTS_PER_GROUP
TOP_K = 2
D_FF_EXPERT = D_MODEL // 4
ROW_BLOCK = 128
NORM_EPS = 1e-6

kernel_name = "hymba_style_swa_pool_diffattn_hiermoe"


def rms_norm(x, g, eps=NORM_EPS):
    xf = x.astype(jnp.float32)
    y = xf * lax.rsqrt(jnp.mean(xf * xf, axis=-1, keepdims=True) + eps)
    return (y * g.astype(jnp.float32)).astype(x.dtype)


def rope(x, positions):
    half = x.shape[-1] // 2
    inv = ROPE_THETA ** (-jnp.arange(half, dtype=jnp.float32) / half)
    ang = positions.astype(jnp.float32)[:, :, None] * inv
    cos = jnp.cos(ang)[:, :, None, :]
    sin = jnp.sin(ang)[:, :, None, :]
    xf = x.astype(jnp.float32)
    x1, x2 = xf[..., :half], xf[..., half:]
    return jnp.concatenate([x1 * cos - x2 * sin, x2 * cos + x1 * sin], axis=-1).astype(x.dtype)


def split_projection(proj):
    idx, out = 0, []
    for w in IN_WIDTHS:
        out.append(proj[..., idx:idx + w])
        idx += w
    return out


def sliding_window_attention(q, k, v, sinks):
    B, S = q.shape[:2]
    nb = S // BLOCK
    qb = q.reshape(B, nb, BLOCK, A_KV_HEADS, A_GROUP, HEAD_DIM)

    def band(t):
        tb = t.reshape(B, nb, BLOCK, A_KV_HEADS, HEAD_DIM)
        prev = jnp.concatenate([jnp.zeros_like(tb[:, :1]), tb[:, :-1]], axis=1)
        return jnp.concatenate([prev, tb], axis=2)

    kb, vb = band(k), band(v)
    s = jnp.einsum('bnqkgd,bnjkd->bnkgqj', qb, kb).astype(jnp.float32) * (HEAD_DIM ** -0.5)
    qi = jnp.arange(BLOCK)[:, None]
    j = jnp.arange(2 * BLOCK)[None, :]
    dist = qi + BLOCK - j
    kpos = jnp.arange(nb)[:, None, None] * BLOCK - BLOCK + j[None]
    mask = (dist >= 0)[None] & (dist < WINDOW)[None] & (kpos >= 0)
    s = jnp.where(mask[None, :, None, None], s, -jnp.inf)
    sink = sinks.astype(jnp.float32).reshape(1, 1, A_KV_HEADS, A_GROUP, 1, 1)
    sink = jnp.broadcast_to(sink, s.shape[:-1] + (1,))
    p = jax.nn.softmax(jnp.concatenate([s, sink], axis=-1), axis=-1)[..., :-1]
    o = jnp.einsum('bnkgqj,bnjkd->bnqkgd', p.astype(v.dtype), vb)
    return o.reshape(B, S, A_WIDTH)


def multiscale_pool(u, w_pool, scale):
    S = u.shape[1]
    uf = u.astype(jnp.float32)
    cs = jnp.cumsum(uf, axis=1)
    t = jnp.arange(S)
    outs = []
    for gi, w in enumerate(POOL_WINDOWS):
        sl = slice(gi * B_GROUP_DIM, (gi + 1) * B_GROUP_DIM)
        c = cs[:, :, sl]
        prev = jnp.pad(c, ((0, 0), (w, 0), (0, 0)))[:, :S]
        cnt = jnp.minimum(t + 1, w).astype(jnp.float32)[None, :, None]
        y = ((c - prev) / cnt - uf[:, :, sl]).astype(u.dtype)
        outs.append(y @ w_pool[gi])
    return jnp.concatenate(outs, axis=-1) * scale


def differential_attention(q, k, v, lam, lam_init, subln):
    B, S = q.shape[:2]
    nb = S // BLOCK
    qb = q.reshape(B, nb, BLOCK, C_HEADS, 2, HEAD_DIM).transpose(1, 0, 2, 3, 4, 5)
    kpos = jnp.arange(S)

    def one_block(args):
        n, qblk = args
        s = jnp.einsum('bqhcd,bkhcd->bhcqk', qblk, k).astype(jnp.float32) * (HEAD_DIM ** -0.5)
        qpos = n * BLOCK + jnp.arange(BLOCK)
        s = jnp.where(kpos[None, :] <= qpos[:, None], s, -jnp.inf)
        p = jax.nn.softmax(s, axis=-1)
        a = p[:, :, 0] - lam * p[:, :, 1]
        return jnp.einsum('bhqk,bkhe->bqhe', a.astype(v.dtype), v)

    o = lax.map(one_block, (jnp.arange(nb), qb))
    o = o.transpose(1, 0, 2, 3, 4).reshape(B, S, C_HEADS, C_VDIM)
    o = rms_norm(o, subln, eps=DIFF_EPS) * (1.0 - lam_init)
    return o.reshape(B, S, C_WIDTH)


def hierarchical_moe(h, w_rg, b_rg, w_re, b_re, w_g, w_u, w_d):
    B, S, D = h.shape
    T = B * S
    xs = h.reshape(T, D)
    pg = jax.nn.softmax((xs @ w_rg + b_rg).astype(jnp.float32), axis=-1)
    gsel = jnp.argmax(pg, axis=-1)
    pgsel = jnp.take_along_axis(pg, gsel[:, None], axis=1)
    le = (xs @ w_re + b_re).astype(jnp.float32).reshape(T, N_GROUPS, EXPERTS_PER_GROUP)
    le = jnp.take_along_axis(le, gsel[:, None, None], axis=1)[:, 0]
    pe = jax.nn.softmax(le, axis=-1)
    top_p, top_i = lax.top_k(pe, TOP_K)
    gates = pgsel * top_p / jnp.sum(top_p, axis=-1, keepdims=True)
    eidx = (gsel[:, None] * EXPERTS_PER_GROUP + top_i).astype(jnp.int32)

    N = T * TOP_K
    flat_e = eidx.reshape(N)
    order = jnp.argsort(flat_e)
    sorted_e = flat_e[order]
    tok = order // TOP_K
    g_sorted = gates.reshape(N)[order]
    xs_sorted = xs[tok]
    counts = jnp.bincount(flat_e, length=N_EXPERTS)
    starts = (jnp.cumsum(counts) - counts).astype(jnp.int32)
    nb = N // ROW_BLOCK
    bounds = jnp.sort(jnp.concatenate([jnp.arange(nb, dtype=jnp.int32) * ROW_BLOCK, starts]))
    ends = jnp.concatenate([bounds[1:], jnp.full((1,), N, bounds.dtype)])
    seg_blk = jnp.minimum(bounds // ROW_BLOCK, nb - 1)
    seg_exp = sorted_e[jnp.minimum(bounds, N - 1)]

    def segment(args):
        s0, s1, blk, ex = args
        row0 = blk * ROW_BLOCK
        xb = lax.dynamic_slice_in_dim(xs_sorted, row0, ROW_BLOCK, 0)
        a = jax.nn.silu(xb @ w_g[ex]) * (xb @ w_u[ex])
        yb = a @ w_d[ex]
        rows = row0 + jnp.arange(ROW_BLOCK)
        wgt = jnp.where((rows >= s0) & (rows < s1), g_sorted[rows], 0.0)
        return yb * wgt[:, None].astype(yb.dtype), tok[rows]

    ys, tok_rows = lax.map(segment, (bounds, ends, seg_blk, seg_exp))
    y = jnp.zeros((T, D), h.dtype).at[tok_rows.reshape(-1)].add(ys.reshape(-1, D))
    return y.reshape(B, S, D)


def setup_inputs(seed: int = 0) -> dict:
    key = jax.random.key(seed)
    ks = jax.random.split(key, 24)
    f32 = jnp.float32
    nrm = lambda k, shape, scale: jax.random.normal(k, shape, f32) * scale
    L = DEPTH
    return {
        "x": nrm(ks[0], (BATCH, SEQ, D_MODEL), 1.0),
        "positions": jnp.broadcast_to(jnp.arange(SEQ, dtype=jnp.int32)[None], (BATCH, SEQ)),
        "attn_norm": 1.0 + nrm(ks[1], (L, D_MODEL), 0.02),
        "w_in": nrm(ks[2], (L, D_MODEL, IN_COLS), D_MODEL ** -0.5),
        "sinks": nrm(ks[3], (L, A_HEADS), 1.0),
        "branch_norm_a": 1.0 + nrm(ks[4], (L, A_WIDTH), 0.02),
        "w_pool": nrm(ks[5], (L, B_GROUPS, B_GROUP_DIM, B_GROUP_DIM), B_GROUP_DIM ** -0.5),
        "pool_scale": 1.0 + nrm(ks[6], (L, B_WIDTH), 0.02),
        "lambda_q1": nrm(ks[7], (L, HEAD_DIM), 0.1),
        "lambda_k1": nrm(ks[8], (L, HEAD_DIM), 0.1),
        "lambda_q2": nrm(ks[9], (L, HEAD_DIM), 0.1),
        "lambda_k2": nrm(ks[10], (L, HEAD_DIM), 0.1),
        "subln": 1.0 + nrm(ks[11], (L, C_VDIM), 0.02),
        "w_out": nrm(ks[12], (L, D_MIX, D_MODEL), D_MIX ** -0.5),
        "ffn_norm": 1.0 + nrm(ks[13], (L, D_MODEL), 0.02),
        "w_router_group": nrm(ks[14], (L, D_MODEL, N_GROUPS), D_MODEL ** -0.5),
        "b_router_group": nrm(ks[15], (L, N_GROUPS), 0.01),
        "w_router_expert": nrm(ks[16], (L, D_MODEL, N_EXPERTS), D_MODEL ** -0.5),
        "b_router_expert": nrm(ks[17], (L, N_EXPERTS), 0.01),
        "w_expert_gate": nrm(ks[18], (L, N_EXPERTS, D_MODEL, D_FF_EXPERT), D_MODEL ** -0.5),
        "w_expert_up": nrm(ks[19], (L, N_EXPERTS, D_MODEL, D_FF_EXPERT), D_MODEL ** -0.5),
        "w_expert_down": nrm(ks[20], (L, N_EXPERTS, D_FF_EXPERT, D_MODEL), D_FF_EXPERT ** -0.5),
        "final_norm": 1.0 + nrm(ks[21], (D_MODEL,), 0.02),
    }


def reference(x, positions, attn_norm, w_in, sinks, branch_norm_a, w_pool, pool_scale,
              lambda_q1, lambda_k1, lambda_q2, lambda_k2, subln, w_out, ffn_norm,
              w_router_group, b_router_group, w_router_expert, b_router_expert,
              w_expert_gate, w_expert_up, w_expert_down, final_norm):
    B, S, _ = x.shape
    h = x
    for l in range(DEPTH):
        hn = rms_norm(h, attn_norm[l])
        qa, ka, va, ub, qc, kc, vc = split_projection(hn @ w_in[l])
        qa = rope(qa.reshape(B, S, A_HEADS, HEAD_DIM), positions)
        ka = rope(ka.reshape(B, S, A_KV_HEADS, HEAD_DIM), positions)
        va = va.reshape(B, S, A_KV_HEADS, HEAD_DIM)
        oa = rms_norm(sliding_window_attention(qa, ka, va, sinks[l]), branch_norm_a[l])
        ob = multiscale_pool(ub, w_pool[l], pool_scale[l])
        qc = rope(qc.reshape(B, S, 2 * C_HEADS, HEAD_DIM), positions).reshape(B, S, C_HEADS, 2, HEAD_DIM)
        kc = rope(kc.reshape(B, S, 2 * C_HEADS, HEAD_DIM), positions).reshape(B, S, C_HEADS, 2, HEAD_DIM)
        vc = vc.reshape(B, S, C_HEADS, C_VDIM)
        lam_init = 0.8 - 0.6 * math.exp(-0.3 * l)
        lam = (jnp.exp(jnp.sum(lambda_q1[l].astype(jnp.float32) * lambda_k1[l].astype(jnp.float32)))
               - jnp.exp(jnp.sum(lambda_q2[l].astype(jnp.float32) * lambda_k2[l].astype(jnp.float32)))
               + lam_init)
        oc = differential_attention(qc, kc, vc, lam, lam_init, subln[l])
        h = h + jnp.concatenate([oa, ob, oc], axis=-1) @ w_out[l]
        h = h + hierarchical_moe(rms_norm(h, ffn_norm[l]), w_router_group[l], b_router_group[l],
                                 w_router_expert[l], b_router_expert[l], w_expert_gate[l],
                                 w_expert_up[l], w_expert_down[l])
    return rms_norm(h, final_norm)
```

```python
import functools
import math

import jax
import jax.numpy as jnp
from jax import lax
from jax.experimental import pallas as pl
from jax.experimental.pallas import tpu as pltpu

F32 = jnp.float32
BF16 = jnp.bfloat16

D_MODEL = 2048
HEAD_DIM = 64
ROPE_THETA = 10000.0
A_HEADS = 12
A_KV_HEADS = 4
A_GROUP = 3
A_WIDTH = 768
WINDOW = 128
POOL_WINDOWS = (2, 4, 8, 16)
B_WIDTH = 512
B_GROUP_DIM = 128
C_VDIM = 128
C_WIDTH = 768
C_HEADS = 6
DIFF_EPS = 1e-5
IN_COLS = 4096
N_GROUPS = 4
EXPERTS_PER_GROUP = 8
N_EXPERTS = 32
TOP_K = 2
D_FF = 512
NORM_EPS = 1e-6
N_ROUTER_ROWS = 40
NEG = -1e30

COL_QA, COL_QC, COL_KA, COL_KC, COL_VA, COL_UB, COL_VC = 0, 768, 1536, 1792, 2560, 2816, 3328
PROJ_TN = 512
N_SCALED_TILES = (COL_KA) // PROJ_TN
N_ROPE_TILES = (COL_VA) // PROJ_TN

VMEM_LIMIT = 48 * 1024 * 1024


def _cparams(sem):
    return pltpu.CompilerParams(dimension_semantics=sem, vmem_limit_bytes=VMEM_LIMIT)


def _norm_kernel(*refs, n_add, emit_h, emit_bf16, eps):
    h_ref = refs[0]
    add_refs = refs[1:1 + n_add]
    g_ref = refs[1 + n_add]
    outs = refs[2 + n_add:]
    h = h_ref[...]
    for r in add_refs:
        h = h + r[...].astype(F32)
    ms = jnp.mean(h * h, axis=-1, keepdims=True)
    y = h * lax.rsqrt(ms + eps) * g_ref[...]
    k = 0
    if emit_h:
        outs[k][...] = h
        k += 1
    outs[k][...] = y.astype(BF16 if emit_bf16 else F32)


def _norm(h, adds, g, *, emit_h, emit_bf16, tm=512):
    T, D = h.shape
    row = pl.BlockSpec((tm, D), lambda i: (i, 0))
    out_shape, out_specs = [], []
    if emit_h:
        out_shape.append(jax.ShapeDtypeStruct((T, D), F32))
        out_specs.append(row)
    out_shape.append(jax.ShapeDtypeStruct((T, D), BF16 if emit_bf16 else F32))
    out_specs.append(row)
    return pl.pallas_call(
        functools.partial(_norm_kernel, n_add=len(adds), emit_h=emit_h, emit_bf16=emit_bf16, eps=NORM_EPS),
        grid=(T // tm,),
        in_specs=[row] * (1 + len(adds)) + [pl.BlockSpec((1, D), lambda i: (0, 0))],
        out_specs=out_specs,
        out_shape=out_shape,
        compiler_params=_cparams(("parallel",)),
        name="norm",
    )(h, *adds, g.reshape(1, D))


def _proj_kernel(x_ref, w_ref, cos_ref, sin_ref, o_ref):
    j = pl.program_id(1)
    acc = jnp.dot(x_ref[...], w_ref[...], preferred_element_type=F32)

    @pl.when(j < N_ROPE_TILES)
    def _():
        scale = jnp.where(j < N_SCALED_TILES, HEAD_DIM ** -0.5, 1.0).astype(F32)
        reps = acc.shape[1] // 128
        c = jnp.concatenate([cos_ref[...] * scale] * reps, axis=1)
        s = jnp.concatenate([sin_ref[...] * scale] * reps, axis=1)
        lane = lax.broadcasted_iota(jnp.int32, acc.shape, 1)
        first_half = (lane & (HEAD_DIM // 2)) == 0
        width = acc.shape[1]
        swapped = jnp.where(first_half,
                            pltpu.roll(acc, width - HEAD_DIM // 2, 1),
                            pltpu.roll(acc, HEAD_DIM // 2, 1))
        o_ref[...] = (acc * c + swapped * s).astype(o_ref.dtype)

    @pl.when(j >= N_ROPE_TILES)
    def _():
        o_ref[...] = acc.astype(o_ref.dtype)


def _proj(hn, w, cos_t, sin_t, *, tm=512):
    T, D = hn.shape
    return pl.pallas_call(
        _proj_kernel,
        grid=(T // tm, IN_COLS // PROJ_TN),
        in_specs=[
            pl.BlockSpec((tm, D), lambda i, j: (i, 0)),
            pl.BlockSpec((D, PROJ_TN), lambda i, j: (0, j)),
            pl.BlockSpec((tm, 128), lambda i, j: (i, 0)),
            pl.BlockSpec((tm, 128), lambda i, j: (i, 0)),
        ],
        out_specs=pl.BlockSpec((tm, PROJ_TN), lambda i, j: (i, j)),
        out_shape=jax.ShapeDtypeStruct((T, IN_COLS), BF16),
        compiler_params=_cparams(("parallel", "arbitrary")),
        name="proj",
    )(hn, w, cos_t, sin_t)


def _swa_kernel(sink_ref, q_ref, kc_ref, vc_ref, kp_ref, vp_ref, g_ref, o_ref, *, tq):
    i = pl.program_id(1)
    kext = jnp.concatenate([kp_ref[...], kc_ref[...]], axis=0)
    vext = jnp.concatenate([vp_ref[...], vc_ref[...]], axis=0)
    qi = lax.broadcasted_iota(jnp.int32, (WINDOW, 2 * WINDOW), 0)
    jj = lax.broadcasted_iota(jnp.int32, (WINDOW, 2 * WINDOW), 1)
    band = (jj > qi) & (jj <= qi + WINDOW)
    g = g_ref[...]
    for sb in range(tq // WINDOW):
        mask = band
        if sb == 0:
            mask = band & ((jj >= WINDOW) | (i > 0))
        outs = []
        for h in range(A_HEADS):
            kv = h // A_GROUP
            qh = q_ref[sb * WINDOW:(sb + 1) * WINDOW, h * HEAD_DIM:(h + 1) * HEAD_DIM]
            kb = kext[sb * WINDOW:(sb + 2) * WINDOW, kv * HEAD_DIM:(kv + 1) * HEAD_DIM]
            vb = vext[sb * WINDOW:(sb + 2) * WINDOW, kv * HEAD_DIM:(kv + 1) * HEAD_DIM]
            s = lax.dot_general(qh, kb, (((1,), (1,)), ((), ())), preferred_element_type=F32)
            s = jnp.where(mask, s, NEG)
            sink = sink_ref[h]
            m = jnp.maximum(jnp.max(s, axis=1, keepdims=True), sink)
            p = jnp.exp(s - m)
            l = jnp.sum(p, axis=1, keepdims=True) + jnp.exp(sink - m)
            o = jnp.dot(p.astype(BF16), vb, preferred_element_type=F32)
            outs.append(o / l)
        o_all = jnp.concatenate(outs, axis=1)
        ms = jnp.mean(o_all * o_all, axis=1, keepdims=True)
        y = o_all * lax.rsqrt(ms + NORM_EPS) * g
        o_ref[sb * WINDOW:(sb + 1) * WINDOW, :] = y.astype(o_ref.dtype)


def _swa(proj, sinks, g, *, B, S, tq=256):
    nq = S // tq
    rpb = tq // WINDOW

    def prev_idx(col):
        def f(b, i):
            return (jnp.maximum(b * (S // WINDOW) + i * rpb - 1, 0), col)
        return f

    return pl.pallas_call(
        functools.partial(_swa_kernel, tq=tq),
        grid=(B, nq),
        in_specs=[
            pl.BlockSpec(memory_space=pltpu.SMEM),
            pl.BlockSpec((tq, A_WIDTH), lambda b, i: (b * nq + i, COL_QA // A_WIDTH)),
            pl.BlockSpec((tq, 256), lambda b, i: (b * nq + i, COL_KA // 256)),
            pl.BlockSpec((tq, 256), lambda b, i: (b * nq + i, COL_VA // 256)),
            pl.BlockSpec((WINDOW, 256), prev_idx(COL_KA // 256)),
            pl.BlockSpec((WINDOW, 256), prev_idx(COL_VA // 256)),
            pl.BlockSpec((1, A_WIDTH), lambda b, i: (0, 0)),
        ],
        out_specs=pl.BlockSpec((tq, A_WIDTH), lambda b, i: (b * nq + i, 0)),
        out_shape=jax.ShapeDtypeStruct((B * S, A_WIDTH), BF16),
        compiler_params=_cparams(("parallel", "parallel")),
        name="swa",
    )(sinks, proj, proj, proj, proj, proj, g.reshape(1, A_WIDTH))


POOL_HALO = 16


def _pool_kernel(u0_ref, u1_ref, h0_ref, h1_ref, w_ref, s_ref, o_ref, *, tq):
    i = pl.program_id(1)
    u = jnp.concatenate([u0_ref[...], u1_ref[...]], axis=1).astype(F32)
    halo = jnp.concatenate([h0_ref[...], h1_ref[...]], axis=1).astype(F32)
    halo = halo * (i > 0).astype(F32)
    ext = jnp.concatenate([halo, u], axis=0)
    t = i * tq + lax.broadcasted_iota(jnp.int32, (tq, 1), 0)
    for gi, w in enumerate(POOL_WINDOWS):
        sl = slice(gi * B_GROUP_DIM, (gi + 1) * B_GROUP_DIM)
        s = ext[:, sl]
        shift = 1
        while shift < w:
            s = s + pltpu.roll(s, shift, 0)
            shift *= 2
        cnt = jnp.minimum(t + 1, w).astype(F32)
        y = (s[POOL_HALO:, :] / cnt - u[:, sl]).astype(BF16)
        o = jnp.dot(y, w_ref[gi], preferred_element_type=F32) * s_ref[:, sl]
        o_ref[:, sl] = o.astype(o_ref.dtype)


def _pool(proj, w_pool, scale, *, B, S, tq=256):
    nq = S // tq
    c0 = COL_UB // 256

    def cur(c):
        return pl.BlockSpec((tq, 256), lambda b, i: (b * nq + i, c))

    def halo(c):
        return pl.BlockSpec(
            (POOL_HALO, 256),
            lambda b, i: (jnp.maximum((b * S + i * tq) // POOL_HALO - 1, 0), c))

    return pl.pallas_call(
        functools.partial(_pool_kernel, tq=tq),
        grid=(B, nq),
        in_specs=[cur(c0), cur(c0 + 1), halo(c0), halo(c0 + 1),
                  pl.BlockSpec((len(POOL_WINDOWS), B_GROUP_DIM, B_GROUP_DIM), lambda b, i: (0, 0, 0)),
                  pl.BlockSpec((1, B_WIDTH), lambda b, i: (0, 0))],
        out_specs=pl.BlockSpec((tq, B_WIDTH), lambda b, i: (b * nq + i, 0)),
        out_shape=jax.ShapeDtypeStruct((B * S, B_WIDTH), BF16),
        compiler_params=_cparams(("parallel", "parallel")),
        name="pool",
    )(proj, proj, proj, proj, w_pool, scale.reshape(1, B_WIDTH))


def _diff_kernel(q_ref, k_ref, vt_ref, lam_ref, g_ref, o_ref, *, tq, tk, lam_init):
    qi = pl.program_id(2)
    q = q_ref[...]
    qs = (q[:, :HEAD_DIM], q[:, HEAD_DIM:])

    def step(kj, carry, masked):
        off = pl.multiple_of(kj * tk, tk)
        kb = k_ref[pl.ds(off, tk), :]
        vt = vt_ref[kj]
        new = []
        for c in range(2):
            m, l, acc = carry[c]
            st = lax.dot_general(kb[:, c * HEAD_DIM:(c + 1) * HEAD_DIM], qs[c],
                                 (((1,), (1,)), ((), ())), preferred_element_type=F32)
            if masked:
                r = lax.broadcasted_iota(jnp.int32, st.shape, 0)
                cc = lax.broadcasted_iota(jnp.int32, st.shape, 1)
                st = jnp.where(r <= cc, st, NEG)
            m_new = jnp.maximum(m, jnp.max(st, axis=0, keepdims=True))
            alpha = jnp.exp(m - m_new)
            p = jnp.exp(st - m_new)
            l_new = alpha * l + jnp.sum(p, axis=0, keepdims=True)
            acc_new = alpha * acc + jnp.dot(vt, p.astype(BF16), preferred_element_type=F32)
            new.append((m_new, l_new, acc_new))
        return tuple(new)

    init = tuple((jnp.full((1, tq), NEG, F32), jnp.zeros((1, tq), F32), jnp.zeros((C_VDIM, tq), F32))
                 for _ in range(2))
    carry = lax.fori_loop(0, qi, lambda kj, c: step(kj, c, False), init)
    (_, l1, a1), (_, l2, a2) = step(qi, carry, True)

    lv = lam_ref[...]
    lam = (jnp.exp(jnp.sum(lv[0:1] * lv[1:2], axis=1, keepdims=True))
           - jnp.exp(jnp.sum(lv[2:3] * lv[3:4], axis=1, keepdims=True)) + lam_init)
    ot = a1 / l1 - lam * (a2 / l2)
    ms = jnp.mean(ot * ot, axis=0, keepdims=True)
    yt = ot * lax.rsqrt(ms + DIFF_EPS) * (g_ref[...] * (1.0 - lam_init))
    o_ref[...] = yt.T.astype(o_ref.dtype)


def _diff(proj, vt, lam_vecs, subln, *, B, S, lam_init, tq=256, tk=256):
    assert tq == tk
    nq = S // tq
    nkb = S // tk
    return pl.pallas_call(
        functools.partial(_diff_kernel, tq=tq, tk=tk, lam_init=lam_init),
        grid=(B, C_HEADS, nq),
        in_specs=[
            pl.BlockSpec((tq, C_VDIM), lambda b, h, i: (b * nq + i, COL_QC // C_VDIM + h)),
            pl.BlockSpec((S, C_VDIM), lambda b, h, i: (b, COL_KC // C_VDIM + h)),
            pl.BlockSpec((None, None, nkb, C_VDIM, tk), lambda b, h, i: (b, h, 0, 0, 0)),
            pl.BlockSpec((4, HEAD_DIM), lambda b, h, i: (0, 0)),
            pl.BlockSpec((C_VDIM, 1), lambda b, h, i: (0, 0)),
        ],
        out_specs=pl.BlockSpec((tq, C_VDIM), lambda b, h, i: (b * nq + i, h)),
        out_shape=jax.ShapeDtypeStruct((B * S, C_WIDTH), BF16),
        compiler_params=_cparams(("parallel", "parallel", "arbitrary")),
        name="diff",
    )(proj, proj, vt, lam_vecs, subln.reshape(C_VDIM, 1))


def _out_kernel(oa_ref, ob_ref, oc_ref, w_ref, h_ref, g_ref, wr_ref, br_ref, h1_ref, hn_ref, lg_ref):
    acc = jnp.dot(oa_ref[...], w_ref[0:A_WIDTH, :], preferred_element_type=F32)
    acc = acc + jnp.dot(ob_ref[...], w_ref[A_WIDTH:A_WIDTH + B_WIDTH, :], preferred_element_type=F32)
    acc = acc + jnp.dot(oc_ref[...], w_ref[A_WIDTH + B_WIDTH:, :], preferred_element_type=F32)
    h1 = h_ref[...] + acc
    h1_ref[...] = h1
    ms = jnp.mean(h1 * h1, axis=1, keepdims=True)
    hn = h1 * lax.rsqrt(ms + NORM_EPS) * g_ref[...]
    hn_ref[...] = hn.astype(hn_ref.dtype)
    lg = lax.dot_general(wr_ref[...], hn, (((1,), (1,)), ((), ())),
                         preferred_element_type=F32, precision=lax.Precision.HIGHEST)
    lg_ref[...] = lg + br_ref[...]


def _out(oa, ob, oc, w, h, g, wr_t, br, *, tm=256):
    T, D = h.shape
    return pl.pallas_call(
        _out_kernel,
        grid=(T // tm,),
        in_specs=[
            pl.BlockSpec((tm, A_WIDTH), lambda i: (i, 0)),
            pl.BlockSpec((tm, B_WIDTH), lambda i: (i, 0)),
            pl.BlockSpec((tm, C_WIDTH), lambda i: (i, 0)),
            pl.BlockSpec((D, D), lambda i: (0, 0)),
            pl.BlockSpec((tm, D), lambda i: (i, 0)),
            pl.BlockSpec((1, D), lambda i: (0, 0)),
            pl.BlockSpec((N_ROUTER_ROWS, D), lambda i: (0, 0)),
            pl.BlockSpec((N_ROUTER_ROWS, 1), lambda i: (0, 0)),
        ],
        out_specs=[
            pl.BlockSpec((tm, D), lambda i: (i, 0)),
            pl.BlockSpec((tm, D), lambda i: (i, 0)),
            pl.BlockSpec((N_ROUTER_ROWS, tm), lambda i: (0, i)),
        ],
        out_shape=[
            jax.ShapeDtypeStruct((T, D), F32),
            jax.ShapeDtypeStruct((T, D), BF16),
            jax.ShapeDtypeStruct((N_ROUTER_ROWS, T), F32),
        ],
        compiler_params=_cparams(("parallel",)),
        name="out",
    )(oa, ob, oc, w, h, g.reshape(1, D), wr_t, br)


def _expert_kernel(blk_ref, exp_ref, lo_ref, hi_ref, x_ref, g_ref, wg_ref, wu_ref, wd_ref, o_ref, *, tm):
    p = pl.program_id(0)
    blk = blk_ref[p]
    first = (p == 0) | (blk != blk_ref[jnp.maximum(p - 1, 0)])
    lo = lo_ref[p]
    hi = hi_ref[p]

    @pl.when(first)
    def _():
        o_ref[...] = jnp.zeros_like(o_ref)

    @pl.when(hi > lo)
    def _():
        x = x_ref[...]
        a = jnp.dot(x, wg_ref[0], preferred_element_type=F32)
        u = jnp.dot(x, wu_ref[0], preferred_element_type=F32)
        act = (a / (1.0 + jnp.exp(-a))) * u
        y = jnp.dot(act.astype(BF16), wd_ref[0], preferred_element_type=F32)
        rows = blk * tm + lax.broadcasted_iota(jnp.int32, (tm, 1), 0)
        wgt = jnp.where((rows >= lo) & (rows < hi), g_ref[...], 0.0)
        o_ref[...] += (y * wgt).astype(o_ref.dtype)


def _experts(seg_blk, seg_exp, seg_lo, seg_hi, xs, gates, wg, wu, wd, *, tm):
    N, D = xs.shape
    P = seg_blk.shape[0]
    grid_spec = pltpu.PrefetchScalarGridSpec(
        num_scalar_prefetch=4,
        grid=(P,),
        in_specs=[
            pl.BlockSpec((tm, D), lambda p, blk, ex, lo, hi: (blk[p], 0)),
            pl.BlockSpec((tm, 1), lambda p, blk, ex, lo, hi: (blk[p], 0)),
            pl.BlockSpec((1, D, D_FF), lambda p, blk, ex, lo, hi: (ex[p], 0, 0)),
            pl.BlockSpec((1, D, D_FF), lambda p, blk, ex, lo, hi: (ex[p], 0, 0)),
            pl.BlockSpec((1, D_FF, D), lambda p, blk, ex, lo, hi: (ex[p], 0, 0)),
        ],
        out_specs=pl.BlockSpec((tm, D), lambda p, blk, ex, lo, hi: (blk[p], 0)),
    )
    return pl.pallas_call(
        functools.partial(_expert_kernel, tm=tm),
        grid_spec=grid_spec,
        out_shape=jax.ShapeDtypeStruct((N, D), BF16),
        compiler_params=_cparams(("arbitrary",)),
        name="experts",
    )(seg_blk, seg_exp, seg_lo, seg_hi, xs, gates, wg, wu, wd)


def _route(lg_t, T, tm):
    lg = lg_t[:N_GROUPS + N_EXPERTS].T
    pg = jax.nn.softmax(lg[:, :N_GROUPS], axis=-1)
    gsel = jnp.argmax(pg, axis=-1)
    pgsel = jnp.take_along_axis(pg, gsel[:, None], axis=1)
    le = lg[:, N_GROUPS:].reshape(T, N_GROUPS, EXPERTS_PER_GROUP)
    le = jnp.take_along_axis(le, gsel[:, None, None], axis=1)[:, 0]
    pe = jax.nn.softmax(le, axis=-1)
    top_p, top_i = lax.top_k(pe, TOP_K)
    gates = pgsel * top_p / jnp.sum(top_p, axis=-1, keepdims=True)
    eidx = (gsel[:, None] * EXPERTS_PER_GROUP + top_i).astype(jnp.int32)

    N = T * TOP_K
    flat_e = eidx.reshape(N)
    order = jnp.argsort(flat_e).astype(jnp.int32)
    sorted_e = flat_e[order]
    tok = order // TOP_K
    g_sorted = gates.reshape(N)[order]
    pos = jnp.zeros((N,), jnp.int32).at[order].set(jnp.arange(N, dtype=jnp.int32))
    counts = jnp.bincount(flat_e, length=N_EXPERTS)
    starts = (jnp.cumsum(counts) - counts).astype(jnp.int32)
    nb = N // tm
    bounds = jnp.sort(jnp.concatenate([jnp.arange(nb, dtype=jnp.int32) * tm, starts]))
    ends = jnp.concatenate([bounds[1:], jnp.full((1,), N, bounds.dtype)])
    seg_blk = jnp.minimum(bounds // tm, nb - 1).astype(jnp.int32)
    seg_exp = sorted_e[jnp.minimum(bounds, N - 1)].astype(jnp.int32)
    return tok, g_sorted, pos.reshape(T, TOP_K), seg_blk, seg_exp, bounds, ends


def _in_proj_perm():
    offs = {}
    idx = 0
    for name, w in zip(("qa", "ka", "va", "ub", "qc", "kc", "vc"), (768, 256, 256, 512, 768, 768, 768)):
        offs[name] = (idx, w)
        idx += w
    cols = []
    for name in ("qa", "qc", "ka", "kc", "va", "ub", "vc"):
        o, w = offs[name]
        cols.extend(range(o, o + w))
    return jnp.asarray(cols, jnp.int32)


def kernel(x, positions, attn_norm, w_in, sinks, branch_norm_a, w_pool, pool_scale, lambda_q1, lambda_k1,
           lambda_q2, lambda_k2, subln, w_out, ffn_norm, w_router_group, b_router_group, w_router_expert,
           b_router_expert, w_expert_gate, w_expert_up, w_expert_down, final_norm):
    B, S, D = x.shape
    T = B * S
    depth = w_in.shape[0]
    tm_e = 256
    tk = 256

    half = HEAD_DIM // 2
    inv = ROPE_THETA ** (-jnp.arange(half, dtype=F32) / half)
    ang = positions.astype(F32).reshape(T, 1) * inv
    cos, sin = jnp.cos(ang), jnp.sin(ang)
    cos_t = jnp.concatenate([cos, cos, cos, cos], axis=1)
    sin_t = jnp.concatenate([-sin, sin, -sin, sin], axis=1)
    perm = _in_proj_perm()

    h = x.reshape(T, D)
    adds = []
    for l in range(depth):
        if l == 0:
            (hn,) = _norm(h, [], attn_norm[l], emit_h=False, emit_bf16=True)
        else:
            h, hn = _norm(h, adds, attn_norm[l], emit_h=True, emit_bf16=True)
        w_in_l = w_in[l][:, perm].astype(BF16)
        proj = _proj(hn, w_in_l, cos_t, sin_t)

        oa = _swa(proj, sinks[l], branch_norm_a[l], B=B, S=S)
        ob = _pool(proj, w_pool[l].astype(BF16), pool_scale[l], B=B, S=S)

        vc = proj[:, COL_VC:].reshape(B, S // tk, tk, C_HEADS, C_VDIM)
        vt = vc.transpose(0, 3, 1, 4, 2)
        lam_init = 0.8 - 0.6 * math.exp(-0.3 * l)
        lam_vecs = jnp.stack([lambda_q1[l], lambda_k1[l], lambda_q2[l], lambda_k2[l]]).astype(F32)
        oc = _diff(proj, vt, lam_vecs, subln[l], B=B, S=S, lam_init=lam_init, tk=tk)

        wr = jnp.concatenate([w_router_group[l], w_router_expert[l]], axis=1)
        wr_t = jnp.zeros((N_ROUTER_ROWS, D), F32).at[:N_GROUPS + N_EXPERTS].set(wr.T)
        br = jnp.zeros((N_ROUTER_ROWS, 1), F32).at[:N_GROUPS + N_EXPERTS, 0].set(
            jnp.concatenate([b_router_group[l], b_router_expert[l]]))
        h1, hn2, lg_t = _out(oa, ob, oc, w_out[l].astype(BF16), h, ffn_norm[l], wr_t, br)

        tok, g_sorted, pos, seg_blk, seg_exp, seg_lo, seg_hi = _route(lg_t, T, tm_e)
        xs = hn2[tok]
        ys = _experts(seg_blk, seg_exp, seg_lo, seg_hi, xs, g_sorted.reshape(-1, 1),
                      w_expert_gate[l].astype(BF16), w_expert_up[l].astype(BF16),
                      w_expert_down[l].astype(BF16), tm=tm_e)
        adds = [ys[pos[:, 0]], ys[pos[:, 1]]]
        h = h1

    (out,) = _norm(h, adds, final_norm, emit_h=False, emit_bf16=False)
    return out.reshape(B, S, D)
```

```python
import functools
import math

import jax
import jax.numpy as jnp
from jax import lax
from jax.experimental import pallas as pl
from jax.experimental.pallas import tpu as pltpu

F32 = jnp.float32
BF16 = jnp.bfloat16

D_MODEL = 2048
HEAD_DIM = 64
ROPE_THETA = 10000.0
A_HEADS = 12
A_KV_HEADS = 4
A_GROUP = 3
A_WIDTH = 768
WINDOW = 128
POOL_WINDOWS = (2, 4, 8, 16)
B_WIDTH = 512
B_GROUP_DIM = 128
C_VDIM = 128
C_WIDTH = 768
C_HEADS = 6
DIFF_EPS = 1e-5
IN_COLS = 4096
N_GROUPS = 4
EXPERTS_PER_GROUP = 8
N_EXPERTS = 32
TOP_K = 2
D_FF = 512
NORM_EPS = 1e-6
N_ROUTER_ROWS = 40
NEG = -1e30
LOG2E = math.log2(math.e)
Q_SCALE = HEAD_DIM ** -0.5 * LOG2E

COL_QA, COL_QC, COL_KA, COL_KC, COL_VA, COL_UB, COL_VC = 0, 768, 1536, 1792, 2560, 2816, 3328
PROJ_TN = 512
N_SCALED_TILES = (COL_KA) // PROJ_TN
N_ROPE_TILES = (COL_VA) // PROJ_TN

VMEM_LIMIT = 48 * 1024 * 1024


def _cparams(sem):
    return pltpu.CompilerParams(dimension_semantics=sem, vmem_limit_bytes=VMEM_LIMIT)


def _norm_kernel(*refs, n_add, emit_h, emit_bf16, eps):
    h_ref = refs[0]
    add_refs = refs[1:1 + n_add]
    g_ref = refs[1 + n_add]
    outs = refs[2 + n_add:]
    h = h_ref[...]
    for r in add_refs:
        h = h + r[...].astype(F32)
    ms = jnp.mean(h * h, axis=-1, keepdims=True)
    y = h * lax.rsqrt(ms + eps) * g_ref[...]
    k = 0
    if emit_h:
        outs[k][...] = h
        k += 1
    outs[k][...] = y.astype(BF16 if emit_bf16 else F32)


def _norm(h, adds, g, *, emit_h, emit_bf16, tm=512):
    T, D = h.shape
    row = pl.BlockSpec((tm, D), lambda i: (i, 0))
    out_shape, out_specs = [], []
    if emit_h:
        out_shape.append(jax.ShapeDtypeStruct((T, D), F32))
        out_specs.append(row)
    out_shape.append(jax.ShapeDtypeStruct((T, D), BF16 if emit_bf16 else F32))
    out_specs.append(row)
    return pl.pallas_call(
        functools.partial(_norm_kernel, n_add=len(adds), emit_h=emit_h, emit_bf16=emit_bf16, eps=NORM_EPS),
        grid=(T // tm,),
        in_specs=[row] * (1 + len(adds)) + [pl.BlockSpec((1, D), lambda i: (0, 0))],
        out_specs=out_specs,
        out_shape=out_shape,
        compiler_params=_cparams(("parallel",)),
        name="norm",
    )(h, *adds, g.reshape(1, D))


def _proj_kernel(x_ref, w_ref, cos_ref, sin_ref, o_ref):
    j = pl.program_id(1)
    acc = jnp.dot(x_ref[...], w_ref[...], preferred_element_type=F32)

    @pl.when(j < N_ROPE_TILES)
    def _():
        scale = jnp.where(j < N_SCALED_TILES, Q_SCALE, 1.0).astype(F32)
        reps = acc.shape[1] // 128
        c = jnp.concatenate([cos_ref[...] * scale] * reps, axis=1)
        s = jnp.concatenate([sin_ref[...] * scale] * reps, axis=1)
        lane = lax.broadcasted_iota(jnp.int32, acc.shape, 1)
        first_half = (lane & (HEAD_DIM // 2)) == 0
        width = acc.shape[1]
        swapped = jnp.where(first_half,
                            pltpu.roll(acc, width - HEAD_DIM // 2, 1),
                            pltpu.roll(acc, HEAD_DIM // 2, 1))
        o_ref[...] = (acc * c + swapped * s).astype(o_ref.dtype)

    @pl.when(j >= N_ROPE_TILES)
    def _():
        o_ref[...] = acc.astype(o_ref.dtype)


def _proj(hn, w, cos_t, sin_t, *, tm=512):
    T, D = hn.shape
    return pl.pallas_call(
        _proj_kernel,
        grid=(T // tm, IN_COLS // PROJ_TN),
        in_specs=[
            pl.BlockSpec((tm, D), lambda i, j: (i, 0)),
            pl.BlockSpec((D, PROJ_TN), lambda i, j: (0, j)),
            pl.BlockSpec((tm, 128), lambda i, j: (i, 0)),
            pl.BlockSpec((tm, 128), lambda i, j: (i, 0)),
        ],
        out_specs=pl.BlockSpec((tm, PROJ_TN), lambda i, j: (i, j)),
        out_shape=jax.ShapeDtypeStruct((T, IN_COLS), BF16),
        compiler_params=_cparams(("parallel", "arbitrary")),
        name="proj",
    )(hn, w, cos_t, sin_t)


def _swa_kernel(sink_ref, q_ref, kc_ref, vc_ref, kp_ref, vp_ref, g_ref, o_ref, *, tq):
    i = pl.program_id(1)
    kext = jnp.concatenate([kp_ref[...], kc_ref[...]], axis=0)
    vext = jnp.concatenate([vp_ref[...], vc_ref[...]], axis=0)
    qi = lax.broadcasted_iota(jnp.int32, (WINDOW, 2 * WINDOW), 0)
    jj = lax.broadcasted_iota(jnp.int32, (WINDOW, 2 * WINDOW), 1)
    band = (jj > qi) & (jj <= qi + WINDOW)
    g = g_ref[...]
    for sb in range(tq // WINDOW):
        mask = band
        if sb == 0:
            mask = band & ((jj >= WINDOW) | (i > 0))
        outs = []
        for h in range(A_HEADS):
            kv = h // A_GROUP
            qh = q_ref[sb * WINDOW:(sb + 1) * WINDOW, h * HEAD_DIM:(h + 1) * HEAD_DIM]
            kb = kext[sb * WINDOW:(sb + 2) * WINDOW, kv * HEAD_DIM:(kv + 1) * HEAD_DIM]
            vb = vext[sb * WINDOW:(sb + 2) * WINDOW, kv * HEAD_DIM:(kv + 1) * HEAD_DIM]
            s = lax.dot_general(qh, kb, (((1,), (1,)), ((), ())), preferred_element_type=F32)
            s = jnp.where(mask, s, NEG)
            sink = sink_ref[h] * LOG2E
            m = jnp.maximum(jnp.max(s, axis=1, keepdims=True), sink)
            p = jnp.exp2(s - m)
            l = jnp.sum(p, axis=1, keepdims=True) + jnp.exp2(sink - m)
            o = jnp.dot(p.astype(BF16), vb, preferred_element_type=F32)
            outs.append(o / l)
        o_all = jnp.concatenate(outs, axis=1)
        ms = jnp.mean(o_all * o_all, axis=1, keepdims=True)
        y = o_all * lax.rsqrt(ms + NORM_EPS) * g
        o_ref[sb * WINDOW:(sb + 1) * WINDOW, :] = y.astype(o_ref.dtype)


def _swa(proj, sinks, g, *, B, S, tq=256):
    nq = S // tq
    rpb = tq // WINDOW

    def prev_idx(col):
        def f(b, i):
            return (jnp.maximum(b * (S // WINDOW) + i * rpb - 1, 0), col)
        return f

    return pl.pallas_call(
        functools.partial(_swa_kernel, tq=tq),
        grid=(B, nq),
        in_specs=[
            pl.BlockSpec(memory_space=pltpu.SMEM),
            pl.BlockSpec((tq, A_WIDTH), lambda b, i: (b * nq + i, COL_QA // A_WIDTH)),
            pl.BlockSpec((tq, 256), lambda b, i: (b * nq + i, COL_KA // 256)),
            pl.BlockSpec((tq, 256), lambda b, i: (b * nq + i, COL_VA // 256)),
            pl.BlockSpec((WINDOW, 256), prev_idx(COL_KA // 256)),
            pl.BlockSpec((WINDOW, 256), prev_idx(COL_VA // 256)),
            pl.BlockSpec((1, A_WIDTH), lambda b, i: (0, 0)),
        ],
        out_specs=pl.BlockSpec((tq, A_WIDTH), lambda b, i: (b * nq + i, 0)),
        out_shape=jax.ShapeDtypeStruct((B * S, A_WIDTH), BF16),
        compiler_params=_cparams(("parallel", "parallel")),
        name="swa",
    )(sinks, proj, proj, proj, proj, proj, g.reshape(1, A_WIDTH))


POOL_HALO = 16


def _pool_kernel(u0_ref, u1_ref, h0_ref, h1_ref, w_ref, s_ref, o_ref, *, tq):
    i = pl.program_id(1)
    u = jnp.concatenate([u0_ref[...], u1_ref[...]], axis=1).astype(F32)
    halo = jnp.concatenate([h0_ref[...], h1_ref[...]], axis=1).astype(F32)
    halo = halo * (i > 0).astype(F32)
    ext = jnp.concatenate([halo, u], axis=0)
    t = i * tq + lax.broadcasted_iota(jnp.int32, (tq, 1), 0)
    for gi, w in enumerate(POOL_WINDOWS):
        sl = slice(gi * B_GROUP_DIM, (gi + 1) * B_GROUP_DIM)
        s = ext[:, sl]
        shift = 1
        while shift < w:
            s = s + pltpu.roll(s, shift, 0)
            shift *= 2
        cnt = jnp.minimum(t + 1, w).astype(F32)
        y = (s[POOL_HALO:, :] / cnt - u[:, sl]).astype(BF16)
        o = jnp.dot(y, w_ref[gi], preferred_element_type=F32) * s_ref[:, sl]
        o_ref[:, sl] = o.astype(o_ref.dtype)


def _pool(proj, w_pool, scale, *, B, S, tq=256):
    nq = S // tq
    c0 = COL_UB // 256

    def cur(c):
        return pl.BlockSpec((tq, 256), lambda b, i: (b * nq + i, c))

    def halo(c):
        return pl.BlockSpec(
            (POOL_HALO, 256),
            lambda b, i: (jnp.maximum((b * S + i * tq) // POOL_HALO - 1, 0), c))

    return pl.pallas_call(
        functools.partial(_pool_kernel, tq=tq),
        grid=(B, nq),
        in_specs=[cur(c0), cur(c0 + 1), halo(c0), halo(c0 + 1),
                  pl.BlockSpec((len(POOL_WINDOWS), B_GROUP_DIM, B_GROUP_DIM), lambda b, i: (0, 0, 0)),
                  pl.BlockSpec((1, B_WIDTH), lambda b, i: (0, 0))],
        out_specs=pl.BlockSpec((tq, B_WIDTH), lambda b, i: (b * nq + i, 0)),
        out_shape=jax.ShapeDtypeStruct((B * S, B_WIDTH), BF16),
        compiler_params=_cparams(("parallel", "parallel")),
        name="pool",
    )(proj, proj, proj, proj, w_pool, scale.reshape(1, B_WIDTH))


DIFF_KB = 256
DIFF_CHUNK = 4


def _diff_kernel(q_ref, k_ref, vt_ref, lam_ref, g_ref, o_ref, m_ref, l_ref, acc_ref, *, tq, lam_init):
    qi = pl.program_id(2)
    spq = tq // DIFF_KB
    q = q_ref[...]
    lane = lax.broadcasted_iota(jnp.int32, q.shape, 1)
    zero = jnp.zeros_like(q)
    qcat = jnp.concatenate([jnp.where(lane < HEAD_DIM, q, zero),
                            jnp.where(lane >= HEAD_DIM, q, zero)], axis=0)

    m_ref[...] = jnp.full(m_ref.shape, NEG, F32)
    l_ref[...] = jnp.zeros(l_ref.shape, F32)
    acc_ref[...] = jnp.zeros(acc_ref.shape, F32)

    def process(blocks):
        scores = []
        for kidx, diag in blocks:
            off = pl.multiple_of(kidx * DIFF_KB, DIFF_KB)
            st = lax.dot_general(k_ref[pl.ds(off, DIFF_KB), :], qcat, (((1,), (1,)), ((), ())),
                                 preferred_element_type=F32)
            if diag is not None:
                r = lax.broadcasted_iota(jnp.int32, st.shape, 0)
                c = lax.broadcasted_iota(jnp.int32, st.shape, 1)
                c = jnp.where(c >= tq, c - tq, c)
                st = jnp.where(r + diag <= c, st, NEG)
            scores.append(st)
        for (kidx, _), st in zip(blocks, scores):
            m = m_ref[...]
            m_new = jnp.maximum(m, jnp.max(st, axis=0, keepdims=True))
            alpha = jnp.exp2(m - m_new)
            p = jnp.exp2(st - m_new)
            l_ref[...] = alpha * l_ref[...] + jnp.sum(p, axis=0, keepdims=True)
            m_ref[...] = m_new
            pv = jnp.dot(vt_ref[kidx], p.astype(BF16), preferred_element_type=F32)
            acc_ref[...] = alpha * acc_ref[...] + pv

    n_below = qi * spq
    n_full = n_below // DIFF_CHUNK

    def body(c, carry):
        process([(c * DIFF_CHUNK + u, None) for u in range(DIFF_CHUNK)])
        return carry

    lax.fori_loop(0, n_full, body, 0)
    for rem in range(0, DIFF_CHUNK, spq):
        @pl.when(n_below % DIFF_CHUNK == rem)
        def _(rem=rem):
            base = n_full * DIFF_CHUNK
            process([(base + u, None) for u in range(rem)]
                    + [(base + rem + u, u * DIFF_KB) for u in range(spq)])

    lv = lam_ref[...]
    lam = (jnp.exp(jnp.sum(lv[0:1] * lv[1:2], axis=1, keepdims=True))
           - jnp.exp(jnp.sum(lv[2:3] * lv[3:4], axis=1, keepdims=True)) + lam_init)
    l = l_ref[...]
    acc = acc_ref[...]
    ot = acc[:, :tq] / l[:, :tq] - lam * (acc[:, tq:] / l[:, tq:])
    ms = jnp.mean(ot * ot, axis=0, keepdims=True)
    yt = ot * lax.rsqrt(ms + DIFF_EPS) * (g_ref[...] * (1.0 - lam_init))
    o_ref[...] = yt.T.astype(o_ref.dtype)


def _diff(proj, vt, lam_vecs, subln, *, B, S, lam_init, tq=512):
    nq = S // tq
    nkb = S // DIFF_KB
    assert tq % DIFF_KB == 0 and DIFF_CHUNK % (tq // DIFF_KB) == 0
    return pl.pallas_call(
        functools.partial(_diff_kernel, tq=tq, lam_init=lam_init),
        grid=(B, C_HEADS, nq),
        in_specs=[
            pl.BlockSpec((tq, C_VDIM), lambda b, h, i: (b * nq + i, COL_QC // C_VDIM + h)),
            pl.BlockSpec((S, C_VDIM), lambda b, h, i: (b, COL_KC // C_VDIM + h)),
            pl.BlockSpec((None, None, nkb, C_VDIM, DIFF_KB), lambda b, h, i: (b, h, 0, 0, 0)),
            pl.BlockSpec((4, HEAD_DIM), lambda b, h, i: (0, 0)),
            pl.BlockSpec((C_VDIM, 1), lambda b, h, i: (0, 0)),
        ],
        out_specs=pl.BlockSpec((tq, C_VDIM), lambda b, h, i: (b * nq + i, h)),
        out_shape=jax.ShapeDtypeStruct((B * S, C_WIDTH), BF16),
        scratch_shapes=[pltpu.VMEM((1, 2 * tq), F32), pltpu.VMEM((1, 2 * tq), F32),
                        pltpu.VMEM((C_VDIM, 2 * tq), F32)],
        compiler_params=_cparams(("parallel", "parallel", "arbitrary")),
        name="diff",
    )(proj, proj, vt, lam_vecs, subln.reshape(C_VDIM, 1))


def _out_kernel(oa_ref, ob_ref, oc_ref, w_ref, h_ref, g_ref, wr_ref, br_ref, h1_ref, hn_ref, lg_ref):
    acc = jnp.dot(oa_ref[...], w_ref[0:A_WIDTH, :], preferred_element_type=F32)
    acc = acc + jnp.dot(ob_ref[...], w_ref[A_WIDTH:A_WIDTH + B_WIDTH, :], preferred_element_type=F32)
    acc = acc + jnp.dot(oc_ref[...], w_ref[A_WIDTH + B_WIDTH:, :], preferred_element_type=F32)
    h1 = h_ref[...] + acc
    h1_ref[...] = h1
    ms = jnp.mean(h1 * h1, axis=1, keepdims=True)
    hn = h1 * lax.rsqrt(ms + NORM_EPS) * g_ref[...]
    hn_ref[...] = hn.astype(hn_ref.dtype)
    lg = lax.dot_general(wr_ref[...], hn, (((1,), (1,)), ((), ())),
                         preferred_element_type=F32, precision=lax.Precision.HIGHEST)
    lg_ref[...] = lg + br_ref[...]


def _out(oa, ob, oc, w, h, g, wr_t, br, *, tm=256):
    T, D = h.shape
    return pl.pallas_call(
        _out_kernel,
        grid=(T // tm,),
        in_specs=[
            pl.BlockSpec((tm, A_WIDTH), lambda i: (i, 0)),
            pl.BlockSpec((tm, B_WIDTH), lambda i: (i, 0)),
            pl.BlockSpec((tm, C_WIDTH), lambda i: (i, 0)),
            pl.BlockSpec((D, D), lambda i: (0, 0)),
            pl.BlockSpec((tm, D), lambda i: (i, 0)),
            pl.BlockSpec((1, D), lambda i: (0, 0)),
            pl.BlockSpec((N_ROUTER_ROWS, D), lambda i: (0, 0)),
            pl.BlockSpec((N_ROUTER_ROWS, 1), lambda i: (0, 0)),
        ],
        out_specs=[
            pl.BlockSpec((tm, D), lambda i: (i, 0)),
            pl.BlockSpec((tm, D), lambda i: (i, 0)),
            pl.BlockSpec((N_ROUTER_ROWS, tm), lambda i: (0, i)),
        ],
        out_shape=[
            jax.ShapeDtypeStruct((T, D), F32),
            jax.ShapeDtypeStruct((T, D), BF16),
            jax.ShapeDtypeStruct((N_ROUTER_ROWS, T), F32),
        ],
        compiler_params=_cparams(("parallel",)),
        name="out",
    )(oa, ob, oc, w, h, g.reshape(1, D), wr_t, br)


def _expert_kernel(blk_ref, exp_ref, lo_ref, hi_ref, x_ref, g_ref, wg_ref, wu_ref, wd_ref, o_ref, *, tm):
    p = pl.program_id(0)
    blk = blk_ref[p]
    first = (p == 0) | (blk != blk_ref[jnp.maximum(p - 1, 0)])
    lo = lo_ref[p]
    hi = hi_ref[p]

    @pl.when(first)
    def _():
        o_ref[...] = jnp.zeros_like(o_ref)

    @pl.when(hi > lo)
    def _():
        x = x_ref[...]
        a = jnp.dot(x, wg_ref[0], preferred_element_type=F32)
        u = jnp.dot(x, wu_ref[0], preferred_element_type=F32)
        act = (a / (1.0 + jnp.exp(-a))) * u
        y = jnp.dot(act.astype(BF16), wd_ref[0], preferred_element_type=F32)
        rows = blk * tm + lax.broadcasted_iota(jnp.int32, (tm, 1), 0)
        wgt = jnp.where((rows >= lo) & (rows < hi), g_ref[...], 0.0)
        o_ref[...] += (y * wgt).astype(o_ref.dtype)


def _experts(seg_blk, seg_exp, seg_lo, seg_hi, xs, gates, wg, wu, wd, *, tm):
    N, D = xs.shape
    P = seg_blk.shape[0]
    grid_spec = pltpu.PrefetchScalarGridSpec(
        num_scalar_prefetch=4,
        grid=(P,),
        in_specs=[
            pl.BlockSpec((tm, D), lambda p, blk, ex, lo, hi: (blk[p], 0)),
            pl.BlockSpec((tm, 1), lambda p, blk, ex, lo, hi: (blk[p], 0)),
            pl.BlockSpec((1, D, D_FF), lambda p, blk, ex, lo, hi: (ex[p], 0, 0)),
            pl.BlockSpec((1, D, D_FF), lambda p, blk, ex, lo, hi: (ex[p], 0, 0)),
            pl.BlockSpec((1, D_FF, D), lambda p, blk, ex, lo, hi: (ex[p], 0, 0)),
        ],
        out_specs=pl.BlockSpec((tm, D), lambda p, blk, ex, lo, hi: (blk[p], 0)),
    )
    return pl.pallas_call(
        functools.partial(_expert_kernel, tm=tm),
        grid_spec=grid_spec,
        out_shape=jax.ShapeDtypeStruct((N, D), BF16),
        compiler_params=_cparams(("arbitrary",)),
        name="experts",
    )(seg_blk, seg_exp, seg_lo, seg_hi, xs, gates, wg, wu, wd)


def _route(lg_t, T, tm):
    lg = lg_t[:N_GROUPS + N_EXPERTS].T
    pg = jax.nn.softmax(lg[:, :N_GROUPS], axis=-1)
    gsel = jnp.argmax(pg, axis=-1)
    pgsel = jnp.take_along_axis(pg, gsel[:, None], axis=1)
    le = lg[:, N_GROUPS:].reshape(T, N_GROUPS, EXPERTS_PER_GROUP)
    le = jnp.take_along_axis(le, gsel[:, None, None], axis=1)[:, 0]
    pe = jax.nn.softmax(le, axis=-1)
    top_p, top_i = lax.top_k(pe, TOP_K)
    gates = pgsel * top_p / jnp.sum(top_p, axis=-1, keepdims=True)
    eidx = (gsel[:, None] * EXPERTS_PER_GROUP + top_i).astype(jnp.int32)

    N = T * TOP_K
    flat_e = eidx.reshape(N)
    order = jnp.argsort(flat_e).astype(jnp.int32)
    sorted_e = flat_e[order]
    tok = order // TOP_K
    g_sorted = gates.reshape(N)[order]
    pos = jnp.zeros((N,), jnp.int32).at[order].set(jnp.arange(N, dtype=jnp.int32))
    counts = jnp.bincount(flat_e, length=N_EXPERTS)
    starts = (jnp.cumsum(counts) - counts).astype(jnp.int32)
    nb = N // tm
    bounds = jnp.sort(jnp.concatenate([jnp.arange(nb, dtype=jnp.int32) * tm, starts]))
    ends = jnp.concatenate([bounds[1:], jnp.full((1,), N, bounds.dtype)])
    seg_blk = jnp.minimum(bounds // tm, nb - 1).astype(jnp.int32)
    seg_exp = sorted_e[jnp.minimum(bounds, N - 1)].astype(jnp.int32)
    return tok, g_sorted, pos.reshape(T, TOP_K), seg_blk, seg_exp, bounds, ends


def _in_proj_perm():
    offs = {}
    idx = 0
    for name, w in zip(("qa", "ka", "va", "ub", "qc", "kc", "vc"), (768, 256, 256, 512, 768, 768, 768)):
        offs[name] = (idx, w)
        idx += w
    cols = []
    for name in ("qa", "qc", "ka", "kc", "va", "ub", "vc"):
        o, w = offs[name]
        cols.extend(range(o, o + w))
    return jnp.asarray(cols, jnp.int32)


def kernel(x, positions, attn_norm, w_in, sinks, branch_norm_a, w_pool, pool_scale, lambda_q1, lambda_k1,
           lambda_q2, lambda_k2, subln, w_out, ffn_norm, w_router_group, b_router_group, w_router_expert,
           b_router_expert, w_expert_gate, w_expert_up, w_expert_down, final_norm):
    B, S, D = x.shape
    T = B * S
    depth = w_in.shape[0]
    tm_e = 256

    half = HEAD_DIM // 2
    inv = ROPE_THETA ** (-jnp.arange(half, dtype=F32) / half)
    ang = positions.astype(F32).reshape(T, 1) * inv
    cos, sin = jnp.cos(ang), jnp.sin(ang)
    cos_t = jnp.concatenate([cos, cos, cos, cos], axis=1)
    sin_t = jnp.concatenate([-sin, sin, -sin, sin], axis=1)
    perm = _in_proj_perm()

    h = x.reshape(T, D)
    adds = []
    for l in range(depth):
        if l == 0:
            (hn,) = _norm(h, [], attn_norm[l], emit_h=False, emit_bf16=True)
        else:
            h, hn = _norm(h, adds, attn_norm[l], emit_h=True, emit_bf16=True)
        w_in_l = w_in[l][:, perm].astype(BF16)
        proj = _proj(hn, w_in_l, cos_t, sin_t)

        oa = _swa(proj, sinks[l], branch_norm_a[l], B=B, S=S)
        ob = _pool(proj, w_pool[l].astype(BF16), pool_scale[l], B=B, S=S)

        vc = proj[:, COL_VC:].reshape(B, S // DIFF_KB, DIFF_KB, C_HEADS, C_VDIM)
        vt = vc.transpose(0, 3, 1, 4, 2)
        lam_init = 0.8 - 0.6 * math.exp(-0.3 * l)
        lam_vecs = jnp.stack([lambda_q1[l], lambda_k1[l], lambda_q2[l], lambda_k2[l]]).astype(F32)
        oc = _diff(proj, vt, lam_vecs, subln[l], B=B, S=S, lam_init=lam_init)

        wr = jnp.concatenate([w_router_group[l], w_router_expert[l]], axis=1)
        wr_t = jnp.zeros((N_ROUTER_ROWS, D), F32).at[:N_GROUPS + N_EXPERTS].set(wr.T)
        br = jnp.zeros((N_ROUTER_ROWS, 1), F32).at[:N_GROUPS + N_EXPERTS, 0].set(
            jnp.concatenate([b_router_group[l], b_router_expert[l]]))
        h1, hn2, lg_t = _out(oa, ob, oc, w_out[l].astype(BF16), h, ffn_norm[l], wr_t, br)

        tok, g_sorted, pos, seg_blk, seg_exp, seg_lo, seg_hi = _route(lg_t, T, tm_e)
        xs = hn2[tok]
        ys = _experts(seg_blk, seg_exp, seg_lo, seg_hi, xs, g_sorted.reshape(-1, 1),
                      w_expert_gate[l].astype(BF16), w_expert_up[l].astype(BF16),
                      w_expert_down[l].astype(BF16), tm=tm_e)
        adds = [ys[pos[:, 0]], ys[pos[:, 1]]]
        h = h1

    (out,) = _norm(h, adds, final_norm, emit_h=False, emit_bf16=False)
    return out.reshape(B, S, D)
```

```python
import functools
import math

import jax
import jax.numpy as jnp
from jax import lax
from jax.experimental import pallas as pl
from jax.experimental.pallas import tpu as pltpu

F32 = jnp.float32
BF16 = jnp.bfloat16

D_MODEL = 2048
HEAD_DIM = 64
ROPE_THETA = 10000.0
A_HEADS = 12
A_KV_HEADS = 4
A_GROUP = 3
A_WIDTH = 768
WINDOW = 128
POOL_WINDOWS = (2, 4, 8, 16)
B_WIDTH = 512
B_GROUP_DIM = 128
C_VDIM = 128
C_WIDTH = 768
C_HEADS = 6
DIFF_EPS = 1e-5
IN_COLS = 4096
N_GROUPS = 4
EXPERTS_PER_GROUP = 8
N_EXPERTS = 32
TOP_K = 2
D_FF = 512
NORM_EPS = 1e-6
NEG = -1e30
LOG2E = math.log2(math.e)
Q_SCALE = HEAD_DIM ** -0.5 * LOG2E

COL_QA, COL_QC, COL_KA, COL_KC, COL_VA, COL_UB, COL_VC = 0, 768, 1536, 1792, 2560, 2816, 3328
PROJ_TN = 256
PROJ_SUB = 256
N_SCALED_TILES = COL_KA // PROJ_TN
N_ROPE_TILES = COL_VA // PROJ_TN
N_MAIN_TILES = COL_VC // PROJ_TN
DIFF_KB = 512
VT_ROWS = C_VDIM + 16

VMEM_LIMIT = 48 * 1024 * 1024


def _cparams(sem):
    return pltpu.CompilerParams(dimension_semantics=sem, vmem_limit_bytes=VMEM_LIMIT)


def _norm_kernel(*refs, n_add, emit_h, emit_bf16, eps):
    h_ref = refs[0]
    add_refs = refs[1:1 + n_add]
    g_ref = refs[1 + n_add]
    outs = refs[2 + n_add:]
    h = h_ref[...]
    for r in add_refs:
        h = h + r[...].astype(F32)
    ms = jnp.mean(h * h, axis=-1, keepdims=True)
    y = h * lax.rsqrt(ms + eps) * g_ref[...]
    k = 0
    if emit_h:
        outs[k][...] = h
        k += 1
    outs[k][...] = y.astype(BF16 if emit_bf16 else F32)


def _norm(h, adds, g, *, emit_h, emit_bf16, tm=512):
    T, D = h.shape
    row = pl.BlockSpec((tm, D), lambda i: (i, 0))
    out_shape, out_specs = [], []
    if emit_h:
        out_shape.append(jax.ShapeDtypeStruct((T, D), F32))
        out_specs.append(row)
    out_shape.append(jax.ShapeDtypeStruct((T, D), BF16 if emit_bf16 else F32))
    out_specs.append(row)
    return pl.pallas_call(
        functools.partial(_norm_kernel, n_add=len(adds), emit_h=emit_h, emit_bf16=emit_bf16, eps=NORM_EPS),
        grid=(T // tm,),
        in_specs=[row] * (1 + len(adds)) + [pl.BlockSpec((1, D), lambda i: (0, 0))],
        out_specs=out_specs,
        out_shape=out_shape,
        compiler_params=_cparams(("parallel",)),
        name="norm",
    )(h, *adds, g.reshape(1, D))


def _proj_kernel(x_ref, w_ref, wt_ref, cos_ref, sin_ref, o_ref, vt_ref, *, tm):
    j = pl.program_id(1)
    subs = [slice(s * PROJ_SUB, (s + 1) * PROJ_SUB) for s in range(tm // PROJ_SUB)]

    def matmuls():
        return [jnp.dot(x_ref[rows, :], w_ref[...], preferred_element_type=F32) for rows in subs]

    @pl.when(j < N_ROPE_TILES)
    def _():
        accs = matmuls()
        scale = jnp.where(j < N_SCALED_TILES, Q_SCALE, 1.0).astype(F32)
        lane = lax.broadcasted_iota(jnp.int32, (PROJ_SUB, PROJ_TN), 1)
        first_half = (lane & (HEAD_DIM // 2)) == 0
        for rows, acc in zip(subs, accs):
            c = jnp.concatenate([cos_ref[rows, :] * scale] * (PROJ_TN // 128), axis=1)
            s = jnp.concatenate([sin_ref[rows, :] * scale] * (PROJ_TN // 128), axis=1)
            swapped = jnp.where(first_half,
                                pltpu.roll(acc, PROJ_TN - HEAD_DIM // 2, 1),
                                pltpu.roll(acc, HEAD_DIM // 2, 1))
            o_ref[rows, :] = (acc * c + swapped * s).astype(o_ref.dtype)

    @pl.when((j >= N_ROPE_TILES) & (j < N_MAIN_TILES))
    def _():
        for rows, acc in zip(subs, matmuls()):
            o_ref[rows, :] = acc.astype(o_ref.dtype)

    @pl.when(j >= N_MAIN_TILES)
    def _():
        accs = [lax.dot_general(wt_ref[...], x_ref[rows, :], (((1,), (1,)), ((), ())),
                                preferred_element_type=F32) for rows in subs]
        row = lax.broadcasted_iota(jnp.int32, (VT_ROWS - C_VDIM, DIFF_KB), 0)
        extra = jnp.where(row == 0, 1.0, 0.0).astype(vt_ref.dtype)
        per_kb = DIFF_KB // PROJ_SUB
        for s, acc in enumerate(accs):
            cols = slice((s % per_kb) * PROJ_SUB, (s % per_kb + 1) * PROJ_SUB)
            for hh in range(PROJ_TN // C_VDIM):
                vt_ref[hh, s // per_kb, 0:C_VDIM, cols] = acc[hh * C_VDIM:(hh + 1) * C_VDIM, :].astype(vt_ref.dtype)
        for hh in range(PROJ_TN // C_VDIM):
            for kb in range(tm // DIFF_KB):
                vt_ref[hh, kb, C_VDIM:VT_ROWS, :] = extra


def _proj(hn, w, wt, cos_t, sin_t, *, B, S, tm=1024):
    T, D = hn.shape
    spt = S // tm
    last_main = N_MAIN_TILES - 1
    return pl.pallas_call(
        functools.partial(_proj_kernel, tm=tm),
        grid=(T // tm, IN_COLS // PROJ_TN),
        in_specs=[
            pl.BlockSpec((tm, D), lambda i, j: (i, 0)),
            pl.BlockSpec((D, PROJ_TN), lambda i, j: (0, jnp.minimum(j, last_main))),
            pl.BlockSpec((PROJ_TN, D), lambda i, j: (jnp.maximum(j - N_MAIN_TILES, 0), 0)),
            pl.BlockSpec((tm, 128), lambda i, j: (i, 0)),
            pl.BlockSpec((tm, 128), lambda i, j: (i, 0)),
        ],
        out_specs=[
            pl.BlockSpec((tm, PROJ_TN), lambda i, j: (i, jnp.minimum(j, last_main))),
            pl.BlockSpec((None, PROJ_TN // C_VDIM, tm // DIFF_KB, VT_ROWS, DIFF_KB),
                         lambda i, j: (i // spt, jnp.maximum(j - N_MAIN_TILES, 0), i % spt, 0, 0)),
        ],
        out_shape=[
            jax.ShapeDtypeStruct((T, COL_VC), BF16),
            jax.ShapeDtypeStruct((B, C_HEADS, S // DIFF_KB, VT_ROWS, DIFF_KB), BF16),
        ],
        compiler_params=_cparams(("parallel", "arbitrary")),
        name="proj",
    )(hn, w, wt, cos_t, sin_t)


def _swa_kernel(sink_ref, q_ref, kc_ref, vc_ref, kp_ref, vp_ref, g_ref, o_ref, *, tq):
    i = pl.program_id(1)
    kext = jnp.concatenate([kp_ref[...], kc_ref[...]], axis=0)
    vext = jnp.concatenate([vp_ref[...], vc_ref[...]], axis=0)
    qi = lax.broadcasted_iota(jnp.int32, (WINDOW, 2 * WINDOW), 0)
    jj = lax.broadcasted_iota(jnp.int32, (WINDOW, 2 * WINDOW), 1)
    band = (jj > qi) & (jj <= qi + WINDOW)
    g = g_ref[...]
    for sb in range(tq // WINDOW):
        mask = band
        if sb == 0:
            mask = band & ((jj >= WINDOW) | (i > 0))
        outs = []
        for h in range(A_HEADS):
            kv = h // A_GROUP
            qh = q_ref[sb * WINDOW:(sb + 1) * WINDOW, h * HEAD_DIM:(h + 1) * HEAD_DIM]
            kb = kext[sb * WINDOW:(sb + 2) * WINDOW, kv * HEAD_DIM:(kv + 1) * HEAD_DIM]
            vb = vext[sb * WINDOW:(sb + 2) * WINDOW, kv * HEAD_DIM:(kv + 1) * HEAD_DIM]
            s = lax.dot_general(qh, kb, (((1,), (1,)), ((), ())), preferred_element_type=F32)
            s = jnp.where(mask, s, NEG)
            sink = sink_ref[h] * LOG2E
            m = jnp.maximum(jnp.max(s, axis=1, keepdims=True), sink)
            p = jnp.exp2(s - m)
            l = jnp.sum(p, axis=1, keepdims=True) + jnp.exp2(sink - m)
            o = jnp.dot(p.astype(BF16), vb, preferred_element_type=F32)
            outs.append(o / l)
        o_all = jnp.concatenate(outs, axis=1)
        ms = jnp.mean(o_all * o_all, axis=1, keepdims=True)
        y = o_all * lax.rsqrt(ms + NORM_EPS) * g
        o_ref[sb * WINDOW:(sb + 1) * WINDOW, :] = y.astype(o_ref.dtype)


def _swa(proj, sinks, g, *, B, S, tq=256):
    nq = S // tq
    rpb = tq // WINDOW

    def prev_idx(col):
        def f(b, i):
            return (jnp.maximum(b * (S // WINDOW) + i * rpb - 1, 0), col)
        return f

    return pl.pallas_call(
        functools.partial(_swa_kernel, tq=tq),
        grid=(B, nq),
        in_specs=[
            pl.BlockSpec(memory_space=pltpu.SMEM),
            pl.BlockSpec((tq, A_WIDTH), lambda b, i: (b * nq + i, COL_QA // A_WIDTH)),
            pl.BlockSpec((tq, 256), lambda b, i: (b * nq + i, COL_KA // 256)),
            pl.BlockSpec((tq, 256), lambda b, i: (b * nq + i, COL_VA // 256)),
            pl.BlockSpec((WINDOW, 256), prev_idx(COL_KA // 256)),
            pl.BlockSpec((WINDOW, 256), prev_idx(COL_VA // 256)),
            pl.BlockSpec((1, A_WIDTH), lambda b, i: (0, 0)),
        ],
        out_specs=pl.BlockSpec((tq, A_WIDTH), lambda b, i: (b * nq + i, 0)),
        out_shape=jax.ShapeDtypeStruct((B * S, A_WIDTH), BF16),
        compiler_params=_cparams(("parallel", "parallel")),
        name="swa",
    )(sinks, proj, proj, proj, proj, proj, g.reshape(1, A_WIDTH))


POOL_HALO = 16


def _pool_kernel(u0_ref, u1_ref, h0_ref, h1_ref, w_ref, s_ref, o_ref, *, tq):
    i = pl.program_id(1)
    u = jnp.concatenate([u0_ref[...], u1_ref[...]], axis=1).astype(F32)
    halo = jnp.concatenate([h0_ref[...], h1_ref[...]], axis=1).astype(F32)
    halo = halo * (i > 0).astype(F32)
    ext = jnp.concatenate([halo, u], axis=0)
    t = i * tq + lax.broadcasted_iota(jnp.int32, (tq, 1), 0)
    for gi, w in enumerate(POOL_WINDOWS):
        sl = slice(gi * B_GROUP_DIM, (gi + 1) * B_GROUP_DIM)
        s = ext[:, sl]
        shift = 1
        while shift < w:
            s = s + pltpu.roll(s, shift, 0)
            shift *= 2
        cnt = jnp.minimum(t + 1, w).astype(F32)
        y = (s[POOL_HALO:, :] / cnt - u[:, sl]).astype(BF16)
        o = jnp.dot(y, w_ref[gi], preferred_element_type=F32) * s_ref[:, sl]
        o_ref[:, sl] = o.astype(o_ref.dtype)


def _pool(proj, w_pool, scale, *, B, S, tq=256):
    nq = S // tq
    c0 = COL_UB // 256

    def cur(c):
        return pl.BlockSpec((tq, 256), lambda b, i: (b * nq + i, c))

    def halo(c):
        return pl.BlockSpec(
            (POOL_HALO, 256),
            lambda b, i: (jnp.maximum((b * S + i * tq) // POOL_HALO - 1, 0), c))

    return pl.pallas_call(
        functools.partial(_pool_kernel, tq=tq),
        grid=(B, nq),
        in_specs=[cur(c0), cur(c0 + 1), halo(c0), halo(c0 + 1),
                  pl.BlockSpec((len(POOL_WINDOWS), B_GROUP_DIM, B_GROUP_DIM), lambda b, i: (0, 0, 0)),
                  pl.BlockSpec((1, B_WIDTH), lambda b, i: (0, 0))],
        out_specs=pl.BlockSpec((tq, B_WIDTH), lambda b, i: (b * nq + i, 0)),
        out_shape=jax.ShapeDtypeStruct((B * S, B_WIDTH), BF16),
        compiler_params=_cparams(("parallel", "parallel")),
        name="pool",
    )(proj, proj, proj, proj, w_pool, scale.reshape(1, B_WIDTH))


def _diff_kernel(q_ref, k_ref, vt_ref, lam_ref, g_ref, o_ref, qcat_ref, s0_ref, s1_ref, m_ref, acc_ref,
                 *, tq, lam_init):
    qi = pl.program_id(2)
    q = q_ref[...]
    lane = lax.broadcasted_iota(jnp.int32, q.shape, 1)
    zero = jnp.zeros_like(q)
    qcat_ref[0:tq, :] = jnp.where(lane < HEAD_DIM, q, zero)
    qcat_ref[tq:2 * tq, :] = jnp.where(lane >= HEAD_DIM, q, zero)

    m_ref[...] = jnp.full(m_ref.shape, NEG, F32)
    acc_ref[...] = jnp.zeros(acc_ref.shape, F32)

    def scores(kidx):
        off = pl.multiple_of(kidx * DIFF_KB, DIFF_KB)
        return lax.dot_general(k_ref[pl.ds(off, DIFF_KB), :], qcat_ref[...], (((1,), (1,)), ((), ())),
                               preferred_element_type=F32)

    def consume(s_ref, kidx, diagonal):
        st = s_ref[...]
        if diagonal:
            r = lax.broadcasted_iota(jnp.int32, st.shape, 0)
            c = lax.broadcasted_iota(jnp.int32, st.shape, 1)
            c = jnp.where(c >= tq, c - tq, c)
            st = jnp.where(r <= c, st, NEG)
        m = m_ref[...]
        m_new = jnp.maximum(m, jnp.max(st, axis=0, keepdims=True))
        alpha = jnp.exp2(m - m_new)
        p = jnp.exp2(st - m_new)
        m_ref[...] = m_new
        pv = jnp.dot(vt_ref[kidx], p.astype(BF16), preferred_element_type=F32)
        acc_ref[...] = alpha * acc_ref[...] + pv

    def step(cur_ref, nxt_ref, kidx):
        s_next = scores(kidx + 1)
        consume(cur_ref, kidx, False)
        nxt_ref[...] = s_next

    s0_ref[...] = scores(0)

    def body(c, carry):
        step(s0_ref, s1_ref, 2 * c)
        step(s1_ref, s0_ref, 2 * c + 1)
        return carry

    lax.fori_loop(0, qi // 2, body, 0)

    @pl.when(qi % 2 == 1)
    def _():
        step(s0_ref, s1_ref, qi - 1)
        consume(s1_ref, qi, True)

    @pl.when(qi % 2 == 0)
    def _():
        consume(s0_ref, qi, True)

    lv = lam_ref[...]
    lam = (jnp.exp(jnp.sum(lv[0:1] * lv[1:2], axis=1, keepdims=True))
           - jnp.exp(jnp.sum(lv[2:3] * lv[3:4], axis=1, keepdims=True)) + lam_init)
    l = acc_ref[C_VDIM:C_VDIM + 1, :]
    acc = acc_ref[0:C_VDIM, :]
    ot = acc[:, :tq] / l[:, :tq] - lam * (acc[:, tq:] / l[:, tq:])
    ms = jnp.mean(ot * ot, axis=0, keepdims=True)
    yt = ot * lax.rsqrt(ms + DIFF_EPS) * (g_ref[...] * (1.0 - lam_init))
    o_ref[...] = yt.T.astype(o_ref.dtype)


def _diff(proj, vt, lam_vecs, subln, *, B, S, lam_init, tq=512):
    nq = S // tq
    nkb = S // DIFF_KB
    assert tq == DIFF_KB
    return pl.pallas_call(
        functools.partial(_diff_kernel, tq=tq, lam_init=lam_init),
        grid=(B, C_HEADS, nq),
        in_specs=[
            pl.BlockSpec((tq, C_VDIM), lambda b, h, i: (b * nq + i, COL_QC // C_VDIM + h)),
            pl.BlockSpec((S, C_VDIM), lambda b, h, i: (b, COL_KC // C_VDIM + h)),
            pl.BlockSpec((None, None, nkb, VT_ROWS, DIFF_KB), lambda b, h, i: (b, h, 0, 0, 0)),
            pl.BlockSpec((4, HEAD_DIM), lambda b, h, i: (0, 0)),
            pl.BlockSpec((C_VDIM, 1), lambda b, h, i: (0, 0)),
        ],
        out_specs=pl.BlockSpec((tq, C_VDIM), lambda b, h, i: (b * nq + i, h)),
        out_shape=jax.ShapeDtypeStruct((B * S, C_WIDTH), BF16),
        scratch_shapes=[pltpu.VMEM((2 * tq, C_VDIM), BF16),
                        pltpu.VMEM((DIFF_KB, 2 * tq), F32), pltpu.VMEM((DIFF_KB, 2 * tq), F32),
                        pltpu.VMEM((1, 2 * tq), F32), pltpu.VMEM((VT_ROWS, 2 * tq), F32)],
        compiler_params=_cparams(("parallel", "parallel", "arbitrary")),
        name="diff",
    )(proj, proj, vt, lam_vecs, subln.reshape(C_VDIM, 1))


OUT_SUB = 256


def _out_kernel(oa_ref, ob_ref, oc_ref, w_ref, h_ref, g_ref, h1_ref, hn_ref, *, tm):
    subs = [slice(s * OUT_SUB, (s + 1) * OUT_SUB) for s in range(tm // OUT_SUB)]
    accs = []
    for rows in subs:
        acc = jnp.dot(oa_ref[rows, :], w_ref[0:A_WIDTH, :], preferred_element_type=F32)
        acc = acc + jnp.dot(ob_ref[rows, :], w_ref[A_WIDTH:A_WIDTH + B_WIDTH, :], preferred_element_type=F32)
        acc = acc + jnp.dot(oc_ref[rows, :], w_ref[A_WIDTH + B_WIDTH:, :], preferred_element_type=F32)
        accs.append(acc)
    for rows, acc in zip(subs, accs):
        h1 = h_ref[rows, :] + acc
        h1_ref[rows, :] = h1
        ms = jnp.mean(h1 * h1, axis=1, keepdims=True)
        hn = h1 * lax.rsqrt(ms + NORM_EPS) * g_ref[...]
        hn_ref[rows, :] = hn.astype(hn_ref.dtype)


def _out(oa, ob, oc, w, h, g, *, tm=512):
    T, D = h.shape
    return pl.pallas_call(
        functools.partial(_out_kernel, tm=tm),
        grid=(T // tm,),
        in_specs=[
            pl.BlockSpec((tm, A_WIDTH), lambda i: (i, 0)),
            pl.BlockSpec((tm, B_WIDTH), lambda i: (i, 0)),
            pl.BlockSpec((tm, C_WIDTH), lambda i: (i, 0)),
            pl.BlockSpec((D, D), lambda i: (0, 0)),
            pl.BlockSpec((tm, D), lambda i: (i, 0)),
            pl.BlockSpec((1, D), lambda i: (0, 0)),
        ],
        out_specs=[
            pl.BlockSpec((tm, D), lambda i: (i, 0)),
            pl.BlockSpec((tm, D), lambda i: (i, 0)),
        ],
        out_shape=[
            jax.ShapeDtypeStruct((T, D), F32),
            jax.ShapeDtypeStruct((T, D), BF16),
        ],
        compiler_params=_cparams(("parallel",)),
        name="out",
    )(oa, ob, oc, w, h, g.reshape(1, D))


ROUTER_COLS = 128
ROUTER_LO = 64


def _router_kernel(x_ref, w_ref, b_ref, o_ref):
    o_ref[...] = jnp.dot(x_ref[...], w_ref[...], preferred_element_type=F32) + b_ref[...]


def _router(hn, w2, b2, *, tm=1024):
    T, D = hn.shape
    return pl.pallas_call(
        _router_kernel,
        grid=(T // tm,),
        in_specs=[
            pl.BlockSpec((tm, D), lambda i: (i, 0)),
            pl.BlockSpec((D, ROUTER_COLS), lambda i: (0, 0)),
            pl.BlockSpec((1, ROUTER_COLS), lambda i: (0, 0)),
        ],
        out_specs=pl.BlockSpec((tm, ROUTER_COLS), lambda i: (i, 0)),
        out_shape=jax.ShapeDtypeStruct((T, ROUTER_COLS), F32),
        compiler_params=_cparams(("parallel",)),
        name="router",
    )(hn, w2, b2)


def _expert_kernel(blk_ref, exp_ref, lo_ref, hi_ref, x_ref, g_ref, wg_ref, wu_ref, wd_ref, o_ref,
                   wg_bf, wu_bf, wd_bf, *, tm):
    p = pl.program_id(0)
    prev = jnp.maximum(p - 1, 0)
    blk = blk_ref[p]
    first = (p == 0) | (blk != blk_ref[prev])
    new_expert = (p == 0) | (exp_ref[p] != exp_ref[prev])
    lo = lo_ref[p]
    hi = hi_ref[p]

    @pl.when(new_expert)
    def _():
        wg_bf[...] = wg_ref[0].astype(BF16)
        wu_bf[...] = wu_ref[0].astype(BF16)
        wd_bf[...] = wd_ref[0].astype(BF16)

    @pl.when(first)
    def _():
        o_ref[...] = jnp.zeros_like(o_ref)

    @pl.when(hi > lo)
    def _():
        x = x_ref[...]
        a = jnp.dot(x, wg_bf[...], preferred_element_type=F32)
        u = jnp.dot(x, wu_bf[...], preferred_element_type=F32)
        act = (a / (1.0 + jnp.exp(-a))) * u
        y = jnp.dot(act.astype(BF16), wd_bf[...], preferred_element_type=F32)
        rows = blk * tm + lax.broadcasted_iota(jnp.int32, (tm, 1), 0)
        wgt = jnp.where((rows >= lo) & (rows < hi), g_ref[...], 0.0)
        o_ref[...] += (y * wgt).astype(o_ref.dtype)


def _experts(seg_blk, seg_exp, seg_lo, seg_hi, xs, gates, wg, wu, wd, *, tm):
    N, D = xs.shape
    P = seg_blk.shape[0]
    grid_spec = pltpu.PrefetchScalarGridSpec(
        num_scalar_prefetch=4,
        grid=(P,),
        in_specs=[
            pl.BlockSpec((tm, D), lambda p, blk, ex, lo, hi: (blk[p], 0)),
            pl.BlockSpec((tm, 1), lambda p, blk, ex, lo, hi: (blk[p], 0)),
            pl.BlockSpec((1, D, D_FF), lambda p, blk, ex, lo, hi: (ex[p], 0, 0)),
            pl.BlockSpec((1, D, D_FF), lambda p, blk, ex, lo, hi: (ex[p], 0, 0)),
            pl.BlockSpec((1, D_FF, D), lambda p, blk, ex, lo, hi: (ex[p], 0, 0)),
        ],
        out_specs=pl.BlockSpec((tm, D), lambda p, blk, ex, lo, hi: (blk[p], 0)),
        scratch_shapes=[pltpu.VMEM((D, D_FF), BF16), pltpu.VMEM((D, D_FF), BF16), pltpu.VMEM((D_FF, D), BF16)],
    )
    return pl.pallas_call(
        functools.partial(_expert_kernel, tm=tm),
        grid_spec=grid_spec,
        out_shape=jax.ShapeDtypeStruct((N, D), BF16),
        compiler_params=_cparams(("arbitrary",)),
        name="experts",
    )(seg_blk, seg_exp, seg_lo, seg_hi, xs, gates, wg, wu, wd)


def _route(lg2, T, tm):
    n_lg = N_GROUPS + N_EXPERTS
    lg = lg2[:, :n_lg] + lg2[:, ROUTER_LO:ROUTER_LO + n_lg]
    pg = jax.nn.softmax(lg[:, :N_GROUPS], axis=-1)
    gsel = jnp.argmax(pg, axis=-1)
    pgsel = jnp.take_along_axis(pg, gsel[:, None], axis=1)
    le = lg[:, N_GROUPS:].reshape(T, N_GROUPS, EXPERTS_PER_GROUP)
    le = jnp.take_along_axis(le, gsel[:, None, None], axis=1)[:, 0]
    pe = jax.nn.softmax(le, axis=-1)
    top_p, top_i = lax.top_k(pe, TOP_K)
    gates = pgsel * top_p / jnp.sum(top_p, axis=-1, keepdims=True)
    eidx = (gsel[:, None] * EXPERTS_PER_GROUP + top_i).astype(jnp.int32)

    N = T * TOP_K
    flat_e = eidx.reshape(N)
    iota = jnp.arange(N, dtype=jnp.int32)
    sorted_e, order, g_sorted = lax.sort((flat_e, iota, gates.reshape(N)), num_keys=1, is_stable=True)
    tok = order // TOP_K
    _, pos = lax.sort((order, iota), num_keys=1)
    counts = jnp.bincount(flat_e, length=N_EXPERTS)
    starts = (jnp.cumsum(counts) - counts).astype(jnp.int32)
    nb = N // tm
    bounds = jnp.sort(jnp.concatenate([jnp.arange(nb, dtype=jnp.int32) * tm, starts]))
    ends = jnp.concatenate([bounds[1:], jnp.full((1,), N, bounds.dtype)])
    seg_blk = jnp.minimum(bounds // tm, nb - 1).astype(jnp.int32)
    seg_exp = sorted_e[jnp.minimum(bounds, N - 1)].astype(jnp.int32)
    return tok, g_sorted, pos.reshape(T, TOP_K), seg_blk, seg_exp, bounds, ends


def _in_proj_perm():
    offs = {}
    idx = 0
    for name, w in zip(("qa", "ka", "va", "ub", "qc", "kc", "vc"), (768, 256, 256, 512, 768, 768, 768)):
        offs[name] = (idx, w)
        idx += w
    cols = []
    for name in ("qa", "qc", "ka", "kc", "va", "ub", "vc"):
        o, w = offs[name]
        cols.extend(range(o, o + w))
    return jnp.asarray(cols, jnp.int32)


def kernel(x, positions, attn_norm, w_in, sinks, branch_norm_a, w_pool, pool_scale, lambda_q1, lambda_k1,
           lambda_q2, lambda_k2, subln, w_out, ffn_norm, w_router_group, b_router_group, w_router_expert,
           b_router_expert, w_expert_gate, w_expert_up, w_expert_down, final_norm):
    B, S, D = x.shape
    T = B * S
    depth = w_in.shape[0]
    tm_e = 256

    half = HEAD_DIM // 2
    inv = ROPE_THETA ** (-jnp.arange(half, dtype=F32) / half)
    ang = positions.astype(F32).reshape(T, 1) * inv
    cos, sin = jnp.cos(ang), jnp.sin(ang)
    cos_t = jnp.concatenate([cos, cos, cos, cos], axis=1)
    sin_t = jnp.concatenate([-sin, sin, -sin, sin], axis=1)
    perm = _in_proj_perm()

    h = x.reshape(T, D)
    adds = []
    for l in range(depth):
        if l == 0:
            (hn,) = _norm(h, [], attn_norm[l], emit_h=False, emit_bf16=True)
        else:
            h, hn = _norm(h, adds, attn_norm[l], emit_h=True, emit_bf16=True)
        w_in_l = w_in[l][:, perm].astype(BF16)
        proj, vt = _proj(hn, w_in_l[:, :COL_VC], w_in_l[:, COL_VC:].T, cos_t, sin_t, B=B, S=S)

        oa = _swa(proj, sinks[l], branch_norm_a[l], B=B, S=S)
        ob = _pool(proj, w_pool[l].astype(BF16), pool_scale[l], B=B, S=S)

        lam_init = 0.8 - 0.6 * math.exp(-0.3 * l)
        lam_vecs = jnp.stack([lambda_q1[l], lambda_k1[l], lambda_q2[l], lambda_k2[l]]).astype(F32)
        oc = _diff(proj, vt, lam_vecs, subln[l], B=B, S=S, lam_init=lam_init)

        h1, hn2 = _out(oa, ob, oc, w_out[l].astype(BF16), h, ffn_norm[l])

        n_lg = N_GROUPS + N_EXPERTS
        wr = jnp.concatenate([w_router_group[l], w_router_expert[l]], axis=1)
        wr_hi = wr.astype(BF16)
        wr_lo = (wr - wr_hi.astype(F32)).astype(BF16)
        w2 = (jnp.zeros((D, ROUTER_COLS), BF16).at[:, :n_lg].set(wr_hi)
              .at[:, ROUTER_LO:ROUTER_LO + n_lg].set(wr_lo))
        b2 = jnp.zeros((1, ROUTER_COLS), F32).at[0, :n_lg].set(
            jnp.concatenate([b_router_group[l], b_router_expert[l]]))
        lg2 = _router(hn2, w2, b2)

        tok, g_sorted, pos, seg_blk, seg_exp, seg_lo, seg_hi = _route(lg2, T, tm_e)
        xs = hn2[tok]
        ys = _experts(seg_blk, seg_exp, seg_lo, seg_hi, xs, g_sorted.reshape(-1, 1),
                      w_expert_gate[l], w_expert_up[l], w_expert_down[l], tm=tm_e)
        adds = [ys[pos[:, 0]], ys[pos[:, 1]]]
        h = h1

    (out,) = _norm(h, adds, final_norm, emit_h=False, emit_bf16=False)
    return out.reshape(B, S, D)
```

```python
import functools
import math

import jax
import jax.numpy as jnp
from jax import lax
from jax.experimental import pallas as pl
from jax.experimental.pallas import tpu as pltpu

F32 = jnp.float32
BF16 = jnp.bfloat16

D_MODEL = 2048
HEAD_DIM = 64
ROPE_THETA = 10000.0
A_HEADS = 12
A_KV_HEADS = 4
A_GROUP = 3
A_WIDTH = 768
WINDOW = 128
POOL_WINDOWS = (2, 4, 8, 16)
B_WIDTH = 512
B_GROUP_DIM = 128
C_VDIM = 128
C_WIDTH = 768
C_HEADS = 6
DIFF_EPS = 1e-5
IN_COLS = 4096
N_GROUPS = 4
EXPERTS_PER_GROUP = 8
N_EXPERTS = 32
TOP_K = 2
D_FF = 512
NORM_EPS = 1e-6
NEG = -1e30
LOG2E = math.log2(math.e)
Q_SCALE = HEAD_DIM ** -0.5 * LOG2E

COL_QA, COL_QC, COL_KA, COL_KC, COL_VA, COL_UB, COL_VC = 0, 768, 1536, 1792, 2560, 2816, 3328
PROJ_TN = 256
PROJ_SUB = 256
N_SCALED_TILES = COL_KA // PROJ_TN
N_ROPE_TILES = COL_VA // PROJ_TN
N_MAIN_TILES = COL_VC // PROJ_TN
DIFF_KB = 512
VT_ROWS = C_VDIM + 16

VMEM_LIMIT = 48 * 1024 * 1024
D_PACK = D_MODEL // 2
U32 = jnp.uint32


def _cparams(sem, vmem=VMEM_LIMIT):
    return pltpu.CompilerParams(dimension_semantics=sem, vmem_limit_bytes=vmem)


def _pack_rows(x):
    lo = pltpu.bitcast(x[:, :D_PACK].astype(BF16).astype(F32), U32) >> 16
    hi = pltpu.bitcast(x[:, D_PACK:].astype(BF16).astype(F32), U32)
    return hi | lo


def _unpack_rows(w):
    return pltpu.bitcast(w << 16, F32), pltpu.bitcast(w & U32(0xFFFF0000), F32)


def _unpack_rows_bf16(w):
    lo, hi = _unpack_rows(w)
    return jnp.concatenate([lo.astype(BF16), hi.astype(BF16)], axis=1)


def _norm_kernel(*refs, n_add, emit_h, emit_bf16, eps):
    h_ref = refs[0]
    add_refs = refs[1:1 + n_add]
    g_ref = refs[1 + n_add]
    outs = refs[2 + n_add:]
    h = h_ref[...]
    if n_add:
        lo = jnp.zeros((h.shape[0], D_PACK), F32)
        hi = jnp.zeros((h.shape[0], D_PACK), F32)
        for r in add_refs:
            a, b = _unpack_rows(r[...])
            lo, hi = lo + a, hi + b
        h = h + jnp.concatenate([lo, hi], axis=1)
    ms = jnp.mean(h * h, axis=-1, keepdims=True)
    y = h * lax.rsqrt(ms + eps) * g_ref[...]
    k = 0
    if emit_h:
        outs[k][...] = h
        k += 1
    outs[k][...] = y.astype(BF16 if emit_bf16 else F32)


def _norm(h, y_rows, g, *, emit_h, emit_bf16, tm=512):
    T, D = h.shape
    nt = T // tm
    row = pl.BlockSpec((tm, D), lambda i: (i, 0))
    n_add = 0 if y_rows is None else TOP_K
    add_specs = [pl.BlockSpec((tm, D_PACK), lambda i, k=k: (i + k * nt, 0)) for k in range(n_add)]
    out_shape, out_specs = [], []
    if emit_h:
        out_shape.append(jax.ShapeDtypeStruct((T, D), F32))
        out_specs.append(row)
    out_shape.append(jax.ShapeDtypeStruct((T, D), BF16 if emit_bf16 else F32))
    out_specs.append(row)
    return pl.pallas_call(
        functools.partial(_norm_kernel, n_add=n_add, emit_h=emit_h, emit_bf16=emit_bf16, eps=NORM_EPS),
        grid=(nt,),
        in_specs=[row] + add_specs + [pl.BlockSpec((1, D), lambda i: (0, 0))],
        out_specs=out_specs,
        out_shape=out_shape,
        compiler_params=_cparams(("parallel",)),
        name="norm",
    )(h, *([y_rows] * n_add), g.reshape(1, D))


GATHER_ROWS = 1024
GATHER_UNROLL = 8


def _gather_kernel(idx_ref, x_hbm, o_hbm, sem):
    i = pl.program_id(0)
    base = i * GATHER_ROWS

    def body(r, carry):
        for u in range(GATHER_UNROLL):
            rr = r * GATHER_UNROLL + u
            pltpu.make_async_copy(x_hbm.at[pl.ds(idx_ref[rr], 1), :],
                                  o_hbm.at[pl.ds(base + rr, 1), :], sem).start(priority=u % 2)
        return carry

    lax.fori_loop(0, GATHER_ROWS // GATHER_UNROLL, body, 0)

    def wait_one_step():
        pltpu.make_async_copy(x_hbm.at[pl.ds(0, GATHER_ROWS), :], o_hbm.at[pl.ds(0, GATHER_ROWS), :], sem).wait()

    @pl.when(i > 0)
    def _():
        wait_one_step()

    @pl.when(i == pl.num_programs(0) - 1)
    def _():
        wait_one_step()


def _gather_rows(x, idx):
    M = idx.shape[0]
    return pl.pallas_call(
        _gather_kernel,
        grid=(M // GATHER_ROWS,),
        in_specs=[pl.BlockSpec((GATHER_ROWS,), lambda i: (i,), memory_space=pltpu.SMEM),
                  pl.BlockSpec(memory_space=pl.ANY)],
        out_specs=pl.BlockSpec(memory_space=pl.ANY),
        out_shape=jax.ShapeDtypeStruct((M, x.shape[1]), x.dtype),
        scratch_shapes=[pltpu.SemaphoreType.DMA(())],
        compiler_params=pltpu.CompilerParams(dimension_semantics=("arbitrary",)),
        name="gather",
    )(idx, x)


def _proj_kernel(x_ref, w_ref, wt_ref, cos_ref, sin_ref, o_ref, vt_ref, *, tm):
    j = pl.program_id(1)
    subs = [slice(s * PROJ_SUB, (s + 1) * PROJ_SUB) for s in range(tm // PROJ_SUB)]

    def matmuls():
        return [jnp.dot(x_ref[rows, :], w_ref[...], preferred_element_type=F32) for rows in subs]

    @pl.when(j < N_ROPE_TILES)
    def _():
        accs = matmuls()
        scale = jnp.where(j < N_SCALED_TILES, Q_SCALE, 1.0).astype(F32)
        lane = lax.broadcasted_iota(jnp.int32, (PROJ_SUB, PROJ_TN), 1)
        first_half = (lane & (HEAD_DIM // 2)) == 0
        for rows, acc in zip(subs, accs):
            c = jnp.concatenate([cos_ref[rows, :] * scale] * (PROJ_TN // 128), axis=1)
            s = jnp.concatenate([sin_ref[rows, :] * scale] * (PROJ_TN // 128), axis=1)
            swapped = jnp.where(first_half,
                                pltpu.roll(acc, PROJ_TN - HEAD_DIM // 2, 1),
                                pltpu.roll(acc, HEAD_DIM // 2, 1))
            o_ref[rows, :] = (acc * c + swapped * s).astype(o_ref.dtype)

    @pl.when((j >= N_ROPE_TILES) & (j < N_MAIN_TILES))
    def _():
        for rows, acc in zip(subs, matmuls()):
            o_ref[rows, :] = acc.astype(o_ref.dtype)

    @pl.when(j >= N_MAIN_TILES)
    def _():
        accs = [lax.dot_general(wt_ref[...], x_ref[rows, :], (((1,), (1,)), ((), ())),
                                preferred_element_type=F32) for rows in subs]
        row = lax.broadcasted_iota(jnp.int32, (VT_ROWS - C_VDIM, DIFF_KB), 0)
        extra = jnp.where(row == 0, 1.0, 0.0).astype(vt_ref.dtype)
        per_kb = DIFF_KB // PROJ_SUB
        for s, acc in enumerate(accs):
            cols = slice((s % per_kb) * PROJ_SUB, (s % per_kb + 1) * PROJ_SUB)
            for hh in range(PROJ_TN // C_VDIM):
                vt_ref[hh, s // per_kb, 0:C_VDIM, cols] = acc[hh * C_VDIM:(hh + 1) * C_VDIM, :].astype(vt_ref.dtype)
        for hh in range(PROJ_TN // C_VDIM):
            for kb in range(tm // DIFF_KB):
                vt_ref[hh, kb, C_VDIM:VT_ROWS, :] = extra


def _proj(hn, w, wt, cos_t, sin_t, *, B, S, tm=1024):
    T, D = hn.shape
    spt = S // tm
    last_main = N_MAIN_TILES - 1
    return pl.pallas_call(
        functools.partial(_proj_kernel, tm=tm),
        grid=(T // tm, IN_COLS // PROJ_TN),
        in_specs=[
            pl.BlockSpec((tm, D), lambda i, j: (i, 0)),
            pl.BlockSpec((D, PROJ_TN), lambda i, j: (0, jnp.minimum(j, last_main))),
            pl.BlockSpec((PROJ_TN, D), lambda i, j: (jnp.maximum(j - N_MAIN_TILES, 0), 0)),
            pl.BlockSpec((tm, 128), lambda i, j: (i, 0)),
            pl.BlockSpec((tm, 128), lambda i, j: (i, 0)),
        ],
        out_specs=[
            pl.BlockSpec((tm, PROJ_TN), lambda i, j: (i, jnp.minimum(j, last_main))),
            pl.BlockSpec((None, PROJ_TN // C_VDIM, tm // DIFF_KB, VT_ROWS, DIFF_KB),
                         lambda i, j: (i // spt, jnp.maximum(j - N_MAIN_TILES, 0), i % spt, 0, 0)),
        ],
        out_shape=[
            jax.ShapeDtypeStruct((T, COL_VC), BF16),
            jax.ShapeDtypeStruct((B, C_HEADS, S // DIFF_KB, VT_ROWS, DIFF_KB), BF16),
        ],
        compiler_params=_cparams(("parallel", "arbitrary")),
        name="proj",
    )(hn, w, wt, cos_t, sin_t)


def _swa_kernel(sink_ref, q_ref, kc_ref, vc_ref, kp_ref, vp_ref, g_ref, o_ref, *, tq):
    i = pl.program_id(1)
    kext = jnp.concatenate([kp_ref[...], kc_ref[...]], axis=0)
    vext = jnp.concatenate([vp_ref[...], vc_ref[...]], axis=0)
    qi = lax.broadcasted_iota(jnp.int32, (WINDOW, 2 * WINDOW), 0)
    jj = lax.broadcasted_iota(jnp.int32, (WINDOW, 2 * WINDOW), 1)
    band = (jj > qi) & (jj <= qi + WINDOW)
    g = g_ref[...]
    for sb in range(tq // WINDOW):
        mask = band
        if sb == 0:
            mask = band & ((jj >= WINDOW) | (i > 0))
        outs = []
        for h in range(A_HEADS):
            kv = h // A_GROUP
            qh = q_ref[sb * WINDOW:(sb + 1) * WINDOW, h * HEAD_DIM:(h + 1) * HEAD_DIM]
            kb = kext[sb * WINDOW:(sb + 2) * WINDOW, kv * HEAD_DIM:(kv + 1) * HEAD_DIM]
            vb = vext[sb * WINDOW:(sb + 2) * WINDOW, kv * HEAD_DIM:(kv + 1) * HEAD_DIM]
            s = lax.dot_general(qh, kb, (((1,), (1,)), ((), ())), preferred_element_type=F32)
            s = jnp.where(mask, s, NEG)
            sink = sink_ref[h] * LOG2E
            m = jnp.maximum(jnp.max(s, axis=1, keepdims=True), sink)
            p = jnp.exp2(s - m)
            l = jnp.sum(p, axis=1, keepdims=True) + jnp.exp2(sink - m)
            o = jnp.dot(p.astype(BF16), vb, preferred_element_type=F32)
            outs.append(o / l)
        o_all = jnp.concatenate(outs, axis=1)
        ms = jnp.mean(o_all * o_all, axis=1, keepdims=True)
        y = o_all * lax.rsqrt(ms + NORM_EPS) * g
        o_ref[sb * WINDOW:(sb + 1) * WINDOW, :] = y.astype(o_ref.dtype)


def _swa(proj, sinks, g, *, B, S, tq=256):
    nq = S // tq
    rpb = tq // WINDOW

    def prev_idx(col):
        def f(b, i):
            return (jnp.maximum(b * (S // WINDOW) + i * rpb - 1, 0), col)
        return f

    return pl.pallas_call(
        functools.partial(_swa_kernel, tq=tq),
        grid=(B, nq),
        in_specs=[
            pl.BlockSpec(memory_space=pltpu.SMEM),
            pl.BlockSpec((tq, A_WIDTH), lambda b, i: (b * nq + i, COL_QA // A_WIDTH)),
            pl.BlockSpec((tq, 256), lambda b, i: (b * nq + i, COL_KA // 256)),
            pl.BlockSpec((tq, 256), lambda b, i: (b * nq + i, COL_VA // 256)),
            pl.BlockSpec((WINDOW, 256), prev_idx(COL_KA // 256)),
            pl.BlockSpec((WINDOW, 256), prev_idx(COL_VA // 256)),
            pl.BlockSpec((1, A_WIDTH), lambda b, i: (0, 0)),
        ],
        out_specs=pl.BlockSpec((tq, A_WIDTH), lambda b, i: (b * nq + i, 0)),
        out_shape=jax.ShapeDtypeStruct((B * S, A_WIDTH), BF16),
        compiler_params=_cparams(("parallel", "parallel")),
        name="swa",
    )(sinks, proj, proj, proj, proj, proj, g.reshape(1, A_WIDTH))


POOL_HALO = 16


def _pool_kernel(u0_ref, u1_ref, h0_ref, h1_ref, w_ref, s_ref, o_ref, *, tq):
    i = pl.program_id(1)
    u = jnp.concatenate([u0_ref[...], u1_ref[...]], axis=1).astype(F32)
    halo = jnp.concatenate([h0_ref[...], h1_ref[...]], axis=1).astype(F32)
    halo = halo * (i > 0).astype(F32)
    ext = jnp.concatenate([halo, u], axis=0)
    t = i * tq + lax.broadcasted_iota(jnp.int32, (tq, 1), 0)
    for gi, w in enumerate(POOL_WINDOWS):
        sl = slice(gi * B_GROUP_DIM, (gi + 1) * B_GROUP_DIM)
        s = ext[:, sl]
        shift = 1
        while shift < w:
            s = s + pltpu.roll(s, shift, 0)
            shift *= 2
        cnt = jnp.minimum(t + 1, w).astype(F32)
        y = (s[POOL_HALO:, :] / cnt - u[:, sl]).astype(BF16)
        o = jnp.dot(y, w_ref[gi], preferred_element_type=F32) * s_ref[:, sl]
        o_ref[:, sl] = o.astype(o_ref.dtype)


def _pool(proj, w_pool, scale, *, B, S, tq=256):
    nq = S // tq
    c0 = COL_UB // 256

    def cur(c):
        return pl.BlockSpec((tq, 256), lambda b, i: (b * nq + i, c))

    def halo(c):
        return pl.BlockSpec(
            (POOL_HALO, 256),
            lambda b, i: (jnp.maximum((b * S + i * tq) // POOL_HALO - 1, 0), c))

    return pl.pallas_call(
        functools.partial(_pool_kernel, tq=tq),
        grid=(B, nq),
        in_specs=[cur(c0), cur(c0 + 1), halo(c0), halo(c0 + 1),
                  pl.BlockSpec((len(POOL_WINDOWS), B_GROUP_DIM, B_GROUP_DIM), lambda b, i: (0, 0, 0)),
                  pl.BlockSpec((1, B_WIDTH), lambda b, i: (0, 0))],
        out_specs=pl.BlockSpec((tq, B_WIDTH), lambda b, i: (b * nq + i, 0)),
        out_shape=jax.ShapeDtypeStruct((B * S, B_WIDTH), BF16),
        compiler_params=_cparams(("parallel", "parallel")),
        name="pool",
    )(proj, proj, proj, proj, w_pool, scale.reshape(1, B_WIDTH))


def _diff_kernel(q_ref, k_ref, vt_ref, lam_ref, g_ref, o_ref, qcat_ref, s0_ref, s1_ref, m_ref, acc_ref,
                 *, tq, lam_init):
    qi = pl.program_id(2)
    q = q_ref[...]
    lane = lax.broadcasted_iota(jnp.int32, q.shape, 1)
    zero = jnp.zeros_like(q)
    qcat_ref[0:tq, :] = jnp.where(lane < HEAD_DIM, q, zero)
    qcat_ref[tq:2 * tq, :] = jnp.where(lane >= HEAD_DIM, q, zero)

    m_ref[...] = jnp.full(m_ref.shape, NEG, F32)
    acc_ref[...] = jnp.zeros(acc_ref.shape, F32)

    def scores(kidx):
        off = pl.multiple_of(kidx * DIFF_KB, DIFF_KB)
        return lax.dot_general(k_ref[pl.ds(off, DIFF_KB), :], qcat_ref[...], (((1,), (1,)), ((), ())),
                               preferred_element_type=F32)

    def consume(s_ref, kidx, diagonal):
        st = s_ref[...]
        if diagonal:
            r = lax.broadcasted_iota(jnp.int32, st.shape, 0)
            c = lax.broadcasted_iota(jnp.int32, st.shape, 1)
            c = jnp.where(c >= tq, c - tq, c)
            st = jnp.where(r <= c, st, NEG)
        m = m_ref[...]
        m_new = jnp.maximum(m, jnp.max(st, axis=0, keepdims=True))
        alpha = jnp.exp2(m - m_new)
        p = jnp.exp2(st - m_new)
        m_ref[...] = m_new
        pv = jnp.dot(vt_ref[kidx], p.astype(BF16), preferred_element_type=F32)
        acc_ref[...] = alpha * acc_ref[...] + pv

    def step(cur_ref, nxt_ref, kidx):
        s_next = scores(kidx + 1)
        consume(cur_ref, kidx, False)
        nxt_ref[...] = s_next

    s0_ref[...] = scores(0)

    def body(c, carry):
        step(s0_ref, s1_ref, 2 * c)
        step(s1_ref, s0_ref, 2 * c + 1)
        return carry

    lax.fori_loop(0, qi // 2, body, 0)

    @pl.when(qi % 2 == 1)
    def _():
        step(s0_ref, s1_ref, qi - 1)
        consume(s1_ref, qi, True)

    @pl.when(qi % 2 == 0)
    def _():
        consume(s0_ref, qi, True)

    lv = lam_ref[...]
    lam = (jnp.exp(jnp.sum(lv[0:1] * lv[1:2], axis=1, keepdims=True))
           - jnp.exp(jnp.sum(lv[2:3] * lv[3:4], axis=1, keepdims=True)) + lam_init)
    l = acc_ref[C_VDIM:C_VDIM + 1, :]
    acc = acc_ref[0:C_VDIM, :]
    ot = acc[:, :tq] / l[:, :tq] - lam * (acc[:, tq:] / l[:, tq:])
    ms = jnp.mean(ot * ot, axis=0, keepdims=True)
    yt = ot * lax.rsqrt(ms + DIFF_EPS) * (g_ref[...] * (1.0 - lam_init))
    o_ref[...] = yt.T.astype(o_ref.dtype)


def _diff(proj, vt, lam_vecs, subln, *, B, S, lam_init, tq=512):
    nq = S // tq
    nkb = S // DIFF_KB
    assert tq == DIFF_KB
    return pl.pallas_call(
        functools.partial(_diff_kernel, tq=tq, lam_init=lam_init),
        grid=(B, C_HEADS, nq),
        in_specs=[
            pl.BlockSpec((tq, C_VDIM), lambda b, h, i: (b * nq + i, COL_QC // C_VDIM + h)),
            pl.BlockSpec((S, C_VDIM), lambda b, h, i: (b, COL_KC // C_VDIM + h)),
            pl.BlockSpec((None, None, nkb, VT_ROWS, DIFF_KB), lambda b, h, i: (b, h, 0, 0, 0)),
            pl.BlockSpec((4, HEAD_DIM), lambda b, h, i: (0, 0)),
            pl.BlockSpec((C_VDIM, 1), lambda b, h, i: (0, 0)),
        ],
        out_specs=pl.BlockSpec((tq, C_VDIM), lambda b, h, i: (b * nq + i, h)),
        out_shape=jax.ShapeDtypeStruct((B * S, C_WIDTH), BF16),
        scratch_shapes=[pltpu.VMEM((2 * tq, C_VDIM), BF16),
                        pltpu.VMEM((DIFF_KB, 2 * tq), F32), pltpu.VMEM((DIFF_KB, 2 * tq), F32),
                        pltpu.VMEM((1, 2 * tq), F32), pltpu.VMEM((VT_ROWS, 2 * tq), F32)],
        compiler_params=_cparams(("parallel", "parallel", "arbitrary")),
        name="diff",
    )(proj, proj, vt, lam_vecs, subln.reshape(C_VDIM, 1))


OUT_SUB = 256


def _out_kernel(oa_ref, ob_ref, oc_ref, w_ref, h_ref, g_ref, h1_ref, hn_ref, *, tm):
    subs = [slice(s * OUT_SUB, (s + 1) * OUT_SUB) for s in range(tm // OUT_SUB)]
    accs = []
    for rows in subs:
        acc = jnp.dot(oa_ref[rows, :], w_ref[0:A_WIDTH, :], preferred_element_type=F32)
        acc = acc + jnp.dot(ob_ref[rows, :], w_ref[A_WIDTH:A_WIDTH + B_WIDTH, :], preferred_element_type=F32)
        acc = acc + jnp.dot(oc_ref[rows, :], w_ref[A_WIDTH + B_WIDTH:, :], preferred_element_type=F32)
        accs.append(acc)
    for rows, acc in zip(subs, accs):
        h1 = h_ref[rows, :] + acc
        h1_ref[rows, :] = h1
        ms = jnp.mean(h1 * h1, axis=1, keepdims=True)
        hn = h1 * lax.rsqrt(ms + NORM_EPS) * g_ref[...]
        hn_ref[rows, :] = _pack_rows(hn)


def _out(oa, ob, oc, w, h, g, *, tm=512):
    T, D = h.shape
    return pl.pallas_call(
        functools.partial(_out_kernel, tm=tm),
        grid=(T // tm,),
        in_specs=[
            pl.BlockSpec((tm, A_WIDTH), lambda i: (i, 0)),
            pl.BlockSpec((tm, B_WIDTH), lambda i: (i, 0)),
            pl.BlockSpec((tm, C_WIDTH), lambda i: (i, 0)),
            pl.BlockSpec((D, D), lambda i: (0, 0)),
            pl.BlockSpec((tm, D), lambda i: (i, 0)),
            pl.BlockSpec((1, D), lambda i: (0, 0)),
        ],
        out_specs=[
            pl.BlockSpec((tm, D), lambda i: (i, 0)),
            pl.BlockSpec((tm, D_PACK), lambda i: (i, 0)),
        ],
        out_shape=[
            jax.ShapeDtypeStruct((T, D), F32),
            jax.ShapeDtypeStruct((T, D_PACK), U32),
        ],
        compiler_params=_cparams(("parallel",)),
        name="out",
    )(oa, ob, oc, w, h, g.reshape(1, D))


ROUTER_COLS = 128
ROUTER_LO = 64


def _router_kernel(x_ref, w_ref, b_ref, o_ref):
    o_ref[...] = jnp.dot(_unpack_rows_bf16(x_ref[...]), w_ref[...], preferred_element_type=F32) + b_ref[...]


def _router(hn_packed, w2, b2, *, tm=1024):
    T = hn_packed.shape[0]
    D = w2.shape[0]
    return pl.pallas_call(
        _router_kernel,
        grid=(T // tm,),
        in_specs=[
            pl.BlockSpec((tm, D_PACK), lambda i: (i, 0)),
            pl.BlockSpec((D, ROUTER_COLS), lambda i: (0, 0)),
            pl.BlockSpec((1, ROUTER_COLS), lambda i: (0, 0)),
        ],
        out_specs=pl.BlockSpec((tm, ROUTER_COLS), lambda i: (i, 0)),
        out_shape=jax.ShapeDtypeStruct((T, ROUTER_COLS), F32),
        compiler_params=_cparams(("parallel",)),
        name="router",
    )(hn_packed, w2, b2)


EXPERT_SUB = 256
EXPERT_VMEM = 56 * 1024 * 1024


def _expert_kernel(blk_ref, exp_ref, lo_ref, hi_ref, x_ref, g_ref, wg_ref, wu_ref, wd_ref, o_ref,
                   wg_bf, wu_bf, wd_bf, *, tm):
    p = pl.program_id(0)
    prev = jnp.maximum(p - 1, 0)
    blk = blk_ref[p]
    first = (p == 0) | (blk != blk_ref[prev])
    new_expert = (p == 0) | (exp_ref[p] != exp_ref[prev])
    lo = lo_ref[p]
    hi = hi_ref[p]

    @pl.when(new_expert)
    def _():
        wg_bf[...] = wg_ref[...].astype(BF16)
        wu_bf[...] = wu_ref[...].astype(BF16)
        wd_bf[...] = wd_ref[...].astype(BF16)

    @pl.when(first)
    def _():
        o_ref[...] = jnp.zeros_like(o_ref)

    for s in range(tm // EXPERT_SUB):
        row0 = blk * tm + s * EXPERT_SUB
        rows = slice(s * EXPERT_SUB, (s + 1) * EXPERT_SUB)

        @pl.when((hi > row0) & (lo < row0 + EXPERT_SUB))
        def _(row0=row0, rows=rows):
            x = _unpack_rows_bf16(x_ref[rows, :])
            a = jnp.dot(x, wg_bf[...], preferred_element_type=F32)
            u = jnp.dot(x, wu_bf[...], preferred_element_type=F32)
            act = (a / (1.0 + jnp.exp(-a))) * u
            y = jnp.dot(act.astype(BF16), wd_bf[...], preferred_element_type=F32)
            r = row0 + lax.broadcasted_iota(jnp.int32, (EXPERT_SUB, 1), 0)
            mine = (r >= lo) & (r < hi)
            o_ref[rows, :] = jnp.where(mine, _pack_rows(y * g_ref[rows, :]), o_ref[rows, :])


def _experts(seg_blk, seg_exp, seg_lo, seg_hi, xs, gates, wg, wu, wd, layer, *, tm):
    N = xs.shape[0]
    D = wg.shape[2]
    P = seg_blk.shape[0]
    grid_spec = pltpu.PrefetchScalarGridSpec(
        num_scalar_prefetch=4,
        grid=(P,),
        in_specs=[
            pl.BlockSpec((tm, D_PACK), lambda p, blk, ex, lo, hi: (blk[p], 0)),
            pl.BlockSpec((tm, 1), lambda p, blk, ex, lo, hi: (blk[p], 0)),
            pl.BlockSpec((None, None, D, D_FF), lambda p, blk, ex, lo, hi: (layer, ex[p], 0, 0)),
            pl.BlockSpec((None, None, D, D_FF), lambda p, blk, ex, lo, hi: (layer, ex[p], 0, 0)),
            pl.BlockSpec((None, None, D_FF, D), lambda p, blk, ex, lo, hi: (layer, ex[p], 0, 0)),
        ],
        out_specs=pl.BlockSpec((tm, D_PACK), lambda p, blk, ex, lo, hi: (blk[p], 0)),
        scratch_shapes=[pltpu.VMEM((D, D_FF), BF16), pltpu.VMEM((D, D_FF), BF16), pltpu.VMEM((D_FF, D), BF16)],
    )
    return pl.pallas_call(
        functools.partial(_expert_kernel, tm=tm),
        grid_spec=grid_spec,
        out_shape=jax.ShapeDtypeStruct((N, D_PACK), U32),
        compiler_params=_cparams(("arbitrary",), EXPERT_VMEM),
        name="experts",
    )(seg_blk, seg_exp, seg_lo, seg_hi, xs, gates, wg, wu, wd)


def _route(lg2, T, tm):
    n_lg = N_GROUPS + N_EXPERTS
    lg = lg2[:, :n_lg] + lg2[:, ROUTER_LO:ROUTER_LO + n_lg]
    pg = jax.nn.softmax(lg[:, :N_GROUPS], axis=-1)
    gsel = jnp.argmax(pg, axis=-1)
    pgsel = jnp.take_along_axis(pg, gsel[:, None], axis=1)
    le = lg[:, N_GROUPS:].reshape(T, N_GROUPS, EXPERTS_PER_GROUP)
    le = jnp.take_along_axis(le, gsel[:, None, None], axis=1)[:, 0]
    pe = jax.nn.softmax(le, axis=-1)
    top_p, top_i = lax.top_k(pe, TOP_K)
    gates = pgsel * top_p / jnp.sum(top_p, axis=-1, keepdims=True)
    eidx = (gsel[:, None] * EXPERTS_PER_GROUP + top_i).astype(jnp.int32)

    N = T * TOP_K
    flat_e = eidx.reshape(N)
    iota = jnp.arange(N, dtype=jnp.int32)
    sorted_e, order, g_sorted = lax.sort((flat_e, iota, gates.reshape(N)), num_keys=1, is_stable=True)
    tok = order // TOP_K
    _, pos = lax.sort((order, iota), num_keys=1)
    counts = jnp.bincount(flat_e, length=N_EXPERTS)
    starts = (jnp.cumsum(counts) - counts).astype(jnp.int32)
    nb = N // tm
    bounds = jnp.sort(jnp.concatenate([jnp.arange(nb, dtype=jnp.int32) * tm, starts]))
    ends = jnp.concatenate([bounds[1:], jnp.full((1,), N, bounds.dtype)])
    seg_blk = jnp.minimum(bounds // tm, nb - 1).astype(jnp.int32)
    seg_exp = sorted_e[jnp.minimum(bounds, N - 1)].astype(jnp.int32)
    pos_rows = pos.reshape(T, TOP_K).T.reshape(N)
    return tok, g_sorted, pos_rows, seg_blk, seg_exp, bounds, ends


def _in_proj_perm():
    offs = {}
    idx = 0
    for name, w in zip(("qa", "ka", "va", "ub", "qc", "kc", "vc"), (768, 256, 256, 512, 768, 768, 768)):
        offs[name] = (idx, w)
        idx += w
    cols = []
    for name in ("qa", "qc", "ka", "kc", "va", "ub", "vc"):
        o, w = offs[name]
        cols.extend(range(o, o + w))
    return jnp.asarray(cols, jnp.int32)


def kernel(x, positions, attn_norm, w_in, sinks, branch_norm_a, w_pool, pool_scale, lambda_q1, lambda_k1,
           lambda_q2, lambda_k2, subln, w_out, ffn_norm, w_router_group, b_router_group, w_router_expert,
           b_router_expert, w_expert_gate, w_expert_up, w_expert_down, final_norm):
    B, S, D = x.shape
    T = B * S
    depth = w_in.shape[0]
    tm_e = 512

    half = HEAD_DIM // 2
    inv = ROPE_THETA ** (-jnp.arange(half, dtype=F32) / half)
    ang = positions.astype(F32).reshape(T, 1) * inv
    cos, sin = jnp.cos(ang), jnp.sin(ang)
    cos_t = jnp.concatenate([cos, cos, cos, cos], axis=1)
    sin_t = jnp.concatenate([-sin, sin, -sin, sin], axis=1)
    perm = _in_proj_perm()

    h = x.reshape(T, D)
    y_rows = None
    for l in range(depth):
        if l == 0:
            (hn,) = _norm(h, None, attn_norm[l], emit_h=False, emit_bf16=True)
        else:
            h, hn = _norm(h, y_rows, attn_norm[l], emit_h=True, emit_bf16=True)
        w_in_l = w_in[l][:, perm].astype(BF16)
        proj, vt = _proj(hn, w_in_l[:, :COL_VC], w_in_l[:, COL_VC:].T, cos_t, sin_t, B=B, S=S)

        oa = _swa(proj, sinks[l], branch_norm_a[l], B=B, S=S)
        ob = _pool(proj, w_pool[l].astype(BF16), pool_scale[l], B=B, S=S)

        lam_init = 0.8 - 0.6 * math.exp(-0.3 * l)
        lam_vecs = jnp.stack([lambda_q1[l], lambda_k1[l], lambda_q2[l], lambda_k2[l]]).astype(F32)
        oc = _diff(proj, vt, lam_vecs, subln[l], B=B, S=S, lam_init=lam_init)

        h1, hn2 = _out(oa, ob, oc, w_out[l].astype(BF16), h, ffn_norm[l])

        n_lg = N_GROUPS + N_EXPERTS
        wr = jnp.concatenate([w_router_group[l], w_router_expert[l]], axis=1)
        wr_hi = wr.astype(BF16)
        wr_lo = (wr - wr_hi.astype(F32)).astype(BF16)
        w2 = (jnp.zeros((D, ROUTER_COLS), BF16).at[:, :n_lg].set(wr_hi)
              .at[:, ROUTER_LO:ROUTER_LO + n_lg].set(wr_lo))
        b2 = jnp.zeros((1, ROUTER_COLS), F32).at[0, :n_lg].set(
            jnp.concatenate([b_router_group[l], b_router_expert[l]]))
        lg2 = _router(hn2, w2, b2)

        tok, g_sorted, pos_rows, seg_blk, seg_exp, seg_lo, seg_hi = _route(lg2, T, tm_e)
        xs = _gather_rows(hn2, tok)
        ys = _experts(seg_blk, seg_exp, seg_lo, seg_hi, xs, g_sorted.reshape(-1, 1),
                      w_expert_gate, w_expert_up, w_expert_down, l, tm=tm_e)
        y_rows = _gather_rows(ys, pos_rows)
        h = h1

    (out,) = _norm(h, y_rows, final_norm, emit_h=False, emit_bf16=False)
    return out.reshape(B, S, D)
```

```python
import functools
import math

import jax
import jax.numpy as jnp
from jax import lax
from jax.experimental import pallas as pl
from jax.experimental.pallas import tpu as pltpu
from jax.experimental.pallas import tpu_sc as plsc

F32 = jnp.float32
BF16 = jnp.bfloat16

D_MODEL = 2048
HEAD_DIM = 64
ROPE_THETA = 10000.0
A_HEADS = 12
A_KV_HEADS = 4
A_GROUP = 3
A_WIDTH = 768
WINDOW = 128
POOL_WINDOWS = (2, 4, 8, 16)
B_WIDTH = 512
B_GROUP_DIM = 128
C_VDIM = 128
C_WIDTH = 768
C_HEADS = 6
DIFF_EPS = 1e-5
IN_COLS = 4096
N_GROUPS = 4
EXPERTS_PER_GROUP = 8
N_EXPERTS = 32
TOP_K = 2
D_FF = 512
NORM_EPS = 1e-6
NEG = -1e30
LOG2E = math.log2(math.e)
Q_SCALE = HEAD_DIM ** -0.5 * LOG2E

COL_QA, COL_QC, COL_KA, COL_KC, COL_VA, COL_UB, COL_VC = 0, 768, 1536, 1792, 2560, 2816, 3328
PROJ_TN = 256
PROJ_SUB = 256
N_SCALED_TILES = COL_KA // PROJ_TN
N_ROPE_TILES = COL_VA // PROJ_TN
N_MAIN_TILES = COL_VC // PROJ_TN
DIFF_KB = 512
VT_ROWS = C_VDIM + 16

VMEM_LIMIT = 48 * 1024 * 1024
D_PACK = D_MODEL // 2
U32 = jnp.uint32


def _cparams(sem, vmem=VMEM_LIMIT):
    return pltpu.CompilerParams(dimension_semantics=sem, vmem_limit_bytes=vmem)


def _pack_rows(x):
    lo = pltpu.bitcast(x[:, :D_PACK].astype(BF16).astype(F32), U32) >> 16
    hi = pltpu.bitcast(x[:, D_PACK:].astype(BF16).astype(F32), U32)
    return hi | lo


def _unpack_rows(w):
    return pltpu.bitcast(w << 16, F32), pltpu.bitcast(w & U32(0xFFFF0000), F32)


def _unpack_rows_bf16(w):
    lo, hi = _unpack_rows(w)
    return jnp.concatenate([lo.astype(BF16), hi.astype(BF16)], axis=1)


SC_WINDOW = 32
SC_INDEX_LANES = 128


def _sc_gather_rows(x, idx):
    M = idx.shape[0]
    W = x.shape[1]
    idx2 = jnp.pad(idx.reshape(M // SC_WINDOW, SC_WINDOW), ((0, 0), (0, SC_INDEX_LANES - SC_WINDOW)))
    mesh = plsc.VectorSubcoreMesh(core_axis_name="core", subcore_axis_name="subcore")

    @pl.kernel(out_type=jax.ShapeDtypeStruct((M, W), x.dtype), mesh=mesh)
    def gather(x_hbm, i_hbm, o_hbm):
        def body(i_vmem, o_vmem):
            pltpu.sync_copy(x_hbm.at[i_vmem.at[0, pl.ds(0, SC_WINDOW)]], o_vmem)

        pltpu.emit_pipeline(
            body,
            grid=(M // SC_WINDOW,),
            in_specs=[pl.BlockSpec((1, SC_INDEX_LANES), index_map=lambda i: (i, 0))],
            out_specs=[pl.BlockSpec((SC_WINDOW, W), index_map=lambda i: (i, 0))],
            core_axis_name=("core", "subcore"),
            dimension_semantics=(pltpu.PARALLEL,),
        )(i_hbm, o_hbm)

    return gather(x, idx2)


def _norm_kernel(*refs, n_add, emit_h, emit_bf16, eps):
    h_ref = refs[0]
    add_refs = refs[1:1 + n_add]
    g_ref = refs[1 + n_add]
    outs = refs[2 + n_add:]
    h = h_ref[...]
    if n_add:
        lo = jnp.zeros((h.shape[0], D_PACK), F32)
        hi = jnp.zeros((h.shape[0], D_PACK), F32)
        for r in add_refs:
            a, b = _unpack_rows(r[...])
            lo, hi = lo + a, hi + b
        h = h + jnp.concatenate([lo, hi], axis=1)
    ms = jnp.mean(h * h, axis=-1, keepdims=True)
    y = h * lax.rsqrt(ms + eps) * g_ref[...]
    k = 0
    if emit_h:
        outs[k][...] = h
        k += 1
    outs[k][...] = y.astype(BF16 if emit_bf16 else F32)


def _norm(h, y_rows, g, *, emit_h, emit_bf16, tm=512):
    T, D = h.shape
    nt = T // tm
    row = pl.BlockSpec((tm, D), lambda i: (i, 0))
    n_add = 0 if y_rows is None else TOP_K
    add_specs = [pl.BlockSpec((tm, D_PACK), lambda i, k=k: (i + k * nt, 0)) for k in range(n_add)]
    out_shape, out_specs = [], []
    if emit_h:
        out_shape.append(jax.ShapeDtypeStruct((T, D), F32))
        out_specs.append(row)
    out_shape.append(jax.ShapeDtypeStruct((T, D), BF16 if emit_bf16 else F32))
    out_specs.append(row)
    return pl.pallas_call(
        functools.partial(_norm_kernel, n_add=n_add, emit_h=emit_h, emit_bf16=emit_bf16, eps=NORM_EPS),
        grid=(nt,),
        in_specs=[row] + add_specs + [pl.BlockSpec((1, D), lambda i: (0, 0))],
        out_specs=out_specs,
        out_shape=out_shape,
        compiler_params=_cparams(("parallel",)),
        name="norm",
    )(h, *([y_rows] * n_add), g.reshape(1, D))


GATHER_ROWS = 1024


def _gather_kernel(idx_ref, x_hbm, o_ref, sem):
    def body(r, carry):
        pltpu.make_async_copy(x_hbm.at[pl.ds(idx_ref[r], 1), :], o_ref.at[pl.ds(r, 1), :], sem).start()
        return carry

    lax.fori_loop(0, GATHER_ROWS, body, 0, unroll=8)
    pltpu.make_async_copy(x_hbm.at[pl.ds(0, GATHER_ROWS), :], o_ref, sem).wait()


def _gather_rows(x, idx):
    M = idx.shape[0]
    W = x.shape[1]
    return pl.pallas_call(
        _gather_kernel,
        grid=(M // GATHER_ROWS,),
        in_specs=[pl.BlockSpec((GATHER_ROWS,), lambda i: (i,), memory_space=pltpu.SMEM),
                  pl.BlockSpec(memory_space=pl.ANY)],
        out_specs=pl.BlockSpec((GATHER_ROWS, W), lambda i: (i, 0)),
        out_shape=jax.ShapeDtypeStruct((M, W), x.dtype),
        scratch_shapes=[pltpu.SemaphoreType.DMA(())],
        compiler_params=_cparams(("arbitrary",)),
        name="gather",
    )(idx, x)


def _proj_kernel(x_ref, w_ref, wt_ref, cos_ref, sin_ref, o_ref, vt_ref, *, tm):
    j = pl.program_id(1)
    subs = [slice(s * PROJ_SUB, (s + 1) * PROJ_SUB) for s in range(tm // PROJ_SUB)]

    def matmuls():
        return [jnp.dot(x_ref[rows, :], w_ref[...], preferred_element_type=F32) for rows in subs]

    @pl.when(j < N_ROPE_TILES)
    def _():
        accs = matmuls()
        scale = jnp.where(j < N_SCALED_TILES, Q_SCALE, 1.0).astype(F32)
        lane = lax.broadcasted_iota(jnp.int32, (PROJ_SUB, PROJ_TN), 1)
        first_half = (lane & (HEAD_DIM // 2)) == 0
        for rows, acc in zip(subs, accs):
            c = jnp.concatenate([cos_ref[rows, :] * scale] * (PROJ_TN // 128), axis=1)
            s = jnp.concatenate([sin_ref[rows, :] * scale] * (PROJ_TN // 128), axis=1)
            swapped = jnp.where(first_half,
                                pltpu.roll(acc, PROJ_TN - HEAD_DIM // 2, 1),
                                pltpu.roll(acc, HEAD_DIM // 2, 1))
            o_ref[rows, :] = (acc * c + swapped * s).astype(o_ref.dtype)

    @pl.when((j >= N_ROPE_TILES) & (j < N_MAIN_TILES))
    def _():
        for rows, acc in zip(subs, matmuls()):
            o_ref[rows, :] = acc.astype(o_ref.dtype)

    @pl.when(j >= N_MAIN_TILES)
    def _():
        accs = [lax.dot_general(wt_ref[...], x_ref[rows, :], (((1,), (1,)), ((), ())),
                                preferred_element_type=F32) for rows in subs]
        row = lax.broadcasted_iota(jnp.int32, (VT_ROWS - C_VDIM, DIFF_KB), 0)
        extra = jnp.where(row == 0, 1.0, 0.0).astype(vt_ref.dtype)
        per_kb = DIFF_KB // PROJ_SUB
        for s, acc in enumerate(accs):
            cols = slice((s % per_kb) * PROJ_SUB, (s % per_kb + 1) * PROJ_SUB)
            for hh in range(PROJ_TN // C_VDIM):
                vt_ref[hh, s // per_kb, 0:C_VDIM, cols] = acc[hh * C_VDIM:(hh + 1) * C_VDIM, :].astype(vt_ref.dtype)
        for hh in range(PROJ_TN // C_VDIM):
            for kb in range(tm // DIFF_KB):
                vt_ref[hh, kb, C_VDIM:VT_ROWS, :] = extra


def _proj(hn, w, wt, cos_t, sin_t, *, B, S, tm=1024):
    T, D = hn.shape
    spt = S // tm
    last_main = N_MAIN_TILES - 1
    return pl.pallas_call(
        functools.partial(_proj_kernel, tm=tm),
        grid=(T // tm, IN_COLS // PROJ_TN),
        in_specs=[
            pl.BlockSpec((tm, D), lambda i, j: (i, 0)),
            pl.BlockSpec((D, PROJ_TN), lambda i, j: (0, jnp.minimum(j, last_main))),
            pl.BlockSpec((PROJ_TN, D), lambda i, j: (jnp.maximum(j - N_MAIN_TILES, 0), 0)),
            pl.BlockSpec((tm, 128), lambda i, j: (i, 0)),
            pl.BlockSpec((tm, 128), lambda i, j: (i, 0)),
        ],
        out_specs=[
            pl.BlockSpec((tm, PROJ_TN), lambda i, j: (i, jnp.minimum(j, last_main))),
            pl.BlockSpec((None, PROJ_TN // C_VDIM, tm // DIFF_KB, VT_ROWS, DIFF_KB),
                         lambda i, j: (i // spt, jnp.maximum(j - N_MAIN_TILES, 0), i % spt, 0, 0)),
        ],
        out_shape=[
            jax.ShapeDtypeStruct((T, COL_VC), BF16),
            jax.ShapeDtypeStruct((B, C_HEADS, S // DIFF_KB, VT_ROWS, DIFF_KB), BF16),
        ],
        compiler_params=_cparams(("parallel", "arbitrary")),
        name="proj",
    )(hn, w, wt, cos_t, sin_t)


def _swa_kernel(sink_ref, q_ref, kc_ref, vc_ref, kp_ref, vp_ref, g_ref, o_ref, *, tq):
    i = pl.program_id(1)
    kext = jnp.concatenate([kp_ref[...], kc_ref[...]], axis=0)
    vext = jnp.concatenate([vp_ref[...], vc_ref[...]], axis=0)
    qi = lax.broadcasted_iota(jnp.int32, (WINDOW, 2 * WINDOW), 0)
    jj = lax.broadcasted_iota(jnp.int32, (WINDOW, 2 * WINDOW), 1)
    band = (jj > qi) & (jj <= qi + WINDOW)
    g = g_ref[...]
    for sb in range(tq // WINDOW):
        mask = band
        if sb == 0:
            mask = band & ((jj >= WINDOW) | (i > 0))
        outs = []
        for h in range(A_HEADS):
            kv = h // A_GROUP
            qh = q_ref[sb * WINDOW:(sb + 1) * WINDOW, h * HEAD_DIM:(h + 1) * HEAD_DIM]
            kb = kext[sb * WINDOW:(sb + 2) * WINDOW, kv * HEAD_DIM:(kv + 1) * HEAD_DIM]
            vb = vext[sb * WINDOW:(sb + 2) * WINDOW, kv * HEAD_DIM:(kv + 1) * HEAD_DIM]
            s = lax.dot_general(qh, kb, (((1,), (1,)), ((), ())), preferred_element_type=F32)
            s = jnp.where(mask, s, NEG)
            sink = sink_ref[h] * LOG2E
            m = jnp.maximum(jnp.max(s, axis=1, keepdims=True), sink)
            p = jnp.exp2(s - m)
            l = jnp.sum(p, axis=1, keepdims=True) + jnp.exp2(sink - m)
            o = jnp.dot(p.astype(BF16), vb, preferred_element_type=F32)
            outs.append(o / l)
        o_all = jnp.concatenate(outs, axis=1)
        ms = jnp.mean(o_all * o_all, axis=1, keepdims=True)
        y = o_all * lax.rsqrt(ms + NORM_EPS) * g
        o_ref[sb * WINDOW:(sb + 1) * WINDOW, :] = y.astype(o_ref.dtype)


def _swa(proj, sinks, g, *, B, S, tq=256):
    nq = S // tq
    rpb = tq // WINDOW

    def prev_idx(col):
        def f(b, i):
            return (jnp.maximum(b * (S // WINDOW) + i * rpb - 1, 0), col)
        return f

    return pl.pallas_call(
        functools.partial(_swa_kernel, tq=tq),
        grid=(B, nq),
        in_specs=[
            pl.BlockSpec(memory_space=pltpu.SMEM),
            pl.BlockSpec((tq, A_WIDTH), lambda b, i: (b * nq + i, COL_QA // A_WIDTH)),
            pl.BlockSpec((tq, 256), lambda b, i: (b * nq + i, COL_KA // 256)),
            pl.BlockSpec((tq, 256), lambda b, i: (b * nq + i, COL_VA // 256)),
            pl.BlockSpec((WINDOW, 256), prev_idx(COL_KA // 256)),
            pl.BlockSpec((WINDOW, 256), prev_idx(COL_VA // 256)),
            pl.BlockSpec((1, A_WIDTH), lambda b, i: (0, 0)),
        ],
        out_specs=pl.BlockSpec((tq, A_WIDTH), lambda b, i: (b * nq + i, 0)),
        out_shape=jax.ShapeDtypeStruct((B * S, A_WIDTH), BF16),
        compiler_params=_cparams(("parallel", "parallel")),
        name="swa",
    )(sinks, proj, proj, proj, proj, proj, g.reshape(1, A_WIDTH))


POOL_HALO = 16


def _pool_kernel(u0_ref, u1_ref, h0_ref, h1_ref, w_ref, s_ref, o_ref, *, tq):
    i = pl.program_id(1)
    u = jnp.concatenate([u0_ref[...], u1_ref[...]], axis=1).astype(F32)
    halo = jnp.concatenate([h0_ref[...], h1_ref[...]], axis=1).astype(F32)
    halo = halo * (i > 0).astype(F32)
    ext = jnp.concatenate([halo, u], axis=0)
    t = i * tq + lax.broadcasted_iota(jnp.int32, (tq, 1), 0)
    for gi, w in enumerate(POOL_WINDOWS):
        sl = slice(gi * B_GROUP_DIM, (gi + 1) * B_GROUP_DIM)
        s = ext[:, sl]
        shift = 1
        while shift < w:
            s = s + pltpu.roll(s, shift, 0)
            shift *= 2
        cnt = jnp.minimum(t + 1, w).astype(F32)
        y = (s[POOL_HALO:, :] / cnt - u[:, sl]).astype(BF16)
        o = jnp.dot(y, w_ref[gi], preferred_element_type=F32) * s_ref[:, sl]
        o_ref[:, sl] = o.astype(o_ref.dtype)


def _pool(proj, w_pool, scale, *, B, S, tq=256):
    nq = S // tq
    c0 = COL_UB // 256

    def cur(c):
        return pl.BlockSpec((tq, 256), lambda b, i: (b * nq + i, c))

    def halo(c):
        return pl.BlockSpec(
            (POOL_HALO, 256),
            lambda b, i: (jnp.maximum((b * S + i * tq) // POOL_HALO - 1, 0), c))

    return pl.pallas_call(
        functools.partial(_pool_kernel, tq=tq),
        grid=(B, nq),
        in_specs=[cur(c0), cur(c0 + 1), halo(c0), halo(c0 + 1),
                  pl.BlockSpec((len(POOL_WINDOWS), B_GROUP_DIM, B_GROUP_DIM), lambda b, i: (0, 0, 0)),
                  pl.BlockSpec((1, B_WIDTH), lambda b, i: (0, 0))],
        out_specs=pl.BlockSpec((tq, B_WIDTH), lambda b, i: (b * nq + i, 0)),
        out_shape=jax.ShapeDtypeStruct((B * S, B_WIDTH), BF16),
        compiler_params=_cparams(("parallel", "parallel")),
        name="pool",
    )(proj, proj, proj, proj, w_pool, scale.reshape(1, B_WIDTH))


def _diff_kernel(q_ref, k_ref, vt_ref, lam_ref, g_ref, o_ref, qcat_ref, s0_ref, s1_ref, m_ref, acc_ref,
                 *, tq, lam_init):
    qi = pl.program_id(2)
    q = q_ref[...]
    lane = lax.broadcasted_iota(jnp.int32, q.shape, 1)
    zero = jnp.zeros_like(q)
    qcat_ref[0:tq, :] = jnp.where(lane < HEAD_DIM, q, zero)
    qcat_ref[tq:2 * tq, :] = jnp.where(lane >= HEAD_DIM, q, zero)

    m_ref[...] = jnp.full(m_ref.shape, NEG, F32)
    acc_ref[...] = jnp.zeros(acc_ref.shape, F32)

    def scores(kidx):
        off = pl.multiple_of(kidx * DIFF_KB, DIFF_KB)
        return lax.dot_general(k_ref[pl.ds(off, DIFF_KB), :], qcat_ref[...], (((1,), (1,)), ((), ())),
                               preferred_element_type=F32)

    def consume(s_ref, kidx, diagonal):
        st = s_ref[...]
        if diagonal:
            r = lax.broadcasted_iota(jnp.int32, st.shape, 0)
            c = lax.broadcasted_iota(jnp.int32, st.shape, 1)
            c = jnp.where(c >= tq, c - tq, c)
            st = jnp.where(r <= c, st, NEG)
        m = m_ref[...]
        m_new = jnp.maximum(m, jnp.max(st, axis=0, keepdims=True))
        alpha = jnp.exp2(m - m_new)
        p = jnp.exp2(st - m_new)
        m_ref[...] = m_new
        pv = jnp.dot(vt_ref[kidx], p.astype(BF16), preferred_element_type=F32)
        acc_ref[...] = alpha * acc_ref[...] + pv

    def step(cur_ref, nxt_ref, kidx):
        s_next = scores(kidx + 1)
        consume(cur_ref, kidx, False)
        nxt_ref[...] = s_next

    s0_ref[...] = scores(0)

    def body(c, carry):
        step(s0_ref, s1_ref, 2 * c)
        step(s1_ref, s0_ref, 2 * c + 1)
        return carry

    lax.fori_loop(0, qi // 2, body, 0)

    @pl.when(qi % 2 == 1)
    def _():
        step(s0_ref, s1_ref, qi - 1)
        consume(s1_ref, qi, True)

    @pl.when(qi % 2 == 0)
    def _():
        consume(s0_ref, qi, True)

    lv = lam_ref[...]
    lam = (jnp.exp(jnp.sum(lv[0:1] * lv[1:2], axis=1, keepdims=True))
           - jnp.exp(jnp.sum(lv[2:3] * lv[3:4], axis=1, keepdims=True)) + lam_init)
    l = acc_ref[C_VDIM:C_VDIM + 1, :]
    acc = acc_ref[0:C_VDIM, :]
    ot = acc[:, :tq] / l[:, :tq] - lam * (acc[:, tq:] / l[:, tq:])
    ms = jnp.mean(ot * ot, axis=0, keepdims=True)
    yt = ot * lax.rsqrt(ms + DIFF_EPS) * (g_ref[...] * (1.0 - lam_init))
    o_ref[...] = yt.T.astype(o_ref.dtype)


def _diff(proj, vt, lam_vecs, subln, *, B, S, lam_init, tq=512):
    nq = S // tq
    nkb = S // DIFF_KB
    assert tq == DIFF_KB
    return pl.pallas_call(
        functools.partial(_diff_kernel, tq=tq, lam_init=lam_init),
        grid=(B, C_HEADS, nq),
        in_specs=[
            pl.BlockSpec((tq, C_VDIM), lambda b, h, i: (b * nq + i, COL_QC // C_VDIM + h)),
            pl.BlockSpec((S, C_VDIM), lambda b, h, i: (b, COL_KC // C_VDIM + h)),
            pl.BlockSpec((None, None, nkb, VT_ROWS, DIFF_KB), lambda b, h, i: (b, h, 0, 0, 0)),
            pl.BlockSpec((4, HEAD_DIM), lambda b, h, i: (0, 0)),
            pl.BlockSpec((C_VDIM, 1), lambda b, h, i: (0, 0)),
        ],
        out_specs=pl.BlockSpec((tq, C_VDIM), lambda b, h, i: (b * nq + i, h)),
        out_shape=jax.ShapeDtypeStruct((B * S, C_WIDTH), BF16),
        scratch_shapes=[pltpu.VMEM((2 * tq, C_VDIM), BF16),
                        pltpu.VMEM((DIFF_KB, 2 * tq), F32), pltpu.VMEM((DIFF_KB, 2 * tq), F32),
                        pltpu.VMEM((1, 2 * tq), F32), pltpu.VMEM((VT_ROWS, 2 * tq), F32)],
        compiler_params=_cparams(("parallel", "parallel", "arbitrary")),
        name="diff",
    )(proj, proj, vt, lam_vecs, subln.reshape(C_VDIM, 1))


OUT_SUB = 256


def _out_kernel(oa_ref, ob_ref, oc_ref, w_ref, h_ref, g_ref, h1_ref, hn_ref, *, tm):
    subs = [slice(s * OUT_SUB, (s + 1) * OUT_SUB) for s in range(tm // OUT_SUB)]
    accs = []
    for rows in subs:
        acc = jnp.dot(oa_ref[rows, :], w_ref[0:A_WIDTH, :], preferred_element_type=F32)
        acc = acc + jnp.dot(ob_ref[rows, :], w_ref[A_WIDTH:A_WIDTH + B_WIDTH, :], preferred_element_type=F32)
        acc = acc + jnp.dot(oc_ref[rows, :], w_ref[A_WIDTH + B_WIDTH:, :], preferred_element_type=F32)
        accs.append(acc)
    for rows, acc in zip(subs, accs):
        h1 = h_ref[rows, :] + acc
        h1_ref[rows, :] = h1
        ms = jnp.mean(h1 * h1, axis=1, keepdims=True)
        hn = h1 * lax.rsqrt(ms + NORM_EPS) * g_ref[...]
        hn_ref[rows, :] = _pack_rows(hn)


def _out(oa, ob, oc, w, h, g, *, tm=512):
    T, D = h.shape
    return pl.pallas_call(
        functools.partial(_out_kernel, tm=tm),
        grid=(T // tm,),
        in_specs=[
            pl.BlockSpec((tm, A_WIDTH), lambda i: (i, 0)),
            pl.BlockSpec((tm, B_WIDTH), lambda i: (i, 0)),
            pl.BlockSpec((tm, C_WIDTH), lambda i: (i, 0)),
            pl.BlockSpec((D, D), lambda i: (0, 0)),
            pl.BlockSpec((tm, D), lambda i: (i, 0)),
            pl.BlockSpec((1, D), lambda i: (0, 0)),
        ],
        out_specs=[
            pl.BlockSpec((tm, D), lambda i: (i, 0)),
            pl.BlockSpec((tm, D_PACK), lambda i: (i, 0)),
        ],
        out_shape=[
            jax.ShapeDtypeStruct((T, D), F32),
            jax.ShapeDtypeStruct((T, D_PACK), U32),
        ],
        compiler_params=_cparams(("parallel",)),
        name="out",
    )(oa, ob, oc, w, h, g.reshape(1, D))


ROUTER_COLS = 128
ROUTER_LO = 64


def _router_kernel(x_ref, w_ref, b_ref, o_ref):
    o_ref[...] = jnp.dot(_unpack_rows_bf16(x_ref[...]), w_ref[...], preferred_element_type=F32) + b_ref[...]


def _router(hn_packed, w2, b2, *, tm=1024):
    T = hn_packed.shape[0]
    D = w2.shape[0]
    return pl.pallas_call(
        _router_kernel,
        grid=(T // tm,),
        in_specs=[
            pl.BlockSpec((tm, D_PACK), lambda i: (i, 0)),
            pl.BlockSpec((D, ROUTER_COLS), lambda i: (0, 0)),
            pl.BlockSpec((1, ROUTER_COLS), lambda i: (0, 0)),
        ],
        out_specs=pl.BlockSpec((tm, ROUTER_COLS), lambda i: (i, 0)),
        out_shape=jax.ShapeDtypeStruct((T, ROUTER_COLS), F32),
        compiler_params=_cparams(("parallel",)),
        name="router",
    )(hn_packed, w2, b2)


EXPERT_SUB = 256
EXPERT_VMEM = 56 * 1024 * 1024


def _expert_kernel(blk_ref, exp_ref, lo_ref, hi_ref, x_ref, g_ref, wg_ref, wu_ref, wd_ref, o_ref,
                   wg_bf, wu_bf, wd_bf, *, tm):
    p = pl.program_id(0)
    prev = jnp.maximum(p - 1, 0)
    blk = blk_ref[p]
    first = (p == 0) | (blk != blk_ref[prev])
    new_expert = (p == 0) | (exp_ref[p] != exp_ref[prev])
    lo = lo_ref[p]
    hi = hi_ref[p]

    @pl.when(new_expert)
    def _():
        wg_bf[...] = wg_ref[...].astype(BF16)
        wu_bf[...] = wu_ref[...].astype(BF16)
        wd_bf[...] = wd_ref[...].astype(BF16)

    @pl.when(first)
    def _():
        o_ref[...] = jnp.zeros_like(o_ref)

    for s in range(tm // EXPERT_SUB):
        row0 = blk * tm + s * EXPERT_SUB
        rows = slice(s * EXPERT_SUB, (s + 1) * EXPERT_SUB)

        @pl.when((hi > row0) & (lo < row0 + EXPERT_SUB))
        def _(row0=row0, rows=rows):
            x = _unpack_rows_bf16(x_ref[rows, :])
            a = jnp.dot(x, wg_bf[...], preferred_element_type=F32)
            u = jnp.dot(x, wu_bf[...], preferred_element_type=F32)
            act = (a / (1.0 + jnp.exp(-a))) * u
            y = jnp.dot(act.astype(BF16), wd_bf[...], preferred_element_type=F32)
            r = row0 + lax.broadcasted_iota(jnp.int32, (EXPERT_SUB, 1), 0)
            mine = (r >= lo) & (r < hi)
            o_ref[rows, :] = jnp.where(mine, _pack_rows(y * g_ref[rows, :]), o_ref[rows, :])


def _experts(seg_blk, seg_exp, seg_lo, seg_hi, xs, gates, wg, wu, wd, layer, *, tm):
    N = xs.shape[0]
    D = wg.shape[2]
    P = seg_blk.shape[0]
    grid_spec = pltpu.PrefetchScalarGridSpec(
        num_scalar_prefetch=4,
        grid=(P,),
        in_specs=[
            pl.BlockSpec((tm, D_PACK), lambda p, blk, ex, lo, hi: (blk[p], 0)),
            pl.BlockSpec((tm, 1), lambda p, blk, ex, lo, hi: (blk[p], 0)),
            pl.BlockSpec((None, None, D, D_FF), lambda p, blk, ex, lo, hi: (layer, ex[p], 0, 0)),
            pl.BlockSpec((None, None, D, D_FF), lambda p, blk, ex, lo, hi: (layer, ex[p], 0, 0)),
            pl.BlockSpec((None, None, D_FF, D), lambda p, blk, ex, lo, hi: (layer, ex[p], 0, 0)),
        ],
        out_specs=pl.BlockSpec((tm, D_PACK), lambda p, blk, ex, lo, hi: (blk[p], 0)),
        scratch_shapes=[pltpu.VMEM((D, D_FF), BF16), pltpu.VMEM((D, D_FF), BF16), pltpu.VMEM((D_FF, D), BF16)],
    )
    return pl.pallas_call(
        functools.partial(_expert_kernel, tm=tm),
        grid_spec=grid_spec,
        out_shape=jax.ShapeDtypeStruct((N, D_PACK), U32),
        compiler_params=_cparams(("arbitrary",), EXPERT_VMEM),
        name="experts",
    )(seg_blk, seg_exp, seg_lo, seg_hi, xs, gates, wg, wu, wd)


def _route(lg2, T, tm):
    n_lg = N_GROUPS + N_EXPERTS
    lg = lg2[:, :n_lg] + lg2[:, ROUTER_LO:ROUTER_LO + n_lg]
    pg = jax.nn.softmax(lg[:, :N_GROUPS], axis=-1)
    gsel = jnp.argmax(pg, axis=-1)
    pgsel = jnp.take_along_axis(pg, gsel[:, None], axis=1)
    le = lg[:, N_GROUPS:].reshape(T, N_GROUPS, EXPERTS_PER_GROUP)
    le = jnp.take_along_axis(le, gsel[:, None, None], axis=1)[:, 0]
    pe = jax.nn.softmax(le, axis=-1)
    top_p, top_i = lax.top_k(pe, TOP_K)
    gates = pgsel * top_p / jnp.sum(top_p, axis=-1, keepdims=True)
    eidx = (gsel[:, None] * EXPERTS_PER_GROUP + top_i).astype(jnp.int32)

    N = T * TOP_K
    flat_e = eidx.reshape(N)
    iota = jnp.arange(N, dtype=jnp.int32)
    sorted_e, order, g_sorted = lax.sort((flat_e, iota, gates.reshape(N)), num_keys=1, is_stable=True)
    tok = order // TOP_K
    _, pos = lax.sort((order, iota), num_keys=1)
    counts = jnp.bincount(flat_e, length=N_EXPERTS)
    starts = (jnp.cumsum(counts) - counts).astype(jnp.int32)
    nb = N // tm
    bounds = jnp.sort(jnp.concatenate([jnp.arange(nb, dtype=jnp.int32) * tm, starts]))
    ends = jnp.concatenate([bounds[1:], jnp.full((1,), N, bounds.dtype)])
    seg_blk = jnp.minimum(bounds // tm, nb - 1).astype(jnp.int32)
    seg_exp = sorted_e[jnp.minimum(bounds, N - 1)].astype(jnp.int32)
    pos_rows = pos.reshape(T, TOP_K).T.reshape(N)
    return tok, g_sorted, pos_rows, seg_blk, seg_exp, bounds, ends


def _in_proj_perm():
    offs = {}
    idx = 0
    for name, w in zip(("qa", "ka", "va", "ub", "qc", "kc", "vc"), (768, 256, 256, 512, 768, 768, 768)):
        offs[name] = (idx, w)
        idx += w
    cols = []
    for name in ("qa", "qc", "ka", "kc", "va", "ub", "vc"):
        o, w = offs[name]
        cols.extend(range(o, o + w))
    return jnp.asarray(cols, jnp.int32)


def kernel(x, positions, attn_norm, w_in, sinks, branch_norm_a, w_pool, pool_scale, lambda_q1, lambda_k1,
           lambda_q2, lambda_k2, subln, w_out, ffn_norm, w_router_group, b_router_group, w_router_expert,
           b_router_expert, w_expert_gate, w_expert_up, w_expert_down, final_norm):
    B, S, D = x.shape
    T = B * S
    depth = w_in.shape[0]
    tm_e = 512

    half = HEAD_DIM // 2
    inv = ROPE_THETA ** (-jnp.arange(half, dtype=F32) / half)
    ang = positions.astype(F32).reshape(T, 1) * inv
    cos, sin = jnp.cos(ang), jnp.sin(ang)
    cos_t = jnp.concatenate([cos, cos, cos, cos], axis=1)
    sin_t = jnp.concatenate([-sin, sin, -sin, sin], axis=1)
    perm = _in_proj_perm()

    h = x.reshape(T, D)
    y_rows = None
    for l in range(depth):
        if l == 0:
            (hn,) = _norm(h, None, attn_norm[l], emit_h=False, emit_bf16=True)
        else:
            h, hn = _norm(h, y_rows, attn_norm[l], emit_h=True, emit_bf16=True)
        w_in_l = w_in[l][:, perm].astype(BF16)
        proj, vt = _proj(hn, w_in_l[:, :COL_VC], w_in_l[:, COL_VC:].T, cos_t, sin_t, B=B, S=S)

        oa = _swa(proj, sinks[l], branch_norm_a[l], B=B, S=S)
        ob = _pool(proj, w_pool[l].astype(BF16), pool_scale[l], B=B, S=S)

        lam_init = 0.8 - 0.6 * math.exp(-0.3 * l)
        lam_vecs = jnp.stack([lambda_q1[l], lambda_k1[l], lambda_q2[l], lambda_k2[l]]).astype(F32)
        oc = _diff(proj, vt, lam_vecs, subln[l], B=B, S=S, lam_init=lam_init)

        h1, hn2 = _out(oa, ob, oc, w_out[l].astype(BF16), h, ffn_norm[l])

        n_lg = N_GROUPS + N_EXPERTS
        wr = jnp.concatenate([w_router_group[l], w_router_expert[l]], axis=1)
        wr_hi = wr.astype(BF16)
        wr_lo = (wr - wr_hi.astype(F32)).astype(BF16)
        w2 = (jnp.zeros((D, ROUTER_COLS), BF16).at[:, :n_lg].set(wr_hi)
              .at[:, ROUTER_LO:ROUTER_LO + n_lg].set(wr_lo))
        b2 = jnp.zeros((1, ROUTER_COLS), F32).at[0, :n_lg].set(
            jnp.concatenate([b_router_group[l], b_router_expert[l]]))
        lg2 = _router(hn2, w2, b2)

        tok, g_sorted, pos_rows, seg_blk, seg_exp, seg_lo, seg_hi = _route(lg2, T, tm_e)
        xs = _gather_rows(hn2, tok)
        ys = _experts(seg_blk, seg_exp, seg_lo, seg_hi, xs, g_sorted.reshape(-1, 1),
                      w_expert_gate, w_expert_up, w_expert_down, l, tm=tm_e)
        y_rows = _sc_gather_rows(ys, pos_rows)
        h = h1

    (out,) = _norm(h, y_rows, final_norm, emit_h=False, emit_bf16=False)
    return out.reshape(B, S, D)
```

```python
import functools
import math

import jax
import jax.numpy as jnp
from jax import lax
from jax.experimental import pallas as pl
from jax.experimental.pallas import tpu as pltpu
from jax.experimental.pallas import tpu_sc as plsc

F32 = jnp.float32
BF16 = jnp.bfloat16

D_MODEL = 2048
HEAD_DIM = 64
ROPE_THETA = 10000.0
A_HEADS = 12
A_KV_HEADS = 4
A_GROUP = 3
A_WIDTH = 768
WINDOW = 128
POOL_WINDOWS = (2, 4, 8, 16)
B_WIDTH = 512
B_GROUP_DIM = 128
C_VDIM = 128
C_WIDTH = 768
C_HEADS = 6
DIFF_EPS = 1e-5
IN_COLS = 4096
N_GROUPS = 4
EXPERTS_PER_GROUP = 8
N_EXPERTS = 32
TOP_K = 2
D_FF = 512
NORM_EPS = 1e-6
NEG = -1e30
LOG2E = math.log2(math.e)
Q_SCALE = HEAD_DIM ** -0.5 * LOG2E

COL_QA, COL_QC, COL_KA, COL_KC, COL_VA, COL_UB, COL_VC = 0, 768, 1536, 1792, 2560, 2816, 3328
PROJ_TN = 256
PROJ_SUB = 256
N_SCALED_TILES = COL_KA // PROJ_TN
N_ROPE_TILES = COL_VA // PROJ_TN
N_MAIN_TILES = COL_VC // PROJ_TN
DIFF_KB = 512
VT_ROWS = C_VDIM + 16

VMEM_LIMIT = 48 * 1024 * 1024
D_PACK = D_MODEL // 2
U32 = jnp.uint32


def _cparams(sem, vmem=VMEM_LIMIT):
    return pltpu.CompilerParams(dimension_semantics=sem, vmem_limit_bytes=vmem)


def _pack_rows(x):
    lo = pltpu.bitcast(x[:, :D_PACK].astype(BF16).astype(F32), U32) >> 16
    hi = pltpu.bitcast(x[:, D_PACK:].astype(BF16).astype(F32), U32)
    return hi | lo


def _unpack_rows(w):
    return pltpu.bitcast(w << 16, F32), pltpu.bitcast(w & U32(0xFFFF0000), F32)


def _unpack_rows_bf16(w):
    lo, hi = _unpack_rows(w)
    return jnp.concatenate([lo.astype(BF16), hi.astype(BF16)], axis=1)


SC_WINDOW = 32
SC_INDEX_LANES = 128


def _sc_gather_rows(x, idx):
    M = idx.shape[0]
    W = x.shape[1]
    idx2 = jnp.pad(idx.reshape(M // SC_WINDOW, SC_WINDOW), ((0, 0), (0, SC_INDEX_LANES - SC_WINDOW)))
    mesh = plsc.VectorSubcoreMesh(core_axis_name="core", subcore_axis_name="subcore")

    @pl.kernel(out_type=jax.ShapeDtypeStruct((M, W), x.dtype), mesh=mesh)
    def gather(x_hbm, i_hbm, o_hbm):
        def body(i_vmem, o_vmem):
            pltpu.sync_copy(x_hbm.at[i_vmem.at[0, pl.ds(0, SC_WINDOW)]], o_vmem)

        pltpu.emit_pipeline(
            body,
            grid=(M // SC_WINDOW,),
            in_specs=[pl.BlockSpec((1, SC_INDEX_LANES), index_map=lambda i: (i, 0))],
            out_specs=[pl.BlockSpec((SC_WINDOW, W), index_map=lambda i: (i, 0))],
            core_axis_name=("core", "subcore"),
            dimension_semantics=(pltpu.PARALLEL,),
        )(i_hbm, o_hbm)

    return gather(x, idx2)


def _norm_kernel(*refs, n_add, emit_h, emit_bf16, eps):
    h_ref = refs[0]
    add_refs = refs[1:1 + n_add]
    g_ref = refs[1 + n_add]
    outs = refs[2 + n_add:]
    h = h_ref[...]
    if n_add:
        lo = jnp.zeros((h.shape[0], D_PACK), F32)
        hi = jnp.zeros((h.shape[0], D_PACK), F32)
        for r in add_refs:
            a, b = _unpack_rows(r[...])
            lo, hi = lo + a, hi + b
        h = h + jnp.concatenate([lo, hi], axis=1)
    ms = jnp.mean(h * h, axis=-1, keepdims=True)
    y = h * lax.rsqrt(ms + eps) * g_ref[...]
    k = 0
    if emit_h:
        outs[k][...] = h
        k += 1
    outs[k][...] = y.astype(BF16 if emit_bf16 else F32)


def _norm(h, y_rows, g, *, emit_h, emit_bf16, tm=512):
    T, D = h.shape
    nt = T // tm
    row = pl.BlockSpec((tm, D), lambda i: (i, 0))
    n_add = 0 if y_rows is None else TOP_K
    add_specs = [pl.BlockSpec((tm, D_PACK), lambda i, k=k: (i + k * nt, 0)) for k in range(n_add)]
    out_shape, out_specs = [], []
    if emit_h:
        out_shape.append(jax.ShapeDtypeStruct((T, D), F32))
        out_specs.append(row)
    out_shape.append(jax.ShapeDtypeStruct((T, D), BF16 if emit_bf16 else F32))
    out_specs.append(row)
    return pl.pallas_call(
        functools.partial(_norm_kernel, n_add=n_add, emit_h=emit_h, emit_bf16=emit_bf16, eps=NORM_EPS),
        grid=(nt,),
        in_specs=[row] + add_specs + [pl.BlockSpec((1, D), lambda i: (0, 0))],
        out_specs=out_specs,
        out_shape=out_shape,
        compiler_params=_cparams(("parallel",)),
        name="norm",
    )(h, *([y_rows] * n_add), g.reshape(1, D))


def _proj_kernel(x_ref, w_ref, wt_ref, cos_ref, sin_ref, o_ref, vt_ref, *, tm):
    j = pl.program_id(1)
    subs = [slice(s * PROJ_SUB, (s + 1) * PROJ_SUB) for s in range(tm // PROJ_SUB)]

    def matmuls():
        return [jnp.dot(x_ref[rows, :], w_ref[...], preferred_element_type=F32) for rows in subs]

    @pl.when(j < N_ROPE_TILES)
    def _():
        accs = matmuls()
        scale = jnp.where(j < N_SCALED_TILES, Q_SCALE, 1.0).astype(F32)
        lane = lax.broadcasted_iota(jnp.int32, (PROJ_SUB, PROJ_TN), 1)
        first_half = (lane & (HEAD_DIM // 2)) == 0
        for rows, acc in zip(subs, accs):
            c = jnp.concatenate([cos_ref[rows, :] * scale] * (PROJ_TN // 128), axis=1)
            s = jnp.concatenate([sin_ref[rows, :] * scale] * (PROJ_TN // 128), axis=1)
            swapped = jnp.where(first_half,
                                pltpu.roll(acc, PROJ_TN - HEAD_DIM // 2, 1),
                                pltpu.roll(acc, HEAD_DIM // 2, 1))
            o_ref[rows, :] = (acc * c + swapped * s).astype(o_ref.dtype)

    @pl.when((j >= N_ROPE_TILES) & (j < N_MAIN_TILES))
    def _():
        for rows, acc in zip(subs, matmuls()):
            o_ref[rows, :] = acc.astype(o_ref.dtype)

    @pl.when(j >= N_MAIN_TILES)
    def _():
        accs = [lax.dot_general(wt_ref[...], x_ref[rows, :], (((1,), (1,)), ((), ())),
                                preferred_element_type=F32) for rows in subs]
        row = lax.broadcasted_iota(jnp.int32, (VT_ROWS - C_VDIM, DIFF_KB), 0)
        extra = jnp.where(row == 0, 1.0, 0.0).astype(vt_ref.dtype)
        per_kb = DIFF_KB // PROJ_SUB
        for s, acc in enumerate(accs):
            cols = slice((s % per_kb) * PROJ_SUB, (s % per_kb + 1) * PROJ_SUB)
            for hh in range(PROJ_TN // C_VDIM):
                vt_ref[hh, s // per_kb, 0:C_VDIM, cols] = acc[hh * C_VDIM:(hh + 1) * C_VDIM, :].astype(vt_ref.dtype)
        for hh in range(PROJ_TN // C_VDIM):
            for kb in range(tm // DIFF_KB):
                vt_ref[hh, kb, C_VDIM:VT_ROWS, :] = extra


def _proj(hn, w, wt, cos_t, sin_t, *, B, S, tm=1024):
    T, D = hn.shape
    spt = S // tm
    last_main = N_MAIN_TILES - 1
    return pl.pallas_call(
        functools.partial(_proj_kernel, tm=tm),
        grid=(T // tm, IN_COLS // PROJ_TN),
        in_specs=[
            pl.BlockSpec((tm, D), lambda i, j: (i, 0)),
            pl.BlockSpec((D, PROJ_TN), lambda i, j: (0, jnp.minimum(j, last_main))),
            pl.BlockSpec((PROJ_TN, D), lambda i, j: (jnp.maximum(j - N_MAIN_TILES, 0), 0)),
            pl.BlockSpec((tm, 128), lambda i, j: (i, 0)),
            pl.BlockSpec((tm, 128), lambda i, j: (i, 0)),
        ],
        out_specs=[
            pl.BlockSpec((tm, PROJ_TN), lambda i, j: (i, jnp.minimum(j, last_main))),
            pl.BlockSpec((None, PROJ_TN // C_VDIM, tm // DIFF_KB, VT_ROWS, DIFF_KB),
                         lambda i, j: (i // spt, jnp.maximum(j - N_MAIN_TILES, 0), i % spt, 0, 0)),
        ],
        out_shape=[
            jax.ShapeDtypeStruct((T, COL_VC), BF16),
            jax.ShapeDtypeStruct((B, C_HEADS, S // DIFF_KB, VT_ROWS, DIFF_KB), BF16),
        ],
        compiler_params=_cparams(("parallel", "arbitrary")),
        name="proj",
    )(hn, w, wt, cos_t, sin_t)


def _swa_kernel(sink_ref, q_ref, kc_ref, vc_ref, kp_ref, vp_ref, g_ref, o_ref, *, tq):
    i = pl.program_id(1)
    kext = jnp.concatenate([kp_ref[...], kc_ref[...]], axis=0)
    vext = jnp.concatenate([vp_ref[...], vc_ref[...]], axis=0)
    qi = lax.broadcasted_iota(jnp.int32, (WINDOW, 2 * WINDOW), 0)
    jj = lax.broadcasted_iota(jnp.int32, (WINDOW, 2 * WINDOW), 1)
    band = (jj > qi) & (jj <= qi + WINDOW)
    g = g_ref[...]
    for sb in range(tq // WINDOW):
        mask = band
        if sb == 0:
            mask = band & ((jj >= WINDOW) | (i > 0))
        outs = []
        for h in range(A_HEADS):
            kv = h // A_GROUP
            qh = q_ref[sb * WINDOW:(sb + 1) * WINDOW, h * HEAD_DIM:(h + 1) * HEAD_DIM]
            kb = kext[sb * WINDOW:(sb + 2) * WINDOW, kv * HEAD_DIM:(kv + 1) * HEAD_DIM]
            vb = vext[sb * WINDOW:(sb + 2) * WINDOW, kv * HEAD_DIM:(kv + 1) * HEAD_DIM]
            s = lax.dot_general(qh, kb, (((1,), (1,)), ((), ())), preferred_element_type=F32)
            s = jnp.where(mask, s, NEG)
            sink = sink_ref[h] * LOG2E
            m = jnp.maximum(jnp.max(s, axis=1, keepdims=True), sink)
            p = jnp.exp2(s - m)
            l = jnp.sum(p, axis=1, keepdims=True) + jnp.exp2(sink - m)
            o = jnp.dot(p.astype(BF16), vb, preferred_element_type=F32)
            outs.append(o / l)
        o_all = jnp.concatenate(outs, axis=1)
        ms = jnp.mean(o_all * o_all, axis=1, keepdims=True)
        y = o_all * lax.rsqrt(ms + NORM_EPS) * g
        o_ref[sb * WINDOW:(sb + 1) * WINDOW, :] = y.astype(o_ref.dtype)


def _swa(proj, sinks, g, *, B, S, tq=256):
    nq = S // tq
    rpb = tq // WINDOW

    def prev_idx(col):
        def f(b, i):
            return (jnp.maximum(b * (S // WINDOW) + i * rpb - 1, 0), col)
        return f

    return pl.pallas_call(
        functools.partial(_swa_kernel, tq=tq),
        grid=(B, nq),
        in_specs=[
            pl.BlockSpec(memory_space=pltpu.SMEM),
            pl.BlockSpec((tq, A_WIDTH), lambda b, i: (b * nq + i, COL_QA // A_WIDTH)),
            pl.BlockSpec((tq, 256), lambda b, i: (b * nq + i, COL_KA // 256)),
            pl.BlockSpec((tq, 256), lambda b, i: (b * nq + i, COL_VA // 256)),
            pl.BlockSpec((WINDOW, 256), prev_idx(COL_KA // 256)),
            pl.BlockSpec((WINDOW, 256), prev_idx(COL_VA // 256)),
            pl.BlockSpec((1, A_WIDTH), lambda b, i: (0, 0)),
        ],
        out_specs=pl.BlockSpec((tq, A_WIDTH), lambda b, i: (b * nq + i, 0)),
        out_shape=jax.ShapeDtypeStruct((B * S, A_WIDTH), BF16),
        compiler_params=_cparams(("parallel", "parallel")),
        name="swa",
    )(sinks, proj, proj, proj, proj, proj, g.reshape(1, A_WIDTH))


POOL_HALO = 16


def _pool_kernel(u0_ref, u1_ref, h0_ref, h1_ref, w_ref, s_ref, o_ref, *, tq):
    i = pl.program_id(1)
    u = jnp.concatenate([u0_ref[...], u1_ref[...]], axis=1).astype(F32)
    halo = jnp.concatenate([h0_ref[...], h1_ref[...]], axis=1).astype(F32)
    halo = halo * (i > 0).astype(F32)
    ext = jnp.concatenate([halo, u], axis=0)
    t = i * tq + lax.broadcasted_iota(jnp.int32, (tq, 1), 0)
    for gi, w in enumerate(POOL_WINDOWS):
        sl = slice(gi * B_GROUP_DIM, (gi + 1) * B_GROUP_DIM)
        s = ext[:, sl]
        shift = 1
        while shift < w:
            s = s + pltpu.roll(s, shift, 0)
            shift *= 2
        cnt = jnp.minimum(t + 1, w).astype(F32)
        y = (s[POOL_HALO:, :] / cnt - u[:, sl]).astype(BF16)
        o = jnp.dot(y, w_ref[gi], preferred_element_type=F32) * s_ref[:, sl]
        o_ref[:, sl] = o.astype(o_ref.dtype)


def _pool(proj, w_pool, scale, *, B, S, tq=256):
    nq = S // tq
    c0 = COL_UB // 256

    def cur(c):
        return pl.BlockSpec((tq, 256), lambda b, i: (b * nq + i, c))

    def halo(c):
        return pl.BlockSpec(
            (POOL_HALO, 256),
            lambda b, i: (jnp.maximum((b * S + i * tq) // POOL_HALO - 1, 0), c))

    return pl.pallas_call(
        functools.partial(_pool_kernel, tq=tq),
        grid=(B, nq),
        in_specs=[cur(c0), cur(c0 + 1), halo(c0), halo(c0 + 1),
                  pl.BlockSpec((len(POOL_WINDOWS), B_GROUP_DIM, B_GROUP_DIM), lambda b, i: (0, 0, 0)),
                  pl.BlockSpec((1, B_WIDTH), lambda b, i: (0, 0))],
        out_specs=pl.BlockSpec((tq, B_WIDTH), lambda b, i: (b * nq + i, 0)),
        out_shape=jax.ShapeDtypeStruct((B * S, B_WIDTH), BF16),
        compiler_params=_cparams(("parallel", "parallel")),
        name="pool",
    )(proj, proj, proj, proj, w_pool, scale.reshape(1, B_WIDTH))


def _diff_kernel(q_ref, k_ref, vt_ref, lam_ref, g_ref, o_ref, qcat_ref, s0_ref, s1_ref, m_ref, acc_ref,
                 *, tq, lam_init):
    qi = pl.program_id(2)
    q = q_ref[...]
    lane = lax.broadcasted_iota(jnp.int32, q.shape, 1)
    zero = jnp.zeros_like(q)
    qcat_ref[0:tq, :] = jnp.where(lane < HEAD_DIM, q, zero)
    qcat_ref[tq:2 * tq, :] = jnp.where(lane >= HEAD_DIM, q, zero)

    m_ref[...] = jnp.full(m_ref.shape, NEG, F32)
    acc_ref[...] = jnp.zeros(acc_ref.shape, F32)

    def scores(kidx):
        off = pl.multiple_of(kidx * DIFF_KB, DIFF_KB)
        return lax.dot_general(k_ref[pl.ds(off, DIFF_KB), :], qcat_ref[...], (((1,), (1,)), ((), ())),
                               preferred_element_type=F32)

    def consume(s_ref, kidx, diagonal):
        st = s_ref[...]
        if diagonal:
            r = lax.broadcasted_iota(jnp.int32, st.shape, 0)
            c = lax.broadcasted_iota(jnp.int32, st.shape, 1)
            c = jnp.where(c >= tq, c - tq, c)
            st = jnp.where(r <= c, st, NEG)
        m = m_ref[...]
        m_new = jnp.maximum(m, jnp.max(st, axis=0, keepdims=True))
        alpha = jnp.exp2(m - m_new)
        p = jnp.exp2(st - m_new)
        m_ref[...] = m_new
        pv = jnp.dot(vt_ref[kidx], p.astype(BF16), preferred_element_type=F32)
        acc_ref[...] = alpha * acc_ref[...] + pv

    def step(cur_ref, nxt_ref, kidx):
        s_next = scores(kidx + 1)
        consume(cur_ref, kidx, False)
        nxt_ref[...] = s_next

    s0_ref[...] = scores(0)

    def body(c, carry):
        step(s0_ref, s1_ref, 2 * c)
        step(s1_ref, s0_ref, 2 * c + 1)
        return carry

    lax.fori_loop(0, qi // 2, body, 0)

    @pl.when(qi % 2 == 1)
    def _():
        step(s0_ref, s1_ref, qi - 1)
        consume(s1_ref, qi, True)

    @pl.when(qi % 2 == 0)
    def _():
        consume(s0_ref, qi, True)

    lv = lam_ref[...]
    lam = (jnp.exp(jnp.sum(lv[0:1] * lv[1:2], axis=1, keepdims=True))
           - jnp.exp(jnp.sum(lv[2:3] * lv[3:4], axis=1, keepdims=True)) + lam_init)
    l = acc_ref[C_VDIM:C_VDIM + 1, :]
    acc = acc_ref[0:C_VDIM, :]
    ot = acc[:, :tq] / l[:, :tq] - lam * (acc[:, tq:] / l[:, tq:])
    ms = jnp.mean(ot * ot, axis=0, keepdims=True)
    yt = ot * lax.rsqrt(ms + DIFF_EPS) * (g_ref[...] * (1.0 - lam_init))
    o_ref[...] = yt.T.astype(o_ref.dtype)


def _diff(proj, vt, lam_vecs, subln, *, B, S, lam_init, tq=512):
    nq = S // tq
    nkb = S // DIFF_KB
    assert tq == DIFF_KB
    return pl.pallas_call(
        functools.partial(_diff_kernel, tq=tq, lam_init=lam_init),
        grid=(B, C_HEADS, nq),
        in_specs=[
            pl.BlockSpec((tq, C_VDIM), lambda b, h, i: (b * nq + i, COL_QC // C_VDIM + h)),
            pl.BlockSpec((S, C_VDIM), lambda b, h, i: (b, COL_KC // C_VDIM + h)),
            pl.BlockSpec((None, None, nkb, VT_ROWS, DIFF_KB), lambda b, h, i: (b, h, 0, 0, 0)),
            pl.BlockSpec((4, HEAD_DIM), lambda b, h, i: (0, 0)),
            pl.BlockSpec((C_VDIM, 1), lambda b, h, i: (0, 0)),
        ],
        out_specs=pl.BlockSpec((tq, C_VDIM), lambda b, h, i: (b * nq + i, h)),
        out_shape=jax.ShapeDtypeStruct((B * S, C_WIDTH), BF16),
        scratch_shapes=[pltpu.VMEM((2 * tq, C_VDIM), BF16),
                        pltpu.VMEM((DIFF_KB, 2 * tq), F32), pltpu.VMEM((DIFF_KB, 2 * tq), F32),
                        pltpu.VMEM((1, 2 * tq), F32), pltpu.VMEM((VT_ROWS, 2 * tq), F32)],
        compiler_params=_cparams(("parallel", "parallel", "arbitrary")),
        name="diff",
    )(proj, proj, vt, lam_vecs, subln.reshape(C_VDIM, 1))


OUT_SUB = 256


def _out_kernel(oa_ref, ob_ref, oc_ref, w_ref, h_ref, g_ref, h1_ref, hn_ref, *, tm):
    subs = [slice(s * OUT_SUB, (s + 1) * OUT_SUB) for s in range(tm // OUT_SUB)]
    accs = []
    for rows in subs:
        acc = jnp.dot(oa_ref[rows, :], w_ref[0:A_WIDTH, :], preferred_element_type=F32)
        acc = acc + jnp.dot(ob_ref[rows, :], w_ref[A_WIDTH:A_WIDTH + B_WIDTH, :], preferred_element_type=F32)
        acc = acc + jnp.dot(oc_ref[rows, :], w_ref[A_WIDTH + B_WIDTH:, :], preferred_element_type=F32)
        accs.append(acc)
    for rows, acc in zip(subs, accs):
        h1 = h_ref[rows, :] + acc
        h1_ref[rows, :] = h1
        ms = jnp.mean(h1 * h1, axis=1, keepdims=True)
        hn = h1 * lax.rsqrt(ms + NORM_EPS) * g_ref[...]
        hn_ref[rows, :] = _pack_rows(hn)


def _out(oa, ob, oc, w, h, g, *, tm=512):
    T, D = h.shape
    return pl.pallas_call(
        functools.partial(_out_kernel, tm=tm),
        grid=(T // tm,),
        in_specs=[
            pl.BlockSpec((tm, A_WIDTH), lambda i: (i, 0)),
            pl.BlockSpec((tm, B_WIDTH), lambda i: (i, 0)),
            pl.BlockSpec((tm, C_WIDTH), lambda i: (i, 0)),
            pl.BlockSpec((D, D), lambda i: (0, 0)),
            pl.BlockSpec((tm, D), lambda i: (i, 0)),
            pl.BlockSpec((1, D), lambda i: (0, 0)),
        ],
        out_specs=[
            pl.BlockSpec((tm, D), lambda i: (i, 0)),
            pl.BlockSpec((tm, D_PACK), lambda i: (i, 0)),
        ],
        out_shape=[
            jax.ShapeDtypeStruct((T, D), F32),
            jax.ShapeDtypeStruct((T, D_PACK), U32),
        ],
        compiler_params=_cparams(("parallel",)),
        name="out",
    )(oa, ob, oc, w, h, g.reshape(1, D))


ROUTER_ROWS = 128
ROUTER_LO = 64
ROUTER_GROUP_ROW = 0
ROUTER_EXPERT_ROW = 8


def _router_kernel(x_ref, wt_ref, b_ref, e_ref, g_ref):
    x = _unpack_rows_bf16(x_ref[...])
    lg = lax.dot_general(wt_ref[...], x, (((1,), (1,)), ((), ())), preferred_element_type=F32)
    lg = lg[0:ROUTER_LO] + lg[ROUTER_LO:ROUTER_ROWS] + b_ref[...]
    grp = lg[ROUTER_GROUP_ROW:ROUTER_GROUP_ROW + N_GROUPS]
    ex = lg[ROUTER_EXPERT_ROW:ROUTER_EXPERT_ROW + N_EXPERTS]

    mg = jnp.max(grp, axis=0, keepdims=True)
    pg_sel = 1.0 / jnp.sum(jnp.exp(grp - mg), axis=0, keepdims=True)
    gi = lax.broadcasted_iota(jnp.int32, grp.shape, 0)
    g_sel = jnp.min(jnp.where(grp == mg, gi, N_GROUPS), axis=0, keepdims=True)

    er = lax.broadcasted_iota(jnp.int32, ex.shape, 0)
    group_of = lax.shift_right_logical(er, EXPERTS_PER_GROUP.bit_length() - 1)
    cand = jnp.where(group_of == g_sel, ex, NEG)
    v1 = jnp.max(cand, axis=0, keepdims=True)
    i1 = jnp.min(jnp.where(cand == v1, er, N_EXPERTS), axis=0, keepdims=True)
    rest = jnp.where(er == i1, NEG, cand)
    v2 = jnp.max(rest, axis=0, keepdims=True)
    i2 = jnp.min(jnp.where(rest == v2, er, N_EXPERTS), axis=0, keepdims=True)
    t = jnp.exp(v2 - v1)
    g1 = pg_sel / (1.0 + t)
    e_ref[...] = jnp.concatenate([i1, i2], axis=0)
    g_ref[...] = jnp.concatenate([g1, g1 * t], axis=0)


def _router(hn_packed, wt, b, *, tm=1024):
    T = hn_packed.shape[0]
    D = wt.shape[1]
    return pl.pallas_call(
        _router_kernel,
        grid=(T // tm,),
        in_specs=[
            pl.BlockSpec((tm, D_PACK), lambda i: (i, 0)),
            pl.BlockSpec((ROUTER_ROWS, D), lambda i: (0, 0)),
            pl.BlockSpec((ROUTER_LO, 1), lambda i: (0, 0)),
        ],
        out_specs=[pl.BlockSpec((TOP_K, tm), lambda i: (0, i)), pl.BlockSpec((TOP_K, tm), lambda i: (0, i))],
        out_shape=[jax.ShapeDtypeStruct((TOP_K, T), jnp.int32), jax.ShapeDtypeStruct((TOP_K, T), F32)],
        compiler_params=_cparams(("parallel",)),
        name="router",
    )(hn_packed, wt, b)


EXPERT_SUB = 256
EXPERT_VMEM = 56 * 1024 * 1024


def _expert_kernel(blk_ref, exp_ref, lo_ref, hi_ref, x_ref, g_ref, wg_ref, wu_ref, wd_ref, o_ref,
                   wg_bf, wu_bf, wd_bf, *, tm):
    p = pl.program_id(0)
    prev = jnp.maximum(p - 1, 0)
    blk = blk_ref[p]
    first = (p == 0) | (blk != blk_ref[prev])
    new_expert = (p == 0) | (exp_ref[p] != exp_ref[prev])
    lo = lo_ref[p]
    hi = hi_ref[p]

    @pl.when(new_expert)
    def _():
        wg_bf[...] = wg_ref[...].astype(BF16)
        wu_bf[...] = wu_ref[...].astype(BF16)
        wd_bf[...] = wd_ref[...].astype(BF16)

    @pl.when(first)
    def _():
        o_ref[...] = jnp.zeros_like(o_ref)

    for s in range(tm // EXPERT_SUB):
        row0 = blk * tm + s * EXPERT_SUB
        rows = slice(s * EXPERT_SUB, (s + 1) * EXPERT_SUB)

        @pl.when((hi > row0) & (lo < row0 + EXPERT_SUB))
        def _(row0=row0, rows=rows):
            x = _unpack_rows_bf16(x_ref[rows, :])
            a = jnp.dot(x, wg_bf[...], preferred_element_type=F32)
            u = jnp.dot(x, wu_bf[...], preferred_element_type=F32)
            act = (a / (1.0 + jnp.exp(-a))) * u
            y = jnp.dot(act.astype(BF16), wd_bf[...], preferred_element_type=F32)
            r = row0 + lax.broadcasted_iota(jnp.int32, (EXPERT_SUB, 1), 0)
            mine = (r >= lo) & (r < hi)
            o_ref[rows, :] = jnp.where(mine, _pack_rows(y * g_ref[rows, :]), o_ref[rows, :])


def _experts(seg_blk, seg_exp, seg_lo, seg_hi, xs, gates, wg, wu, wd, layer, *, tm):
    N = xs.shape[0]
    D = wg.shape[2]
    P = seg_blk.shape[0]
    grid_spec = pltpu.PrefetchScalarGridSpec(
        num_scalar_prefetch=4,
        grid=(P,),
        in_specs=[
            pl.BlockSpec((tm, D_PACK), lambda p, blk, ex, lo, hi: (blk[p], 0)),
            pl.BlockSpec((tm, 1), lambda p, blk, ex, lo, hi: (blk[p], 0)),
            pl.BlockSpec((None, None, D, D_FF), lambda p, blk, ex, lo, hi: (layer, ex[p], 0, 0)),
            pl.BlockSpec((None, None, D, D_FF), lambda p, blk, ex, lo, hi: (layer, ex[p], 0, 0)),
            pl.BlockSpec((None, None, D_FF, D), lambda p, blk, ex, lo, hi: (layer, ex[p], 0, 0)),
        ],
        out_specs=pl.BlockSpec((tm, D_PACK), lambda p, blk, ex, lo, hi: (blk[p], 0)),
        scratch_shapes=[pltpu.VMEM((D, D_FF), BF16), pltpu.VMEM((D, D_FF), BF16), pltpu.VMEM((D_FF, D), BF16)],
    )
    return pl.pallas_call(
        functools.partial(_expert_kernel, tm=tm),
        grid_spec=grid_spec,
        out_shape=jax.ShapeDtypeStruct((N, D_PACK), U32),
        compiler_params=_cparams(("arbitrary",), EXPERT_VMEM),
        name="experts",
    )(seg_blk, seg_exp, seg_lo, seg_hi, xs, gates, wg, wu, wd)


def _route(eidx_t, gates_t, T, tm):
    N = T * TOP_K
    flat_e = eidx_t.T.reshape(N)
    iota = jnp.arange(N, dtype=jnp.int32)
    sorted_e, order, g_sorted = lax.sort((flat_e, iota, gates_t.T.reshape(N)), num_keys=1, is_stable=True)
    tok = order // TOP_K
    _, pos = lax.sort((order, iota), num_keys=1)
    counts = jnp.bincount(flat_e, length=N_EXPERTS)
    starts = (jnp.cumsum(counts) - counts).astype(jnp.int32)
    nb = N // tm
    bounds = jnp.sort(jnp.concatenate([jnp.arange(nb, dtype=jnp.int32) * tm, starts]))
    ends = jnp.concatenate([bounds[1:], jnp.full((1,), N, bounds.dtype)])
    seg_blk = jnp.minimum(bounds // tm, nb - 1).astype(jnp.int32)
    seg_exp = sorted_e[jnp.minimum(bounds, N - 1)].astype(jnp.int32)
    pos_rows = pos.reshape(T, TOP_K).T.reshape(N)
    return tok, g_sorted, pos_rows, seg_blk, seg_exp, bounds, ends


def _in_proj_perm():
    offs = {}
    idx = 0
    for name, w in zip(("qa", "ka", "va", "ub", "qc", "kc", "vc"), (768, 256, 256, 512, 768, 768, 768)):
        offs[name] = (idx, w)
        idx += w
    cols = []
    for name in ("qa", "qc", "ka", "kc", "va", "ub", "vc"):
        o, w = offs[name]
        cols.extend(range(o, o + w))
    return jnp.asarray(cols, jnp.int32)


def kernel(x, positions, attn_norm, w_in, sinks, branch_norm_a, w_pool, pool_scale, lambda_q1, lambda_k1,
           lambda_q2, lambda_k2, subln, w_out, ffn_norm, w_router_group, b_router_group, w_router_expert,
           b_router_expert, w_expert_gate, w_expert_up, w_expert_down, final_norm):
    B, S, D = x.shape
    T = B * S
    depth = w_in.shape[0]
    tm_e = 512

    half = HEAD_DIM // 2
    inv = ROPE_THETA ** (-jnp.arange(half, dtype=F32) / half)
    ang = positions.astype(F32).reshape(T, 1) * inv
    cos, sin = jnp.cos(ang), jnp.sin(ang)
    cos_t = jnp.concatenate([cos, cos, cos, cos], axis=1)
    sin_t = jnp.concatenate([-sin, sin, -sin, sin], axis=1)
    perm = _in_proj_perm()

    h = x.reshape(T, D)
    y_rows = None
    for l in range(depth):
        if l == 0:
            (hn,) = _norm(h, None, attn_norm[l], emit_h=False, emit_bf16=True)
        else:
            h, hn = _norm(h, y_rows, attn_norm[l], emit_h=True, emit_bf16=True)
        w_in_l = w_in[l][:, perm].astype(BF16)
        proj, vt = _proj(hn, w_in_l[:, :COL_VC], w_in_l[:, COL_VC:].T, cos_t, sin_t, B=B, S=S)

        oa = _swa(proj, sinks[l], branch_norm_a[l], B=B, S=S)
        ob = _pool(proj, w_pool[l].astype(BF16), pool_scale[l], B=B, S=S)

        lam_init = 0.8 - 0.6 * math.exp(-0.3 * l)
        lam_vecs = jnp.stack([lambda_q1[l], lambda_k1[l], lambda_q2[l], lambda_k2[l]]).astype(F32)
        oc = _diff(proj, vt, lam_vecs, subln[l], B=B, S=S, lam_init=lam_init)

        h1, hn2 = _out(oa, ob, oc, w_out[l].astype(BF16), h, ffn_norm[l])

        wr = jnp.zeros((ROUTER_LO, D), F32)
        wr = wr.at[ROUTER_GROUP_ROW:ROUTER_GROUP_ROW + N_GROUPS].set(w_router_group[l].T)
        wr = wr.at[ROUTER_EXPERT_ROW:ROUTER_EXPERT_ROW + N_EXPERTS].set(w_router_expert[l].T)
        wr_hi = wr.astype(BF16)
        wr_lo = (wr - wr_hi.astype(F32)).astype(BF16)
        br = jnp.zeros((ROUTER_LO, 1), F32)
        br = br.at[ROUTER_GROUP_ROW:ROUTER_GROUP_ROW + N_GROUPS, 0].set(b_router_group[l])
        br = br.at[ROUTER_EXPERT_ROW:ROUTER_EXPERT_ROW + N_EXPERTS, 0].set(b_router_expert[l])
        eidx_t, gates_t = _router(hn2, jnp.concatenate([wr_hi, wr_lo], axis=0), br)

        tok, g_sorted, pos_rows, seg_blk, seg_exp, seg_lo, seg_hi = _route(eidx_t, gates_t, T, tm_e)
        xs = _sc_gather_rows(hn2, tok)
        ys = _experts(seg_blk, seg_exp, seg_lo, seg_hi, xs, g_sorted.reshape(-1, 1),
                      w_expert_gate, w_expert_up, w_expert_down, l, tm=tm_e)
        y_rows = _sc_gather_rows(ys, pos_rows)
        h = h1

    (out,) = _norm(h, y_rows, final_norm, emit_h=False, emit_bf16=False)
    return out.reshape(B, S, D)
```

```python
import functools
import math

import jax
import jax.numpy as jnp
from jax import lax
from jax.experimental import pallas as pl
from jax.experimental.pallas import tpu as pltpu
from jax.experimental.pallas import tpu_sc as plsc
import numpy as np

F32 = jnp.float32
BF16 = jnp.bfloat16

D_MODEL = 2048
HEAD_DIM = 64
ROPE_THETA = 10000.0
A_HEADS = 12
A_KV_HEADS = 4
A_GROUP = 3
A_WIDTH = 768
WINDOW = 128
POOL_WINDOWS = (2, 4, 8, 16)
B_WIDTH = 512
B_GROUP_DIM = 128
C_VDIM = 128
C_WIDTH = 768
C_HEADS = 6
DIFF_EPS = 1e-5
IN_COLS = 4096
N_GROUPS = 4
EXPERTS_PER_GROUP = 8
N_EXPERTS = 32
TOP_K = 2
D_FF = 512
NORM_EPS = 1e-6
NEG = -1e30
LOG2E = math.log2(math.e)
Q_SCALE = HEAD_DIM ** -0.5 * LOG2E

COL_QA, COL_QC, COL_KA, COL_KC, COL_UB, COL_VA, COL_VC = 0, 768, 1536, 1792, 2560, 3072, 3328
PROJ_TN = 256
PROJ_SUB = 256
N_SCALED_TILES = COL_KA // PROJ_TN
N_ROPE_TILES = COL_UB // PROJ_TN
N_MAIN_TILES = COL_VA // PROJ_TN
DIFF_KB = 512
VT_ROWS = C_VDIM + 16

VMEM_LIMIT = 48 * 1024 * 1024
D_PACK = D_MODEL // 2
U32 = jnp.uint32


def _cparams(sem, vmem=VMEM_LIMIT):
    return pltpu.CompilerParams(dimension_semantics=sem, vmem_limit_bytes=vmem)


def _pack_rows(x):
    lo = pltpu.bitcast(x[:, :D_PACK].astype(BF16).astype(F32), U32) >> 16
    hi = pltpu.bitcast(x[:, D_PACK:].astype(BF16).astype(F32), U32)
    return hi | lo


def _unpack_rows(w):
    return pltpu.bitcast(w << 16, F32), pltpu.bitcast(w & U32(0xFFFF0000), F32)


def _unpack_rows_bf16(w):
    lo, hi = _unpack_rows(w)
    return jnp.concatenate([lo.astype(BF16), hi.astype(BF16)], axis=1)


SC_WINDOW = 32
SC_INDEX_LANES = 128


def _sc_gather_rows(x, idx):
    M = idx.shape[0]
    W = x.shape[1]
    idx2 = jnp.pad(idx.reshape(M // SC_WINDOW, SC_WINDOW), ((0, 0), (0, SC_INDEX_LANES - SC_WINDOW)))
    mesh = plsc.VectorSubcoreMesh(core_axis_name="core", subcore_axis_name="subcore")

    @pl.kernel(out_type=jax.ShapeDtypeStruct((M, W), x.dtype), mesh=mesh)
    def gather(x_hbm, i_hbm, o_hbm):
        def body(i_vmem, o_vmem):
            pltpu.sync_copy(x_hbm.at[i_vmem.at[0, pl.ds(0, SC_WINDOW)]], o_vmem)

        pltpu.emit_pipeline(
            body,
            grid=(M // SC_WINDOW,),
            in_specs=[pl.BlockSpec((1, SC_INDEX_LANES), index_map=lambda i: (i, 0))],
            out_specs=[pl.BlockSpec((SC_WINDOW, W), index_map=lambda i: (i, 0))],
            core_axis_name=("core", "subcore"),
            dimension_semantics=(pltpu.PARALLEL,),
        )(i_hbm, o_hbm)

    return gather(x, idx2)


def _norm_kernel(*refs, n_add, emit_h, emit_bf16, eps):
    h_ref = refs[0]
    add_refs = refs[1:1 + n_add]
    g_ref = refs[1 + n_add]
    outs = refs[2 + n_add:]
    h = h_ref[...]
    if n_add:
        lo = jnp.zeros((h.shape[0], D_PACK), F32)
        hi = jnp.zeros((h.shape[0], D_PACK), F32)
        for r in add_refs:
            a, b = _unpack_rows(r[...])
            lo, hi = lo + a, hi + b
        h = h + jnp.concatenate([lo, hi], axis=1)
    ms = jnp.mean(h * h, axis=-1, keepdims=True)
    y = h * lax.rsqrt(ms + eps) * g_ref[...]
    k = 0
    if emit_h:
        outs[k][...] = h
        k += 1
    outs[k][...] = y.astype(BF16 if emit_bf16 else F32)


def _norm(h, y_rows, g, *, emit_h, emit_bf16, tm=512):
    T, D = h.shape
    nt = T // tm
    row = pl.BlockSpec((tm, D), lambda i: (i, 0))
    n_add = 0 if y_rows is None else TOP_K
    add_specs = [pl.BlockSpec((tm, D_PACK), lambda i, k=k: (i + k * nt, 0)) for k in range(n_add)]
    out_shape, out_specs = [], []
    if emit_h:
        out_shape.append(jax.ShapeDtypeStruct((T, D), F32))
        out_specs.append(row)
    out_shape.append(jax.ShapeDtypeStruct((T, D), BF16 if emit_bf16 else F32))
    out_specs.append(row)
    return pl.pallas_call(
        functools.partial(_norm_kernel, n_add=n_add, emit_h=emit_h, emit_bf16=emit_bf16, eps=NORM_EPS),
        grid=(nt,),
        in_specs=[row] + add_specs + [pl.BlockSpec((1, D), lambda i: (0, 0))],
        out_specs=out_specs,
        out_shape=out_shape,
        compiler_params=_cparams(("parallel",)),
        name="norm",
    )(h, *([y_rows] * n_add), g.reshape(1, D))


def _proj_kernel(x_ref, w_ref, wt_ref, cos_ref, sin_ref, o_ref, vat_ref, vt_ref, *, tm):
    j = pl.program_id(1)
    subs = [slice(s * PROJ_SUB, (s + 1) * PROJ_SUB) for s in range(tm // PROJ_SUB)]

    def matmuls():
        return [jnp.dot(x_ref[rows, :], w_ref[...], preferred_element_type=F32) for rows in subs]

    @pl.when(j < N_ROPE_TILES)
    def _():
        accs = matmuls()
        scale = jnp.where(j < N_SCALED_TILES, Q_SCALE, 1.0).astype(F32)
        lane = lax.broadcasted_iota(jnp.int32, (PROJ_SUB, PROJ_TN), 1)
        first_half = (lane & (HEAD_DIM // 2)) == 0
        for rows, acc in zip(subs, accs):
            c = jnp.concatenate([cos_ref[rows, :] * scale] * (PROJ_TN // 128), axis=1)
            s = jnp.concatenate([sin_ref[rows, :] * scale] * (PROJ_TN // 128), axis=1)
            swapped = jnp.where(first_half,
                                pltpu.roll(acc, PROJ_TN - HEAD_DIM // 2, 1),
                                pltpu.roll(acc, HEAD_DIM // 2, 1))
            o_ref[rows, :] = (acc * c + swapped * s).astype(o_ref.dtype)

    @pl.when((j >= N_ROPE_TILES) & (j < N_MAIN_TILES))
    def _():
        for rows, acc in zip(subs, matmuls()):
            o_ref[rows, :] = acc.astype(o_ref.dtype)

    def matmuls_t():
        return [lax.dot_general(wt_ref[...], x_ref[rows, :], (((1,), (1,)), ((), ())),
                                preferred_element_type=F32) for rows in subs]

    @pl.when(j == N_MAIN_TILES)
    def _():
        per_sub = PROJ_SUB // WINDOW
        for s, acc in enumerate(matmuls_t()):
            for b in range(per_sub):
                vat_ref[s * per_sub + b] = acc[:, b * WINDOW:(b + 1) * WINDOW].astype(vat_ref.dtype)

    @pl.when(j > N_MAIN_TILES)
    def _():
        accs = matmuls_t()
        row = lax.broadcasted_iota(jnp.int32, (VT_ROWS - C_VDIM, DIFF_KB), 0)
        extra = jnp.where(row == 0, 1.0, 0.0).astype(vt_ref.dtype)
        per_kb = DIFF_KB // PROJ_SUB
        for s, acc in enumerate(accs):
            cols = slice((s % per_kb) * PROJ_SUB, (s % per_kb + 1) * PROJ_SUB)
            for hh in range(PROJ_TN // C_VDIM):
                vt_ref[hh, s // per_kb, 0:C_VDIM, cols] = acc[hh * C_VDIM:(hh + 1) * C_VDIM, :].astype(vt_ref.dtype)
        for hh in range(PROJ_TN // C_VDIM):
            for kb in range(tm // DIFF_KB):
                vt_ref[hh, kb, C_VDIM:VT_ROWS, :] = extra


def _proj(hn, w, wt, cos_t, sin_t, *, B, S, tm=1024):
    T, D = hn.shape
    spt = S // tm
    last_main = N_MAIN_TILES - 1
    return pl.pallas_call(
        functools.partial(_proj_kernel, tm=tm),
        grid=(T // tm, IN_COLS // PROJ_TN),
        in_specs=[
            pl.BlockSpec((tm, D), lambda i, j: (i, 0)),
            pl.BlockSpec((D, PROJ_TN), lambda i, j: (0, jnp.minimum(j, last_main))),
            pl.BlockSpec((PROJ_TN, D), lambda i, j: (jnp.maximum(j - N_MAIN_TILES, 0), 0)),
            pl.BlockSpec((tm, 128), lambda i, j: (i, 0)),
            pl.BlockSpec((tm, 128), lambda i, j: (i, 0)),
        ],
        out_specs=[
            pl.BlockSpec((tm, PROJ_TN), lambda i, j: (i, jnp.minimum(j, last_main))),
            pl.BlockSpec((tm // WINDOW, PROJ_TN, WINDOW), lambda i, j: (i, 0, 0)),
            pl.BlockSpec((None, PROJ_TN // C_VDIM, tm // DIFF_KB, VT_ROWS, DIFF_KB),
                         lambda i, j: (i // spt, jnp.maximum(j - N_MAIN_TILES - 1, 0), i % spt, 0, 0)),
        ],
        out_shape=[
            jax.ShapeDtypeStruct((T, COL_VA), BF16),
            jax.ShapeDtypeStruct((T // WINDOW, PROJ_TN, WINDOW), BF16),
            jax.ShapeDtypeStruct((B, C_HEADS, S // DIFF_KB, VT_ROWS, DIFF_KB), BF16),
        ],
        compiler_params=_cparams(("parallel", "arbitrary")),
        name="proj",
    )(hn, w, wt, cos_t, sin_t)


A_KV_WIDTH = A_KV_HEADS * HEAD_DIM
A_COLS = A_HEADS * WINDOW


def _swa_kernel(q_ref, kc_ref, kp_ref, vc_ref, vp_ref, sink_ref, g_ref, o_ref, *, tq):
    i = pl.program_id(1)
    nblk = tq // WINDOW
    kext = jnp.concatenate([kp_ref[...], kc_ref[...]], axis=0)
    lane_kv = lax.broadcasted_iota(jnp.int32, (WINDOW, A_KV_WIDTH), 1) // HEAD_DIM
    zero = jnp.zeros((WINDOW, A_KV_WIDTH), BF16)

    scores = []
    for b in range(nblk):
        parts = []
        for g in range(A_GROUP):
            qg = q_ref[b * WINDOW:(b + 1) * WINDOW, g * A_KV_WIDTH:(g + 1) * A_KV_WIDTH]
            parts += [jnp.where(lane_kv == j, qg, zero) for j in range(A_KV_HEADS)]
        qcat = jnp.concatenate(parts, axis=0)
        scores.append(lax.dot_general(kext[b * WINDOW:(b + 2) * WINDOW, :], qcat, (((1,), (1,)), ((), ())),
                                      preferred_element_type=F32))

    r = lax.broadcasted_iota(jnp.int32, (2 * WINDOW, A_COLS), 0)
    c = lax.broadcasted_iota(jnp.int32, (2 * WINDOW, A_COLS), 1) & (WINDOW - 1)
    band = (r > c) & (r <= c + WINDOW)
    sink = sink_ref[...]
    gain = g_ref[...]
    for b, st in enumerate(scores):
        mask = band
        if b == 0:
            mask = band & ((r >= WINDOW) | (i > 0))
        st = jnp.where(mask, st, NEG)
        m = jnp.maximum(jnp.max(st, axis=0, keepdims=True), sink)
        p = jnp.exp2(st - m)
        l = jnp.sum(p, axis=0, keepdims=True) + jnp.exp2(sink - m)
        vt = jnp.concatenate([vp_ref[0] if b == 0 else vc_ref[b - 1], vc_ref[b]], axis=1)
        pv = jnp.dot(vt, p.astype(BF16), preferred_element_type=F32) / l
        heads = []
        for g in range(A_GROUP):
            for j in range(A_KV_HEADS):
                cb = g * A_KV_HEADS + j
                heads.append(pv[j * HEAD_DIM:(j + 1) * HEAD_DIM, cb * WINDOW:(cb + 1) * WINDOW])
        ot = jnp.concatenate(heads, axis=0)
        ms = jnp.mean(ot * ot, axis=0, keepdims=True)
        y = (ot * lax.rsqrt(ms + NORM_EPS)).T * gain
        o_ref[b * WINDOW:(b + 1) * WINDOW, :] = y.astype(o_ref.dtype)


def _swa(proj, vat, sink_row, g, *, B, S, tq=512):
    nq = S // tq
    rpb = tq // WINDOW

    def prev_blk(b, i):
        return jnp.maximum(b * (S // WINDOW) + i * rpb - 1, 0)

    return pl.pallas_call(
        functools.partial(_swa_kernel, tq=tq),
        grid=(B, nq),
        in_specs=[
            pl.BlockSpec((tq, A_WIDTH), lambda b, i: (b * nq + i, COL_QA // A_WIDTH)),
            pl.BlockSpec((tq, A_KV_WIDTH), lambda b, i: (b * nq + i, COL_KA // A_KV_WIDTH)),
            pl.BlockSpec((WINDOW, A_KV_WIDTH), lambda b, i: (prev_blk(b, i), COL_KA // A_KV_WIDTH)),
            pl.BlockSpec((rpb, A_KV_WIDTH, WINDOW), lambda b, i: (b * nq + i, 0, 0)),
            pl.BlockSpec((1, A_KV_WIDTH, WINDOW), lambda b, i: (prev_blk(b, i), 0, 0)),
            pl.BlockSpec((1, A_COLS), lambda b, i: (0, 0)),
            pl.BlockSpec((1, A_WIDTH), lambda b, i: (0, 0)),
        ],
        out_specs=pl.BlockSpec((tq, A_WIDTH), lambda b, i: (b * nq + i, 0)),
        out_shape=jax.ShapeDtypeStruct((B * S, A_WIDTH), BF16),
        compiler_params=_cparams(("parallel", "parallel")),
        name="swa",
    )(proj, proj, proj, vat, vat, sink_row, g.reshape(1, A_WIDTH))


POOL_HALO = 16


def _pool_kernel(u0_ref, u1_ref, h0_ref, h1_ref, w_ref, s_ref, o_ref, *, tq):
    i = pl.program_id(1)
    u = jnp.concatenate([u0_ref[...], u1_ref[...]], axis=1).astype(F32)
    halo = jnp.concatenate([h0_ref[...], h1_ref[...]], axis=1).astype(F32)
    halo = halo * (i > 0).astype(F32)
    ext = jnp.concatenate([halo, u], axis=0)
    t = i * tq + lax.broadcasted_iota(jnp.int32, (tq, 1), 0)
    for gi, w in enumerate(POOL_WINDOWS):
        sl = slice(gi * B_GROUP_DIM, (gi + 1) * B_GROUP_DIM)
        s = ext[:, sl]
        shift = 1
        while shift < w:
            s = s + pltpu.roll(s, shift, 0)
            shift *= 2
        cnt = jnp.minimum(t + 1, w).astype(F32)
        y = (s[POOL_HALO:, :] / cnt - u[:, sl]).astype(BF16)
        o = jnp.dot(y, w_ref[gi], preferred_element_type=F32) * s_ref[:, sl]
        o_ref[:, sl] = o.astype(o_ref.dtype)


def _pool(proj, w_pool, scale, *, B, S, tq=256):
    nq = S // tq
    c0 = COL_UB // 256

    def cur(c):
        return pl.BlockSpec((tq, 256), lambda b, i: (b * nq + i, c))

    def halo(c):
        return pl.BlockSpec(
            (POOL_HALO, 256),
            lambda b, i: (jnp.maximum((b * S + i * tq) // POOL_HALO - 1, 0), c))

    return pl.pallas_call(
        functools.partial(_pool_kernel, tq=tq),
        grid=(B, nq),
        in_specs=[cur(c0), cur(c0 + 1), halo(c0), halo(c0 + 1),
                  pl.BlockSpec((len(POOL_WINDOWS), B_GROUP_DIM, B_GROUP_DIM), lambda b, i: (0, 0, 0)),
                  pl.BlockSpec((1, B_WIDTH), lambda b, i: (0, 0))],
        out_specs=pl.BlockSpec((tq, B_WIDTH), lambda b, i: (b * nq + i, 0)),
        out_shape=jax.ShapeDtypeStruct((B * S, B_WIDTH), BF16),
        compiler_params=_cparams(("parallel", "parallel")),
        name="pool",
    )(proj, proj, proj, proj, w_pool, scale.reshape(1, B_WIDTH))


def _diff_kernel(q_ref, k_ref, vt_ref, lam_ref, g_ref, o_ref, qcat_ref, s0_ref, s1_ref, m_ref, acc_ref,
                 *, tq, lam_init):
    qi = pl.program_id(2)
    q = q_ref[...]
    lane = lax.broadcasted_iota(jnp.int32, q.shape, 1)
    zero = jnp.zeros_like(q)
    qcat_ref[0:tq, :] = jnp.where(lane < HEAD_DIM, q, zero)
    qcat_ref[tq:2 * tq, :] = jnp.where(lane >= HEAD_DIM, q, zero)

    m_ref[...] = jnp.full(m_ref.shape, NEG, F32)
    acc_ref[...] = jnp.zeros(acc_ref.shape, F32)

    def scores(kidx):
        off = pl.multiple_of(kidx * DIFF_KB, DIFF_KB)
        return lax.dot_general(k_ref[pl.ds(off, DIFF_KB), :], qcat_ref[...], (((1,), (1,)), ((), ())),
                               preferred_element_type=F32)

    def consume(s_ref, kidx, diagonal):
        st = s_ref[...]
        if diagonal:
            r = lax.broadcasted_iota(jnp.int32, st.shape, 0)
            c = lax.broadcasted_iota(jnp.int32, st.shape, 1)
            c = jnp.where(c >= tq, c - tq, c)
            st = jnp.where(r <= c, st, NEG)
        m = m_ref[...]
        m_new = jnp.maximum(m, jnp.max(st, axis=0, keepdims=True))
        alpha = jnp.exp2(m - m_new)
        p = jnp.exp2(st - m_new)
        m_ref[...] = m_new
        pv = jnp.dot(vt_ref[kidx], p.astype(BF16), preferred_element_type=F32)
        acc_ref[...] = alpha * acc_ref[...] + pv

    def step(cur_ref, nxt_ref, kidx):
        s_next = scores(kidx + 1)
        consume(cur_ref, kidx, False)
        nxt_ref[...] = s_next

    s0_ref[...] = scores(0)

    def body(c, carry):
        step(s0_ref, s1_ref, 2 * c)
        step(s1_ref, s0_ref, 2 * c + 1)
        return carry

    lax.fori_loop(0, qi // 2, body, 0)

    @pl.when(qi % 2 == 1)
    def _():
        step(s0_ref, s1_ref, qi - 1)
        consume(s1_ref, qi, True)

    @pl.when(qi % 2 == 0)
    def _():
        consume(s0_ref, qi, True)

    lv = lam_ref[...]
    lam = (jnp.exp(jnp.sum(lv[0:1] * lv[1:2], axis=1, keepdims=True))
           - jnp.exp(jnp.sum(lv[2:3] * lv[3:4], axis=1, keepdims=True)) + lam_init)
    l = acc_ref[C_VDIM:C_VDIM + 1, :]
    acc = acc_ref[0:C_VDIM, :]
    ot = acc[:, :tq] / l[:, :tq] - lam * (acc[:, tq:] / l[:, tq:])
    ms = jnp.mean(ot * ot, axis=0, keepdims=True)
    yt = ot * lax.rsqrt(ms + DIFF_EPS) * (g_ref[...] * (1.0 - lam_init))
    o_ref[...] = yt.T.astype(o_ref.dtype)


def _diff(proj, vt, lam_vecs, subln, *, B, S, lam_init, tq=512):
    nq = S // tq
    nkb = S // DIFF_KB
    assert tq == DIFF_KB
    return pl.pallas_call(
        functools.partial(_diff_kernel, tq=tq, lam_init=lam_init),
        grid=(B, C_HEADS, nq),
        in_specs=[
            pl.BlockSpec((tq, C_VDIM), lambda b, h, i: (b * nq + i, COL_QC // C_VDIM + h)),
            pl.BlockSpec((S, C_VDIM), lambda b, h, i: (b, COL_KC // C_VDIM + h)),
            pl.BlockSpec((None, None, nkb, VT_ROWS, DIFF_KB), lambda b, h, i: (b, h, 0, 0, 0)),
            pl.BlockSpec((4, HEAD_DIM), lambda b, h, i: (0, 0)),
            pl.BlockSpec((C_VDIM, 1), lambda b, h, i: (0, 0)),
        ],
        out_specs=pl.BlockSpec((tq, C_VDIM), lambda b, h, i: (b * nq + i, h)),
        out_shape=jax.ShapeDtypeStruct((B * S, C_WIDTH), BF16),
        scratch_shapes=[pltpu.VMEM((2 * tq, C_VDIM), BF16),
                        pltpu.VMEM((DIFF_KB, 2 * tq), F32), pltpu.VMEM((DIFF_KB, 2 * tq), F32),
                        pltpu.VMEM((1, 2 * tq), F32), pltpu.VMEM((VT_ROWS, 2 * tq), F32)],
        compiler_params=_cparams(("parallel", "parallel", "arbitrary")),
        name="diff",
    )(proj, proj, vt, lam_vecs, subln.reshape(C_VDIM, 1))


OUT_SUB = 256


def _out_kernel(oa_ref, ob_ref, oc_ref, w_ref, h_ref, g_ref, h1_ref, hn_ref, *, tm):
    subs = [slice(s * OUT_SUB, (s + 1) * OUT_SUB) for s in range(tm // OUT_SUB)]
    accs = []
    for rows in subs:
        acc = jnp.dot(oa_ref[rows, :], w_ref[0:A_WIDTH, :], preferred_element_type=F32)
        acc = acc + jnp.dot(ob_ref[rows, :], w_ref[A_WIDTH:A_WIDTH + B_WIDTH, :], preferred_element_type=F32)
        acc = acc + jnp.dot(oc_ref[rows, :], w_ref[A_WIDTH + B_WIDTH:, :], preferred_element_type=F32)
        accs.append(acc)
    for rows, acc in zip(subs, accs):
        h1 = h_ref[rows, :] + acc
        h1_ref[rows, :] = h1
        ms = jnp.mean(h1 * h1, axis=1, keepdims=True)
        hn = h1 * lax.rsqrt(ms + NORM_EPS) * g_ref[...]
        hn_ref[rows, :] = _pack_rows(hn)


def _out(oa, ob, oc, w, h, g, *, tm=512):
    T, D = h.shape
    return pl.pallas_call(
        functools.partial(_out_kernel, tm=tm),
        grid=(T // tm,),
        in_specs=[
            pl.BlockSpec((tm, A_WIDTH), lambda i: (i, 0)),
            pl.BlockSpec((tm, B_WIDTH), lambda i: (i, 0)),
            pl.BlockSpec((tm, C_WIDTH), lambda i: (i, 0)),
            pl.BlockSpec((D, D), lambda i: (0, 0)),
            pl.BlockSpec((tm, D), lambda i: (i, 0)),
            pl.BlockSpec((1, D), lambda i: (0, 0)),
        ],
        out_specs=[
            pl.BlockSpec((tm, D), lambda i: (i, 0)),
            pl.BlockSpec((tm, D_PACK), lambda i: (i, 0)),
        ],
        out_shape=[
            jax.ShapeDtypeStruct((T, D), F32),
            jax.ShapeDtypeStruct((T, D_PACK), U32),
        ],
        compiler_params=_cparams(("parallel",)),
        name="out",
    )(oa, ob, oc, w, h, g.reshape(1, D))


ROUTER_ROWS = 128
ROUTER_LO = 64
ROUTER_GROUP_ROW = 0
ROUTER_EXPERT_ROW = 8


def _router_kernel(x_ref, wt_ref, b_ref, e_ref, g_ref):
    x = _unpack_rows_bf16(x_ref[...])
    lg = lax.dot_general(wt_ref[...], x, (((1,), (1,)), ((), ())), preferred_element_type=F32)
    lg = lg[0:ROUTER_LO] + lg[ROUTER_LO:ROUTER_ROWS] + b_ref[...]
    grp = lg[ROUTER_GROUP_ROW:ROUTER_GROUP_ROW + N_GROUPS]
    ex = lg[ROUTER_EXPERT_ROW:ROUTER_EXPERT_ROW + N_EXPERTS]

    mg = jnp.max(grp, axis=0, keepdims=True)
    pg_sel = 1.0 / jnp.sum(jnp.exp(grp - mg), axis=0, keepdims=True)
    gi = lax.broadcasted_iota(jnp.int32, grp.shape, 0)
    g_sel = jnp.min(jnp.where(grp == mg, gi, N_GROUPS), axis=0, keepdims=True)

    er = lax.broadcasted_iota(jnp.int32, ex.shape, 0)
    group_of = lax.shift_right_logical(er, EXPERTS_PER_GROUP.bit_length() - 1)
    cand = jnp.where(group_of == g_sel, ex, NEG)
    v1 = jnp.max(cand, axis=0, keepdims=True)
    i1 = jnp.min(jnp.where(cand == v1, er, N_EXPERTS), axis=0, keepdims=True)
    rest = jnp.where(er == i1, NEG, cand)
    v2 = jnp.max(rest, axis=0, keepdims=True)
    i2 = jnp.min(jnp.where(rest == v2, er, N_EXPERTS), axis=0, keepdims=True)
    t = jnp.exp(v2 - v1)
    g1 = pg_sel / (1.0 + t)
    e_ref[...] = jnp.concatenate([i1, i2], axis=0)
    g_ref[...] = jnp.concatenate([g1, g1 * t], axis=0)


def _router(hn_packed, wt, b, *, tm=1024):
    T = hn_packed.shape[0]
    D = wt.shape[1]
    return pl.pallas_call(
        _router_kernel,
        grid=(T // tm,),
        in_specs=[
            pl.BlockSpec((tm, D_PACK), lambda i: (i, 0)),
            pl.BlockSpec((ROUTER_ROWS, D), lambda i: (0, 0)),
            pl.BlockSpec((ROUTER_LO, 1), lambda i: (0, 0)),
        ],
        out_specs=[pl.BlockSpec((TOP_K, tm), lambda i: (0, i)), pl.BlockSpec((TOP_K, tm), lambda i: (0, i))],
        out_shape=[jax.ShapeDtypeStruct((TOP_K, T), jnp.int32), jax.ShapeDtypeStruct((TOP_K, T), F32)],
        compiler_params=_cparams(("parallel",)),
        name="router",
    )(hn_packed, wt, b)


EXPERT_SUB = 256
EXPERT_VMEM = 56 * 1024 * 1024


def _expert_kernel(blk_ref, exp_ref, lo_ref, hi_ref, x_ref, g_ref, wg_ref, wu_ref, wd_ref, o_ref,
                   wg_bf, wu_bf, wd_bf, *, tm):
    p = pl.program_id(0)
    prev = jnp.maximum(p - 1, 0)
    blk = blk_ref[p]
    first = (p == 0) | (blk != blk_ref[prev])
    new_expert = (p == 0) | (exp_ref[p] != exp_ref[prev])
    lo = lo_ref[p]
    hi = hi_ref[p]

    @pl.when(new_expert)
    def _():
        wg_bf[...] = wg_ref[...].astype(BF16)
        wu_bf[...] = wu_ref[...].astype(BF16)
        wd_bf[...] = wd_ref[...].astype(BF16)

    @pl.when(first)
    def _():
        o_ref[...] = jnp.zeros_like(o_ref)

    for s in range(tm // EXPERT_SUB):
        row0 = blk * tm + s * EXPERT_SUB
        rows = slice(s * EXPERT_SUB, (s + 1) * EXPERT_SUB)

        @pl.when((hi > row0) & (lo < row0 + EXPERT_SUB))
        def _(row0=row0, rows=rows):
            x = _unpack_rows_bf16(x_ref[rows, :])
            a = jnp.dot(x, wg_bf[...], preferred_element_type=F32)
            u = jnp.dot(x, wu_bf[...], preferred_element_type=F32)
            act = (a / (1.0 + jnp.exp(-a))) * u
            y = jnp.dot(act.astype(BF16), wd_bf[...], preferred_element_type=F32)
            r = row0 + lax.broadcasted_iota(jnp.int32, (EXPERT_SUB, 1), 0)
            mine = (r >= lo) & (r < hi)
            o_ref[rows, :] = jnp.where(mine, _pack_rows(y * g_ref[rows, :]), o_ref[rows, :])


def _experts(seg_blk, seg_exp, seg_lo, seg_hi, xs, gates, wg, wu, wd, layer, *, tm):
    N = xs.shape[0]
    D = wg.shape[2]
    P = seg_blk.shape[0]
    grid_spec = pltpu.PrefetchScalarGridSpec(
        num_scalar_prefetch=4,
        grid=(P,),
        in_specs=[
            pl.BlockSpec((tm, D_PACK), lambda p, blk, ex, lo, hi: (blk[p], 0)),
            pl.BlockSpec((tm, 1), lambda p, blk, ex, lo, hi: (blk[p], 0)),
            pl.BlockSpec((None, None, D, D_FF), lambda p, blk, ex, lo, hi: (layer, ex[p], 0, 0)),
            pl.BlockSpec((None, None, D, D_FF), lambda p, blk, ex, lo, hi: (layer, ex[p], 0, 0)),
            pl.BlockSpec((None, None, D_FF, D), lambda p, blk, ex, lo, hi: (layer, ex[p], 0, 0)),
        ],
        out_specs=pl.BlockSpec((tm, D_PACK), lambda p, blk, ex, lo, hi: (blk[p], 0)),
        scratch_shapes=[pltpu.VMEM((D, D_FF), BF16), pltpu.VMEM((D, D_FF), BF16), pltpu.VMEM((D_FF, D), BF16)],
    )
    return pl.pallas_call(
        functools.partial(_expert_kernel, tm=tm),
        grid_spec=grid_spec,
        out_shape=jax.ShapeDtypeStruct((N, D_PACK), U32),
        compiler_params=_cparams(("arbitrary",), EXPERT_VMEM),
        name="experts",
    )(seg_blk, seg_exp, seg_lo, seg_hi, xs, gates, wg, wu, wd)


def _route(eidx_t, gates_t, T, tm):
    N = T * TOP_K
    flat_e = eidx_t.T.reshape(N)
    iota = jnp.arange(N, dtype=jnp.int32)
    sorted_e, order, g_sorted = lax.sort((flat_e, iota, gates_t.T.reshape(N)), num_keys=1, is_stable=True)
    tok = order // TOP_K
    _, pos = lax.sort((order, iota), num_keys=1)
    counts = jnp.bincount(flat_e, length=N_EXPERTS)
    starts = (jnp.cumsum(counts) - counts).astype(jnp.int32)
    nb = N // tm
    bounds = jnp.sort(jnp.concatenate([jnp.arange(nb, dtype=jnp.int32) * tm, starts]))
    ends = jnp.concatenate([bounds[1:], jnp.full((1,), N, bounds.dtype)])
    seg_blk = jnp.minimum(bounds // tm, nb - 1).astype(jnp.int32)
    seg_exp = sorted_e[jnp.minimum(bounds, N - 1)].astype(jnp.int32)
    pos_rows = pos.reshape(T, TOP_K).T.reshape(N)
    return tok, g_sorted, pos_rows, seg_blk, seg_exp, bounds, ends


def _group_major(a, axis):
    shape = a.shape
    a = a.reshape(shape[:axis] + (A_KV_HEADS, A_GROUP, HEAD_DIM) + shape[axis + 1:])
    return jnp.swapaxes(a, axis, axis + 1).reshape(shape)


def _permute_in_proj(w):
    bounds = np.cumsum([0, 768, 256, 256, 512, 768, 768, 768])
    qa, ka, va, ub, qc, kc, vc = [w[:, bounds[s]:bounds[s + 1]] for s in range(7)]
    return jnp.concatenate([_group_major(qa, 1), qc, ka, kc, ub, va, vc], axis=1)


def kernel(x, positions, attn_norm, w_in, sinks, branch_norm_a, w_pool, pool_scale, lambda_q1, lambda_k1,
           lambda_q2, lambda_k2, subln, w_out, ffn_norm, w_router_group, b_router_group, w_router_expert,
           b_router_expert, w_expert_gate, w_expert_up, w_expert_down, final_norm):
    B, S, D = x.shape
    T = B * S
    depth = w_in.shape[0]
    tm_e = 512

    half = HEAD_DIM // 2
    inv = ROPE_THETA ** (-jnp.arange(half, dtype=F32) / half)
    ang = positions.astype(F32).reshape(T, 1) * inv
    cos, sin = jnp.cos(ang), jnp.sin(ang)
    cos_t = jnp.concatenate([cos, cos, cos, cos], axis=1)
    sin_t = jnp.concatenate([-sin, sin, -sin, sin], axis=1)

    h = x.reshape(T, D)
    y_rows = None
    for l in range(depth):
        if l == 0:
            (hn,) = _norm(h, None, attn_norm[l], emit_h=False, emit_bf16=True)
        else:
            h, hn = _norm(h, y_rows, attn_norm[l], emit_h=True, emit_bf16=True)
        w_in_l = _permute_in_proj(w_in[l]).astype(BF16)
        proj, vat, vt = _proj(hn, w_in_l[:, :COL_VA], w_in_l[:, COL_VA:].T, cos_t, sin_t, B=B, S=S)

        sink_heads = sinks[l].reshape(A_KV_HEADS, A_GROUP).T.reshape(A_HEADS) * LOG2E
        sink_row = jnp.repeat(sink_heads, WINDOW).reshape(1, A_COLS)
        oa = _swa(proj, vat, sink_row, _group_major(branch_norm_a[l], 0), B=B, S=S)
        ob = _pool(proj, w_pool[l].astype(BF16), pool_scale[l], B=B, S=S)

        lam_init = 0.8 - 0.6 * math.exp(-0.3 * l)
        lam_vecs = jnp.stack([lambda_q1[l], lambda_k1[l], lambda_q2[l], lambda_k2[l]]).astype(F32)
        oc = _diff(proj, vt, lam_vecs, subln[l], B=B, S=S, lam_init=lam_init)

        w_out_l = jnp.concatenate([_group_major(w_out[l][:A_WIDTH], 0), w_out[l][A_WIDTH:]], axis=0)
        h1, hn2 = _out(oa, ob, oc, w_out_l.astype(BF16), h, ffn_norm[l])

        wr = jnp.zeros((ROUTER_LO, D), F32)
        wr = wr.at[ROUTER_GROUP_ROW:ROUTER_GROUP_ROW + N_GROUPS].set(w_router_group[l].T)
        wr = wr.at[ROUTER_EXPERT_ROW:ROUTER_EXPERT_ROW + N_EXPERTS].set(w_router_expert[l].T)
        wr_hi = wr.astype(BF16)
        wr_lo = (wr - wr_hi.astype(F32)).astype(BF16)
        br = jnp.zeros((ROUTER_LO, 1), F32)
        br = br.at[ROUTER_GROUP_ROW:ROUTER_GROUP_ROW + N_GROUPS, 0].set(b_router_group[l])
        br = br.at[ROUTER_EXPERT_ROW:ROUTER_EXPERT_ROW + N_EXPERTS, 0].set(b_router_expert[l])
        eidx_t, gates_t = _router(hn2, jnp.concatenate([wr_hi, wr_lo], axis=0), br)

        tok, g_sorted, pos_rows, seg_blk, seg_exp, seg_lo, seg_hi = _route(eidx_t, gates_t, T, tm_e)
        xs = _sc_gather_rows(hn2, tok)
        ys = _experts(seg_blk, seg_exp, seg_lo, seg_hi, xs, g_sorted.reshape(-1, 1),
                      w_expert_gate, w_expert_up, w_expert_down, l, tm=tm_e)
        y_rows = _sc_gather_rows(ys, pos_rows)
        h = h1

    (out,) = _norm(h, y_rows, final_norm, emit_h=False, emit_bf16=False)
    return out.reshape(B, S, D)
```

```python
import functools
import math

import jax
import jax.numpy as jnp
from jax import lax
from jax.experimental import pallas as pl
from jax.experimental.pallas import tpu as pltpu
from jax.experimental.pallas import tpu_sc as plsc
import numpy as np

F32 = jnp.float32
BF16 = jnp.bfloat16

D_MODEL = 2048
HEAD_DIM = 64
ROPE_THETA = 10000.0
A_HEADS = 12
A_KV_HEADS = 4
A_GROUP = 3
A_WIDTH = 768
WINDOW = 128
POOL_WINDOWS = (2, 4, 8, 16)
B_WIDTH = 512
B_GROUP_DIM = 128
C_VDIM = 128
C_WIDTH = 768
C_HEADS = 6
DIFF_EPS = 1e-5
IN_COLS = 4096
N_GROUPS = 4
EXPERTS_PER_GROUP = 8
N_EXPERTS = 32
TOP_K = 2
D_FF = 512
NORM_EPS = 1e-6
NEG = -1e30
LOG2E = math.log2(math.e)
Q_SCALE = HEAD_DIM ** -0.5 * LOG2E

COL_QA, COL_QC, COL_KA, COL_KC, COL_UB, COL_VA, COL_VC = 0, 768, 1536, 1792, 2560, 3072, 3328
PROJ_TN = 256
PROJ_SUB = 256
N_SCALED_TILES = COL_KA // PROJ_TN
N_ROPE_TILES = COL_UB // PROJ_TN
N_MAIN_TILES = COL_VA // PROJ_TN
DIFF_KB = 512
VT_ROWS = C_VDIM + 16

VMEM_LIMIT = 48 * 1024 * 1024
D_PACK = D_MODEL // 2
U32 = jnp.uint32


def _cparams(sem, vmem=VMEM_LIMIT):
    return pltpu.CompilerParams(dimension_semantics=sem, vmem_limit_bytes=vmem)


def _pack_rows(x):
    lo = pltpu.bitcast(x[:, :D_PACK].astype(BF16).astype(F32), U32) >> 16
    hi = pltpu.bitcast(x[:, D_PACK:].astype(BF16).astype(F32), U32)
    return hi | lo


def _unpack_rows(w):
    return pltpu.bitcast(w << 16, F32), pltpu.bitcast(w & U32(0xFFFF0000), F32)


def _unpack_rows_bf16(w):
    lo, hi = _unpack_rows(w)
    return jnp.concatenate([lo.astype(BF16), hi.astype(BF16)], axis=1)


SC_WINDOW = 32
SC_INDEX_LANES = 128


def _sc_gather_rows(x, idx):
    M = idx.shape[0]
    W = x.shape[1]
    idx2 = jnp.pad(idx.reshape(M // SC_WINDOW, SC_WINDOW), ((0, 0), (0, SC_INDEX_LANES - SC_WINDOW)))
    mesh = plsc.VectorSubcoreMesh(core_axis_name="core", subcore_axis_name="subcore")

    @pl.kernel(out_type=jax.ShapeDtypeStruct((M, W), x.dtype), mesh=mesh)
    def gather(x_hbm, i_hbm, o_hbm):
        def body(i_vmem, o_vmem):
            pltpu.sync_copy(x_hbm.at[i_vmem.at[0, pl.ds(0, SC_WINDOW)]], o_vmem)

        pltpu.emit_pipeline(
            body,
            grid=(M // SC_WINDOW,),
            in_specs=[pl.BlockSpec((1, SC_INDEX_LANES), index_map=lambda i: (i, 0))],
            out_specs=[pl.BlockSpec((SC_WINDOW, W), index_map=lambda i: (i, 0))],
            core_axis_name=("core", "subcore"),
            dimension_semantics=(pltpu.PARALLEL,),
        )(i_hbm, o_hbm)

    return gather(x, idx2)


def _norm_kernel(*refs, n_add, emit_h, emit_bf16, eps):
    h_ref = refs[0]
    add_refs = refs[1:1 + n_add]
    g_ref = refs[1 + n_add]
    outs = refs[2 + n_add:]
    h = h_ref[...]
    if n_add:
        lo = jnp.zeros((h.shape[0], D_PACK), F32)
        hi = jnp.zeros((h.shape[0], D_PACK), F32)
        for r in add_refs:
            a, b = _unpack_rows(r[...])
            lo, hi = lo + a, hi + b
        h = h + jnp.concatenate([lo, hi], axis=1)
    ms = jnp.mean(h * h, axis=-1, keepdims=True)
    y = h * lax.rsqrt(ms + eps) * g_ref[...]
    k = 0
    if emit_h:
        outs[k][...] = h
        k += 1
    outs[k][...] = y.astype(BF16 if emit_bf16 else F32)


def _norm(h, y_rows, g, *, emit_h, emit_bf16, tm=512):
    T, D = h.shape
    nt = T // tm
    row = pl.BlockSpec((tm, D), lambda i: (i, 0))
    n_add = 0 if y_rows is None else TOP_K
    add_specs = [pl.BlockSpec((tm, D_PACK), lambda i, k=k: (i + k * nt, 0)) for k in range(n_add)]
    out_shape, out_specs = [], []
    if emit_h:
        out_shape.append(jax.ShapeDtypeStruct((T, D), F32))
        out_specs.append(row)
    out_shape.append(jax.ShapeDtypeStruct((T, D), BF16 if emit_bf16 else F32))
    out_specs.append(row)
    return pl.pallas_call(
        functools.partial(_norm_kernel, n_add=n_add, emit_h=emit_h, emit_bf16=emit_bf16, eps=NORM_EPS),
        grid=(nt,),
        in_specs=[row] + add_specs + [pl.BlockSpec((1, D), lambda i: (0, 0))],
        out_specs=out_specs,
        out_shape=out_shape,
        compiler_params=_cparams(("parallel",)),
        name="norm",
    )(h, *([y_rows] * n_add), g.reshape(1, D))


def _proj_kernel(x_ref, w_ref, wt_ref, cos_ref, sin_ref, o_ref, vat_ref, vt_ref, *, tm):
    j = pl.program_id(1)
    subs = [slice(s * PROJ_SUB, (s + 1) * PROJ_SUB) for s in range(tm // PROJ_SUB)]

    def matmuls():
        return [jnp.dot(x_ref[rows, :], w_ref[...], preferred_element_type=F32) for rows in subs]

    @pl.when(j < N_ROPE_TILES)
    def _():
        accs = matmuls()
        scale = jnp.where(j < N_SCALED_TILES, Q_SCALE, 1.0).astype(F32)
        lane = lax.broadcasted_iota(jnp.int32, (PROJ_SUB, PROJ_TN), 1)
        first_half = (lane & (HEAD_DIM // 2)) == 0
        for rows, acc in zip(subs, accs):
            c = jnp.concatenate([cos_ref[rows, :] * scale] * (PROJ_TN // 128), axis=1)
            s = jnp.concatenate([sin_ref[rows, :] * scale] * (PROJ_TN // 128), axis=1)
            swapped = jnp.where(first_half,
                                pltpu.roll(acc, PROJ_TN - HEAD_DIM // 2, 1),
                                pltpu.roll(acc, HEAD_DIM // 2, 1))
            o_ref[rows, :] = (acc * c + swapped * s).astype(o_ref.dtype)

    @pl.when((j >= N_ROPE_TILES) & (j < N_MAIN_TILES))
    def _():
        for rows, acc in zip(subs, matmuls()):
            o_ref[rows, :] = acc.astype(o_ref.dtype)

    def matmuls_t():
        return [lax.dot_general(wt_ref[...], x_ref[rows, :], (((1,), (1,)), ((), ())),
                                preferred_element_type=F32) for rows in subs]

    @pl.when(j == N_MAIN_TILES)
    def _():
        per_sub = PROJ_SUB // WINDOW
        for s, acc in enumerate(matmuls_t()):
            for b in range(per_sub):
                vat_ref[s * per_sub + b] = acc[:, b * WINDOW:(b + 1) * WINDOW].astype(vat_ref.dtype)

    @pl.when(j > N_MAIN_TILES)
    def _():
        accs = matmuls_t()
        row = lax.broadcasted_iota(jnp.int32, (VT_ROWS - C_VDIM, DIFF_KB), 0)
        extra = jnp.where(row == 0, 1.0, 0.0).astype(vt_ref.dtype)
        per_kb = DIFF_KB // PROJ_SUB
        for s, acc in enumerate(accs):
            cols = slice((s % per_kb) * PROJ_SUB, (s % per_kb + 1) * PROJ_SUB)
            for hh in range(PROJ_TN // C_VDIM):
                vt_ref[hh, s // per_kb, 0:C_VDIM, cols] = acc[hh * C_VDIM:(hh + 1) * C_VDIM, :].astype(vt_ref.dtype)
        for hh in range(PROJ_TN // C_VDIM):
            for kb in range(tm // DIFF_KB):
                vt_ref[hh, kb, C_VDIM:VT_ROWS, :] = extra


def _proj(hn, w, wt, cos_t, sin_t, *, B, S, tm=2048):
    T, D = hn.shape
    spt = S // tm
    last_main = N_MAIN_TILES - 1
    return pl.pallas_call(
        functools.partial(_proj_kernel, tm=tm),
        grid=(T // tm, IN_COLS // PROJ_TN),
        in_specs=[
            pl.BlockSpec((tm, D), lambda i, j: (i, 0)),
            pl.BlockSpec((D, PROJ_TN), lambda i, j: (0, jnp.minimum(j, last_main))),
            pl.BlockSpec((PROJ_TN, D), lambda i, j: (jnp.maximum(j - N_MAIN_TILES, 0), 0)),
            pl.BlockSpec((tm, 128), lambda i, j: (i, 0)),
            pl.BlockSpec((tm, 128), lambda i, j: (i, 0)),
        ],
        out_specs=[
            pl.BlockSpec((tm, PROJ_TN), lambda i, j: (i, jnp.minimum(j, last_main))),
            pl.BlockSpec((tm // WINDOW, PROJ_TN, WINDOW), lambda i, j: (i, 0, 0)),
            pl.BlockSpec((None, PROJ_TN // C_VDIM, tm // DIFF_KB, VT_ROWS, DIFF_KB),
                         lambda i, j: (i // spt, jnp.maximum(j - N_MAIN_TILES - 1, 0), i % spt, 0, 0)),
        ],
        out_shape=[
            jax.ShapeDtypeStruct((T, COL_VA), BF16),
            jax.ShapeDtypeStruct((T // WINDOW, PROJ_TN, WINDOW), BF16),
            jax.ShapeDtypeStruct((B, C_HEADS, S // DIFF_KB, VT_ROWS, DIFF_KB), BF16),
        ],
        compiler_params=_cparams(("parallel", "arbitrary")),
        name="proj",
    )(hn, w, wt, cos_t, sin_t)


A_KV_WIDTH = A_KV_HEADS * HEAD_DIM
A_COLS = A_HEADS * WINDOW


def _swa_kernel(q_ref, kc_ref, kp_ref, vc_ref, vp_ref, sink_ref, g_ref, o_ref, *, tq):
    i = pl.program_id(1)
    nblk = tq // WINDOW
    kext = jnp.concatenate([kp_ref[...], kc_ref[...]], axis=0)
    lane_kv = lax.broadcasted_iota(jnp.int32, (WINDOW, A_KV_WIDTH), 1) // HEAD_DIM
    zero = jnp.zeros((WINDOW, A_KV_WIDTH), BF16)

    scores = []
    for b in range(nblk):
        parts = []
        for g in range(A_GROUP):
            qg = q_ref[b * WINDOW:(b + 1) * WINDOW, g * A_KV_WIDTH:(g + 1) * A_KV_WIDTH]
            parts += [jnp.where(lane_kv == j, qg, zero) for j in range(A_KV_HEADS)]
        qcat = jnp.concatenate(parts, axis=0)
        scores.append(lax.dot_general(kext[b * WINDOW:(b + 2) * WINDOW, :], qcat, (((1,), (1,)), ((), ())),
                                      preferred_element_type=F32))

    r = lax.broadcasted_iota(jnp.int32, (2 * WINDOW, A_COLS), 0)
    c = lax.broadcasted_iota(jnp.int32, (2 * WINDOW, A_COLS), 1) & (WINDOW - 1)
    band = (r > c) & (r <= c + WINDOW)
    sink = sink_ref[...]
    gain = g_ref[...]
    for b, st in enumerate(scores):
        mask = band
        if b == 0:
            mask = band & ((r >= WINDOW) | (i > 0))
        st = jnp.where(mask, st, NEG)
        m = jnp.maximum(jnp.max(st, axis=0, keepdims=True), sink)
        p = jnp.exp2(st - m)
        l = jnp.sum(p, axis=0, keepdims=True) + jnp.exp2(sink - m)
        vt = jnp.concatenate([vp_ref[0] if b == 0 else vc_ref[b - 1], vc_ref[b]], axis=1)
        pv = jnp.dot(vt, p.astype(BF16), preferred_element_type=F32) / l
        heads = []
        for g in range(A_GROUP):
            for j in range(A_KV_HEADS):
                cb = g * A_KV_HEADS + j
                heads.append(pv[j * HEAD_DIM:(j + 1) * HEAD_DIM, cb * WINDOW:(cb + 1) * WINDOW])
        ot = jnp.concatenate(heads, axis=0)
        ms = jnp.mean(ot * ot, axis=0, keepdims=True)
        y = (ot * lax.rsqrt(ms + NORM_EPS)).T * gain
        o_ref[b * WINDOW:(b + 1) * WINDOW, :] = y.astype(o_ref.dtype)


def _swa(proj, vat, sink_row, g, *, B, S, tq=512):
    nq = S // tq
    rpb = tq // WINDOW

    def prev_blk(b, i):
        return jnp.maximum(b * (S // WINDOW) + i * rpb - 1, 0)

    return pl.pallas_call(
        functools.partial(_swa_kernel, tq=tq),
        grid=(B, nq),
        in_specs=[
            pl.BlockSpec((tq, A_WIDTH), lambda b, i: (b * nq + i, COL_QA // A_WIDTH)),
            pl.BlockSpec((tq, A_KV_WIDTH), lambda b, i: (b * nq + i, COL_KA // A_KV_WIDTH)),
            pl.BlockSpec((WINDOW, A_KV_WIDTH), lambda b, i: (prev_blk(b, i), COL_KA // A_KV_WIDTH)),
            pl.BlockSpec((rpb, A_KV_WIDTH, WINDOW), lambda b, i: (b * nq + i, 0, 0)),
            pl.BlockSpec((1, A_KV_WIDTH, WINDOW), lambda b, i: (prev_blk(b, i), 0, 0)),
            pl.BlockSpec((1, A_COLS), lambda b, i: (0, 0)),
            pl.BlockSpec((1, A_WIDTH), lambda b, i: (0, 0)),
        ],
        out_specs=pl.BlockSpec((tq, A_WIDTH), lambda b, i: (b * nq + i, 0)),
        out_shape=jax.ShapeDtypeStruct((B * S, A_WIDTH), BF16),
        compiler_params=_cparams(("parallel", "parallel")),
        name="swa",
    )(proj, proj, proj, vat, vat, sink_row, g.reshape(1, A_WIDTH))


POOL_HALO = 16


def _pool_kernel(u0_ref, u1_ref, h0_ref, h1_ref, w_ref, s_ref, o_ref, *, tq):
    i = pl.program_id(1)
    u = jnp.concatenate([u0_ref[...], u1_ref[...]], axis=1).astype(F32)
    halo = jnp.concatenate([h0_ref[...], h1_ref[...]], axis=1).astype(F32)
    halo = halo * (i > 0).astype(F32)
    ext = jnp.concatenate([halo, u], axis=0)
    t = i * tq + lax.broadcasted_iota(jnp.int32, (tq, 1), 0)
    for gi, w in enumerate(POOL_WINDOWS):
        sl = slice(gi * B_GROUP_DIM, (gi + 1) * B_GROUP_DIM)
        s = ext[:, sl]
        shift = 1
        while shift < w:
            s = s + pltpu.roll(s, shift, 0)
            shift *= 2
        cnt = jnp.minimum(t + 1, w).astype(F32)
        y = (s[POOL_HALO:, :] / cnt - u[:, sl]).astype(BF16)
        o = jnp.dot(y, w_ref[gi], preferred_element_type=F32) * s_ref[:, sl]
        o_ref[:, sl] = o.astype(o_ref.dtype)


def _pool(proj, w_pool, scale, *, B, S, tq=256):
    nq = S // tq
    c0 = COL_UB // 256

    def cur(c):
        return pl.BlockSpec((tq, 256), lambda b, i: (b * nq + i, c))

    def halo(c):
        return pl.BlockSpec(
            (POOL_HALO, 256),
            lambda b, i: (jnp.maximum((b * S + i * tq) // POOL_HALO - 1, 0), c))

    return pl.pallas_call(
        functools.partial(_pool_kernel, tq=tq),
        grid=(B, nq),
        in_specs=[cur(c0), cur(c0 + 1), halo(c0), halo(c0 + 1),
                  pl.BlockSpec((len(POOL_WINDOWS), B_GROUP_DIM, B_GROUP_DIM), lambda b, i: (0, 0, 0)),
                  pl.BlockSpec((1, B_WIDTH), lambda b, i: (0, 0))],
        out_specs=pl.BlockSpec((tq, B_WIDTH), lambda b, i: (b * nq + i, 0)),
        out_shape=jax.ShapeDtypeStruct((B * S, B_WIDTH), BF16),
        compiler_params=_cparams(("parallel", "parallel")),
        name="pool",
    )(proj, proj, proj, proj, w_pool, scale.reshape(1, B_WIDTH))


DIFF_UNROLL = 4


def _diff_kernel(q_ref, k_ref, vt_ref, lam_ref, g_ref, o_ref, qcat_ref, s0_ref, s1_ref, m_ref, acc_ref,
                 *, tq, lam_init):
    qi = pl.program_id(2)
    q = q_ref[...]
    lane = lax.broadcasted_iota(jnp.int32, q.shape, 1)
    zero = jnp.zeros_like(q)
    qcat_ref[0:tq, :] = jnp.where(lane < HEAD_DIM, q, zero)
    qcat_ref[tq:2 * tq, :] = jnp.where(lane >= HEAD_DIM, q, zero)

    m_ref[...] = jnp.full(m_ref.shape, NEG, F32)
    acc_ref[...] = jnp.zeros(acc_ref.shape, F32)

    def scores(kidx):
        off = pl.multiple_of(kidx * DIFF_KB, DIFF_KB)
        return lax.dot_general(k_ref[pl.ds(off, DIFF_KB), :], qcat_ref[...], (((1,), (1,)), ((), ())),
                               preferred_element_type=F32)

    def consume(s_ref, kidx, diagonal):
        st = s_ref[...]
        if diagonal:
            r = lax.broadcasted_iota(jnp.int32, st.shape, 0)
            c = lax.broadcasted_iota(jnp.int32, st.shape, 1)
            c = jnp.where(c >= tq, c - tq, c)
            st = jnp.where(r <= c, st, NEG)
        m = m_ref[...]
        m_new = jnp.maximum(m, jnp.max(st, axis=0, keepdims=True))
        alpha = jnp.exp2(m - m_new)
        p = jnp.exp2(st - m_new)
        m_ref[...] = m_new
        pv = jnp.dot(vt_ref[kidx], p.astype(BF16), preferred_element_type=F32)
        acc_ref[...] = alpha * acc_ref[...] + pv

    def step(cur_ref, nxt_ref, kidx):
        s_next = scores(kidx + 1)
        consume(cur_ref, kidx, False)
        nxt_ref[...] = s_next

    s0_ref[...] = scores(0)
    bufs = (s0_ref, s1_ref)

    def steps(first, n):
        for u in range(n):
            step(bufs[u % 2], bufs[(u + 1) % 2], first + u)

    def body(c, carry):
        steps(c * DIFF_UNROLL, DIFF_UNROLL)
        return carry

    lax.fori_loop(0, qi // DIFF_UNROLL, body, 0)
    for rem in range(DIFF_UNROLL):
        @pl.when(qi % DIFF_UNROLL == rem)
        def _(rem=rem):
            steps(qi - rem, rem)
            consume(bufs[rem % 2], qi, True)

    lv = lam_ref[...]
    lam = (jnp.exp(jnp.sum(lv[0:1] * lv[1:2], axis=1, keepdims=True))
           - jnp.exp(jnp.sum(lv[2:3] * lv[3:4], axis=1, keepdims=True)) + lam_init)
    l = acc_ref[C_VDIM:C_VDIM + 1, :]
    acc = acc_ref[0:C_VDIM, :]
    ot = acc[:, :tq] / l[:, :tq] - lam * (acc[:, tq:] / l[:, tq:])
    ms = jnp.mean(ot * ot, axis=0, keepdims=True)
    yt = ot * lax.rsqrt(ms + DIFF_EPS) * (g_ref[...] * (1.0 - lam_init))
    o_ref[...] = yt.T.astype(o_ref.dtype)


def _diff(proj, vt, lam_vecs, subln, *, B, S, lam_init, tq=512):
    nq = S // tq
    nkb = S // DIFF_KB
    assert tq == DIFF_KB
    return pl.pallas_call(
        functools.partial(_diff_kernel, tq=tq, lam_init=lam_init),
        grid=(B, C_HEADS, nq),
        in_specs=[
            pl.BlockSpec((tq, C_VDIM), lambda b, h, i: (b * nq + i, COL_QC // C_VDIM + h)),
            pl.BlockSpec((S, C_VDIM), lambda b, h, i: (b, COL_KC // C_VDIM + h)),
            pl.BlockSpec((None, None, nkb, VT_ROWS, DIFF_KB), lambda b, h, i: (b, h, 0, 0, 0)),
            pl.BlockSpec((4, HEAD_DIM), lambda b, h, i: (0, 0)),
            pl.BlockSpec((C_VDIM, 1), lambda b, h, i: (0, 0)),
        ],
        out_specs=pl.BlockSpec((tq, C_VDIM), lambda b, h, i: (b * nq + i, h)),
        out_shape=jax.ShapeDtypeStruct((B * S, C_WIDTH), BF16),
        scratch_shapes=[pltpu.VMEM((2 * tq, C_VDIM), BF16),
                        pltpu.VMEM((DIFF_KB, 2 * tq), F32), pltpu.VMEM((DIFF_KB, 2 * tq), F32),
                        pltpu.VMEM((1, 2 * tq), F32), pltpu.VMEM((VT_ROWS, 2 * tq), F32)],
        compiler_params=_cparams(("parallel", "parallel", "arbitrary")),
        name="diff",
    )(proj, proj, vt, lam_vecs, subln.reshape(C_VDIM, 1))


OUT_SUB = 256


def _out_kernel(oa_ref, ob_ref, oc_ref, w_ref, h_ref, g_ref, h1_ref, hn_ref, *, tm):
    subs = [slice(s * OUT_SUB, (s + 1) * OUT_SUB) for s in range(tm // OUT_SUB)]
    accs = []
    for rows in subs:
        acc = jnp.dot(oa_ref[rows, :], w_ref[0:A_WIDTH, :], preferred_element_type=F32)
        acc = acc + jnp.dot(ob_ref[rows, :], w_ref[A_WIDTH:A_WIDTH + B_WIDTH, :], preferred_element_type=F32)
        acc = acc + jnp.dot(oc_ref[rows, :], w_ref[A_WIDTH + B_WIDTH:, :], preferred_element_type=F32)
        accs.append(acc)
    for rows, acc in zip(subs, accs):
        h1 = h_ref[rows, :] + acc
        h1_ref[rows, :] = h1
        ms = jnp.mean(h1 * h1, axis=1, keepdims=True)
        hn = h1 * lax.rsqrt(ms + NORM_EPS) * g_ref[...]
        hn_ref[rows, :] = _pack_rows(hn)


def _out(oa, ob, oc, w, h, g, *, tm=512):
    T, D = h.shape
    return pl.pallas_call(
        functools.partial(_out_kernel, tm=tm),
        grid=(T // tm,),
        in_specs=[
            pl.BlockSpec((tm, A_WIDTH), lambda i: (i, 0)),
            pl.BlockSpec((tm, B_WIDTH), lambda i: (i, 0)),
            pl.BlockSpec((tm, C_WIDTH), lambda i: (i, 0)),
            pl.BlockSpec((D, D), lambda i: (0, 0)),
            pl.BlockSpec((tm, D), lambda i: (i, 0)),
            pl.BlockSpec((1, D), lambda i: (0, 0)),
        ],
        out_specs=[
            pl.BlockSpec((tm, D), lambda i: (i, 0)),
            pl.BlockSpec((tm, D_PACK), lambda i: (i, 0)),
        ],
        out_shape=[
            jax.ShapeDtypeStruct((T, D), F32),
            jax.ShapeDtypeStruct((T, D_PACK), U32),
        ],
        compiler_params=_cparams(("parallel",)),
        name="out",
    )(oa, ob, oc, w, h, g.reshape(1, D))


ROUTER_ROWS = 128
ROUTER_LO = 64
ROUTER_GROUP_ROW = 0
ROUTER_EXPERT_ROW = 8


def _router_kernel(x_ref, wt_ref, b_ref, e_ref, g_ref):
    x = _unpack_rows_bf16(x_ref[...])
    lg = lax.dot_general(wt_ref[...], x, (((1,), (1,)), ((), ())), preferred_element_type=F32)
    lg = lg[0:ROUTER_LO] + lg[ROUTER_LO:ROUTER_ROWS] + b_ref[...]
    grp = lg[ROUTER_GROUP_ROW:ROUTER_GROUP_ROW + N_GROUPS]
    ex = lg[ROUTER_EXPERT_ROW:ROUTER_EXPERT_ROW + N_EXPERTS]

    mg = jnp.max(grp, axis=0, keepdims=True)
    pg_sel = 1.0 / jnp.sum(jnp.exp(grp - mg), axis=0, keepdims=True)
    gi = lax.broadcasted_iota(jnp.int32, grp.shape, 0)
    g_sel = jnp.min(jnp.where(grp == mg, gi, N_GROUPS), axis=0, keepdims=True)

    er = lax.broadcasted_iota(jnp.int32, ex.shape, 0)
    group_of = lax.shift_right_logical(er, EXPERTS_PER_GROUP.bit_length() - 1)
    cand = jnp.where(group_of == g_sel, ex, NEG)
    v1 = jnp.max(cand, axis=0, keepdims=True)
    i1 = jnp.min(jnp.where(cand == v1, er, N_EXPERTS), axis=0, keepdims=True)
    rest = jnp.where(er == i1, NEG, cand)
    v2 = jnp.max(rest, axis=0, keepdims=True)
    i2 = jnp.min(jnp.where(rest == v2, er, N_EXPERTS), axis=0, keepdims=True)
    t = jnp.exp(v2 - v1)
    g1 = pg_sel / (1.0 + t)
    e_ref[...] = jnp.concatenate([i1, i2], axis=0)
    g_ref[...] = jnp.concatenate([g1, g1 * t], axis=0)


def _router(hn_packed, wt, b, *, tm=1024):
    T = hn_packed.shape[0]
    D = wt.shape[1]
    return pl.pallas_call(
        _router_kernel,
        grid=(T // tm,),
        in_specs=[
            pl.BlockSpec((tm, D_PACK), lambda i: (i, 0)),
            pl.BlockSpec((ROUTER_ROWS, D), lambda i: (0, 0)),
            pl.BlockSpec((ROUTER_LO, 1), lambda i: (0, 0)),
        ],
        out_specs=[pl.BlockSpec((TOP_K, tm), lambda i: (0, i)), pl.BlockSpec((TOP_K, tm), lambda i: (0, i))],
        out_shape=[jax.ShapeDtypeStruct((TOP_K, T), jnp.int32), jax.ShapeDtypeStruct((TOP_K, T), F32)],
        compiler_params=_cparams(("parallel",)),
        name="router",
    )(hn_packed, wt, b)


EXPERT_SUB = 256
EXPERT_VMEM = 56 * 1024 * 1024


def _expert_kernel(blk_ref, exp_ref, lo_ref, hi_ref, x_ref, g_ref, wg_ref, wu_ref, wd_ref, o_ref,
                   wg_bf, wu_bf, wd_bf, *, tm):
    p = pl.program_id(0)
    prev = jnp.maximum(p - 1, 0)
    blk = blk_ref[p]
    first = (p == 0) | (blk != blk_ref[prev])
    new_expert = (p == 0) | (exp_ref[p] != exp_ref[prev])
    lo = lo_ref[p]
    hi = hi_ref[p]

    @pl.when(new_expert)
    def _():
        wg_bf[...] = wg_ref[...].astype(BF16)
        wu_bf[...] = wu_ref[...].astype(BF16)
        wd_bf[...] = wd_ref[...].astype(BF16)

    @pl.when(first)
    def _():
        o_ref[...] = jnp.zeros_like(o_ref)

    for s in range(tm // EXPERT_SUB):
        row0 = blk * tm + s * EXPERT_SUB
        rows = slice(s * EXPERT_SUB, (s + 1) * EXPERT_SUB)

        @pl.when((hi > row0) & (lo < row0 + EXPERT_SUB))
        def _(row0=row0, rows=rows):
            x = _unpack_rows_bf16(x_ref[rows, :])
            a = jnp.dot(x, wg_bf[...], preferred_element_type=F32)
            u = jnp.dot(x, wu_bf[...], preferred_element_type=F32)
            act = (a / (1.0 + jnp.exp(-a))) * u
            y = jnp.dot(act.astype(BF16), wd_bf[...], preferred_element_type=F32)
            r = row0 + lax.broadcasted_iota(jnp.int32, (EXPERT_SUB, 1), 0)
            mine = (r >= lo) & (r < hi)
            o_ref[rows, :] = jnp.where(mine, _pack_rows(y * g_ref[rows, :]), o_ref[rows, :])


def _experts(seg_blk, seg_exp, seg_lo, seg_hi, xs, gates, wg, wu, wd, layer, *, tm):
    N = xs.shape[0]
    D = wg.shape[2]
    P = seg_blk.shape[0]
    grid_spec = pltpu.PrefetchScalarGridSpec(
        num_scalar_prefetch=4,
        grid=(P,),
        in_specs=[
            pl.BlockSpec((tm, D_PACK), lambda p, blk, ex, lo, hi: (blk[p], 0)),
            pl.BlockSpec((tm, 1), lambda p, blk, ex, lo, hi: (blk[p], 0)),
            pl.BlockSpec((None, None, D, D_FF), lambda p, blk, ex, lo, hi: (layer, ex[p], 0, 0)),
            pl.BlockSpec((None, None, D, D_FF), lambda p, blk, ex, lo, hi: (layer, ex[p], 0, 0)),
            pl.BlockSpec((None, None, D_FF, D), lambda p, blk, ex, lo, hi: (layer, ex[p], 0, 0)),
        ],
        out_specs=pl.BlockSpec((tm, D_PACK), lambda p, blk, ex, lo, hi: (blk[p], 0)),
        scratch_shapes=[pltpu.VMEM((D, D_FF), BF16), pltpu.VMEM((D, D_FF), BF16), pltpu.VMEM((D_FF, D), BF16)],
    )
    return pl.pallas_call(
        functools.partial(_expert_kernel, tm=tm),
        grid_spec=grid_spec,
        out_shape=jax.ShapeDtypeStruct((N, D_PACK), U32),
        compiler_params=_cparams(("arbitrary",), EXPERT_VMEM),
        name="experts",
    )(seg_blk, seg_exp, seg_lo, seg_hi, xs, gates, wg, wu, wd)


def _route(eidx_t, gates_t, T, tm):
    N = T * TOP_K
    flat_e = eidx_t.T.reshape(N)
    iota = jnp.arange(N, dtype=jnp.int32)
    sorted_e, order, g_sorted = lax.sort((flat_e, iota, gates_t.T.reshape(N)), num_keys=1, is_stable=True)
    tok = order // TOP_K
    _, pos = lax.sort((order, iota), num_keys=1)
    counts = jnp.bincount(flat_e, length=N_EXPERTS)
    starts = (jnp.cumsum(counts) - counts).astype(jnp.int32)
    nb = N // tm
    bounds = jnp.sort(jnp.concatenate([jnp.arange(nb, dtype=jnp.int32) * tm, starts]))
    ends = jnp.concatenate([bounds[1:], jnp.full((1,), N, bounds.dtype)])
    seg_blk = jnp.minimum(bounds // tm, nb - 1).astype(jnp.int32)
    seg_exp = sorted_e[jnp.minimum(bounds, N - 1)].astype(jnp.int32)
    pos_rows = pos.reshape(T, TOP_K).T.reshape(N)
    return tok, g_sorted, pos_rows, seg_blk, seg_exp, bounds, ends


def _group_major(a, axis):
    shape = a.shape
    a = a.reshape(shape[:axis] + (A_KV_HEADS, A_GROUP, HEAD_DIM) + shape[axis + 1:])
    return jnp.swapaxes(a, axis, axis + 1).reshape(shape)


def _permute_in_proj(w):
    bounds = np.cumsum([0, 768, 256, 256, 512, 768, 768, 768])
    qa, ka, va, ub, qc, kc, vc = [w[:, bounds[s]:bounds[s + 1]] for s in range(7)]
    return jnp.concatenate([_group_major(qa, 1), qc, ka, kc, ub, va, vc], axis=1)


def kernel(x, positions, attn_norm, w_in, sinks, branch_norm_a, w_pool, pool_scale, lambda_q1, lambda_k1,
           lambda_q2, lambda_k2, subln, w_out, ffn_norm, w_router_group, b_router_group, w_router_expert,
           b_router_expert, w_expert_gate, w_expert_up, w_expert_down, final_norm):
    B, S, D = x.shape
    T = B * S
    depth = w_in.shape[0]
    tm_e = 512

    half = HEAD_DIM // 2
    inv = ROPE_THETA ** (-jnp.arange(half, dtype=F32) / half)
    ang = positions.astype(F32).reshape(T, 1) * inv
    cos, sin = jnp.cos(ang), jnp.sin(ang)
    cos_t = jnp.concatenate([cos, cos, cos, cos], axis=1)
    sin_t = jnp.concatenate([-sin, sin, -sin, sin], axis=1)

    h = x.reshape(T, D)
    y_rows = None
    for l in range(depth):
        if l == 0:
            (hn,) = _norm(h, None, attn_norm[l], emit_h=False, emit_bf16=True)
        else:
            h, hn = _norm(h, y_rows, attn_norm[l], emit_h=True, emit_bf16=True)
        w_in_l = _permute_in_proj(w_in[l]).astype(BF16)
        proj, vat, vt = _proj(hn, w_in_l[:, :COL_VA], w_in_l[:, COL_VA:].T, cos_t, sin_t, B=B, S=S)

        sink_heads = sinks[l].reshape(A_KV_HEADS, A_GROUP).T.reshape(A_HEADS) * LOG2E
        sink_row = jnp.repeat(sink_heads, WINDOW).reshape(1, A_COLS)
        oa = _swa(proj, vat, sink_row, _group_major(branch_norm_a[l], 0), B=B, S=S)
        ob = _pool(proj, w_pool[l].astype(BF16), pool_scale[l], B=B, S=S)

        lam_init = 0.8 - 0.6 * math.exp(-0.3 * l)
        lam_vecs = jnp.stack([lambda_q1[l], lambda_k1[l], lambda_q2[l], lambda_k2[l]]).astype(F32)
        oc = _diff(proj, vt, lam_vecs, subln[l], B=B, S=S, lam_init=lam_init)

        w_out_l = jnp.concatenate([_group_major(w_out[l][:A_WIDTH], 0), w_out[l][A_WIDTH:]], axis=0)
        h1, hn2 = _out(oa, ob, oc, w_out_l.astype(BF16), h, ffn_norm[l])

        wr = jnp.zeros((ROUTER_LO, D), F32)
        wr = wr.at[ROUTER_GROUP_ROW:ROUTER_GROUP_ROW + N_GROUPS].set(w_router_group[l].T)
        wr = wr.at[ROUTER_EXPERT_ROW:ROUTER_EXPERT_ROW + N_EXPERTS].set(w_router_expert[l].T)
        wr_hi = wr.astype(BF16)
        wr_lo = (wr - wr_hi.astype(F32)).astype(BF16)
        br = jnp.zeros((ROUTER_LO, 1), F32)
        br = br.at[ROUTER_GROUP_ROW:ROUTER_GROUP_ROW + N_GROUPS, 0].set(b_router_group[l])
        br = br.at[ROUTER_EXPERT_ROW:ROUTER_EXPERT_ROW + N_EXPERTS, 0].set(b_router_expert[l])
        eidx_t, gates_t = _router(hn2, jnp.concatenate([wr_hi, wr_lo], axis=0), br)

        tok, g_sorted, pos_rows, seg_blk, seg_exp, seg_lo, seg_hi = _route(eidx_t, gates_t, T, tm_e)
        xs = _sc_gather_rows(hn2, tok)
        ys = _experts(seg_blk, seg_exp, seg_lo, seg_hi, xs, g_sorted.reshape(-1, 1),
                      w_expert_gate, w_expert_up, w_expert_down, l, tm=tm_e)
        y_rows = _sc_gather_rows(ys, pos_rows)
        h = h1

    (out,) = _norm(h, y_rows, final_norm, emit_h=False, emit_bf16=False)
    return out.reshape(B, S, D)
```

```python
import functools
import math

import jax
import jax.numpy as jnp
from jax import lax
from jax.experimental import pallas as pl
from jax.experimental.pallas import tpu as pltpu
from jax.experimental.pallas import tpu_sc as plsc
import numpy as np

F32 = jnp.float32
BF16 = jnp.bfloat16

D_MODEL = 2048
HEAD_DIM = 64
ROPE_THETA = 10000.0
A_HEADS = 12
A_KV_HEADS = 4
A_GROUP = 3
A_WIDTH = 768
WINDOW = 128
POOL_WINDOWS = (2, 4, 8, 16)
B_WIDTH = 512
B_GROUP_DIM = 128
C_VDIM = 128
C_WIDTH = 768
C_HEADS = 6
DIFF_EPS = 1e-5
IN_COLS = 4096
N_GROUPS = 4
EXPERTS_PER_GROUP = 8
N_EXPERTS = 32
TOP_K = 2
D_FF = 512
NORM_EPS = 1e-6
NEG = -1e30
LOG2E = math.log2(math.e)
Q_SCALE = HEAD_DIM ** -0.5 * LOG2E

COL_QA, COL_QC, COL_KA, COL_KC, COL_UB, COL_VA, COL_VC = 0, 768, 1536, 1792, 2560, 3072, 3328
PROJ_TN = 256
PROJ_SUB = 256
N_SCALED_TILES = COL_KA // PROJ_TN
N_ROPE_TILES = COL_UB // PROJ_TN
N_MAIN_TILES = COL_VA // PROJ_TN
DIFF_KB = 512
VT_ROWS = C_VDIM + 16

VMEM_LIMIT = 48 * 1024 * 1024
D_PACK = D_MODEL // 2
U32 = jnp.uint32


def _cparams(sem, vmem=VMEM_LIMIT):
    return pltpu.CompilerParams(dimension_semantics=sem, vmem_limit_bytes=vmem)


def _pack_rows(x):
    lo = pltpu.bitcast(x[:, :D_PACK].astype(BF16).astype(F32), U32) >> 16
    hi = pltpu.bitcast(x[:, D_PACK:].astype(BF16).astype(F32), U32)
    return hi | lo


def _unpack_rows(w):
    return pltpu.bitcast(w << 16, F32), pltpu.bitcast(w & U32(0xFFFF0000), F32)


def _unpack_rows_bf16(w):
    lo, hi = _unpack_rows(w)
    return jnp.concatenate([lo.astype(BF16), hi.astype(BF16)], axis=1)


SC_WINDOW = 32
SC_INDEX_LANES = 128


def _sc_gather_rows(x, idx):
    M = idx.shape[0]
    W = x.shape[1]
    idx2 = jnp.pad(idx.reshape(M // SC_WINDOW, SC_WINDOW), ((0, 0), (0, SC_INDEX_LANES - SC_WINDOW)))
    mesh = plsc.VectorSubcoreMesh(core_axis_name="core", subcore_axis_name="subcore")

    @pl.kernel(out_type=jax.ShapeDtypeStruct((M, W), x.dtype), mesh=mesh)
    def gather(x_hbm, i_hbm, o_hbm):
        def body(i_vmem, o_vmem):
            pltpu.sync_copy(x_hbm.at[i_vmem.at[0, pl.ds(0, SC_WINDOW)]], o_vmem)

        pltpu.emit_pipeline(
            body,
            grid=(M // SC_WINDOW,),
            in_specs=[pl.BlockSpec((1, SC_INDEX_LANES), index_map=lambda i: (i, 0))],
            out_specs=[pl.BlockSpec((SC_WINDOW, W), index_map=lambda i: (i, 0))],
            core_axis_name=("core", "subcore"),
            dimension_semantics=(pltpu.PARALLEL,),
        )(i_hbm, o_hbm)

    return gather(x, idx2)


def _norm_kernel(*refs, n_add, emit_h, emit_bf16, eps):
    h_ref = refs[0]
    add_refs = refs[1:1 + n_add]
    g_ref = refs[1 + n_add]
    outs = refs[2 + n_add:]
    h = h_ref[...]
    if n_add:
        lo = jnp.zeros((h.shape[0], D_PACK), F32)
        hi = jnp.zeros((h.shape[0], D_PACK), F32)
        for r in add_refs:
            a, b = _unpack_rows(r[...])
            lo, hi = lo + a, hi + b
        h = h + jnp.concatenate([lo, hi], axis=1)
    ms = jnp.mean(h * h, axis=-1, keepdims=True)
    y = h * lax.rsqrt(ms + eps) * g_ref[...]
    k = 0
    if emit_h:
        outs[k][...] = h
        k += 1
    outs[k][...] = y.astype(BF16 if emit_bf16 else F32)


def _norm(h, y_rows, g, *, emit_h, emit_bf16, tm=512):
    T, D = h.shape
    nt = T // tm
    row = pl.BlockSpec((tm, D), lambda i: (i, 0))
    n_add = 0 if y_rows is None else TOP_K
    add_specs = [pl.BlockSpec((tm, D_PACK), lambda i, k=k: (i + k * nt, 0)) for k in range(n_add)]
    out_shape, out_specs = [], []
    if emit_h:
        out_shape.append(jax.ShapeDtypeStruct((T, D), F32))
        out_specs.append(row)
    out_shape.append(jax.ShapeDtypeStruct((T, D), BF16 if emit_bf16 else F32))
    out_specs.append(row)
    return pl.pallas_call(
        functools.partial(_norm_kernel, n_add=n_add, emit_h=emit_h, emit_bf16=emit_bf16, eps=NORM_EPS),
        grid=(nt,),
        in_specs=[row] + add_specs + [pl.BlockSpec((1, D), lambda i: (0, 0))],
        out_specs=out_specs,
        out_shape=out_shape,
        compiler_params=_cparams(("parallel",)),
        name="norm",
    )(h, *([y_rows] * n_add), g.reshape(1, D))


def _proj_kernel(x_ref, w_ref, wt_ref, cos_ref, sin_ref, o_ref, vat_ref, vt_ref, *, tm):
    j = pl.program_id(1)
    subs = [slice(s * PROJ_SUB, (s + 1) * PROJ_SUB) for s in range(tm // PROJ_SUB)]

    def matmuls():
        return [jnp.dot(x_ref[rows, :], w_ref[...], preferred_element_type=F32) for rows in subs]

    @pl.when(j < N_ROPE_TILES)
    def _():
        accs = matmuls()
        scale = jnp.where(j < N_SCALED_TILES, Q_SCALE, 1.0).astype(F32)
        lane = lax.broadcasted_iota(jnp.int32, (PROJ_SUB, PROJ_TN), 1)
        first_half = (lane & (HEAD_DIM // 2)) == 0
        for rows, acc in zip(subs, accs):
            c = jnp.concatenate([cos_ref[rows, :] * scale] * (PROJ_TN // 128), axis=1)
            s = jnp.concatenate([sin_ref[rows, :] * scale] * (PROJ_TN // 128), axis=1)
            swapped = jnp.where(first_half,
                                pltpu.roll(acc, PROJ_TN - HEAD_DIM // 2, 1),
                                pltpu.roll(acc, HEAD_DIM // 2, 1))
            o_ref[rows, :] = (acc * c + swapped * s).astype(o_ref.dtype)

    @pl.when((j >= N_ROPE_TILES) & (j < N_MAIN_TILES))
    def _():
        for rows, acc in zip(subs, matmuls()):
            o_ref[rows, :] = acc.astype(o_ref.dtype)

    def matmuls_t():
        return [lax.dot_general(wt_ref[...], x_ref[rows, :], (((1,), (1,)), ((), ())),
                                preferred_element_type=F32) for rows in subs]

    @pl.when(j == N_MAIN_TILES)
    def _():
        per_sub = PROJ_SUB // WINDOW
        for s, acc in enumerate(matmuls_t()):
            for b in range(per_sub):
                vat_ref[s * per_sub + b] = acc[:, b * WINDOW:(b + 1) * WINDOW].astype(vat_ref.dtype)

    @pl.when(j > N_MAIN_TILES)
    def _():
        accs = matmuls_t()
        row = lax.broadcasted_iota(jnp.int32, (VT_ROWS - C_VDIM, DIFF_KB), 0)
        extra = jnp.where(row == 0, 1.0, 0.0).astype(vt_ref.dtype)
        per_kb = DIFF_KB // PROJ_SUB
        for s, acc in enumerate(accs):
            cols = slice((s % per_kb) * PROJ_SUB, (s % per_kb + 1) * PROJ_SUB)
            for hh in range(PROJ_TN // C_VDIM):
                vt_ref[hh, s // per_kb, 0:C_VDIM, cols] = acc[hh * C_VDIM:(hh + 1) * C_VDIM, :].astype(vt_ref.dtype)
        for hh in range(PROJ_TN // C_VDIM):
            for kb in range(tm // DIFF_KB):
                vt_ref[hh, kb, C_VDIM:VT_ROWS, :] = extra


def _proj(hn, w, wt, cos_t, sin_t, *, B, S, tm=2048):
    T, D = hn.shape
    spt = S // tm
    last_main = N_MAIN_TILES - 1
    return pl.pallas_call(
        functools.partial(_proj_kernel, tm=tm),
        grid=(T // tm, IN_COLS // PROJ_TN),
        in_specs=[
            pl.BlockSpec((tm, D), lambda i, j: (i, 0)),
            pl.BlockSpec((D, PROJ_TN), lambda i, j: (0, jnp.minimum(j, last_main))),
            pl.BlockSpec((PROJ_TN, D), lambda i, j: (jnp.maximum(j - N_MAIN_TILES, 0), 0)),
            pl.BlockSpec((tm, 128), lambda i, j: (i, 0)),
            pl.BlockSpec((tm, 128), lambda i, j: (i, 0)),
        ],
        out_specs=[
            pl.BlockSpec((tm, PROJ_TN), lambda i, j: (i, jnp.minimum(j, last_main))),
            pl.BlockSpec((tm // WINDOW, PROJ_TN, WINDOW), lambda i, j: (i, 0, 0)),
            pl.BlockSpec((None, PROJ_TN // C_VDIM, tm // DIFF_KB, VT_ROWS, DIFF_KB),
                         lambda i, j: (i // spt, jnp.maximum(j - N_MAIN_TILES - 1, 0), i % spt, 0, 0)),
        ],
        out_shape=[
            jax.ShapeDtypeStruct((T, COL_VA), BF16),
            jax.ShapeDtypeStruct((T // WINDOW, PROJ_TN, WINDOW), BF16),
            jax.ShapeDtypeStruct((B, C_HEADS, S // DIFF_KB, VT_ROWS, DIFF_KB), BF16),
        ],
        compiler_params=_cparams(("parallel", "arbitrary")),
        name="proj",
    )(hn, w, wt, cos_t, sin_t)


A_KV_WIDTH = A_KV_HEADS * HEAD_DIM
A_COLS = A_HEADS * WINDOW


def _swa_kernel(q_ref, kc_ref, kp_ref, vc_ref, vp_ref, sink_ref, g_ref, o_ref, *, tq):
    i = pl.program_id(1)
    nblk = tq // WINDOW
    kext = jnp.concatenate([kp_ref[...], kc_ref[...]], axis=0)
    lane_kv = lax.broadcasted_iota(jnp.int32, (WINDOW, A_KV_WIDTH), 1) // HEAD_DIM
    zero = jnp.zeros((WINDOW, A_KV_WIDTH), BF16)

    scores = []
    for b in range(nblk):
        parts = []
        for g in range(A_GROUP):
            qg = q_ref[b * WINDOW:(b + 1) * WINDOW, g * A_KV_WIDTH:(g + 1) * A_KV_WIDTH]
            parts += [jnp.where(lane_kv == j, qg, zero) for j in range(A_KV_HEADS)]
        qcat = jnp.concatenate(parts, axis=0)
        scores.append(lax.dot_general(kext[b * WINDOW:(b + 2) * WINDOW, :], qcat, (((1,), (1,)), ((), ())),
                                      preferred_element_type=F32))

    r = lax.broadcasted_iota(jnp.int32, (2 * WINDOW, A_COLS), 0)
    c = lax.broadcasted_iota(jnp.int32, (2 * WINDOW, A_COLS), 1) & (WINDOW - 1)
    band = (r > c) & (r <= c + WINDOW)
    sink = sink_ref[...]
    gain = g_ref[...]
    for b, st in enumerate(scores):
        mask = band
        if b == 0:
            mask = band & ((r >= WINDOW) | (i > 0))
        st = jnp.where(mask, st, NEG)
        m = jnp.maximum(jnp.max(st, axis=0, keepdims=True), sink)
        p = jnp.exp2(st - m)
        l = jnp.sum(p, axis=0, keepdims=True) + jnp.exp2(sink - m)
        vt = jnp.concatenate([vp_ref[0] if b == 0 else vc_ref[b - 1], vc_ref[b]], axis=1)
        pv = jnp.dot(vt, p.astype(BF16), preferred_element_type=F32)
        inv_l = 1.0 / l
        heads = []
        for g in range(A_GROUP):
            for j in range(A_KV_HEADS):
                cols = slice((g * A_KV_HEADS + j) * WINDOW, (g * A_KV_HEADS + j + 1) * WINDOW)
                heads.append(pv[j * HEAD_DIM:(j + 1) * HEAD_DIM, cols] * inv_l[:, cols])
        ot = jnp.concatenate(heads, axis=0)
        ms = jnp.mean(ot * ot, axis=0, keepdims=True)
        y = (ot * lax.rsqrt(ms + NORM_EPS)).T * gain
        o_ref[b * WINDOW:(b + 1) * WINDOW, :] = y.astype(o_ref.dtype)


def _swa(proj, vat, sink_row, g, *, B, S, tq=512):
    nq = S // tq
    rpb = tq // WINDOW

    def prev_blk(b, i):
        return jnp.maximum(b * (S // WINDOW) + i * rpb - 1, 0)

    return pl.pallas_call(
        functools.partial(_swa_kernel, tq=tq),
        grid=(B, nq),
        in_specs=[
            pl.BlockSpec((tq, A_WIDTH), lambda b, i: (b * nq + i, COL_QA // A_WIDTH)),
            pl.BlockSpec((tq, A_KV_WIDTH), lambda b, i: (b * nq + i, COL_KA // A_KV_WIDTH)),
            pl.BlockSpec((WINDOW, A_KV_WIDTH), lambda b, i: (prev_blk(b, i), COL_KA // A_KV_WIDTH)),
            pl.BlockSpec((rpb, A_KV_WIDTH, WINDOW), lambda b, i: (b * nq + i, 0, 0)),
            pl.BlockSpec((1, A_KV_WIDTH, WINDOW), lambda b, i: (prev_blk(b, i), 0, 0)),
            pl.BlockSpec((1, A_COLS), lambda b, i: (0, 0)),
            pl.BlockSpec((1, A_WIDTH), lambda b, i: (0, 0)),
        ],
        out_specs=pl.BlockSpec((tq, A_WIDTH), lambda b, i: (b * nq + i, 0)),
        out_shape=jax.ShapeDtypeStruct((B * S, A_WIDTH), BF16),
        compiler_params=_cparams(("parallel", "parallel")),
        name="swa",
    )(proj, proj, proj, vat, vat, sink_row, g.reshape(1, A_WIDTH))


POOL_HALO = 16


def _pool_kernel(u0_ref, u1_ref, h0_ref, h1_ref, w_ref, s_ref, o_ref, *, tq):
    i = pl.program_id(1)
    u = jnp.concatenate([u0_ref[...], u1_ref[...]], axis=1).astype(F32)
    halo = jnp.concatenate([h0_ref[...], h1_ref[...]], axis=1).astype(F32)
    halo = halo * (i > 0).astype(F32)
    ext = jnp.concatenate([halo, u], axis=0)
    t = i * tq + lax.broadcasted_iota(jnp.int32, (tq, 1), 0)
    for gi, w in enumerate(POOL_WINDOWS):
        sl = slice(gi * B_GROUP_DIM, (gi + 1) * B_GROUP_DIM)
        s = ext[:, sl]
        shift = 1
        while shift < w:
            s = s + pltpu.roll(s, shift, 0)
            shift *= 2
        cnt = jnp.minimum(t + 1, w).astype(F32)
        y = (s[POOL_HALO:, :] / cnt - u[:, sl]).astype(BF16)
        o = jnp.dot(y, w_ref[gi], preferred_element_type=F32) * s_ref[:, sl]
        o_ref[:, sl] = o.astype(o_ref.dtype)


def _pool(proj, w_pool, scale, *, B, S, tq=256):
    nq = S // tq
    c0 = COL_UB // 256

    def cur(c):
        return pl.BlockSpec((tq, 256), lambda b, i: (b * nq + i, c))

    def halo(c):
        return pl.BlockSpec(
            (POOL_HALO, 256),
            lambda b, i: (jnp.maximum((b * S + i * tq) // POOL_HALO - 1, 0), c))

    return pl.pallas_call(
        functools.partial(_pool_kernel, tq=tq),
        grid=(B, nq),
        in_specs=[cur(c0), cur(c0 + 1), halo(c0), halo(c0 + 1),
                  pl.BlockSpec((len(POOL_WINDOWS), B_GROUP_DIM, B_GROUP_DIM), lambda b, i: (0, 0, 0)),
                  pl.BlockSpec((1, B_WIDTH), lambda b, i: (0, 0))],
        out_specs=pl.BlockSpec((tq, B_WIDTH), lambda b, i: (b * nq + i, 0)),
        out_shape=jax.ShapeDtypeStruct((B * S, B_WIDTH), BF16),
        compiler_params=_cparams(("parallel", "parallel")),
        name="pool",
    )(proj, proj, proj, proj, w_pool, scale.reshape(1, B_WIDTH))


DIFF_UNROLL = 4


DIFF_HEADS = 1


def _diff_kernel(q_ref, k_ref, vt_ref, lam_ref, g_ref, o_ref, qcat_ref, s_ref, bm_ref, m_ref, acc_ref,
                 *, tq, lam_init):
    qi = pl.program_id(2)
    heads = range(DIFF_HEADS)
    lane = lax.broadcasted_iota(jnp.int32, (tq, C_VDIM), 1)
    zero = jnp.zeros((tq, C_VDIM), BF16)
    for hh in heads:
        q = q_ref[:, hh * C_VDIM:(hh + 1) * C_VDIM]
        qcat_ref[hh, 0:tq, :] = jnp.where(lane < HEAD_DIM, q, zero)
        qcat_ref[hh, tq:2 * tq, :] = jnp.where(lane >= HEAD_DIM, q, zero)
    m_ref[...] = jnp.full(m_ref.shape, NEG, F32)
    acc_ref[...] = jnp.zeros(acc_ref.shape, F32)

    def scores(hh, kidx):
        off = pl.multiple_of(kidx * DIFF_KB, DIFF_KB)
        return lax.dot_general(k_ref[pl.ds(off, DIFF_KB), hh * C_VDIM:(hh + 1) * C_VDIM], qcat_ref[hh],
                               (((1,), (1,)), ((), ())), preferred_element_type=F32)

    def produce(hh, kidx):
        st = scores(hh, kidx)
        return st, jnp.max(st, axis=0, keepdims=True)

    def consume(hh, buf, kidx, diagonal):
        st = s_ref[hh, buf]
        if diagonal:
            r = lax.broadcasted_iota(jnp.int32, st.shape, 0)
            c = lax.broadcasted_iota(jnp.int32, st.shape, 1)
            c = jnp.where(c >= tq, c - tq, c)
            st = jnp.where(r <= c, st, NEG)
            block_max = jnp.max(st, axis=0, keepdims=True)
        else:
            block_max = bm_ref[hh, buf]
        m = m_ref[hh]
        m_new = jnp.maximum(m, block_max)
        alpha = jnp.exp2(m - m_new)
        p = jnp.exp2(st - m_new)
        m_ref[hh] = m_new
        pv = jnp.dot(vt_ref[hh, kidx], p.astype(BF16), preferred_element_type=F32)
        acc_ref[hh] = alpha * acc_ref[hh] + pv

    def step(buf, kidx):
        nxt = [produce(hh, kidx + 1) for hh in heads]
        for hh in heads:
            consume(hh, buf, kidx, False)
        for hh in heads:
            s_ref[hh, 1 - buf], bm_ref[hh, 1 - buf] = nxt[hh]

    for hh in heads:
        s_ref[hh, 0], bm_ref[hh, 0] = produce(hh, 0)

    def steps(first, n):
        for u in range(n):
            step(u % 2, first + u)

    def body(c, carry):
        steps(c * DIFF_UNROLL, DIFF_UNROLL)
        return carry

    lax.fori_loop(0, qi // DIFF_UNROLL, body, 0)
    for rem in range(DIFF_UNROLL):
        @pl.when(qi % DIFF_UNROLL == rem)
        def _(rem=rem):
            steps(qi - rem, rem)
            for hh in heads:
                consume(hh, rem % 2, qi, True)

    lv = lam_ref[...]
    lam = (jnp.exp(jnp.sum(lv[0:1] * lv[1:2], axis=1, keepdims=True))
           - jnp.exp(jnp.sum(lv[2:3] * lv[3:4], axis=1, keepdims=True)) + lam_init)
    for hh in heads:
        l = acc_ref[hh, C_VDIM:C_VDIM + 1, :]
        acc = acc_ref[hh, 0:C_VDIM, :]
        ot = acc[:, :tq] / l[:, :tq] - lam * (acc[:, tq:] / l[:, tq:])
        ms = jnp.mean(ot * ot, axis=0, keepdims=True)
        yt = ot * lax.rsqrt(ms + DIFF_EPS) * (g_ref[...] * (1.0 - lam_init))
        o_ref[:, hh * C_VDIM:(hh + 1) * C_VDIM] = yt.T.astype(o_ref.dtype)


def _diff(proj, vt, lam_vecs, subln, *, B, S, lam_init, tq=512):
    nq = S // tq
    nkb = S // DIFF_KB
    width = DIFF_HEADS * C_VDIM
    assert tq == DIFF_KB and DIFF_UNROLL % 2 == 0 and C_HEADS % DIFF_HEADS == 0
    return pl.pallas_call(
        functools.partial(_diff_kernel, tq=tq, lam_init=lam_init),
        grid=(B, C_HEADS // DIFF_HEADS, nq),
        in_specs=[
            pl.BlockSpec((tq, width), lambda b, h, i: (b * nq + i, COL_QC // width + h)),
            pl.BlockSpec((S, width), lambda b, h, i: (b, COL_KC // width + h)),
            pl.BlockSpec((None, DIFF_HEADS, nkb, VT_ROWS, DIFF_KB), lambda b, h, i: (b, h, 0, 0, 0)),
            pl.BlockSpec((4, HEAD_DIM), lambda b, h, i: (0, 0)),
            pl.BlockSpec((C_VDIM, 1), lambda b, h, i: (0, 0)),
        ],
        out_specs=pl.BlockSpec((tq, width), lambda b, h, i: (b * nq + i, h)),
        out_shape=jax.ShapeDtypeStruct((B * S, C_WIDTH), BF16),
        scratch_shapes=[pltpu.VMEM((DIFF_HEADS, 2 * tq, C_VDIM), BF16),
                        pltpu.VMEM((DIFF_HEADS, 2, DIFF_KB, 2 * tq), F32),
                        pltpu.VMEM((DIFF_HEADS, 2, 1, 2 * tq), F32),
                        pltpu.VMEM((DIFF_HEADS, 1, 2 * tq), F32),
                        pltpu.VMEM((DIFF_HEADS, VT_ROWS, 2 * tq), F32)],
        compiler_params=_cparams(("parallel", "parallel", "arbitrary")),
        name="diff",
    )(proj, proj, vt, lam_vecs, subln.reshape(C_VDIM, 1))


OUT_SUB = 256


def _out_kernel(oa_ref, ob_ref, oc_ref, w_ref, h_ref, g_ref, h1_ref, hn_ref, *, tm):
    subs = [slice(s * OUT_SUB, (s + 1) * OUT_SUB) for s in range(tm // OUT_SUB)]
    accs = []
    for rows in subs:
        acc = jnp.dot(oa_ref[rows, :], w_ref[0:A_WIDTH, :], preferred_element_type=F32)
        acc = acc + jnp.dot(ob_ref[rows, :], w_ref[A_WIDTH:A_WIDTH + B_WIDTH, :], preferred_element_type=F32)
        acc = acc + jnp.dot(oc_ref[rows, :], w_ref[A_WIDTH + B_WIDTH:, :], preferred_element_type=F32)
        accs.append(acc)
    for rows, acc in zip(subs, accs):
        h1 = h_ref[rows, :] + acc
        h1_ref[rows, :] = h1
        ms = jnp.mean(h1 * h1, axis=1, keepdims=True)
        hn = h1 * lax.rsqrt(ms + NORM_EPS) * g_ref[...]
        hn_ref[rows, :] = _pack_rows(hn)


def _out(oa, ob, oc, w, h, g, *, tm=512):
    T, D = h.shape
    return pl.pallas_call(
        functools.partial(_out_kernel, tm=tm),
        grid=(T // tm,),
        in_specs=[
            pl.BlockSpec((tm, A_WIDTH), lambda i: (i, 0)),
            pl.BlockSpec((tm, B_WIDTH), lambda i: (i, 0)),
            pl.BlockSpec((tm, C_WIDTH), lambda i: (i, 0)),
            pl.BlockSpec((D, D), lambda i: (0, 0)),
            pl.BlockSpec((tm, D), lambda i: (i, 0)),
            pl.BlockSpec((1, D), lambda i: (0, 0)),
        ],
        out_specs=[
            pl.BlockSpec((tm, D), lambda i: (i, 0)),
            pl.BlockSpec((tm, D_PACK), lambda i: (i, 0)),
        ],
        out_shape=[
            jax.ShapeDtypeStruct((T, D), F32),
            jax.ShapeDtypeStruct((T, D_PACK), U32),
        ],
        compiler_params=_cparams(("parallel",)),
        name="out",
    )(oa, ob, oc, w, h, g.reshape(1, D))


ROUTER_ROWS = 128
ROUTER_LO = 64
ROUTER_GROUP_ROW = 0
ROUTER_EXPERT_ROW = 8


def _router_kernel(x_ref, wt_ref, b_ref, e_ref, g_ref):
    x = _unpack_rows_bf16(x_ref[...])
    lg = lax.dot_general(wt_ref[...], x, (((1,), (1,)), ((), ())), preferred_element_type=F32)
    lg = lg[0:ROUTER_LO] + lg[ROUTER_LO:ROUTER_ROWS] + b_ref[...]
    grp = lg[ROUTER_GROUP_ROW:ROUTER_GROUP_ROW + N_GROUPS]
    ex = lg[ROUTER_EXPERT_ROW:ROUTER_EXPERT_ROW + N_EXPERTS]

    mg = jnp.max(grp, axis=0, keepdims=True)
    pg_sel = 1.0 / jnp.sum(jnp.exp(grp - mg), axis=0, keepdims=True)
    gi = lax.broadcasted_iota(jnp.int32, grp.shape, 0)
    g_sel = jnp.min(jnp.where(grp == mg, gi, N_GROUPS), axis=0, keepdims=True)

    er = lax.broadcasted_iota(jnp.int32, ex.shape, 0)
    group_of = lax.shift_right_logical(er, EXPERTS_PER_GROUP.bit_length() - 1)
    cand = jnp.where(group_of == g_sel, ex, NEG)
    v1 = jnp.max(cand, axis=0, keepdims=True)
    i1 = jnp.min(jnp.where(cand == v1, er, N_EXPERTS), axis=0, keepdims=True)
    rest = jnp.where(er == i1, NEG, cand)
    v2 = jnp.max(rest, axis=0, keepdims=True)
    i2 = jnp.min(jnp.where(rest == v2, er, N_EXPERTS), axis=0, keepdims=True)
    t = jnp.exp(v2 - v1)
    g1 = pg_sel / (1.0 + t)
    e_ref[...] = jnp.concatenate([i1, i2], axis=0)
    g_ref[...] = jnp.concatenate([g1, g1 * t], axis=0)


def _router(hn_packed, wt, b, *, tm=1024):
    T = hn_packed.shape[0]
    D = wt.shape[1]
    return pl.pallas_call(
        _router_kernel,
        grid=(T // tm,),
        in_specs=[
            pl.BlockSpec((tm, D_PACK), lambda i: (i, 0)),
            pl.BlockSpec((ROUTER_ROWS, D), lambda i: (0, 0)),
            pl.BlockSpec((ROUTER_LO, 1), lambda i: (0, 0)),
        ],
        out_specs=[pl.BlockSpec((TOP_K, tm), lambda i: (0, i)), pl.BlockSpec((TOP_K, tm), lambda i: (0, i))],
        out_shape=[jax.ShapeDtypeStruct((TOP_K, T), jnp.int32), jax.ShapeDtypeStruct((TOP_K, T), F32)],
        compiler_params=_cparams(("parallel",)),
        name="router",
    )(hn_packed, wt, b)


EXPERT_SUB = 256
EXPERT_VMEM = 56 * 1024 * 1024


def _expert_kernel(blk_ref, exp_ref, lo_ref, hi_ref, x_ref, g_ref, wg_ref, wu_ref, wd_ref, o_ref,
                   wg_bf, wu_bf, wd_bf, *, tm):
    p = pl.program_id(0)
    prev = jnp.maximum(p - 1, 0)
    blk = blk_ref[p]
    first = (p == 0) | (blk != blk_ref[prev])
    new_expert = (p == 0) | (exp_ref[p] != exp_ref[prev])
    lo = lo_ref[p]
    hi = hi_ref[p]

    @pl.when(new_expert)
    def _():
        wg_bf[...] = wg_ref[...].astype(BF16)
        wu_bf[...] = wu_ref[...].astype(BF16)
        wd_bf[...] = wd_ref[...].astype(BF16)

    @pl.when(first)
    def _():
        o_ref[...] = jnp.zeros_like(o_ref)

    for s in range(tm // EXPERT_SUB):
        row0 = blk * tm + s * EXPERT_SUB
        rows = slice(s * EXPERT_SUB, (s + 1) * EXPERT_SUB)

        @pl.when((hi > row0) & (lo < row0 + EXPERT_SUB))
        def _(row0=row0, rows=rows):
            x = _unpack_rows_bf16(x_ref[rows, :])
            a = jnp.dot(x, wg_bf[...], preferred_element_type=F32)
            u = jnp.dot(x, wu_bf[...], preferred_element_type=F32)
            act = (a / (1.0 + jnp.exp(-a))) * u
            y = jnp.dot(act.astype(BF16), wd_bf[...], preferred_element_type=F32)
            r = row0 + lax.broadcasted_iota(jnp.int32, (EXPERT_SUB, 1), 0)
            mine = (r >= lo) & (r < hi)
            o_ref[rows, :] = jnp.where(mine, _pack_rows(y * g_ref[rows, :]), o_ref[rows, :])


def _experts(seg_blk, seg_exp, seg_lo, seg_hi, xs, gates, wg, wu, wd, layer, *, tm):
    N = xs.shape[0]
    D = wg.shape[2]
    P = seg_blk.shape[0]
    grid_spec = pltpu.PrefetchScalarGridSpec(
        num_scalar_prefetch=4,
        grid=(P,),
        in_specs=[
            pl.BlockSpec((tm, D_PACK), lambda p, blk, ex, lo, hi: (blk[p], 0)),
            pl.BlockSpec((tm, 1), lambda p, blk, ex, lo, hi: (blk[p], 0)),
            pl.BlockSpec((None, None, D, D_FF), lambda p, blk, ex, lo, hi: (layer, ex[p], 0, 0)),
            pl.BlockSpec((None, None, D, D_FF), lambda p, blk, ex, lo, hi: (layer, ex[p], 0, 0)),
            pl.BlockSpec((None, None, D_FF, D), lambda p, blk, ex, lo, hi: (layer, ex[p], 0, 0)),
        ],
        out_specs=pl.BlockSpec((tm, D_PACK), lambda p, blk, ex, lo, hi: (blk[p], 0)),
        scratch_shapes=[pltpu.VMEM((D, D_FF), BF16), pltpu.VMEM((D, D_FF), BF16), pltpu.VMEM((D_FF, D), BF16)],
    )
    return pl.pallas_call(
        functools.partial(_expert_kernel, tm=tm),
        grid_spec=grid_spec,
        out_shape=jax.ShapeDtypeStruct((N, D_PACK), U32),
        compiler_params=_cparams(("arbitrary",), EXPERT_VMEM),
        name="experts",
    )(seg_blk, seg_exp, seg_lo, seg_hi, xs, gates, wg, wu, wd)


def _route(eidx_t, gates_t, T, tm):
    N = T * TOP_K
    flat_e = eidx_t.T.reshape(N)
    iota = jnp.arange(N, dtype=jnp.int32)
    sorted_e, order, g_sorted = lax.sort((flat_e, iota, gates_t.T.reshape(N)), num_keys=1, is_stable=True)
    tok = order // TOP_K
    _, pos = lax.sort((order, iota), num_keys=1)
    counts = jnp.bincount(flat_e, length=N_EXPERTS)
    starts = (jnp.cumsum(counts) - counts).astype(jnp.int32)
    nb = N // tm
    bounds = jnp.sort(jnp.concatenate([jnp.arange(nb, dtype=jnp.int32) * tm, starts]))
    ends = jnp.concatenate([bounds[1:], jnp.full((1,), N, bounds.dtype)])
    seg_blk = jnp.minimum(bounds // tm, nb - 1).astype(jnp.int32)
    seg_exp = sorted_e[jnp.minimum(bounds, N - 1)].astype(jnp.int32)
    pos_rows = pos.reshape(T, TOP_K).T.reshape(N)
    return tok, g_sorted, pos_rows, seg_blk, seg_exp, bounds, ends


def _group_major(a, axis):
    shape = a.shape
    a = a.reshape(shape[:axis] + (A_KV_HEADS, A_GROUP, HEAD_DIM) + shape[axis + 1:])
    return jnp.swapaxes(a, axis, axis + 1).reshape(shape)


def _permute_in_proj(w):
    bounds = np.cumsum([0, 768, 256, 256, 512, 768, 768, 768])
    qa, ka, va, ub, qc, kc, vc = [w[:, bounds[s]:bounds[s + 1]] for s in range(7)]
    return jnp.concatenate([_group_major(qa, 1), qc, ka, kc, ub, va, vc], axis=1)


def kernel(x, positions, attn_norm, w_in, sinks, branch_norm_a, w_pool, pool_scale, lambda_q1, lambda_k1,
           lambda_q2, lambda_k2, subln, w_out, ffn_norm, w_router_group, b_router_group, w_router_expert,
           b_router_expert, w_expert_gate, w_expert_up, w_expert_down, final_norm):
    B, S, D = x.shape
    T = B * S
    depth = w_in.shape[0]
    tm_e = 512

    half = HEAD_DIM // 2
    inv = ROPE_THETA ** (-jnp.arange(half, dtype=F32) / half)
    ang = positions.astype(F32).reshape(T, 1) * inv
    cos, sin = jnp.cos(ang), jnp.sin(ang)
    cos_t = jnp.concatenate([cos, cos, cos, cos], axis=1)
    sin_t = jnp.concatenate([-sin, sin, -sin, sin], axis=1)

    h = x.reshape(T, D)
    y_rows = None
    for l in range(depth):
        if l == 0:
            (hn,) = _norm(h, None, attn_norm[l], emit_h=False, emit_bf16=True)
        else:
            h, hn = _norm(h, y_rows, attn_norm[l], emit_h=True, emit_bf16=True)
        w_in_l = _permute_in_proj(w_in[l]).astype(BF16)
        proj, vat, vt = _proj(hn, w_in_l[:, :COL_VA], w_in_l[:, COL_VA:].T, cos_t, sin_t, B=B, S=S)

        sink_heads = sinks[l].reshape(A_KV_HEADS, A_GROUP).T.reshape(A_HEADS) * LOG2E
        sink_row = jnp.repeat(sink_heads, WINDOW).reshape(1, A_COLS)
        oa = _swa(proj, vat, sink_row, _group_major(branch_norm_a[l], 0), B=B, S=S)
        ob = _pool(proj, w_pool[l].astype(BF16), pool_scale[l], B=B, S=S)

        lam_init = 0.8 - 0.6 * math.exp(-0.3 * l)
        lam_vecs = jnp.stack([lambda_q1[l], lambda_k1[l], lambda_q2[l], lambda_k2[l]]).astype(F32)
        oc = _diff(proj, vt, lam_vecs, subln[l], B=B, S=S, lam_init=lam_init)

        w_out_l = jnp.concatenate([_group_major(w_out[l][:A_WIDTH], 0), w_out[l][A_WIDTH:]], axis=0)
        h1, hn2 = _out(oa, ob, oc, w_out_l.astype(BF16), h, ffn_norm[l])

        wr = jnp.zeros((ROUTER_LO, D), F32)
        wr = wr.at[ROUTER_GROUP_ROW:ROUTER_GROUP_ROW + N_GROUPS].set(w_router_group[l].T)
        wr = wr.at[ROUTER_EXPERT_ROW:ROUTER_EXPERT_ROW + N_EXPERTS].set(w_router_expert[l].T)
        wr_hi = wr.astype(BF16)
        wr_lo = (wr - wr_hi.astype(F32)).astype(BF16)
        br = jnp.zeros((ROUTER_LO, 1), F32)
        br = br.at[ROUTER_GROUP_ROW:ROUTER_GROUP_ROW + N_GROUPS, 0].set(b_router_group[l])
        br = br.at[ROUTER_EXPERT_ROW:ROUTER_EXPERT_ROW + N_EXPERTS, 0].set(b_router_expert[l])
        eidx_t, gates_t = _router(hn2, jnp.concatenate([wr_hi, wr_lo], axis=0), br)

        tok, g_sorted, pos_rows, seg_blk, seg_exp, seg_lo, seg_hi = _route(eidx_t, gates_t, T, tm_e)
        xs = _sc_gather_rows(hn2, tok)
        ys = _experts(seg_blk, seg_exp, seg_lo, seg_hi, xs, g_sorted.reshape(-1, 1),
                      w_expert_gate, w_expert_up, w_expert_down, l, tm=tm_e)
        y_rows = _sc_gather_rows(ys, pos_rows)
        h = h1

    (out,) = _norm(h, y_rows, final_norm, emit_h=False, emit_bf16=False)
    return out.reshape(B, S, D)
```

```python
import functools
import math

import jax
import jax.numpy as jnp
from jax import lax
from jax.experimental import pallas as pl
from jax.experimental.pallas import tpu as pltpu
from jax.experimental.pallas import tpu_sc as plsc
import numpy as np

F32 = jnp.float32
BF16 = jnp.bfloat16

D_MODEL = 2048
HEAD_DIM = 64
ROPE_THETA = 10000.0
A_HEADS = 12
A_KV_HEADS = 4
A_GROUP = 3
A_WIDTH = 768
WINDOW = 128
POOL_WINDOWS = (2, 4, 8, 16)
B_WIDTH = 512
B_GROUP_DIM = 128
C_VDIM = 128
C_WIDTH = 768
C_HEADS = 6
DIFF_EPS = 1e-5
IN_COLS = 4096
N_GROUPS = 4
EXPERTS_PER_GROUP = 8
N_EXPERTS = 32
TOP_K = 2
D_FF = 512
NORM_EPS = 1e-6
NEG = -1e30
LOG2E = math.log2(math.e)
Q_SCALE = HEAD_DIM ** -0.5 * LOG2E

COL_QA, COL_QC, COL_KA, COL_KC, COL_UB, COL_VA, COL_VC = 0, 768, 1536, 1792, 2560, 3072, 3328
PROJ_TN = 256
PROJ_SUB = 256
N_SCALED_TILES = COL_KA // PROJ_TN
N_ROPE_TILES = COL_UB // PROJ_TN
N_MAIN_TILES = COL_VA // PROJ_TN
DIFF_KB = 512
VT_ROWS = C_VDIM + 16

VMEM_LIMIT = 48 * 1024 * 1024
D_PACK = D_MODEL // 2
U32 = jnp.uint32


def _cparams(sem, vmem=VMEM_LIMIT):
    return pltpu.CompilerParams(dimension_semantics=sem, vmem_limit_bytes=vmem)


def _pack_rows(x):
    lo = pltpu.bitcast(x[:, :D_PACK].astype(BF16).astype(F32), U32) >> 16
    hi = pltpu.bitcast(x[:, D_PACK:].astype(BF16).astype(F32), U32)
    return hi | lo


def _unpack_rows(w):
    return pltpu.bitcast(w << 16, F32), pltpu.bitcast(w & U32(0xFFFF0000), F32)


def _unpack_rows_bf16(w):
    lo, hi = _unpack_rows(w)
    return jnp.concatenate([lo.astype(BF16), hi.astype(BF16)], axis=1)


SC_WINDOW = 32
SC_INDEX_LANES = 128


def _sc_gather_rows(x, idx):
    M = idx.shape[0]
    W = x.shape[1]
    idx2 = jnp.pad(idx.reshape(M // SC_WINDOW, SC_WINDOW), ((0, 0), (0, SC_INDEX_LANES - SC_WINDOW)))
    mesh = plsc.VectorSubcoreMesh(core_axis_name="core", subcore_axis_name="subcore")

    @pl.kernel(out_type=jax.ShapeDtypeStruct((M, W), x.dtype), mesh=mesh)
    def gather(x_hbm, i_hbm, o_hbm):
        def body(i_vmem, o_vmem):
            pltpu.sync_copy(x_hbm.at[i_vmem.at[0, pl.ds(0, SC_WINDOW)]], o_vmem)

        pltpu.emit_pipeline(
            body,
            grid=(M // SC_WINDOW,),
            in_specs=[pl.BlockSpec((1, SC_INDEX_LANES), index_map=lambda i: (i, 0))],
            out_specs=[pl.BlockSpec((SC_WINDOW, W), index_map=lambda i: (i, 0))],
            core_axis_name=("core", "subcore"),
            dimension_semantics=(pltpu.PARALLEL,),
        )(i_hbm, o_hbm)

    return gather(x, idx2)


def _norm_kernel(*refs, n_add, emit_h, emit_bf16, eps):
    h_ref = refs[0]
    add_refs = refs[1:1 + n_add]
    g_ref = refs[1 + n_add]
    outs = refs[2 + n_add:]
    h = h_ref[...]
    if n_add:
        lo = jnp.zeros((h.shape[0], D_PACK), F32)
        hi = jnp.zeros((h.shape[0], D_PACK), F32)
        for r in add_refs:
            a, b = _unpack_rows(r[...])
            lo, hi = lo + a, hi + b
        h = h + jnp.concatenate([lo, hi], axis=1)
    ms = jnp.mean(h * h, axis=-1, keepdims=True)
    y = h * lax.rsqrt(ms + eps) * g_ref[...]
    k = 0
    if emit_h:
        outs[k][...] = h
        k += 1
    outs[k][...] = y.astype(BF16 if emit_bf16 else F32)


def _norm(h, y_rows, g, *, emit_h, emit_bf16, tm=512):
    T, D = h.shape
    nt = T // tm
    row = pl.BlockSpec((tm, D), lambda i: (i, 0))
    n_add = 0 if y_rows is None else TOP_K
    add_specs = [pl.BlockSpec((tm, D_PACK), lambda i, k=k: (i + k * nt, 0)) for k in range(n_add)]
    out_shape, out_specs = [], []
    if emit_h:
        out_shape.append(jax.ShapeDtypeStruct((T, D), F32))
        out_specs.append(row)
    out_shape.append(jax.ShapeDtypeStruct((T, D), BF16 if emit_bf16 else F32))
    out_specs.append(row)
    return pl.pallas_call(
        functools.partial(_norm_kernel, n_add=n_add, emit_h=emit_h, emit_bf16=emit_bf16, eps=NORM_EPS),
        grid=(nt,),
        in_specs=[row] + add_specs + [pl.BlockSpec((1, D), lambda i: (0, 0))],
        out_specs=out_specs,
        out_shape=out_shape,
        compiler_params=_cparams(("parallel",)),
        name="norm",
    )(h, *([y_rows] * n_add), g.reshape(1, D))


def _proj_kernel(x_ref, w_ref, wt_ref, cos_ref, sin_ref, o_ref, vat_ref, vt_ref, *, tm):
    j = pl.program_id(1)
    subs = [slice(s * PROJ_SUB, (s + 1) * PROJ_SUB) for s in range(tm // PROJ_SUB)]

    def matmuls():
        return [jnp.dot(x_ref[rows, :], w_ref[...], preferred_element_type=F32) for rows in subs]

    @pl.when(j < N_ROPE_TILES)
    def _():
        accs = matmuls()
        scale = jnp.where(j < N_SCALED_TILES, Q_SCALE, 1.0).astype(F32)
        lane = lax.broadcasted_iota(jnp.int32, (PROJ_SUB, PROJ_TN), 1)
        first_half = (lane & (HEAD_DIM // 2)) == 0
        for rows, acc in zip(subs, accs):
            c = jnp.concatenate([cos_ref[rows, :] * scale] * (PROJ_TN // 128), axis=1)
            s = jnp.concatenate([sin_ref[rows, :] * scale] * (PROJ_TN // 128), axis=1)
            swapped = jnp.where(first_half,
                                pltpu.roll(acc, PROJ_TN - HEAD_DIM // 2, 1),
                                pltpu.roll(acc, HEAD_DIM // 2, 1))
            o_ref[rows, :] = (acc * c + swapped * s).astype(o_ref.dtype)

    @pl.when((j >= N_ROPE_TILES) & (j < N_MAIN_TILES))
    def _():
        for rows, acc in zip(subs, matmuls()):
            o_ref[rows, :] = acc.astype(o_ref.dtype)

    def matmuls_t():
        return [lax.dot_general(wt_ref[...], x_ref[rows, :], (((1,), (1,)), ((), ())),
                                preferred_element_type=F32) for rows in subs]

    @pl.when(j == N_MAIN_TILES)
    def _():
        per_sub = PROJ_SUB // WINDOW
        for s, acc in enumerate(matmuls_t()):
            for b in range(per_sub):
                vat_ref[s * per_sub + b] = acc[:, b * WINDOW:(b + 1) * WINDOW].astype(vat_ref.dtype)

    @pl.when(j > N_MAIN_TILES)
    def _():
        accs = matmuls_t()
        row = lax.broadcasted_iota(jnp.int32, (VT_ROWS - C_VDIM, DIFF_KB), 0)
        extra = jnp.where(row == 0, 1.0, 0.0).astype(vt_ref.dtype)
        per_kb = DIFF_KB // PROJ_SUB
        for s, acc in enumerate(accs):
            cols = slice((s % per_kb) * PROJ_SUB, (s % per_kb + 1) * PROJ_SUB)
            for hh in range(PROJ_TN // C_VDIM):
                vt_ref[hh, s // per_kb, 0:C_VDIM, cols] = acc[hh * C_VDIM:(hh + 1) * C_VDIM, :].astype(vt_ref.dtype)
        for hh in range(PROJ_TN // C_VDIM):
            for kb in range(tm // DIFF_KB):
                vt_ref[hh, kb, C_VDIM:VT_ROWS, :] = extra


def _proj(hn, w, wt, cos_t, sin_t, *, B, S, tm=2048):
    T, D = hn.shape
    spt = S // tm
    last_main = N_MAIN_TILES - 1
    return pl.pallas_call(
        functools.partial(_proj_kernel, tm=tm),
        grid=(T // tm, IN_COLS // PROJ_TN),
        in_specs=[
            pl.BlockSpec((tm, D), lambda i, j: (i, 0)),
            pl.BlockSpec((D, PROJ_TN), lambda i, j: (0, jnp.minimum(j, last_main))),
            pl.BlockSpec((PROJ_TN, D), lambda i, j: (jnp.maximum(j - N_MAIN_TILES, 0), 0)),
            pl.BlockSpec((tm, 128), lambda i, j: (i, 0)),
            pl.BlockSpec((tm, 128), lambda i, j: (i, 0)),
        ],
        out_specs=[
            pl.BlockSpec((tm, PROJ_TN), lambda i, j: (i, jnp.minimum(j, last_main))),
            pl.BlockSpec((tm // WINDOW, PROJ_TN, WINDOW), lambda i, j: (i, 0, 0)),
            pl.BlockSpec((None, PROJ_TN // C_VDIM, tm // DIFF_KB, VT_ROWS, DIFF_KB),
                         lambda i, j: (i // spt, jnp.maximum(j - N_MAIN_TILES - 1, 0), i % spt, 0, 0)),
        ],
        out_shape=[
            jax.ShapeDtypeStruct((T, COL_VA), BF16),
            jax.ShapeDtypeStruct((T // WINDOW, PROJ_TN, WINDOW), BF16),
            jax.ShapeDtypeStruct((B, C_HEADS, S // DIFF_KB, VT_ROWS, DIFF_KB), BF16),
        ],
        compiler_params=_cparams(("parallel", "arbitrary")),
        name="proj",
    )(hn, w, wt, cos_t, sin_t)


A_KV_WIDTH = A_KV_HEADS * HEAD_DIM
A_COLS = A_HEADS * WINDOW


def _swa_kernel(q_ref, kc_ref, kp_ref, vc_ref, vp_ref, sink_ref, g_ref, o_ref, *, tq):
    i = pl.program_id(1)
    nblk = tq // WINDOW
    kext = jnp.concatenate([kp_ref[...], kc_ref[...]], axis=0)
    lane_kv = lax.broadcasted_iota(jnp.int32, (WINDOW, A_KV_WIDTH), 1) // HEAD_DIM
    zero = jnp.zeros((WINDOW, A_KV_WIDTH), BF16)

    scores = []
    for b in range(nblk):
        parts = []
        for g in range(A_GROUP):
            qg = q_ref[b * WINDOW:(b + 1) * WINDOW, g * A_KV_WIDTH:(g + 1) * A_KV_WIDTH]
            parts += [jnp.where(lane_kv == j, qg, zero) for j in range(A_KV_HEADS)]
        qcat = jnp.concatenate(parts, axis=0)
        scores.append(lax.dot_general(kext[b * WINDOW:(b + 2) * WINDOW, :], qcat, (((1,), (1,)), ((), ())),
                                      preferred_element_type=F32))

    r = lax.broadcasted_iota(jnp.int32, (2 * WINDOW, A_COLS), 0)
    c = lax.broadcasted_iota(jnp.int32, (2 * WINDOW, A_COLS), 1) & (WINDOW - 1)
    band = (r > c) & (r <= c + WINDOW)
    sink = sink_ref[...]
    gain = g_ref[...]
    for b, st in enumerate(scores):
        mask = band
        if b == 0:
            mask = band & ((r >= WINDOW) | (i > 0))
        st = jnp.where(mask, st, NEG)
        m = jnp.maximum(jnp.max(st, axis=0, keepdims=True), sink)
        p = jnp.exp2(st - m)
        l = jnp.sum(p, axis=0, keepdims=True) + jnp.exp2(sink - m)
        vt = jnp.concatenate([vp_ref[0] if b == 0 else vc_ref[b - 1], vc_ref[b]], axis=1)
        pv = jnp.dot(vt, p.astype(BF16), preferred_element_type=F32)
        inv_l = 1.0 / l
        heads = []
        for g in range(A_GROUP):
            for j in range(A_KV_HEADS):
                cols = slice((g * A_KV_HEADS + j) * WINDOW, (g * A_KV_HEADS + j + 1) * WINDOW)
                heads.append(pv[j * HEAD_DIM:(j + 1) * HEAD_DIM, cols] * inv_l[:, cols])
        ot = jnp.concatenate(heads, axis=0)
        ms = jnp.mean(ot * ot, axis=0, keepdims=True)
        y = (ot * lax.rsqrt(ms + NORM_EPS)).T * gain
        o_ref[b * WINDOW:(b + 1) * WINDOW, :] = y.astype(o_ref.dtype)


def _swa(proj, vat, sink_row, g, *, B, S, tq=512):
    nq = S // tq
    rpb = tq // WINDOW

    def prev_blk(b, i):
        return jnp.maximum(b * (S // WINDOW) + i * rpb - 1, 0)

    return pl.pallas_call(
        functools.partial(_swa_kernel, tq=tq),
        grid=(B, nq),
        in_specs=[
            pl.BlockSpec((tq, A_WIDTH), lambda b, i: (b * nq + i, COL_QA // A_WIDTH)),
            pl.BlockSpec((tq, A_KV_WIDTH), lambda b, i: (b * nq + i, COL_KA // A_KV_WIDTH)),
            pl.BlockSpec((WINDOW, A_KV_WIDTH), lambda b, i: (prev_blk(b, i), COL_KA // A_KV_WIDTH)),
            pl.BlockSpec((rpb, A_KV_WIDTH, WINDOW), lambda b, i: (b * nq + i, 0, 0)),
            pl.BlockSpec((1, A_KV_WIDTH, WINDOW), lambda b, i: (prev_blk(b, i), 0, 0)),
            pl.BlockSpec((1, A_COLS), lambda b, i: (0, 0)),
            pl.BlockSpec((1, A_WIDTH), lambda b, i: (0, 0)),
        ],
        out_specs=pl.BlockSpec((tq, A_WIDTH), lambda b, i: (b * nq + i, 0)),
        out_shape=jax.ShapeDtypeStruct((B * S, A_WIDTH), BF16),
        compiler_params=_cparams(("parallel", "parallel")),
        name="swa",
    )(proj, proj, proj, vat, vat, sink_row, g.reshape(1, A_WIDTH))


POOL_HALO = 16


def _pool_kernel(u0_ref, u1_ref, h0_ref, h1_ref, w_ref, s_ref, o_ref, *, tq):
    i = pl.program_id(1)
    u = jnp.concatenate([u0_ref[...], u1_ref[...]], axis=1).astype(F32)
    halo = jnp.concatenate([h0_ref[...], h1_ref[...]], axis=1).astype(F32)
    halo = halo * (i > 0).astype(F32)
    ext = jnp.concatenate([halo, u], axis=0)
    t = i * tq + lax.broadcasted_iota(jnp.int32, (tq, 1), 0)
    for gi, w in enumerate(POOL_WINDOWS):
        sl = slice(gi * B_GROUP_DIM, (gi + 1) * B_GROUP_DIM)
        s = ext[:, sl]
        shift = 1
        while shift < w:
            s = s + pltpu.roll(s, shift, 0)
            shift *= 2
        cnt = jnp.minimum(t + 1, w).astype(F32)
        y = (s[POOL_HALO:, :] / cnt - u[:, sl]).astype(BF16)
        o = jnp.dot(y, w_ref[gi], preferred_element_type=F32) * s_ref[:, sl]
        o_ref[:, sl] = o.astype(o_ref.dtype)


def _pool(proj, w_pool, scale, *, B, S, tq=256):
    nq = S // tq
    c0 = COL_UB // 256

    def cur(c):
        return pl.BlockSpec((tq, 256), lambda b, i: (b * nq + i, c))

    def halo(c):
        return pl.BlockSpec(
            (POOL_HALO, 256),
            lambda b, i: (jnp.maximum((b * S + i * tq) // POOL_HALO - 1, 0), c))

    return pl.pallas_call(
        functools.partial(_pool_kernel, tq=tq),
        grid=(B, nq),
        in_specs=[cur(c0), cur(c0 + 1), halo(c0), halo(c0 + 1),
                  pl.BlockSpec((len(POOL_WINDOWS), B_GROUP_DIM, B_GROUP_DIM), lambda b, i: (0, 0, 0)),
                  pl.BlockSpec((1, B_WIDTH), lambda b, i: (0, 0))],
        out_specs=pl.BlockSpec((tq, B_WIDTH), lambda b, i: (b * nq + i, 0)),
        out_shape=jax.ShapeDtypeStruct((B * S, B_WIDTH), BF16),
        compiler_params=_cparams(("parallel", "parallel")),
        name="pool",
    )(proj, proj, proj, proj, w_pool, scale.reshape(1, B_WIDTH))


DIFF_UNROLL = 4


def _diff_kernel(q_ref, k_ref, vt_ref, lam_ref, g_ref, o_ref, qcat_ref, s_ref, bm_ref, m_ref, acc_ref,
                 *, tq, lam_init):
    qi = pl.program_id(2)
    q = q_ref[...]
    lane = lax.broadcasted_iota(jnp.int32, q.shape, 1)
    zero = jnp.zeros_like(q)
    qcat_ref[0:tq, :] = jnp.where(lane < HEAD_DIM, q, zero)
    qcat_ref[tq:2 * tq, :] = jnp.where(lane >= HEAD_DIM, q, zero)
    m_ref[...] = jnp.full(m_ref.shape, NEG, F32)
    acc_ref[...] = jnp.zeros(acc_ref.shape, F32)

    def produce(kidx):
        off = pl.multiple_of(kidx * DIFF_KB, DIFF_KB)
        st = lax.dot_general(k_ref[pl.ds(off, DIFF_KB), :], qcat_ref[...], (((1,), (1,)), ((), ())),
                             preferred_element_type=F32)
        return st, jnp.max(st, axis=0, keepdims=True)

    def store(buf, block):
        s_ref[buf], bm_ref[buf] = block

    def consume(buf, kidx, diagonal):
        st = s_ref[buf]
        if diagonal:
            r = lax.broadcasted_iota(jnp.int32, st.shape, 0)
            c = lax.broadcasted_iota(jnp.int32, st.shape, 1)
            c = jnp.where(c >= tq, c - tq, c)
            st = jnp.where(r <= c, st, NEG)
            block_max = jnp.max(st, axis=0, keepdims=True)
        else:
            block_max = bm_ref[buf]
        m = m_ref[...]
        m_new = jnp.maximum(m, block_max)
        m_ref[...] = m_new
        p = jnp.exp2(st - m_new)
        pv = jnp.dot(vt_ref[kidx], p.astype(BF16), preferred_element_type=F32)
        acc_ref[...] = jnp.exp2(m - m_new) * acc_ref[...] + pv

    def steps(first, n):
        for u in range(n):
            nxt = produce(first + u + 1)
            consume(u % 2, first + u, False)
            store(1 - u % 2, nxt)

    store(0, produce(0))

    def body(c, carry):
        steps(c * DIFF_UNROLL, DIFF_UNROLL)
        return carry

    lax.fori_loop(0, qi // DIFF_UNROLL, body, 0)
    for rem in range(DIFF_UNROLL):
        @pl.when(qi % DIFF_UNROLL == rem)
        def _(rem=rem):
            steps(qi - rem, rem)
            consume(rem % 2, qi, True)

    lv = lam_ref[...]
    lam = (jnp.exp(jnp.sum(lv[0:1] * lv[1:2], axis=1, keepdims=True))
           - jnp.exp(jnp.sum(lv[2:3] * lv[3:4], axis=1, keepdims=True)) + lam_init)
    l = acc_ref[C_VDIM:C_VDIM + 1, :]
    acc = acc_ref[0:C_VDIM, :]
    ot = acc[:, :tq] / l[:, :tq] - lam * (acc[:, tq:] / l[:, tq:])
    ms = jnp.mean(ot * ot, axis=0, keepdims=True)
    yt = ot * lax.rsqrt(ms + DIFF_EPS) * (g_ref[...] * (1.0 - lam_init))
    o_ref[...] = yt.T.astype(o_ref.dtype)


def _diff(proj, vt, lam_vecs, subln, *, B, S, lam_init, tq=512):
    nq = S // tq
    nkb = S // DIFF_KB
    assert tq == DIFF_KB and DIFF_UNROLL % 2 == 0
    return pl.pallas_call(
        functools.partial(_diff_kernel, tq=tq, lam_init=lam_init),
        grid=(B, C_HEADS, nq),
        in_specs=[
            pl.BlockSpec((tq, C_VDIM), lambda b, h, i: (b * nq + i, COL_QC // C_VDIM + h)),
            pl.BlockSpec((S, C_VDIM), lambda b, h, i: (b, COL_KC // C_VDIM + h)),
            pl.BlockSpec((None, None, nkb, VT_ROWS, DIFF_KB), lambda b, h, i: (b, h, 0, 0, 0)),
            pl.BlockSpec((4, HEAD_DIM), lambda b, h, i: (0, 0)),
            pl.BlockSpec((C_VDIM, 1), lambda b, h, i: (0, 0)),
        ],
        out_specs=pl.BlockSpec((tq, C_VDIM), lambda b, h, i: (b * nq + i, h)),
        out_shape=jax.ShapeDtypeStruct((B * S, C_WIDTH), BF16),
        scratch_shapes=[pltpu.VMEM((2 * tq, C_VDIM), BF16),
                        pltpu.VMEM((2, DIFF_KB, 2 * tq), F32),
                        pltpu.VMEM((2, 1, 2 * tq), F32),
                        pltpu.VMEM((1, 2 * tq), F32),
                        pltpu.VMEM((VT_ROWS, 2 * tq), F32)],
        compiler_params=_cparams(("parallel", "parallel", "arbitrary")),
        name="diff",
    )(proj, proj, vt, lam_vecs, subln.reshape(C_VDIM, 1))


OUT_SUB = 256


def _out_kernel(oa_ref, ob_ref, oc_ref, w_ref, h_ref, g_ref, h1_ref, hn_ref, *, tm):
    subs = [slice(s * OUT_SUB, (s + 1) * OUT_SUB) for s in range(tm // OUT_SUB)]
    accs = []
    for rows in subs:
        acc = jnp.dot(oa_ref[rows, :], w_ref[0:A_WIDTH, :], preferred_element_type=F32)
        acc = acc + jnp.dot(ob_ref[rows, :], w_ref[A_WIDTH:A_WIDTH + B_WIDTH, :], preferred_element_type=F32)
        acc = acc + jnp.dot(oc_ref[rows, :], w_ref[A_WIDTH + B_WIDTH:, :], preferred_element_type=F32)
        accs.append(acc)
    for rows, acc in zip(subs, accs):
        h1 = h_ref[rows, :] + acc
        h1_ref[rows, :] = h1
        ms = jnp.mean(h1 * h1, axis=1, keepdims=True)
        hn = h1 * lax.rsqrt(ms + NORM_EPS) * g_ref[...]
        hn_ref[rows, :] = _pack_rows(hn)


def _out(oa, ob, oc, w, h, g, *, tm=512):
    T, D = h.shape
    return pl.pallas_call(
        functools.partial(_out_kernel, tm=tm),
        grid=(T // tm,),
        in_specs=[
            pl.BlockSpec((tm, A_WIDTH), lambda i: (i, 0)),
            pl.BlockSpec((tm, B_WIDTH), lambda i: (i, 0)),
            pl.BlockSpec((tm, C_WIDTH), lambda i: (i, 0)),
            pl.BlockSpec((D, D), lambda i: (0, 0)),
            pl.BlockSpec((tm, D), lambda i: (i, 0)),
            pl.BlockSpec((1, D), lambda i: (0, 0)),
        ],
        out_specs=[
            pl.BlockSpec((tm, D), lambda i: (i, 0)),
            pl.BlockSpec((tm, D_PACK), lambda i: (i, 0)),
        ],
        out_shape=[
            jax.ShapeDtypeStruct((T, D), F32),
            jax.ShapeDtypeStruct((T, D_PACK), U32),
        ],
        compiler_params=_cparams(("parallel",)),
        name="out",
    )(oa, ob, oc, w, h, g.reshape(1, D))


ROUTER_ROWS = 128
ROUTER_LO = 64
ROUTER_GROUP_ROW = 0
ROUTER_EXPERT_ROW = 8


def _router_kernel(x_ref, wt_ref, b_ref, e_ref, g_ref):
    x = _unpack_rows_bf16(x_ref[...])
    lg = lax.dot_general(wt_ref[...], x, (((1,), (1,)), ((), ())), preferred_element_type=F32)
    lg = lg[0:ROUTER_LO] + lg[ROUTER_LO:ROUTER_ROWS] + b_ref[...]
    grp = lg[ROUTER_GROUP_ROW:ROUTER_GROUP_ROW + N_GROUPS]
    ex = lg[ROUTER_EXPERT_ROW:ROUTER_EXPERT_ROW + N_EXPERTS]

    mg = jnp.max(grp, axis=0, keepdims=True)
    pg_sel = 1.0 / jnp.sum(jnp.exp(grp - mg), axis=0, keepdims=True)
    gi = lax.broadcasted_iota(jnp.int32, grp.shape, 0)
    g_sel = jnp.min(jnp.where(grp == mg, gi, N_GROUPS), axis=0, keepdims=True)

    er = lax.broadcasted_iota(jnp.int32, ex.shape, 0)
    group_of = lax.shift_right_logical(er, EXPERTS_PER_GROUP.bit_length() - 1)
    cand = jnp.where(group_of == g_sel, ex, NEG)
    v1 = jnp.max(cand, axis=0, keepdims=True)
    i1 = jnp.min(jnp.where(cand == v1, er, N_EXPERTS), axis=0, keepdims=True)
    rest = jnp.where(er == i1, NEG, cand)
    v2 = jnp.max(rest, axis=0, keepdims=True)
    i2 = jnp.min(jnp.where(rest == v2, er, N_EXPERTS), axis=0, keepdims=True)
    t = jnp.exp(v2 - v1)
    g1 = pg_sel / (1.0 + t)
    e_ref[...] = jnp.concatenate([i1, i2], axis=0)
    g_ref[...] = jnp.concatenate([g1, g1 * t], axis=0)


def _router(hn_packed, wt, b, *, tm=1024):
    T = hn_packed.shape[0]
    D = wt.shape[1]
    return pl.pallas_call(
        _router_kernel,
        grid=(T // tm,),
        in_specs=[
            pl.BlockSpec((tm, D_PACK), lambda i: (i, 0)),
            pl.BlockSpec((ROUTER_ROWS, D), lambda i: (0, 0)),
            pl.BlockSpec((ROUTER_LO, 1), lambda i: (0, 0)),
        ],
        out_specs=[pl.BlockSpec((TOP_K, tm), lambda i: (0, i)), pl.BlockSpec((TOP_K, tm), lambda i: (0, i))],
        out_shape=[jax.ShapeDtypeStruct((TOP_K, T), jnp.int32), jax.ShapeDtypeStruct((TOP_K, T), F32)],
        compiler_params=_cparams(("parallel",)),
        name="router",
    )(hn_packed, wt, b)


EXPERT_SUB = 256
EXPERT_VMEM = 56 * 1024 * 1024


def _expert_kernel(blk_ref, exp_ref, lo_ref, hi_ref, x_ref, g_ref, wg_ref, wu_ref, wd_ref, o_ref,
                   wg_bf, wu_bf, wd_bf, *, tm):
    p = pl.program_id(0)
    prev = jnp.maximum(p - 1, 0)
    blk = blk_ref[p]
    first = (p == 0) | (blk != blk_ref[prev])
    new_expert = (p == 0) | (exp_ref[p] != exp_ref[prev])
    lo = lo_ref[p]
    hi = hi_ref[p]

    @pl.when(new_expert)
    def _():
        wg_bf[...] = wg_ref[...].astype(BF16)
        wu_bf[...] = wu_ref[...].astype(BF16)
        wd_bf[...] = wd_ref[...].astype(BF16)

    @pl.when(first)
    def _():
        o_ref[...] = jnp.zeros_like(o_ref)

    for s in range(tm // EXPERT_SUB):
        row0 = blk * tm + s * EXPERT_SUB
        rows = slice(s * EXPERT_SUB, (s + 1) * EXPERT_SUB)

        @pl.when((hi > row0) & (lo < row0 + EXPERT_SUB))
        def _(row0=row0, rows=rows):
            x = _unpack_rows_bf16(x_ref[rows, :])
            a = jnp.dot(x, wg_bf[...], preferred_element_type=F32)
            u = jnp.dot(x, wu_bf[...], preferred_element_type=F32)
            act = (a / (1.0 + jnp.exp(-a))) * u
            y = jnp.dot(act.astype(BF16), wd_bf[...], preferred_element_type=F32)
            r = row0 + lax.broadcasted_iota(jnp.int32, (EXPERT_SUB, 1), 0)
            mine = (r >= lo) & (r < hi)
            o_ref[rows, :] = jnp.where(mine, _pack_rows(y * g_ref[rows, :]), o_ref[rows, :])


def _experts(seg_blk, seg_exp, seg_lo, seg_hi, xs, gates, wg, wu, wd, layer, *, tm):
    N = xs.shape[0]
    D = wg.shape[2]
    P = seg_blk.shape[0]
    grid_spec = pltpu.PrefetchScalarGridSpec(
        num_scalar_prefetch=4,
        grid=(P,),
        in_specs=[
            pl.BlockSpec((tm, D_PACK), lambda p, blk, ex, lo, hi: (blk[p], 0)),
            pl.BlockSpec((tm, 1), lambda p, blk, ex, lo, hi: (blk[p], 0)),
            pl.BlockSpec((None, None, D, D_FF), lambda p, blk, ex, lo, hi: (layer, ex[p], 0, 0)),
            pl.BlockSpec((None, None, D, D_FF), lambda p, blk, ex, lo, hi: (layer, ex[p], 0, 0)),
            pl.BlockSpec((None, None, D_FF, D), lambda p, blk, ex, lo, hi: (layer, ex[p], 0, 0)),
        ],
        out_specs=pl.BlockSpec((tm, D_PACK), lambda p, blk, ex, lo, hi: (blk[p], 0)),
        scratch_shapes=[pltpu.VMEM((D, D_FF), BF16), pltpu.VMEM((D, D_FF), BF16), pltpu.VMEM((D_FF, D), BF16)],
    )
    return pl.pallas_call(
        functools.partial(_expert_kernel, tm=tm),
        grid_spec=grid_spec,
        out_shape=jax.ShapeDtypeStruct((N, D_PACK), U32),
        compiler_params=_cparams(("arbitrary",), EXPERT_VMEM),
        name="experts",
    )(seg_blk, seg_exp, seg_lo, seg_hi, xs, gates, wg, wu, wd)


def _route(eidx_t, gates_t, T, tm):
    N = T * TOP_K
    flat_e = eidx_t.T.reshape(N)
    iota = jnp.arange(N, dtype=jnp.int32)
    sorted_e, order, g_sorted = lax.sort((flat_e, iota, gates_t.T.reshape(N)), num_keys=1, is_stable=True)
    tok = order // TOP_K
    _, pos = lax.sort((order, iota), num_keys=1)
    counts = jnp.bincount(flat_e, length=N_EXPERTS)
    starts = (jnp.cumsum(counts) - counts).astype(jnp.int32)
    nb = N // tm
    bounds = jnp.sort(jnp.concatenate([jnp.arange(nb, dtype=jnp.int32) * tm, starts]))
    ends = jnp.concatenate([bounds[1:], jnp.full((1,), N, bounds.dtype)])
    seg_blk = jnp.minimum(bounds // tm, nb - 1).astype(jnp.int32)
    seg_exp = sorted_e[jnp.minimum(bounds, N - 1)].astype(jnp.int32)
    pos_rows = pos.reshape(T, TOP_K).T.reshape(N)
    return tok, g_sorted, pos_rows, seg_blk, seg_exp, bounds, ends


def _group_major(a, axis):
    shape = a.shape
    a = a.reshape(shape[:axis] + (A_KV_HEADS, A_GROUP, HEAD_DIM) + shape[axis + 1:])
    return jnp.swapaxes(a, axis, axis + 1).reshape(shape)


def _permute_in_proj(w):
    bounds = np.cumsum([0, 768, 256, 256, 512, 768, 768, 768])
    qa, ka, va, ub, qc, kc, vc = [w[:, bounds[s]:bounds[s + 1]] for s in range(7)]
    return jnp.concatenate([_group_major(qa, 1), qc, ka, kc, ub, va, vc], axis=1)


def kernel(x, positions, attn_norm, w_in, sinks, branch_norm_a, w_pool, pool_scale, lambda_q1, lambda_k1,
           lambda_q2, lambda_k2, subln, w_out, ffn_norm, w_router_group, b_router_group, w_router_expert,
           b_router_expert, w_expert_gate, w_expert_up, w_expert_down, final_norm):
    B, S, D = x.shape
    T = B * S
    depth = w_in.shape[0]
    tm_e = 512

    half = HEAD_DIM // 2
    inv = ROPE_THETA ** (-jnp.arange(half, dtype=F32) / half)
    ang = positions.astype(F32).reshape(T, 1) * inv
    cos, sin = jnp.cos(ang), jnp.sin(ang)
    cos_t = jnp.concatenate([cos, cos, cos, cos], axis=1)
    sin_t = jnp.concatenate([-sin, sin, -sin, sin], axis=1)

    h = x.reshape(T, D)
    y_rows = None
    for l in range(depth):
        if l == 0:
            (hn,) = _norm(h, None, attn_norm[l], emit_h=False, emit_bf16=True)
        else:
            h, hn = _norm(h, y_rows, attn_norm[l], emit_h=True, emit_bf16=True)
        w_in_l = _permute_in_proj(w_in[l]).astype(BF16)
        proj, vat, vt = _proj(hn, w_in_l[:, :COL_VA], w_in_l[:, COL_VA:].T, cos_t, sin_t, B=B, S=S)

        sink_heads = sinks[l].reshape(A_KV_HEADS, A_GROUP).T.reshape(A_HEADS) * LOG2E
        sink_row = jnp.repeat(sink_heads, WINDOW).reshape(1, A_COLS)
        oa = _swa(proj, vat, sink_row, _group_major(branch_norm_a[l], 0), B=B, S=S)
        ob = _pool(proj, w_pool[l].astype(BF16), pool_scale[l], B=B, S=S)

        lam_init = 0.8 - 0.6 * math.exp(-0.3 * l)
        lam_vecs = jnp.stack([lambda_q1[l], lambda_k1[l], lambda_q2[l], lambda_k2[l]]).astype(F32)
        oc = _diff(proj, vt, lam_vecs, subln[l], B=B, S=S, lam_init=lam_init)

        w_out_l = jnp.concatenate([_group_major(w_out[l][:A_WIDTH], 0), w_out[l][A_WIDTH:]], axis=0)
        h1, hn2 = _out(oa, ob, oc, w_out_l.astype(BF16), h, ffn_norm[l])

        wr = jnp.zeros((ROUTER_LO, D), F32)
        wr = wr.at[ROUTER_GROUP_ROW:ROUTER_GROUP_ROW + N_GROUPS].set(w_router_group[l].T)
        wr = wr.at[ROUTER_EXPERT_ROW:ROUTER_EXPERT_ROW + N_EXPERTS].set(w_router_expert[l].T)
        wr_hi = wr.astype(BF16)
        wr_lo = (wr - wr_hi.astype(F32)).astype(BF16)
        br = jnp.zeros((ROUTER_LO, 1), F32)
        br = br.at[ROUTER_GROUP_ROW:ROUTER_GROUP_ROW + N_GROUPS, 0].set(b_router_group[l])
        br = br.at[ROUTER_EXPERT_ROW:ROUTER_EXPERT_ROW + N_EXPERTS, 0].set(b_router_expert[l])
        eidx_t, gates_t = _router(hn2, jnp.concatenate([wr_hi, wr_lo], axis=0), br)

        tok, g_sorted, pos_rows, seg_blk, seg_exp, seg_lo, seg_hi = _route(eidx_t, gates_t, T, tm_e)
        xs = _sc_gather_rows(hn2, tok)
        ys = _experts(seg_blk, seg_exp, seg_lo, seg_hi, xs, g_sorted.reshape(-1, 1),
                      w_expert_gate, w_expert_up, w_expert_down, l, tm=tm_e)
        y_rows = _sc_gather_rows(ys, pos_rows)
        h = h1

    (out,) = _norm(h, y_rows, final_norm, emit_h=False, emit_bf16=False)
    return out.reshape(B, S, D)
```

```python
import functools
import math

import jax
import jax.numpy as jnp
from jax import lax
from jax.experimental import pallas as pl
from jax.experimental.pallas import tpu as pltpu
from jax.experimental.pallas import tpu_sc as plsc
import numpy as np

F32 = jnp.float32
BF16 = jnp.bfloat16

D_MODEL = 2048
HEAD_DIM = 64
ROPE_THETA = 10000.0
A_HEADS = 12
A_KV_HEADS = 4
A_GROUP = 3
A_WIDTH = 768
WINDOW = 128
POOL_WINDOWS = (2, 4, 8, 16)
B_WIDTH = 512
B_GROUP_DIM = 128
C_VDIM = 128
C_WIDTH = 768
C_HEADS = 6
DIFF_EPS = 1e-5
IN_COLS = 4096
N_GROUPS = 4
EXPERTS_PER_GROUP = 8
N_EXPERTS = 32
TOP_K = 2
D_FF = 512
NORM_EPS = 1e-6
NEG = -1e30
LOG2E = math.log2(math.e)
Q_SCALE = HEAD_DIM ** -0.5 * LOG2E

COL_QA, COL_QC, COL_KA, COL_KC, COL_UB, COL_VA, COL_VC = 0, 768, 1536, 1792, 2560, 3072, 3328
PROJ_TN = 256
PROJ_SUB = 256
N_SCALED_TILES = COL_KA // PROJ_TN
N_ROPE_TILES = COL_UB // PROJ_TN
N_MAIN_TILES = COL_VA // PROJ_TN
DIFF_KB = 512
VT_ROWS = C_VDIM + 16

VMEM_LIMIT = 48 * 1024 * 1024
D_PACK = D_MODEL // 2
U32 = jnp.uint32


def _cparams(sem, vmem=VMEM_LIMIT):
    return pltpu.CompilerParams(dimension_semantics=sem, vmem_limit_bytes=vmem)


def _pack_rows(x):
    lo = pltpu.bitcast(x[:, :D_PACK].astype(BF16).astype(F32), U32) >> 16
    hi = pltpu.bitcast(x[:, D_PACK:].astype(BF16).astype(F32), U32)
    return hi | lo


def _unpack_rows(w):
    return pltpu.bitcast(w << 16, F32), pltpu.bitcast(w & U32(0xFFFF0000), F32)


def _unpack_rows_bf16(w):
    lo, hi = _unpack_rows(w)
    return jnp.concatenate([lo.astype(BF16), hi.astype(BF16)], axis=1)


SC_WINDOW = 32
SC_INDEX_LANES = 128


def _sc_gather_rows(x, idx):
    M = idx.shape[0]
    W = x.shape[1]
    idx2 = jnp.pad(idx.reshape(M // SC_WINDOW, SC_WINDOW), ((0, 0), (0, SC_INDEX_LANES - SC_WINDOW)))
    mesh = plsc.VectorSubcoreMesh(core_axis_name="core", subcore_axis_name="subcore")

    @pl.kernel(out_type=jax.ShapeDtypeStruct((M, W), x.dtype), mesh=mesh)
    def gather(x_hbm, i_hbm, o_hbm):
        def body(i_vmem, o_vmem):
            pltpu.sync_copy(x_hbm.at[i_vmem.at[0, pl.ds(0, SC_WINDOW)]], o_vmem)

        pltpu.emit_pipeline(
            body,
            grid=(M // SC_WINDOW,),
            in_specs=[pl.BlockSpec((1, SC_INDEX_LANES), index_map=lambda i: (i, 0))],
            out_specs=[pl.BlockSpec((SC_WINDOW, W), index_map=lambda i: (i, 0))],
            core_axis_name=("core", "subcore"),
            dimension_semantics=(pltpu.PARALLEL,),
        )(i_hbm, o_hbm)

    return gather(x, idx2)


def _norm_kernel(*refs, n_add, emit_h, emit_bf16, eps):
    h_ref = refs[0]
    add_refs = refs[1:1 + n_add]
    g_ref = refs[1 + n_add]
    outs = refs[2 + n_add:]
    h = h_ref[...]
    if n_add:
        lo = jnp.zeros((h.shape[0], D_PACK), F32)
        hi = jnp.zeros((h.shape[0], D_PACK), F32)
        for r in add_refs:
            a, b = _unpack_rows(r[...])
            lo, hi = lo + a, hi + b
        h = h + jnp.concatenate([lo, hi], axis=1)
    ms = jnp.mean(h * h, axis=-1, keepdims=True)
    y = h * lax.rsqrt(ms + eps) * g_ref[...]
    k = 0
    if emit_h:
        outs[k][...] = h
        k += 1
    outs[k][...] = y.astype(BF16 if emit_bf16 else F32)


def _norm(h, y_rows, g, *, emit_h, emit_bf16, tm=512):
    T, D = h.shape
    nt = T // tm
    row = pl.BlockSpec((tm, D), lambda i: (i, 0))
    n_add = 0 if y_rows is None else TOP_K
    add_specs = [pl.BlockSpec((tm, D_PACK), lambda i, k=k: (i + k * nt, 0)) for k in range(n_add)]
    out_shape, out_specs = [], []
    if emit_h:
        out_shape.append(jax.ShapeDtypeStruct((T, D), F32))
        out_specs.append(row)
    out_shape.append(jax.ShapeDtypeStruct((T, D), BF16 if emit_bf16 else F32))
    out_specs.append(row)
    return pl.pallas_call(
        functools.partial(_norm_kernel, n_add=n_add, emit_h=emit_h, emit_bf16=emit_bf16, eps=NORM_EPS),
        grid=(nt,),
        in_specs=[row] + add_specs + [pl.BlockSpec((1, D), lambda i: (0, 0))],
        out_specs=out_specs,
        out_shape=out_shape,
        compiler_params=_cparams(("parallel",)),
        name="norm",
    )(h, *([y_rows] * n_add), g.reshape(1, D))


def _proj_kernel(x_ref, w_ref, wt_ref, cos_ref, sin_ref, o_ref, vat_ref, vt_ref, *, tm):
    j = pl.program_id(1)
    subs = [slice(s * PROJ_SUB, (s + 1) * PROJ_SUB) for s in range(tm // PROJ_SUB)]

    def matmuls():
        return [jnp.dot(x_ref[rows, :], w_ref[...], preferred_element_type=F32) for rows in subs]

    @pl.when(j < N_ROPE_TILES)
    def _():
        accs = matmuls()
        scale = jnp.where(j < N_SCALED_TILES, Q_SCALE, 1.0).astype(F32)
        lane = lax.broadcasted_iota(jnp.int32, (PROJ_SUB, PROJ_TN), 1)
        first_half = (lane & (HEAD_DIM // 2)) == 0
        for rows, acc in zip(subs, accs):
            c = jnp.concatenate([cos_ref[rows, :] * scale] * (PROJ_TN // 128), axis=1)
            s = jnp.concatenate([sin_ref[rows, :] * scale] * (PROJ_TN // 128), axis=1)
            swapped = jnp.where(first_half,
                                pltpu.roll(acc, PROJ_TN - HEAD_DIM // 2, 1),
                                pltpu.roll(acc, HEAD_DIM // 2, 1))
            o_ref[rows, :] = (acc * c + swapped * s).astype(o_ref.dtype)

    @pl.when((j >= N_ROPE_TILES) & (j < N_MAIN_TILES))
    def _():
        for rows, acc in zip(subs, matmuls()):
            o_ref[rows, :] = acc.astype(o_ref.dtype)

    def matmuls_t():
        return [lax.dot_general(wt_ref[...], x_ref[rows, :], (((1,), (1,)), ((), ())),
                                preferred_element_type=F32) for rows in subs]

    @pl.when(j == N_MAIN_TILES)
    def _():
        per_sub = PROJ_SUB // WINDOW
        for s, acc in enumerate(matmuls_t()):
            for b in range(per_sub):
                vat_ref[s * per_sub + b] = acc[:, b * WINDOW:(b + 1) * WINDOW].astype(vat_ref.dtype)

    @pl.when(j > N_MAIN_TILES)
    def _():
        accs = matmuls_t()
        row = lax.broadcasted_iota(jnp.int32, (VT_ROWS - C_VDIM, DIFF_KB), 0)
        extra = jnp.where(row == 0, 1.0, 0.0).astype(vt_ref.dtype)
        per_kb = DIFF_KB // PROJ_SUB
        for s, acc in enumerate(accs):
            cols = slice((s % per_kb) * PROJ_SUB, (s % per_kb + 1) * PROJ_SUB)
            for hh in range(PROJ_TN // C_VDIM):
                vt_ref[hh, s // per_kb, 0:C_VDIM, cols] = acc[hh * C_VDIM:(hh + 1) * C_VDIM, :].astype(vt_ref.dtype)
        for hh in range(PROJ_TN // C_VDIM):
            for kb in range(tm // DIFF_KB):
                vt_ref[hh, kb, C_VDIM:VT_ROWS, :] = extra


def _proj(hn, w, wt, cos_t, sin_t, *, B, S, tm=2048):
    T, D = hn.shape
    spt = S // tm
    last_main = N_MAIN_TILES - 1
    return pl.pallas_call(
        functools.partial(_proj_kernel, tm=tm),
        grid=(T // tm, IN_COLS // PROJ_TN),
        in_specs=[
            pl.BlockSpec((tm, D), lambda i, j: (i, 0)),
            pl.BlockSpec((D, PROJ_TN), lambda i, j: (0, jnp.minimum(j, last_main))),
            pl.BlockSpec((PROJ_TN, D), lambda i, j: (jnp.maximum(j - N_MAIN_TILES, 0), 0)),
            pl.BlockSpec((tm, 128), lambda i, j: (i, 0)),
            pl.BlockSpec((tm, 128), lambda i, j: (i, 0)),
        ],
        out_specs=[
            pl.BlockSpec((tm, PROJ_TN), lambda i, j: (i, jnp.minimum(j, last_main))),
            pl.BlockSpec((tm // WINDOW, PROJ_TN, WINDOW), lambda i, j: (i, 0, 0)),
            pl.BlockSpec((None, PROJ_TN // C_VDIM, tm // DIFF_KB, VT_ROWS, DIFF_KB),
                         lambda i, j: (i // spt, jnp.maximum(j - N_MAIN_TILES - 1, 0), i % spt, 0, 0)),
        ],
        out_shape=[
            jax.ShapeDtypeStruct((T, COL_VA), BF16),
            jax.ShapeDtypeStruct((T // WINDOW, PROJ_TN, WINDOW), BF16),
            jax.ShapeDtypeStruct((B, C_HEADS, S // DIFF_KB, VT_ROWS, DIFF_KB), BF16),
        ],
        compiler_params=_cparams(("parallel", "arbitrary")),
        name="proj",
    )(hn, w, wt, cos_t, sin_t)


A_KV_WIDTH = A_KV_HEADS * HEAD_DIM
A_COLS = A_HEADS * WINDOW


def _swa_kernel(q_ref, kc_ref, kp_ref, vc_ref, vp_ref, sink_ref, g_ref, o_ref, *, tq):
    i = pl.program_id(1)
    nblk = tq // WINDOW
    kext = jnp.concatenate([kp_ref[...], kc_ref[...]], axis=0)
    lane_kv = lax.broadcasted_iota(jnp.int32, (WINDOW, A_KV_WIDTH), 1) // HEAD_DIM
    zero = jnp.zeros((WINDOW, A_KV_WIDTH), BF16)

    scores = []
    for b in range(nblk):
        parts = []
        for g in range(A_GROUP):
            qg = q_ref[b * WINDOW:(b + 1) * WINDOW, g * A_KV_WIDTH:(g + 1) * A_KV_WIDTH]
            parts += [jnp.where(lane_kv == j, qg, zero) for j in range(A_KV_HEADS)]
        qcat = jnp.concatenate(parts, axis=0)
        scores.append(lax.dot_general(kext[b * WINDOW:(b + 2) * WINDOW, :], qcat, (((1,), (1,)), ((), ())),
                                      preferred_element_type=F32))

    r = lax.broadcasted_iota(jnp.int32, (2 * WINDOW, A_COLS), 0)
    c = lax.broadcasted_iota(jnp.int32, (2 * WINDOW, A_COLS), 1) & (WINDOW - 1)
    band = (r > c) & (r <= c + WINDOW)
    sink = sink_ref[...]
    gain = g_ref[...]
    for b, st in enumerate(scores):
        mask = band
        if b == 0:
            mask = band & ((r >= WINDOW) | (i > 0))
        st = jnp.where(mask, st, NEG)
        m = jnp.maximum(jnp.max(st, axis=0, keepdims=True), sink)
        p = jnp.exp2(st - m)
        l = jnp.sum(p, axis=0, keepdims=True) + jnp.exp2(sink - m)
        vt = jnp.concatenate([vp_ref[0] if b == 0 else vc_ref[b - 1], vc_ref[b]], axis=1)
        pv = jnp.dot(vt, p.astype(BF16), preferred_element_type=F32)
        inv_l = 1.0 / l
        heads = []
        for g in range(A_GROUP):
            for j in range(A_KV_HEADS):
                cols = slice((g * A_KV_HEADS + j) * WINDOW, (g * A_KV_HEADS + j + 1) * WINDOW)
                heads.append(pv[j * HEAD_DIM:(j + 1) * HEAD_DIM, cols] * inv_l[:, cols])
        ot = jnp.concatenate(heads, axis=0)
        ms = jnp.mean(ot * ot, axis=0, keepdims=True)
        y = (ot * lax.rsqrt(ms + NORM_EPS)).T * gain
        o_ref[b * WINDOW:(b + 1) * WINDOW, :] = y.astype(o_ref.dtype)


def _swa(proj, vat, sink_row, g, *, B, S, tq=512):
    nq = S // tq
    rpb = tq // WINDOW

    def prev_blk(b, i):
        return jnp.maximum(b * (S // WINDOW) + i * rpb - 1, 0)

    return pl.pallas_call(
        functools.partial(_swa_kernel, tq=tq),
        grid=(B, nq),
        in_specs=[
            pl.BlockSpec((tq, A_WIDTH), lambda b, i: (b * nq + i, COL_QA // A_WIDTH)),
            pl.BlockSpec((tq, A_KV_WIDTH), lambda b, i: (b * nq + i, COL_KA // A_KV_WIDTH)),
            pl.BlockSpec((WINDOW, A_KV_WIDTH), lambda b, i: (prev_blk(b, i), COL_KA // A_KV_WIDTH)),
            pl.BlockSpec((rpb, A_KV_WIDTH, WINDOW), lambda b, i: (b * nq + i, 0, 0)),
            pl.BlockSpec((1, A_KV_WIDTH, WINDOW), lambda b, i: (prev_blk(b, i), 0, 0)),
            pl.BlockSpec((1, A_COLS), lambda b, i: (0, 0)),
            pl.BlockSpec((1, A_WIDTH), lambda b, i: (0, 0)),
        ],
        out_specs=pl.BlockSpec((tq, A_WIDTH), lambda b, i: (b * nq + i, 0)),
        out_shape=jax.ShapeDtypeStruct((B * S, A_WIDTH), BF16),
        compiler_params=_cparams(("parallel", "parallel")),
        name="swa",
    )(proj, proj, proj, vat, vat, sink_row, g.reshape(1, A_WIDTH))


POOL_HALO = 16


def _pool_kernel(u0_ref, u1_ref, h0_ref, h1_ref, w_ref, s_ref, o_ref, *, tq):
    i = pl.program_id(1)
    u = jnp.concatenate([u0_ref[...], u1_ref[...]], axis=1).astype(F32)
    halo = jnp.concatenate([h0_ref[...], h1_ref[...]], axis=1).astype(F32)
    halo = halo * (i > 0).astype(F32)
    ext = jnp.concatenate([halo, u], axis=0)
    t = i * tq + lax.broadcasted_iota(jnp.int32, (tq, 1), 0)
    for gi, w in enumerate(POOL_WINDOWS):
        sl = slice(gi * B_GROUP_DIM, (gi + 1) * B_GROUP_DIM)
        s = ext[:, sl]
        shift = 1
        while shift < w:
            s = s + pltpu.roll(s, shift, 0)
            shift *= 2
        cnt = jnp.minimum(t + 1, w).astype(F32)
        y = (s[POOL_HALO:, :] / cnt - u[:, sl]).astype(BF16)
        o = jnp.dot(y, w_ref[gi], preferred_element_type=F32) * s_ref[:, sl]
        o_ref[:, sl] = o.astype(o_ref.dtype)


def _pool(proj, w_pool, scale, *, B, S, tq=512):
    nq = S // tq
    c0 = COL_UB // 256

    def cur(c):
        return pl.BlockSpec((tq, 256), lambda b, i: (b * nq + i, c))

    def halo(c):
        return pl.BlockSpec(
            (POOL_HALO, 256),
            lambda b, i: (jnp.maximum((b * S + i * tq) // POOL_HALO - 1, 0), c))

    return pl.pallas_call(
        functools.partial(_pool_kernel, tq=tq),
        grid=(B, nq),
        in_specs=[cur(c0), cur(c0 + 1), halo(c0), halo(c0 + 1),
                  pl.BlockSpec((len(POOL_WINDOWS), B_GROUP_DIM, B_GROUP_DIM), lambda b, i: (0, 0, 0)),
                  pl.BlockSpec((1, B_WIDTH), lambda b, i: (0, 0))],
        out_specs=pl.BlockSpec((tq, B_WIDTH), lambda b, i: (b * nq + i, 0)),
        out_shape=jax.ShapeDtypeStruct((B * S, B_WIDTH), BF16),
        compiler_params=_cparams(("parallel", "parallel")),
        name="pool",
    )(proj, proj, proj, proj, w_pool, scale.reshape(1, B_WIDTH))


DIFF_UNROLL = 4


def _diff_kernel(q_ref, k_ref, vt_ref, lam_ref, g_ref, o_ref, qcat_ref, s_ref, bm_ref, m_ref, acc_ref,
                 *, tq, lam_init):
    qi = pl.program_id(2)
    n_diag = tq // DIFF_KB
    n_below = qi * n_diag
    q = q_ref[...]
    lane = lax.broadcasted_iota(jnp.int32, q.shape, 1)
    zero = jnp.zeros_like(q)
    qcat_ref[0:tq, :] = jnp.where(lane < HEAD_DIM, q, zero)
    qcat_ref[tq:2 * tq, :] = jnp.where(lane >= HEAD_DIM, q, zero)
    m_ref[...] = jnp.full(m_ref.shape, NEG, F32)
    acc_ref[...] = jnp.zeros(acc_ref.shape, F32)

    def produce(kidx):
        off = pl.multiple_of(kidx * DIFF_KB, DIFF_KB)
        st = lax.dot_general(k_ref[pl.ds(off, DIFF_KB), :], qcat_ref[...], (((1,), (1,)), ((), ())),
                             preferred_element_type=F32)
        return st, jnp.max(st, axis=0, keepdims=True)

    def store(buf, block):
        s_ref[buf], bm_ref[buf] = block

    def consume(buf, kidx, first_key):
        st = s_ref[buf]
        if first_key is not None:
            r = lax.broadcasted_iota(jnp.int32, st.shape, 0)
            c = lax.broadcasted_iota(jnp.int32, st.shape, 1)
            c = jnp.where(c >= tq, c - tq, c)
            st = jnp.where(r + first_key <= c, st, NEG)
            block_max = jnp.max(st, axis=0, keepdims=True)
        else:
            block_max = bm_ref[buf]
        m = m_ref[...]
        m_new = jnp.maximum(m, block_max)
        m_ref[...] = m_new
        p = jnp.exp2(st - m_new)
        pv = jnp.dot(vt_ref[kidx], p.astype(BF16), preferred_element_type=F32)
        acc_ref[...] = jnp.exp2(m - m_new) * acc_ref[...] + pv

    def steps(first, n_plain, n_masked, last):
        n = n_plain + n_masked
        for u in range(n):
            more = not (last and u == n - 1)
            if more:
                nxt = produce(first + u + 1)
            consume(u % 2, first + u, None if u < n_plain else (u - n_plain) * DIFF_KB)
            if more:
                store(1 - u % 2, nxt)

    store(0, produce(0))

    def body(c, carry):
        steps(c * DIFF_UNROLL, DIFF_UNROLL, 0, False)
        return carry

    lax.fori_loop(0, n_below // DIFF_UNROLL, body, 0)
    for rem in range(0, DIFF_UNROLL, math.gcd(n_diag, DIFF_UNROLL)):
        @pl.when(n_below % DIFF_UNROLL == rem)
        def _(rem=rem):
            steps(n_below - rem, rem, n_diag, True)

    lv = lam_ref[...]
    lam = (jnp.exp(jnp.sum(lv[0:1] * lv[1:2], axis=1, keepdims=True))
           - jnp.exp(jnp.sum(lv[2:3] * lv[3:4], axis=1, keepdims=True)) + lam_init)
    l = acc_ref[C_VDIM:C_VDIM + 1, :]
    acc = acc_ref[0:C_VDIM, :]
    ot = acc[:, :tq] / l[:, :tq] - lam * (acc[:, tq:] / l[:, tq:])
    ms = jnp.mean(ot * ot, axis=0, keepdims=True)
    yt = ot * lax.rsqrt(ms + DIFF_EPS) * (g_ref[...] * (1.0 - lam_init))
    o_ref[...] = yt.T.astype(o_ref.dtype)


def _diff(proj, vt, lam_vecs, subln, *, B, S, lam_init, tq=1024):
    nq = S // tq
    nkb = S // DIFF_KB
    assert tq % DIFF_KB == 0 and DIFF_UNROLL % 2 == 0
    return pl.pallas_call(
        functools.partial(_diff_kernel, tq=tq, lam_init=lam_init),
        grid=(B, C_HEADS, nq),
        in_specs=[
            pl.BlockSpec((tq, C_VDIM), lambda b, h, i: (b * nq + i, COL_QC // C_VDIM + h)),
            pl.BlockSpec((S, C_VDIM), lambda b, h, i: (b, COL_KC // C_VDIM + h)),
            pl.BlockSpec((None, None, nkb, VT_ROWS, DIFF_KB), lambda b, h, i: (b, h, 0, 0, 0)),
            pl.BlockSpec((4, HEAD_DIM), lambda b, h, i: (0, 0)),
            pl.BlockSpec((C_VDIM, 1), lambda b, h, i: (0, 0)),
        ],
        out_specs=pl.BlockSpec((tq, C_VDIM), lambda b, h, i: (b * nq + i, h)),
        out_shape=jax.ShapeDtypeStruct((B * S, C_WIDTH), BF16),
        scratch_shapes=[pltpu.VMEM((2 * tq, C_VDIM), BF16),
                        pltpu.VMEM((2, DIFF_KB, 2 * tq), F32),
                        pltpu.VMEM((2, 1, 2 * tq), F32),
                        pltpu.VMEM((1, 2 * tq), F32),
                        pltpu.VMEM((VT_ROWS, 2 * tq), F32)],
        compiler_params=_cparams(("parallel", "parallel", "arbitrary")),
        name="diff",
    )(proj, proj, vt, lam_vecs, subln.reshape(C_VDIM, 1))


OUT_SUB = 256


def _out_kernel(oa_ref, ob_ref, oc_ref, w_ref, h_ref, g_ref, h1_ref, hn_ref, *, tm):
    subs = [slice(s * OUT_SUB, (s + 1) * OUT_SUB) for s in range(tm // OUT_SUB)]
    accs = []
    for rows in subs:
        acc = jnp.dot(oa_ref[rows, :], w_ref[0:A_WIDTH, :], preferred_element_type=F32)
        acc = acc + jnp.dot(ob_ref[rows, :], w_ref[A_WIDTH:A_WIDTH + B_WIDTH, :], preferred_element_type=F32)
        acc = acc + jnp.dot(oc_ref[rows, :], w_ref[A_WIDTH + B_WIDTH:, :], preferred_element_type=F32)
        accs.append(acc)
    for rows, acc in zip(subs, accs):
        h1 = h_ref[rows, :] + acc
        h1_ref[rows, :] = h1
        ms = jnp.mean(h1 * h1, axis=1, keepdims=True)
        hn = h1 * lax.rsqrt(ms + NORM_EPS) * g_ref[...]
        hn_ref[rows, :] = _pack_rows(hn)


def _out(oa, ob, oc, w, h, g, *, tm=512):
    T, D = h.shape
    return pl.pallas_call(
        functools.partial(_out_kernel, tm=tm),
        grid=(T // tm,),
        in_specs=[
            pl.BlockSpec((tm, A_WIDTH), lambda i: (i, 0)),
            pl.BlockSpec((tm, B_WIDTH), lambda i: (i, 0)),
            pl.BlockSpec((tm, C_WIDTH), lambda i: (i, 0)),
            pl.BlockSpec((D, D), lambda i: (0, 0)),
            pl.BlockSpec((tm, D), lambda i: (i, 0)),
            pl.BlockSpec((1, D), lambda i: (0, 0)),
        ],
        out_specs=[
            pl.BlockSpec((tm, D), lambda i: (i, 0)),
            pl.BlockSpec((tm, D_PACK), lambda i: (i, 0)),
        ],
        out_shape=[
            jax.ShapeDtypeStruct((T, D), F32),
            jax.ShapeDtypeStruct((T, D_PACK), U32),
        ],
        compiler_params=_cparams(("parallel",)),
        name="out",
    )(oa, ob, oc, w, h, g.reshape(1, D))


ROUTER_ROWS = 128
ROUTER_LO = 64
ROUTER_GROUP_ROW = 0
ROUTER_EXPERT_ROW = 8


def _router_kernel(x_ref, wt_ref, b_ref, e_ref, g_ref):
    x = _unpack_rows_bf16(x_ref[...])
    lg = lax.dot_general(wt_ref[...], x, (((1,), (1,)), ((), ())), preferred_element_type=F32)
    lg = lg[0:ROUTER_LO] + lg[ROUTER_LO:ROUTER_ROWS] + b_ref[...]
    grp = lg[ROUTER_GROUP_ROW:ROUTER_GROUP_ROW + N_GROUPS]
    ex = lg[ROUTER_EXPERT_ROW:ROUTER_EXPERT_ROW + N_EXPERTS]

    mg = jnp.max(grp, axis=0, keepdims=True)
    pg_sel = 1.0 / jnp.sum(jnp.exp(grp - mg), axis=0, keepdims=True)
    gi = lax.broadcasted_iota(jnp.int32, grp.shape, 0)
    g_sel = jnp.min(jnp.where(grp == mg, gi, N_GROUPS), axis=0, keepdims=True)

    er = lax.broadcasted_iota(jnp.int32, ex.shape, 0)
    group_of = lax.shift_right_logical(er, EXPERTS_PER_GROUP.bit_length() - 1)
    cand = jnp.where(group_of == g_sel, ex, NEG)
    v1 = jnp.max(cand, axis=0, keepdims=True)
    i1 = jnp.min(jnp.where(cand == v1, er, N_EXPERTS), axis=0, keepdims=True)
    rest = jnp.where(er == i1, NEG, cand)
    v2 = jnp.max(rest, axis=0, keepdims=True)
    i2 = jnp.min(jnp.where(rest == v2, er, N_EXPERTS), axis=0, keepdims=True)
    t = jnp.exp(v2 - v1)
    g1 = pg_sel / (1.0 + t)
    e_ref[...] = jnp.concatenate([i1, i2], axis=0)
    g_ref[...] = jnp.concatenate([g1, g1 * t], axis=0)


def _router(hn_packed, wt, b, *, tm=1024):
    T = hn_packed.shape[0]
    D = wt.shape[1]
    return pl.pallas_call(
        _router_kernel,
        grid=(T // tm,),
        in_specs=[
            pl.BlockSpec((tm, D_PACK), lambda i: (i, 0)),
            pl.BlockSpec((ROUTER_ROWS, D), lambda i: (0, 0)),
            pl.BlockSpec((ROUTER_LO, 1), lambda i: (0, 0)),
        ],
        out_specs=[pl.BlockSpec((TOP_K, tm), lambda i: (0, i)), pl.BlockSpec((TOP_K, tm), lambda i: (0, i))],
        out_shape=[jax.ShapeDtypeStruct((TOP_K, T), jnp.int32), jax.ShapeDtypeStruct((TOP_K, T), F32)],
        compiler_params=_cparams(("parallel",)),
        name="router",
    )(hn_packed, wt, b)


EXPERT_SUB = 256
EXPERT_VMEM = 56 * 1024 * 1024


def _expert_kernel(blk_ref, exp_ref, lo_ref, hi_ref, x_ref, g_ref, wg_ref, wu_ref, wd_ref, o_ref,
                   wg_bf, wu_bf, wd_bf, *, tm):
    p = pl.program_id(0)
    prev = jnp.maximum(p - 1, 0)
    blk = blk_ref[p]
    first = (p == 0) | (blk != blk_ref[prev])
    new_expert = (p == 0) | (exp_ref[p] != exp_ref[prev])
    lo = lo_ref[p]
    hi = hi_ref[p]

    @pl.when(new_expert)
    def _():
        wg_bf[...] = wg_ref[...].astype(BF16)
        wu_bf[...] = wu_ref[...].astype(BF16)
        wd_bf[...] = wd_ref[...].astype(BF16)

    @pl.when(first)
    def _():
        o_ref[...] = jnp.zeros_like(o_ref)

    for s in range(tm // EXPERT_SUB):
        row0 = blk * tm + s * EXPERT_SUB
        rows = slice(s * EXPERT_SUB, (s + 1) * EXPERT_SUB)

        @pl.when((hi > row0) & (lo < row0 + EXPERT_SUB))
        def _(row0=row0, rows=rows):
            x = _unpack_rows_bf16(x_ref[rows, :])
            a = jnp.dot(x, wg_bf[...], preferred_element_type=F32)
            u = jnp.dot(x, wu_bf[...], preferred_element_type=F32)
            act = (a / (1.0 + jnp.exp(-a))) * u
            y = jnp.dot(act.astype(BF16), wd_bf[...], preferred_element_type=F32)
            r = row0 + lax.broadcasted_iota(jnp.int32, (EXPERT_SUB, 1), 0)
            mine = (r >= lo) & (r < hi)
            o_ref[rows, :] = jnp.where(mine, _pack_rows(y * g_ref[rows, :]), o_ref[rows, :])


def _experts(seg_blk, seg_exp, seg_lo, seg_hi, xs, gates, wg, wu, wd, layer, *, tm):
    N = xs.shape[0]
    D = wg.shape[2]
    P = seg_blk.shape[0]
    grid_spec = pltpu.PrefetchScalarGridSpec(
        num_scalar_prefetch=4,
        grid=(P,),
        in_specs=[
            pl.BlockSpec((tm, D_PACK), lambda p, blk, ex, lo, hi: (blk[p], 0)),
            pl.BlockSpec((tm, 1), lambda p, blk, ex, lo, hi: (blk[p], 0)),
            pl.BlockSpec((None, None, D, D_FF), lambda p, blk, ex, lo, hi: (layer, ex[p], 0, 0)),
            pl.BlockSpec((None, None, D, D_FF), lambda p, blk, ex, lo, hi: (layer, ex[p], 0, 0)),
            pl.BlockSpec((None, None, D_FF, D), lambda p, blk, ex, lo, hi: (layer, ex[p], 0, 0)),
        ],
        out_specs=pl.BlockSpec((tm, D_PACK), lambda p, blk, ex, lo, hi: (blk[p], 0)),
        scratch_shapes=[pltpu.VMEM((D, D_FF), BF16), pltpu.VMEM((D, D_FF), BF16), pltpu.VMEM((D_FF, D), BF16)],
    )
    return pl.pallas_call(
        functools.partial(_expert_kernel, tm=tm),
        grid_spec=grid_spec,
        out_shape=jax.ShapeDtypeStruct((N, D_PACK), U32),
        compiler_params=_cparams(("arbitrary",), EXPERT_VMEM),
        name="experts",
    )(seg_blk, seg_exp, seg_lo, seg_hi, xs, gates, wg, wu, wd)


def _route(eidx_t, gates_t, T, tm):
    N = T * TOP_K
    flat_e = eidx_t.T.reshape(N)
    iota = jnp.arange(N, dtype=jnp.int32)
    sorted_e, order, g_sorted = lax.sort((flat_e, iota, gates_t.T.reshape(N)), num_keys=1, is_stable=True)
    tok = order // TOP_K
    _, pos = lax.sort((order, iota), num_keys=1)
    counts = jnp.bincount(flat_e, length=N_EXPERTS)
    starts = (jnp.cumsum(counts) - counts).astype(jnp.int32)
    nb = N // tm
    bounds = jnp.sort(jnp.concatenate([jnp.arange(nb, dtype=jnp.int32) * tm, starts]))
    ends = jnp.concatenate([bounds[1:], jnp.full((1,), N, bounds.dtype)])
    seg_blk = jnp.minimum(bounds // tm, nb - 1).astype(jnp.int32)
    seg_exp = sorted_e[jnp.minimum(bounds, N - 1)].astype(jnp.int32)
    pos_rows = pos.reshape(T, TOP_K).T.reshape(N)
    return tok, g_sorted, pos_rows, seg_blk, seg_exp, bounds, ends


def _group_major(a, axis):
    shape = a.shape
    a = a.reshape(shape[:axis] + (A_KV_HEADS, A_GROUP, HEAD_DIM) + shape[axis + 1:])
    return jnp.swapaxes(a, axis, axis + 1).reshape(shape)


def _permute_in_proj(w):
    bounds = np.cumsum([0, 768, 256, 256, 512, 768, 768, 768])
    qa, ka, va, ub, qc, kc, vc = [w[:, bounds[s]:bounds[s + 1]] for s in range(7)]
    return jnp.concatenate([_group_major(qa, 1), qc, ka, kc, ub, va, vc], axis=1)


def kernel(x, positions, attn_norm, w_in, sinks, branch_norm_a, w_pool, pool_scale, lambda_q1, lambda_k1,
           lambda_q2, lambda_k2, subln, w_out, ffn_norm, w_router_group, b_router_group, w_router_expert,
           b_router_expert, w_expert_gate, w_expert_up, w_expert_down, final_norm):
    B, S, D = x.shape
    T = B * S
    depth = w_in.shape[0]
    tm_e = 512

    half = HEAD_DIM // 2
    inv = ROPE_THETA ** (-jnp.arange(half, dtype=F32) / half)
    ang = positions.astype(F32).reshape(T, 1) * inv
    cos, sin = jnp.cos(ang), jnp.sin(ang)
    cos_t = jnp.concatenate([cos, cos, cos, cos], axis=1)
    sin_t = jnp.concatenate([-sin, sin, -sin, sin], axis=1)

    h = x.reshape(T, D)
    y_rows = None
    for l in range(depth):
        if l == 0:
            (hn,) = _norm(h, None, attn_norm[l], emit_h=False, emit_bf16=True)
        else:
            h, hn = _norm(h, y_rows, attn_norm[l], emit_h=True, emit_bf16=True)
        w_in_l = _permute_in_proj(w_in[l]).astype(BF16)
        proj, vat, vt = _proj(hn, w_in_l[:, :COL_VA], w_in_l[:, COL_VA:].T, cos_t, sin_t, B=B, S=S)

        sink_heads = sinks[l].reshape(A_KV_HEADS, A_GROUP).T.reshape(A_HEADS) * LOG2E
        sink_row = jnp.repeat(sink_heads, WINDOW).reshape(1, A_COLS)
        oa = _swa(proj, vat, sink_row, _group_major(branch_norm_a[l], 0), B=B, S=S)
        ob = _pool(proj, w_pool[l].astype(BF16), pool_scale[l], B=B, S=S)

        lam_init = 0.8 - 0.6 * math.exp(-0.3 * l)
        lam_vecs = jnp.stack([lambda_q1[l], lambda_k1[l], lambda_q2[l], lambda_k2[l]]).astype(F32)
        oc = _diff(proj, vt, lam_vecs, subln[l], B=B, S=S, lam_init=lam_init)

        w_out_l = jnp.concatenate([_group_major(w_out[l][:A_WIDTH], 0), w_out[l][A_WIDTH:]], axis=0)
        h1, hn2 = _out(oa, ob, oc, w_out_l.astype(BF16), h, ffn_norm[l])

        wr = jnp.zeros((ROUTER_LO, D), F32)
        wr = wr.at[ROUTER_GROUP_ROW:ROUTER_GROUP_ROW + N_GROUPS].set(w_router_group[l].T)
        wr = wr.at[ROUTER_EXPERT_ROW:ROUTER_EXPERT_ROW + N_EXPERTS].set(w_router_expert[l].T)
        wr_hi = wr.astype(BF16)
        wr_lo = (wr - wr_hi.astype(F32)).astype(BF16)
        br = jnp.zeros((ROUTER_LO, 1), F32)
        br = br.at[ROUTER_GROUP_ROW:ROUTER_GROUP_ROW + N_GROUPS, 0].set(b_router_group[l])
        br = br.at[ROUTER_EXPERT_ROW:ROUTER_EXPERT_ROW + N_EXPERTS, 0].set(b_router_expert[l])
        eidx_t, gates_t = _router(hn2, jnp.concatenate([wr_hi, wr_lo], axis=0), br)

        tok, g_sorted, pos_rows, seg_blk, seg_exp, seg_lo, seg_hi = _route(eidx_t, gates_t, T, tm_e)
        xs = _sc_gather_rows(hn2, tok)
        ys = _experts(seg_blk, seg_exp, seg_lo, seg_hi, xs, g_sorted.reshape(-1, 1),
                      w_expert_gate, w_expert_up, w_expert_down, l, tm=tm_e)
        y_rows = _sc_gather_rows(ys, pos_rows)
        h = h1

    (out,) = _norm(h, y_rows, final_norm, emit_h=False, emit_bf16=False)
    return out.reshape(B, S, D)
```

```python
import functools
import math

import jax
import jax.numpy as jnp
from jax import lax
from jax.experimental import pallas as pl
from jax.experimental.pallas import tpu as pltpu
from jax.experimental.pallas import tpu_sc as plsc
import numpy as np

F32 = jnp.float32
BF16 = jnp.bfloat16

D_MODEL = 2048
HEAD_DIM = 64
ROPE_THETA = 10000.0
A_HEADS = 12
A_KV_HEADS = 4
A_GROUP = 3
A_WIDTH = 768
WINDOW = 128
POOL_WINDOWS = (2, 4, 8, 16)
B_WIDTH = 512
B_GROUP_DIM = 128
C_VDIM = 128
C_WIDTH = 768
C_HEADS = 6
DIFF_EPS = 1e-5
IN_COLS = 4096
N_GROUPS = 4
EXPERTS_PER_GROUP = 8
N_EXPERTS = 32
TOP_K = 2
D_FF = 512
NORM_EPS = 1e-6
NEG = -1e30
LOG2E = math.log2(math.e)
Q_SCALE = HEAD_DIM ** -0.5 * LOG2E

COL_QA, COL_QC, COL_KA, COL_KC, COL_UB, COL_VA, COL_VC = 0, 768, 1536, 1792, 2560, 3072, 3328
PROJ_TN = 512
A_KV_WIDTH = A_KV_HEADS * HEAD_DIM
PROJ_SUB = 256
N_SCALED_TILES = COL_KA // PROJ_TN
N_ROPE_TILES = COL_UB // PROJ_TN
N_MAIN_TILES = COL_VA // PROJ_TN
DIFF_KB = 512
VT_ROWS = C_VDIM + 16

VMEM_LIMIT = 48 * 1024 * 1024
D_PACK = D_MODEL // 2
U32 = jnp.uint32


def _cparams(sem, vmem=VMEM_LIMIT):
    return pltpu.CompilerParams(dimension_semantics=sem, vmem_limit_bytes=vmem)


def _pack_rows(x):
    lo = pltpu.bitcast(x[:, :D_PACK].astype(BF16).astype(F32), U32) >> 16
    hi = pltpu.bitcast(x[:, D_PACK:].astype(BF16).astype(F32), U32)
    return hi | lo


def _unpack_rows(w):
    return pltpu.bitcast(w << 16, F32), pltpu.bitcast(w & U32(0xFFFF0000), F32)


def _unpack_rows_bf16(w):
    lo, hi = _unpack_rows(w)
    return jnp.concatenate([lo.astype(BF16), hi.astype(BF16)], axis=1)


SC_WINDOW = 32
SC_INDEX_LANES = 128


def _sc_gather_rows(x, idx):
    M = idx.shape[0]
    W = x.shape[1]
    idx2 = jnp.pad(idx.reshape(M // SC_WINDOW, SC_WINDOW), ((0, 0), (0, SC_INDEX_LANES - SC_WINDOW)))
    mesh = plsc.VectorSubcoreMesh(core_axis_name="core", subcore_axis_name="subcore")

    @pl.kernel(out_type=jax.ShapeDtypeStruct((M, W), x.dtype), mesh=mesh)
    def gather(x_hbm, i_hbm, o_hbm):
        def body(i_vmem, o_vmem):
            pltpu.sync_copy(x_hbm.at[i_vmem.at[0, pl.ds(0, SC_WINDOW)]], o_vmem)

        pltpu.emit_pipeline(
            body,
            grid=(M // SC_WINDOW,),
            in_specs=[pl.BlockSpec((1, SC_INDEX_LANES), index_map=lambda i: (i, 0))],
            out_specs=[pl.BlockSpec((SC_WINDOW, W), index_map=lambda i: (i, 0))],
            core_axis_name=("core", "subcore"),
            dimension_semantics=(pltpu.PARALLEL,),
        )(i_hbm, o_hbm)

    return gather(x, idx2)


def _norm_kernel(*refs, n_add, emit_h, emit_bf16, eps):
    h_ref = refs[0]
    add_refs = refs[1:1 + n_add]
    g_ref = refs[1 + n_add]
    outs = refs[2 + n_add:]
    h = h_ref[...]
    if n_add:
        lo = jnp.zeros((h.shape[0], D_PACK), F32)
        hi = jnp.zeros((h.shape[0], D_PACK), F32)
        for r in add_refs:
            a, b = _unpack_rows(r[...])
            lo, hi = lo + a, hi + b
        h = h + jnp.concatenate([lo, hi], axis=1)
    ms = jnp.mean(h * h, axis=-1, keepdims=True)
    y = h * lax.rsqrt(ms + eps) * g_ref[...]
    k = 0
    if emit_h:
        outs[k][...] = h
        k += 1
    outs[k][...] = y.astype(BF16 if emit_bf16 else F32)


def _norm(h, y_rows, g, *, emit_h, emit_bf16, tm=512):
    T, D = h.shape
    nt = T // tm
    row = pl.BlockSpec((tm, D), lambda i: (i, 0))
    n_add = 0 if y_rows is None else TOP_K
    add_specs = [pl.BlockSpec((tm, D_PACK), lambda i, k=k: (i + k * nt, 0)) for k in range(n_add)]
    out_shape, out_specs = [], []
    if emit_h:
        out_shape.append(jax.ShapeDtypeStruct((T, D), F32))
        out_specs.append(row)
    out_shape.append(jax.ShapeDtypeStruct((T, D), BF16 if emit_bf16 else F32))
    out_specs.append(row)
    return pl.pallas_call(
        functools.partial(_norm_kernel, n_add=n_add, emit_h=emit_h, emit_bf16=emit_bf16, eps=NORM_EPS),
        grid=(nt,),
        in_specs=[row] + add_specs + [pl.BlockSpec((1, D), lambda i: (0, 0))],
        out_specs=out_specs,
        out_shape=out_shape,
        compiler_params=_cparams(("parallel",)),
        name="norm",
    )(h, *([y_rows] * n_add), g.reshape(1, D))


def _proj_kernel(x_ref, w_ref, wt_ref, cos_ref, sin_ref, o_ref, vat_ref, vt_ref, *, tm):
    j = pl.program_id(1)
    subs = [slice(s * PROJ_SUB, (s + 1) * PROJ_SUB) for s in range(tm // PROJ_SUB)]

    def matmuls():
        return [jnp.dot(x_ref[rows, :], w_ref[...], preferred_element_type=F32) for rows in subs]

    @pl.when(j < N_ROPE_TILES)
    def _():
        accs = matmuls()
        scale = jnp.where(j < N_SCALED_TILES, Q_SCALE, 1.0).astype(F32)
        lane = lax.broadcasted_iota(jnp.int32, (PROJ_SUB, PROJ_TN), 1)
        first_half = (lane & (HEAD_DIM // 2)) == 0
        for rows, acc in zip(subs, accs):
            c = jnp.concatenate([cos_ref[rows, :] * scale] * (PROJ_TN // 128), axis=1)
            s = jnp.concatenate([sin_ref[rows, :] * scale] * (PROJ_TN // 128), axis=1)
            swapped = jnp.where(first_half,
                                pltpu.roll(acc, PROJ_TN - HEAD_DIM // 2, 1),
                                pltpu.roll(acc, HEAD_DIM // 2, 1))
            o_ref[rows, :] = (acc * c + swapped * s).astype(o_ref.dtype)

    @pl.when((j >= N_ROPE_TILES) & (j < N_MAIN_TILES))
    def _():
        for rows, acc in zip(subs, matmuls()):
            o_ref[rows, :] = acc.astype(o_ref.dtype)

    per_sub = PROJ_SUB // WINDOW
    per_kb = DIFF_KB // PROJ_SUB
    for t in range((IN_COLS - COL_VA) // PROJ_TN):
        @pl.when(j == N_MAIN_TILES + t)
        def _(t=t):
            accs = [lax.dot_general(wt_ref[...], x_ref[rows, :], (((1,), (1,)), ((), ())),
                                    preferred_element_type=F32) for rows in subs]
            for s, acc in enumerate(accs):
                cols = slice((s % per_kb) * PROJ_SUB, (s % per_kb + 1) * PROJ_SUB)
                for r0 in range(0, PROJ_TN, C_VDIM):
                    row = t * PROJ_TN + r0
                    if row == 0:
                        for b in range(per_sub):
                            vat_ref[s * per_sub + b] = acc[0:A_KV_WIDTH, b * WINDOW:(b + 1) * WINDOW].astype(
                                vat_ref.dtype)
                    elif row >= A_KV_WIDTH:
                        head = (row - A_KV_WIDTH) // C_VDIM
                        vt_ref[head, s // per_kb, 0:C_VDIM, cols] = acc[r0:r0 + C_VDIM, :].astype(vt_ref.dtype)

    @pl.when(j == N_MAIN_TILES)
    def _():
        row = lax.broadcasted_iota(jnp.int32, (VT_ROWS - C_VDIM, DIFF_KB), 0)
        extra = jnp.where(row == 0, 1.0, 0.0).astype(vt_ref.dtype)
        for head in range(C_HEADS):
            for kb in range(tm // DIFF_KB):
                vt_ref[head, kb, C_VDIM:VT_ROWS, :] = extra


def _proj(hn, w, wt, cos_t, sin_t, *, B, S, tm=2048):
    T, D = hn.shape
    spt = S // tm
    last_main = N_MAIN_TILES - 1
    return pl.pallas_call(
        functools.partial(_proj_kernel, tm=tm),
        grid=(T // tm, IN_COLS // PROJ_TN),
        in_specs=[
            pl.BlockSpec((tm, D), lambda i, j: (i, 0)),
            pl.BlockSpec((D, PROJ_TN), lambda i, j: (0, jnp.minimum(j, last_main))),
            pl.BlockSpec((PROJ_TN, D), lambda i, j: (jnp.maximum(j - N_MAIN_TILES, 0), 0)),
            pl.BlockSpec((tm, 128), lambda i, j: (i, 0)),
            pl.BlockSpec((tm, 128), lambda i, j: (i, 0)),
        ],
        out_specs=[
            pl.BlockSpec((tm, PROJ_TN), lambda i, j: (i, jnp.minimum(j, last_main))),
            pl.BlockSpec((tm // WINDOW, A_KV_WIDTH, WINDOW), lambda i, j: (i, 0, 0)),
            pl.BlockSpec((None, C_HEADS, tm // DIFF_KB, VT_ROWS, DIFF_KB),
                         lambda i, j: (i // spt, 0, i % spt, 0, 0)),
        ],
        out_shape=[
            jax.ShapeDtypeStruct((T, COL_VA), BF16),
            jax.ShapeDtypeStruct((T // WINDOW, A_KV_WIDTH, WINDOW), BF16),
            jax.ShapeDtypeStruct((B, C_HEADS, S // DIFF_KB, VT_ROWS, DIFF_KB), BF16),
        ],
        compiler_params=_cparams(("parallel", "arbitrary")),
        name="proj",
    )(hn, w, wt, cos_t, sin_t)


A_COLS = A_HEADS * WINDOW


def _swa_kernel(q_ref, kc_ref, kp_ref, vc_ref, vp_ref, sink_ref, g_ref, o_ref, *, tq):
    i = pl.program_id(1)
    nblk = tq // WINDOW
    kext = jnp.concatenate([kp_ref[...], kc_ref[...]], axis=0)
    lane_kv = lax.broadcasted_iota(jnp.int32, (WINDOW, A_KV_WIDTH), 1) // HEAD_DIM
    zero = jnp.zeros((WINDOW, A_KV_WIDTH), BF16)

    scores = []
    for b in range(nblk):
        parts = []
        for g in range(A_GROUP):
            qg = q_ref[b * WINDOW:(b + 1) * WINDOW, g * A_KV_WIDTH:(g + 1) * A_KV_WIDTH]
            parts += [jnp.where(lane_kv == j, qg, zero) for j in range(A_KV_HEADS)]
        qcat = jnp.concatenate(parts, axis=0)
        scores.append(lax.dot_general(kext[b * WINDOW:(b + 2) * WINDOW, :], qcat, (((1,), (1,)), ((), ())),
                                      preferred_element_type=F32))

    r = lax.broadcasted_iota(jnp.int32, (2 * WINDOW, A_COLS), 0)
    c = lax.broadcasted_iota(jnp.int32, (2 * WINDOW, A_COLS), 1) & (WINDOW - 1)
    band = (r > c) & (r <= c + WINDOW)
    sink = sink_ref[...]
    gain = g_ref[...]
    for b, st in enumerate(scores):
        mask = band
        if b == 0:
            mask = band & ((r >= WINDOW) | (i > 0))
        st = jnp.where(mask, st, NEG)
        m = jnp.maximum(jnp.max(st, axis=0, keepdims=True), sink)
        p = jnp.exp2(st - m)
        l = jnp.sum(p, axis=0, keepdims=True) + jnp.exp2(sink - m)
        vt = jnp.concatenate([vp_ref[0] if b == 0 else vc_ref[b - 1], vc_ref[b]], axis=1)
        pv = jnp.dot(vt, p.astype(BF16), preferred_element_type=F32)
        inv_l = 1.0 / l
        heads = []
        for g in range(A_GROUP):
            for j in range(A_KV_HEADS):
                cols = slice((g * A_KV_HEADS + j) * WINDOW, (g * A_KV_HEADS + j + 1) * WINDOW)
                heads.append(pv[j * HEAD_DIM:(j + 1) * HEAD_DIM, cols] * inv_l[:, cols])
        ot = jnp.concatenate(heads, axis=0)
        ms = jnp.mean(ot * ot, axis=0, keepdims=True)
        y = (ot * lax.rsqrt(ms + NORM_EPS)).T * gain
        o_ref[b * WINDOW:(b + 1) * WINDOW, :] = y.astype(o_ref.dtype)


def _swa(proj, vat, sink_row, g, *, B, S, tq=512):
    nq = S // tq
    rpb = tq // WINDOW

    def prev_blk(b, i):
        return jnp.maximum(b * (S // WINDOW) + i * rpb - 1, 0)

    return pl.pallas_call(
        functools.partial(_swa_kernel, tq=tq),
        grid=(B, nq),
        in_specs=[
            pl.BlockSpec((tq, A_WIDTH), lambda b, i: (b * nq + i, COL_QA // A_WIDTH)),
            pl.BlockSpec((tq, A_KV_WIDTH), lambda b, i: (b * nq + i, COL_KA // A_KV_WIDTH)),
            pl.BlockSpec((WINDOW, A_KV_WIDTH), lambda b, i: (prev_blk(b, i), COL_KA // A_KV_WIDTH)),
            pl.BlockSpec((rpb, A_KV_WIDTH, WINDOW), lambda b, i: (b * nq + i, 0, 0)),
            pl.BlockSpec((1, A_KV_WIDTH, WINDOW), lambda b, i: (prev_blk(b, i), 0, 0)),
            pl.BlockSpec((1, A_COLS), lambda b, i: (0, 0)),
            pl.BlockSpec((1, A_WIDTH), lambda b, i: (0, 0)),
        ],
        out_specs=pl.BlockSpec((tq, A_WIDTH), lambda b, i: (b * nq + i, 0)),
        out_shape=jax.ShapeDtypeStruct((B * S, A_WIDTH), BF16),
        compiler_params=_cparams(("parallel", "parallel")),
        name="swa",
    )(proj, proj, proj, vat, vat, sink_row, g.reshape(1, A_WIDTH))


POOL_HALO = 16


def _pool_kernel(u0_ref, u1_ref, h0_ref, h1_ref, w_ref, s_ref, o_ref, *, tq):
    i = pl.program_id(1)
    u = jnp.concatenate([u0_ref[...], u1_ref[...]], axis=1).astype(F32)
    halo = jnp.concatenate([h0_ref[...], h1_ref[...]], axis=1).astype(F32)
    halo = halo * (i > 0).astype(F32)
    ext = jnp.concatenate([halo, u], axis=0)
    t = i * tq + lax.broadcasted_iota(jnp.int32, (tq, 1), 0)
    for gi, w in enumerate(POOL_WINDOWS):
        sl = slice(gi * B_GROUP_DIM, (gi + 1) * B_GROUP_DIM)
        s = ext[:, sl]
        shift = 1
        while shift < w:
            s = s + pltpu.roll(s, shift, 0)
            shift *= 2
        cnt = jnp.minimum(t + 1, w).astype(F32)
        y = (s[POOL_HALO:, :] / cnt - u[:, sl]).astype(BF16)
        o = jnp.dot(y, w_ref[gi], preferred_element_type=F32) * s_ref[:, sl]
        o_ref[:, sl] = o.astype(o_ref.dtype)


def _pool(proj, w_pool, scale, *, B, S, tq=512):
    nq = S // tq
    c0 = COL_UB // 256

    def cur(c):
        return pl.BlockSpec((tq, 256), lambda b, i: (b * nq + i, c))

    def halo(c):
        return pl.BlockSpec(
            (POOL_HALO, 256),
            lambda b, i: (jnp.maximum((b * S + i * tq) // POOL_HALO - 1, 0), c))

    return pl.pallas_call(
        functools.partial(_pool_kernel, tq=tq),
        grid=(B, nq),
        in_specs=[cur(c0), cur(c0 + 1), halo(c0), halo(c0 + 1),
                  pl.BlockSpec((len(POOL_WINDOWS), B_GROUP_DIM, B_GROUP_DIM), lambda b, i: (0, 0, 0)),
                  pl.BlockSpec((1, B_WIDTH), lambda b, i: (0, 0))],
        out_specs=pl.BlockSpec((tq, B_WIDTH), lambda b, i: (b * nq + i, 0)),
        out_shape=jax.ShapeDtypeStruct((B * S, B_WIDTH), BF16),
        compiler_params=_cparams(("parallel", "parallel")),
        name="pool",
    )(proj, proj, proj, proj, w_pool, scale.reshape(1, B_WIDTH))


DIFF_UNROLL = 4


def _diff_kernel(q_ref, k_ref, vt_ref, lam_ref, g_ref, o_ref, qcat_ref, s_ref, bm_ref, m_ref, acc_ref,
                 *, tq, lam_init):
    qi = pl.program_id(2)
    n_diag = tq // DIFF_KB
    n_below = qi * n_diag
    q = q_ref[...]
    lane = lax.broadcasted_iota(jnp.int32, q.shape, 1)
    zero = jnp.zeros_like(q)
    qcat_ref[0:tq, :] = jnp.where(lane < HEAD_DIM, q, zero)
    qcat_ref[tq:2 * tq, :] = jnp.where(lane >= HEAD_DIM, q, zero)
    m_ref[...] = jnp.full(m_ref.shape, NEG, F32)
    acc_ref[...] = jnp.zeros(acc_ref.shape, F32)

    def produce(kidx):
        off = pl.multiple_of(kidx * DIFF_KB, DIFF_KB)
        st = lax.dot_general(k_ref[pl.ds(off, DIFF_KB), :], qcat_ref[...], (((1,), (1,)), ((), ())),
                             preferred_element_type=F32)
        return st, jnp.max(st, axis=0, keepdims=True)

    def store(buf, block):
        s_ref[buf], bm_ref[buf] = block

    def consume(buf, kidx, first_key):
        st = s_ref[buf]
        if first_key is not None:
            r = lax.broadcasted_iota(jnp.int32, st.shape, 0)
            c = lax.broadcasted_iota(jnp.int32, st.shape, 1)
            c = jnp.where(c >= tq, c - tq, c)
            st = jnp.where(r + first_key <= c, st, NEG)
            block_max = jnp.max(st, axis=0, keepdims=True)
        else:
            block_max = bm_ref[buf]
        m = m_ref[...]
        m_new = jnp.maximum(m, block_max)
        m_ref[...] = m_new
        p = jnp.exp2(st - m_new)
        pv = jnp.dot(vt_ref[kidx], p.astype(BF16), preferred_element_type=F32)
        acc_ref[...] = jnp.exp2(m - m_new) * acc_ref[...] + pv

    def steps(first, n_plain, n_masked, last):
        n = n_plain + n_masked
        for u in range(n):
            more = not (last and u == n - 1)
            if more:
                nxt = produce(first + u + 1)
            consume(u % 2, first + u, None if u < n_plain else (u - n_plain) * DIFF_KB)
            if more:
                store(1 - u % 2, nxt)

    store(0, produce(0))

    def body(c, carry):
        steps(c * DIFF_UNROLL, DIFF_UNROLL, 0, False)
        return carry

    lax.fori_loop(0, n_below // DIFF_UNROLL, body, 0)
    for rem in range(0, DIFF_UNROLL, math.gcd(n_diag, DIFF_UNROLL)):
        @pl.when(n_below % DIFF_UNROLL == rem)
        def _(rem=rem):
            steps(n_below - rem, rem, n_diag, True)

    lv = lam_ref[...]
    lam = (jnp.exp(jnp.sum(lv[0:1] * lv[1:2], axis=1, keepdims=True))
           - jnp.exp(jnp.sum(lv[2:3] * lv[3:4], axis=1, keepdims=True)) + lam_init)
    l = acc_ref[C_VDIM:C_VDIM + 1, :]
    acc = acc_ref[0:C_VDIM, :]
    ot = acc[:, :tq] / l[:, :tq] - lam * (acc[:, tq:] / l[:, tq:])
    ms = jnp.mean(ot * ot, axis=0, keepdims=True)
    yt = ot * lax.rsqrt(ms + DIFF_EPS) * (g_ref[...] * (1.0 - lam_init))
    o_ref[...] = yt.T.astype(o_ref.dtype)


def _diff(proj, vt, lam_vecs, subln, *, B, S, lam_init, tq=1024):
    nq = S // tq
    nkb = S // DIFF_KB
    assert tq % DIFF_KB == 0 and DIFF_UNROLL % 2 == 0
    return pl.pallas_call(
        functools.partial(_diff_kernel, tq=tq, lam_init=lam_init),
        grid=(B, C_HEADS, nq),
        in_specs=[
            pl.BlockSpec((tq, C_VDIM), lambda b, h, i: (b * nq + i, COL_QC // C_VDIM + h)),
            pl.BlockSpec((S, C_VDIM), lambda b, h, i: (b, COL_KC // C_VDIM + h)),
            pl.BlockSpec((None, None, nkb, VT_ROWS, DIFF_KB), lambda b, h, i: (b, h, 0, 0, 0)),
            pl.BlockSpec((4, HEAD_DIM), lambda b, h, i: (0, 0)),
            pl.BlockSpec((C_VDIM, 1), lambda b, h, i: (0, 0)),
        ],
        out_specs=pl.BlockSpec((tq, C_VDIM), lambda b, h, i: (b * nq + i, h)),
        out_shape=jax.ShapeDtypeStruct((B * S, C_WIDTH), BF16),
        scratch_shapes=[pltpu.VMEM((2 * tq, C_VDIM), BF16),
                        pltpu.VMEM((2, DIFF_KB, 2 * tq), F32),
                        pltpu.VMEM((2, 1, 2 * tq), F32),
                        pltpu.VMEM((1, 2 * tq), F32),
                        pltpu.VMEM((VT_ROWS, 2 * tq), F32)],
        compiler_params=_cparams(("parallel", "parallel", "arbitrary")),
        name="diff",
    )(proj, proj, vt, lam_vecs, subln.reshape(C_VDIM, 1))


OUT_SUB = 256


def _out_kernel(oa_ref, ob_ref, oc_ref, w_ref, h_ref, g_ref, h1_ref, hn_ref, *, tm):
    subs = [slice(s * OUT_SUB, (s + 1) * OUT_SUB) for s in range(tm // OUT_SUB)]
    accs = []
    for rows in subs:
        acc = jnp.dot(oa_ref[rows, :], w_ref[0:A_WIDTH, :], preferred_element_type=F32)
        acc = acc + jnp.dot(ob_ref[rows, :], w_ref[A_WIDTH:A_WIDTH + B_WIDTH, :], preferred_element_type=F32)
        acc = acc + jnp.dot(oc_ref[rows, :], w_ref[A_WIDTH + B_WIDTH:, :], preferred_element_type=F32)
        accs.append(acc)
    for rows, acc in zip(subs, accs):
        h1 = h_ref[rows, :] + acc
        h1_ref[rows, :] = h1
        ms = jnp.mean(h1 * h1, axis=1, keepdims=True)
        hn = h1 * lax.rsqrt(ms + NORM_EPS) * g_ref[...]
        hn_ref[rows, :] = _pack_rows(hn)


def _out(oa, ob, oc, w, h, g, *, tm=512):
    T, D = h.shape
    return pl.pallas_call(
        functools.partial(_out_kernel, tm=tm),
        grid=(T // tm,),
        in_specs=[
            pl.BlockSpec((tm, A_WIDTH), lambda i: (i, 0)),
            pl.BlockSpec((tm, B_WIDTH), lambda i: (i, 0)),
            pl.BlockSpec((tm, C_WIDTH), lambda i: (i, 0)),
            pl.BlockSpec((D, D), lambda i: (0, 0)),
            pl.BlockSpec((tm, D), lambda i: (i, 0)),
            pl.BlockSpec((1, D), lambda i: (0, 0)),
        ],
        out_specs=[
            pl.BlockSpec((tm, D), lambda i: (i, 0)),
            pl.BlockSpec((tm, D_PACK), lambda i: (i, 0)),
        ],
        out_shape=[
            jax.ShapeDtypeStruct((T, D), F32),
            jax.ShapeDtypeStruct((T, D_PACK), U32),
        ],
        compiler_params=_cparams(("parallel",)),
        name="out",
    )(oa, ob, oc, w, h, g.reshape(1, D))


ROUTER_ROWS = 128
ROUTER_LO = 64
ROUTER_GROUP_ROW = 0
ROUTER_EXPERT_ROW = 8


def _router_kernel(x_ref, wt_ref, b_ref, e_ref, g_ref):
    x = _unpack_rows_bf16(x_ref[...])
    lg = lax.dot_general(wt_ref[...], x, (((1,), (1,)), ((), ())), preferred_element_type=F32)
    lg = lg[0:ROUTER_LO] + lg[ROUTER_LO:ROUTER_ROWS] + b_ref[...]
    grp = lg[ROUTER_GROUP_ROW:ROUTER_GROUP_ROW + N_GROUPS]
    ex = lg[ROUTER_EXPERT_ROW:ROUTER_EXPERT_ROW + N_EXPERTS]

    mg = jnp.max(grp, axis=0, keepdims=True)
    pg_sel = 1.0 / jnp.sum(jnp.exp(grp - mg), axis=0, keepdims=True)
    gi = lax.broadcasted_iota(jnp.int32, grp.shape, 0)
    g_sel = jnp.min(jnp.where(grp == mg, gi, N_GROUPS), axis=0, keepdims=True)

    er = lax.broadcasted_iota(jnp.int32, ex.shape, 0)
    group_of = lax.shift_right_logical(er, EXPERTS_PER_GROUP.bit_length() - 1)
    cand = jnp.where(group_of == g_sel, ex, NEG)
    v1 = jnp.max(cand, axis=0, keepdims=True)
    i1 = jnp.min(jnp.where(cand == v1, er, N_EXPERTS), axis=0, keepdims=True)
    rest = jnp.where(er == i1, NEG, cand)
    v2 = jnp.max(rest, axis=0, keepdims=True)
    i2 = jnp.min(jnp.where(rest == v2, er, N_EXPERTS), axis=0, keepdims=True)
    t = jnp.exp(v2 - v1)
    g1 = pg_sel / (1.0 + t)
    e_ref[...] = jnp.concatenate([i1, i2], axis=0)
    g_ref[...] = jnp.concatenate([g1, g1 * t], axis=0)


def _router(hn_packed, wt, b, *, tm=1024):
    T = hn_packed.shape[0]
    D = wt.shape[1]
    return pl.pallas_call(
        _router_kernel,
        grid=(T // tm,),
        in_specs=[
            pl.BlockSpec((tm, D_PACK), lambda i: (i, 0)),
            pl.BlockSpec((ROUTER_ROWS, D), lambda i: (0, 0)),
            pl.BlockSpec((ROUTER_LO, 1), lambda i: (0, 0)),
        ],
        out_specs=[pl.BlockSpec((TOP_K, tm), lambda i: (0, i)), pl.BlockSpec((TOP_K, tm), lambda i: (0, i))],
        out_shape=[jax.ShapeDtypeStruct((TOP_K, T), jnp.int32), jax.ShapeDtypeStruct((TOP_K, T), F32)],
        compiler_params=_cparams(("parallel",)),
        name="router",
    )(hn_packed, wt, b)


EXPERT_SUB = 256
EXPERT_VMEM = 56 * 1024 * 1024


def _expert_kernel(blk_ref, exp_ref, lo_ref, hi_ref, x_ref, g_ref, wg_ref, wu_ref, wd_ref, o_ref,
                   wg_bf, wu_bf, wd_bf, *, tm):
    p = pl.program_id(0)
    prev = jnp.maximum(p - 1, 0)
    blk = blk_ref[p]
    first = (p == 0) | (blk != blk_ref[prev])
    new_expert = (p == 0) | (exp_ref[p] != exp_ref[prev])
    lo = lo_ref[p]
    hi = hi_ref[p]

    @pl.when(new_expert)
    def _():
        wg_bf[...] = wg_ref[...].astype(BF16)
        wu_bf[...] = wu_ref[...].astype(BF16)
        wd_bf[...] = wd_ref[...].astype(BF16)

    @pl.when(first)
    def _():
        o_ref[...] = jnp.zeros_like(o_ref)

    for s in range(tm // EXPERT_SUB):
        row0 = blk * tm + s * EXPERT_SUB
        rows = slice(s * EXPERT_SUB, (s + 1) * EXPERT_SUB)

        @pl.when((hi > row0) & (lo < row0 + EXPERT_SUB))
        def _(row0=row0, rows=rows):
            x = _unpack_rows_bf16(x_ref[rows, :])
            a = jnp.dot(x, wg_bf[...], preferred_element_type=F32)
            u = jnp.dot(x, wu_bf[...], preferred_element_type=F32)
            act = (a / (1.0 + jnp.exp(-a))) * u
            y = jnp.dot(act.astype(BF16), wd_bf[...], preferred_element_type=F32)
            r = row0 + lax.broadcasted_iota(jnp.int32, (EXPERT_SUB, 1), 0)
            mine = (r >= lo) & (r < hi)
            o_ref[rows, :] = jnp.where(mine, _pack_rows(y * g_ref[rows, :]), o_ref[rows, :])


def _experts(seg_blk, seg_exp, seg_lo, seg_hi, xs, gates, wg, wu, wd, layer, *, tm):
    N = xs.shape[0]
    D = wg.shape[2]
    P = seg_blk.shape[0]
    grid_spec = pltpu.PrefetchScalarGridSpec(
        num_scalar_prefetch=4,
        grid=(P,),
        in_specs=[
            pl.BlockSpec((tm, D_PACK), lambda p, blk, ex, lo, hi: (blk[p], 0)),
            pl.BlockSpec((tm, 1), lambda p, blk, ex, lo, hi: (blk[p], 0)),
            pl.BlockSpec((None, None, D, D_FF), lambda p, blk, ex, lo, hi: (layer, ex[p], 0, 0)),
            pl.BlockSpec((None, None, D, D_FF), lambda p, blk, ex, lo, hi: (layer, ex[p], 0, 0)),
            pl.BlockSpec((None, None, D_FF, D), lambda p, blk, ex, lo, hi: (layer, ex[p], 0, 0)),
        ],
        out_specs=pl.BlockSpec((tm, D_PACK), lambda p, blk, ex, lo, hi: (blk[p], 0)),
        scratch_shapes=[pltpu.VMEM((D, D_FF), BF16), pltpu.VMEM((D, D_FF), BF16), pltpu.VMEM((D_FF, D), BF16)],
    )
    return pl.pallas_call(
        functools.partial(_expert_kernel, tm=tm),
        grid_spec=grid_spec,
        out_shape=jax.ShapeDtypeStruct((N, D_PACK), U32),
        compiler_params=_cparams(("arbitrary",), EXPERT_VMEM),
        name="experts",
    )(seg_blk, seg_exp, seg_lo, seg_hi, xs, gates, wg, wu, wd)


def _route(eidx_t, gates_t, T, tm):
    N = T * TOP_K
    flat_e = eidx_t.T.reshape(N)
    iota = jnp.arange(N, dtype=jnp.int32)
    sorted_e, order, g_sorted = lax.sort((flat_e, iota, gates_t.T.reshape(N)), num_keys=1, is_stable=True)
    tok = order // TOP_K
    _, pos = lax.sort((order, iota), num_keys=1)
    counts = jnp.bincount(flat_e, length=N_EXPERTS)
    starts = (jnp.cumsum(counts) - counts).astype(jnp.int32)
    nb = N // tm
    bounds = jnp.sort(jnp.concatenate([jnp.arange(nb, dtype=jnp.int32) * tm, starts]))
    ends = jnp.concatenate([bounds[1:], jnp.full((1,), N, bounds.dtype)])
    seg_blk = jnp.minimum(bounds // tm, nb - 1).astype(jnp.int32)
    seg_exp = sorted_e[jnp.minimum(bounds, N - 1)].astype(jnp.int32)
    pos_rows = pos.reshape(T, TOP_K).T.reshape(N)
    return tok, g_sorted, pos_rows, seg_blk, seg_exp, bounds, ends


def _group_major(a, axis):
    shape = a.shape
    a = a.reshape(shape[:axis] + (A_KV_HEADS, A_GROUP, HEAD_DIM) + shape[axis + 1:])
    return jnp.swapaxes(a, axis, axis + 1).reshape(shape)


def _permute_in_proj(w):
    bounds = np.cumsum([0, 768, 256, 256, 512, 768, 768, 768])
    qa, ka, va, ub, qc, kc, vc = [w[:, bounds[s]:bounds[s + 1]] for s in range(7)]
    return jnp.concatenate([_group_major(qa, 1), qc, ka, kc, ub, va, vc], axis=1)


def kernel(x, positions, attn_norm, w_in, sinks, branch_norm_a, w_pool, pool_scale, lambda_q1, lambda_k1,
           lambda_q2, lambda_k2, subln, w_out, ffn_norm, w_router_group, b_router_group, w_router_expert,
           b_router_expert, w_expert_gate, w_expert_up, w_expert_down, final_norm):
    B, S, D = x.shape
    T = B * S
    depth = w_in.shape[0]
    tm_e = 512

    half = HEAD_DIM // 2
    inv = ROPE_THETA ** (-jnp.arange(half, dtype=F32) / half)
    ang = positions.astype(F32).reshape(T, 1) * inv
    cos, sin = jnp.cos(ang), jnp.sin(ang)
    cos_t = jnp.concatenate([cos, cos, cos, cos], axis=1)
    sin_t = jnp.concatenate([-sin, sin, -sin, sin], axis=1)

    h = x.reshape(T, D)
    y_rows = None
    for l in range(depth):
        if l == 0:
            (hn,) = _norm(h, None, attn_norm[l], emit_h=False, emit_bf16=True)
        else:
            h, hn = _norm(h, y_rows, attn_norm[l], emit_h=True, emit_bf16=True)
        w_in_l = _permute_in_proj(w_in[l]).astype(BF16)
        proj, vat, vt = _proj(hn, w_in_l[:, :COL_VA], w_in_l[:, COL_VA:].T, cos_t, sin_t, B=B, S=S)

        sink_heads = sinks[l].reshape(A_KV_HEADS, A_GROUP).T.reshape(A_HEADS) * LOG2E
        sink_row = jnp.repeat(sink_heads, WINDOW).reshape(1, A_COLS)
        oa = _swa(proj, vat, sink_row, _group_major(branch_norm_a[l], 0), B=B, S=S)
        ob = _pool(proj, w_pool[l].astype(BF16), pool_scale[l], B=B, S=S)

        lam_init = 0.8 - 0.6 * math.exp(-0.3 * l)
        lam_vecs = jnp.stack([lambda_q1[l], lambda_k1[l], lambda_q2[l], lambda_k2[l]]).astype(F32)
        oc = _diff(proj, vt, lam_vecs, subln[l], B=B, S=S, lam_init=lam_init)

        w_out_l = jnp.concatenate([_group_major(w_out[l][:A_WIDTH], 0), w_out[l][A_WIDTH:]], axis=0)
        h1, hn2 = _out(oa, ob, oc, w_out_l.astype(BF16), h, ffn_norm[l])

        wr = jnp.zeros((ROUTER_LO, D), F32)
        wr = wr.at[ROUTER_GROUP_ROW:ROUTER_GROUP_ROW + N_GROUPS].set(w_router_group[l].T)
        wr = wr.at[ROUTER_EXPERT_ROW:ROUTER_EXPERT_ROW + N_EXPERTS].set(w_router_expert[l].T)
        wr_hi = wr.astype(BF16)
        wr_lo = (wr - wr_hi.astype(F32)).astype(BF16)
        br = jnp.zeros((ROUTER_LO, 1), F32)
        br = br.at[ROUTER_GROUP_ROW:ROUTER_GROUP_ROW + N_GROUPS, 0].set(b_router_group[l])
        br = br.at[ROUTER_EXPERT_ROW:ROUTER_EXPERT_ROW + N_EXPERTS, 0].set(b_router_expert[l])
        eidx_t, gates_t = _router(hn2, jnp.concatenate([wr_hi, wr_lo], axis=0), br)

        tok, g_sorted, pos_rows, seg_blk, seg_exp, seg_lo, seg_hi = _route(eidx_t, gates_t, T, tm_e)
        xs = _sc_gather_rows(hn2, tok)
        ys = _experts(seg_blk, seg_exp, seg_lo, seg_hi, xs, g_sorted.reshape(-1, 1),
                      w_expert_gate, w_expert_up, w_expert_down, l, tm=tm_e)
        y_rows = _sc_gather_rows(ys, pos_rows)
        h = h1

    (out,) = _norm(h, y_rows, final_norm, emit_h=False, emit_bf16=False)
    return out.reshape(B, S, D)
```

```python
import functools
import math

import jax
import jax.numpy as jnp
from jax import lax
from jax.experimental import pallas as pl
from jax.experimental.pallas import tpu as pltpu
from jax.experimental.pallas import tpu_sc as plsc
import numpy as np

F32 = jnp.float32
BF16 = jnp.bfloat16

D_MODEL = 2048
HEAD_DIM = 64
ROPE_THETA = 10000.0
A_HEADS = 12
A_KV_HEADS = 4
A_GROUP = 3
A_WIDTH = 768
WINDOW = 128
POOL_WINDOWS = (2, 4, 8, 16)
B_WIDTH = 512
B_GROUP_DIM = 128
C_VDIM = 128
C_WIDTH = 768
C_HEADS = 6
DIFF_EPS = 1e-5
IN_COLS = 4096
N_GROUPS = 4
EXPERTS_PER_GROUP = 8
N_EXPERTS = 32
TOP_K = 2
D_FF = 512
NORM_EPS = 1e-6
NEG = -1e30
LOG2E = math.log2(math.e)
Q_SCALE = HEAD_DIM ** -0.5 * LOG2E

COL_QA, COL_QC, COL_KA, COL_KC, COL_UB, COL_VA, COL_VC = 0, 768, 1536, 1792, 2560, 3072, 3328
PROJ_TN = 512
A_KV_WIDTH = A_KV_HEADS * HEAD_DIM
PROJ_SUB = 256
N_SCALED_TILES = COL_KA // PROJ_TN
N_ROPE_TILES = COL_UB // PROJ_TN
N_MAIN_TILES = COL_VA // PROJ_TN
DIFF_KB = 512
VT_ROWS = C_VDIM + 16

VMEM_LIMIT = 48 * 1024 * 1024
D_PACK = D_MODEL // 2
U32 = jnp.uint32


def _cparams(sem, vmem=VMEM_LIMIT):
    return pltpu.CompilerParams(dimension_semantics=sem, vmem_limit_bytes=vmem)


def _pack_rows(x):
    lo = pltpu.bitcast(x[:, :D_PACK].astype(BF16).astype(F32), U32) >> 16
    hi = pltpu.bitcast(x[:, D_PACK:].astype(BF16).astype(F32), U32)
    return hi | lo


def _unpack_rows(w):
    return pltpu.bitcast(w << 16, F32), pltpu.bitcast(w & U32(0xFFFF0000), F32)


def _unpack_rows_bf16(w):
    lo, hi = _unpack_rows(w)
    return jnp.concatenate([lo.astype(BF16), hi.astype(BF16)], axis=1)


SC_WINDOW = 32
SC_INDEX_LANES = 128


def _sc_gather_rows(x, idx):
    M = idx.shape[0]
    W = x.shape[1]
    idx2 = jnp.pad(idx.reshape(M // SC_WINDOW, SC_WINDOW), ((0, 0), (0, SC_INDEX_LANES - SC_WINDOW)))
    mesh = plsc.VectorSubcoreMesh(core_axis_name="core", subcore_axis_name="subcore")

    @pl.kernel(out_type=jax.ShapeDtypeStruct((M, W), x.dtype), mesh=mesh)
    def gather(x_hbm, i_hbm, o_hbm):
        def body(i_vmem, o_vmem):
            pltpu.sync_copy(x_hbm.at[i_vmem.at[0, pl.ds(0, SC_WINDOW)]], o_vmem)

        pltpu.emit_pipeline(
            body,
            grid=(M // SC_WINDOW,),
            in_specs=[pl.BlockSpec((1, SC_INDEX_LANES), index_map=lambda i: (i, 0))],
            out_specs=[pl.BlockSpec((SC_WINDOW, W), index_map=lambda i: (i, 0))],
            core_axis_name=("core", "subcore"),
            dimension_semantics=(pltpu.PARALLEL,),
        )(i_hbm, o_hbm)

    return gather(x, idx2)


def _norm_kernel(*refs, n_add, emit_h, emit_bf16, eps):
    h_ref = refs[0]
    add_refs = refs[1:1 + n_add]
    g_ref = refs[1 + n_add]
    outs = refs[2 + n_add:]
    h = h_ref[...]
    if n_add:
        lo = jnp.zeros((h.shape[0], D_PACK), F32)
        hi = jnp.zeros((h.shape[0], D_PACK), F32)
        for r in add_refs:
            a, b = _unpack_rows(r[...])
            lo, hi = lo + a, hi + b
        h = h + jnp.concatenate([lo, hi], axis=1)
    ms = jnp.mean(h * h, axis=-1, keepdims=True)
    y = h * lax.rsqrt(ms + eps) * g_ref[...]
    k = 0
    if emit_h:
        outs[k][...] = h
        k += 1
    outs[k][...] = y.astype(BF16 if emit_bf16 else F32)


def _norm(h, y_rows, g, *, emit_h, emit_bf16, tm=512):
    T, D = h.shape
    nt = T // tm
    row = pl.BlockSpec((tm, D), lambda i: (i, 0))
    n_add = 0 if y_rows is None else TOP_K
    add_specs = [pl.BlockSpec((tm, D_PACK), lambda i, k=k: (i + k * nt, 0)) for k in range(n_add)]
    out_shape, out_specs = [], []
    if emit_h:
        out_shape.append(jax.ShapeDtypeStruct((T, D), F32))
        out_specs.append(row)
    out_shape.append(jax.ShapeDtypeStruct((T, D), BF16 if emit_bf16 else F32))
    out_specs.append(row)
    return pl.pallas_call(
        functools.partial(_norm_kernel, n_add=n_add, emit_h=emit_h, emit_bf16=emit_bf16, eps=NORM_EPS),
        grid=(nt,),
        in_specs=[row] + add_specs + [pl.BlockSpec((1, D), lambda i: (0, 0))],
        out_specs=out_specs,
        out_shape=out_shape,
        compiler_params=_cparams(("parallel",)),
        name="norm",
    )(h, *([y_rows] * n_add), g.reshape(1, D))


def _proj_kernel(x_ref, w_ref, wt_ref, cos_ref, sin_ref, o_ref, vat_ref, vt_ref, *, tm):
    j = pl.program_id(1)
    subs = [slice(s * PROJ_SUB, (s + 1) * PROJ_SUB) for s in range(tm // PROJ_SUB)]

    def matmuls():
        return [jnp.dot(x_ref[rows, :], w_ref[...], preferred_element_type=F32) for rows in subs]

    @pl.when(j < N_ROPE_TILES)
    def _():
        accs = matmuls()
        scale = jnp.where(j < N_SCALED_TILES, Q_SCALE, 1.0).astype(F32)
        lane = lax.broadcasted_iota(jnp.int32, (PROJ_SUB, PROJ_TN), 1)
        first_half = (lane & (HEAD_DIM // 2)) == 0
        for rows, acc in zip(subs, accs):
            c = jnp.concatenate([cos_ref[rows, :] * scale] * (PROJ_TN // 128), axis=1)
            s = jnp.concatenate([sin_ref[rows, :] * scale] * (PROJ_TN // 128), axis=1)
            swapped = jnp.where(first_half,
                                pltpu.roll(acc, PROJ_TN - HEAD_DIM // 2, 1),
                                pltpu.roll(acc, HEAD_DIM // 2, 1))
            o_ref[rows, :] = (acc * c + swapped * s).astype(o_ref.dtype)

    @pl.when((j >= N_ROPE_TILES) & (j < N_MAIN_TILES))
    def _():
        for rows, acc in zip(subs, matmuls()):
            o_ref[rows, :] = acc.astype(o_ref.dtype)

    per_sub = PROJ_SUB // WINDOW
    per_kb = DIFF_KB // PROJ_SUB
    for t in range((IN_COLS - COL_VA) // PROJ_TN):
        @pl.when(j == N_MAIN_TILES + t)
        def _(t=t):
            accs = [lax.dot_general(wt_ref[...], x_ref[rows, :], (((1,), (1,)), ((), ())),
                                    preferred_element_type=F32) for rows in subs]
            for s, acc in enumerate(accs):
                cols = slice((s % per_kb) * PROJ_SUB, (s % per_kb + 1) * PROJ_SUB)
                for r0 in range(0, PROJ_TN, C_VDIM):
                    row = t * PROJ_TN + r0
                    if row == 0:
                        for b in range(per_sub):
                            vat_ref[s * per_sub + b] = acc[0:A_KV_WIDTH, b * WINDOW:(b + 1) * WINDOW].astype(
                                vat_ref.dtype)
                    elif row >= A_KV_WIDTH:
                        head = (row - A_KV_WIDTH) // C_VDIM
                        vt_ref[head, s // per_kb, 0:C_VDIM, cols] = acc[r0:r0 + C_VDIM, :].astype(vt_ref.dtype)

    @pl.when(j == N_MAIN_TILES)
    def _():
        row = lax.broadcasted_iota(jnp.int32, (VT_ROWS - C_VDIM, DIFF_KB), 0)
        extra = jnp.where(row == 0, 1.0, 0.0).astype(vt_ref.dtype)
        for head in range(C_HEADS):
            for kb in range(tm // DIFF_KB):
                vt_ref[head, kb, C_VDIM:VT_ROWS, :] = extra


def _proj(hn, w, wt, cos_t, sin_t, *, B, S, tm=2048):
    T, D = hn.shape
    spt = S // tm
    last_main = N_MAIN_TILES - 1
    return pl.pallas_call(
        functools.partial(_proj_kernel, tm=tm),
        grid=(T // tm, IN_COLS // PROJ_TN),
        in_specs=[
            pl.BlockSpec((tm, D), lambda i, j: (i, 0)),
            pl.BlockSpec((D, PROJ_TN), lambda i, j: (0, jnp.minimum(j, last_main))),
            pl.BlockSpec((PROJ_TN, D), lambda i, j: (jnp.maximum(j - N_MAIN_TILES, 0), 0)),
            pl.BlockSpec((tm, 128), lambda i, j: (i, 0)),
            pl.BlockSpec((tm, 128), lambda i, j: (i, 0)),
        ],
        out_specs=[
            pl.BlockSpec((tm, PROJ_TN), lambda i, j: (i, jnp.minimum(j, last_main))),
            pl.BlockSpec((tm // WINDOW, A_KV_WIDTH, WINDOW), lambda i, j: (i, 0, 0)),
            pl.BlockSpec((None, C_HEADS, tm // DIFF_KB, VT_ROWS, DIFF_KB),
                         lambda i, j: (i // spt, 0, i % spt, 0, 0)),
        ],
        out_shape=[
            jax.ShapeDtypeStruct((T, COL_VA), BF16),
            jax.ShapeDtypeStruct((T // WINDOW, A_KV_WIDTH, WINDOW), BF16),
            jax.ShapeDtypeStruct((B, C_HEADS, S // DIFF_KB, VT_ROWS, DIFF_KB), BF16),
        ],
        compiler_params=_cparams(("parallel", "arbitrary")),
        name="proj",
    )(hn, w, wt, cos_t, sin_t)


A_COLS = A_HEADS * WINDOW


def _swa_kernel(q_ref, kc_ref, kp_ref, vc_ref, vp_ref, sink_ref, g_ref, o_ref, *, tq):
    i = pl.program_id(1)
    nblk = tq // WINDOW
    kext = jnp.concatenate([kp_ref[...], kc_ref[...]], axis=0)
    lane_kv = lax.broadcasted_iota(jnp.int32, (WINDOW, A_KV_WIDTH), 1) // HEAD_DIM
    zero = jnp.zeros((WINDOW, A_KV_WIDTH), BF16)

    scores = []
    for b in range(nblk):
        parts = []
        for g in range(A_GROUP):
            qg = q_ref[b * WINDOW:(b + 1) * WINDOW, g * A_KV_WIDTH:(g + 1) * A_KV_WIDTH]
            parts += [jnp.where(lane_kv == j, qg, zero) for j in range(A_KV_HEADS)]
        qcat = jnp.concatenate(parts, axis=0)
        scores.append(lax.dot_general(kext[b * WINDOW:(b + 2) * WINDOW, :], qcat, (((1,), (1,)), ((), ())),
                                      preferred_element_type=F32))

    r = lax.broadcasted_iota(jnp.int32, (2 * WINDOW, A_COLS), 0)
    c = lax.broadcasted_iota(jnp.int32, (2 * WINDOW, A_COLS), 1) & (WINDOW - 1)
    band = (r > c) & (r <= c + WINDOW)
    sink = sink_ref[...]
    gain = g_ref[...]
    for b, st in enumerate(scores):
        mask = band
        if b == 0:
            mask = band & ((r >= WINDOW) | (i > 0))
        st = jnp.where(mask, st, NEG)
        m = jnp.maximum(jnp.max(st, axis=0, keepdims=True), sink)
        p = jnp.exp2(st - m)
        l = jnp.sum(p, axis=0, keepdims=True) + jnp.exp2(sink - m)
        vt = jnp.concatenate([vp_ref[0] if b == 0 else vc_ref[b - 1], vc_ref[b]], axis=1)
        pv = jnp.dot(vt, p.astype(BF16), preferred_element_type=F32)
        inv_l = 1.0 / l
        heads = []
        for g in range(A_GROUP):
            for j in range(A_KV_HEADS):
                cols = slice((g * A_KV_HEADS + j) * WINDOW, (g * A_KV_HEADS + j + 1) * WINDOW)
                heads.append(pv[j * HEAD_DIM:(j + 1) * HEAD_DIM, cols] * inv_l[:, cols])
        ot = jnp.concatenate(heads, axis=0)
        ms = jnp.mean(ot * ot, axis=0, keepdims=True)
        y = (ot * lax.rsqrt(ms + NORM_EPS)).T * gain
        o_ref[b * WINDOW:(b + 1) * WINDOW, :] = y.astype(o_ref.dtype)


def _swa(proj, vat, sink_row, g, *, B, S, tq=512):
    nq = S // tq
    rpb = tq // WINDOW

    def prev_blk(b, i):
        return jnp.maximum(b * (S // WINDOW) + i * rpb - 1, 0)

    return pl.pallas_call(
        functools.partial(_swa_kernel, tq=tq),
        grid=(B, nq),
        in_specs=[
            pl.BlockSpec((tq, A_WIDTH), lambda b, i: (b * nq + i, COL_QA // A_WIDTH)),
            pl.BlockSpec((tq, A_KV_WIDTH), lambda b, i: (b * nq + i, COL_KA // A_KV_WIDTH)),
            pl.BlockSpec((WINDOW, A_KV_WIDTH), lambda b, i: (prev_blk(b, i), COL_KA // A_KV_WIDTH)),
            pl.BlockSpec((rpb, A_KV_WIDTH, WINDOW), lambda b, i: (b * nq + i, 0, 0)),
            pl.BlockSpec((1, A_KV_WIDTH, WINDOW), lambda b, i: (prev_blk(b, i), 0, 0)),
            pl.BlockSpec((1, A_COLS), lambda b, i: (0, 0)),
            pl.BlockSpec((1, A_WIDTH), lambda b, i: (0, 0)),
        ],
        out_specs=pl.BlockSpec((tq, A_WIDTH), lambda b, i: (b * nq + i, 0)),
        out_shape=jax.ShapeDtypeStruct((B * S, A_WIDTH), BF16),
        compiler_params=_cparams(("parallel", "parallel")),
        name="swa",
    )(proj, proj, proj, vat, vat, sink_row, g.reshape(1, A_WIDTH))


POOL_HALO = 16


def _pool_kernel(u0_ref, u1_ref, h0_ref, h1_ref, w_ref, s_ref, o_ref, *, tq):
    i = pl.program_id(1)
    u = jnp.concatenate([u0_ref[...], u1_ref[...]], axis=1).astype(F32)
    halo = jnp.concatenate([h0_ref[...], h1_ref[...]], axis=1).astype(F32)
    halo = halo * (i > 0).astype(F32)
    ext = jnp.concatenate([halo, u], axis=0)
    t = i * tq + lax.broadcasted_iota(jnp.int32, (tq, 1), 0)
    for gi, w in enumerate(POOL_WINDOWS):
        sl = slice(gi * B_GROUP_DIM, (gi + 1) * B_GROUP_DIM)
        s = ext[:, sl]
        shift = 1
        while shift < w:
            s = s + pltpu.roll(s, shift, 0)
            shift *= 2
        cnt = jnp.minimum(t + 1, w).astype(F32)
        y = (s[POOL_HALO:, :] / cnt - u[:, sl]).astype(BF16)
        o = jnp.dot(y, w_ref[gi], preferred_element_type=F32) * s_ref[:, sl]
        o_ref[:, sl] = o.astype(o_ref.dtype)


def _pool(proj, w_pool, scale, *, B, S, tq=512):
    nq = S // tq
    c0 = COL_UB // 256

    def cur(c):
        return pl.BlockSpec((tq, 256), lambda b, i: (b * nq + i, c))

    def halo(c):
        return pl.BlockSpec(
            (POOL_HALO, 256),
            lambda b, i: (jnp.maximum((b * S + i * tq) // POOL_HALO - 1, 0), c))

    return pl.pallas_call(
        functools.partial(_pool_kernel, tq=tq),
        grid=(B, nq),
        in_specs=[cur(c0), cur(c0 + 1), halo(c0), halo(c0 + 1),
                  pl.BlockSpec((len(POOL_WINDOWS), B_GROUP_DIM, B_GROUP_DIM), lambda b, i: (0, 0, 0)),
                  pl.BlockSpec((1, B_WIDTH), lambda b, i: (0, 0))],
        out_specs=pl.BlockSpec((tq, B_WIDTH), lambda b, i: (b * nq + i, 0)),
        out_shape=jax.ShapeDtypeStruct((B * S, B_WIDTH), BF16),
        compiler_params=_cparams(("parallel", "parallel")),
        name="pool",
    )(proj, proj, proj, proj, w_pool, scale.reshape(1, B_WIDTH))


DIFF_UNROLL = 4


def _diff_kernel(q_ref, k_ref, vt_ref, lam_ref, g_ref, o_ref, qcat_ref, s_ref, bm_ref, m_ref, acc_ref,
                 *, tq, lam_init):
    qi = pl.program_id(2)
    n_diag = tq // DIFF_KB
    n_below = qi * n_diag
    q = q_ref[...]
    lane = lax.broadcasted_iota(jnp.int32, q.shape, 1)
    zero = jnp.zeros_like(q)
    qcat_ref[0:tq, :] = jnp.where(lane < HEAD_DIM, q, zero)
    qcat_ref[tq:2 * tq, :] = jnp.where(lane >= HEAD_DIM, q, zero)
    m_ref[...] = jnp.full(m_ref.shape, NEG, F32)
    acc_ref[...] = jnp.zeros(acc_ref.shape, F32)

    def produce(kidx, q0=0):
        off = pl.multiple_of(kidx * DIFF_KB, DIFF_KB)
        queries = qcat_ref[...] if q0 == 0 else jnp.concatenate(
            [qcat_ref[q0:tq, :], qcat_ref[tq + q0:2 * tq, :]], axis=0)
        st = lax.dot_general(k_ref[pl.ds(off, DIFF_KB), :], queries, (((1,), (1,)), ((), ())),
                             preferred_element_type=F32)
        return st, jnp.max(st, axis=0, keepdims=True)

    def store(buf, block):
        width = block[0].shape[1]
        s_ref[buf, :, 0:width], bm_ref[buf, :, 0:width] = block

    def consume(buf, kidx, first_key):
        q0 = first_key or 0
        nq_act = tq - q0
        st = s_ref[buf, :, 0:2 * nq_act]
        if first_key is not None:
            r = lax.broadcasted_iota(jnp.int32, st.shape, 0)
            c = lax.broadcasted_iota(jnp.int32, st.shape, 1)
            c = jnp.where(c >= nq_act, c - nq_act, c)
            st = jnp.where(r <= c, st, NEG)
            block_max = jnp.max(st, axis=0, keepdims=True)
        else:
            block_max = bm_ref[buf]
        maps = (slice(q0, tq), slice(tq + q0, 2 * tq))
        m = m_ref[...] if q0 == 0 else jnp.concatenate([m_ref[:, lanes] for lanes in maps], axis=1)
        m_new = jnp.maximum(m, block_max)
        p = jnp.exp2(st - m_new)
        pv = jnp.dot(vt_ref[kidx], p.astype(BF16), preferred_element_type=F32)
        alpha = jnp.exp2(m - m_new)
        if q0 == 0:
            m_ref[...] = m_new
            acc_ref[...] = alpha * acc_ref[...] + pv
        else:
            for c, lanes in enumerate(maps):
                part = slice(c * nq_act, (c + 1) * nq_act)
                m_ref[:, lanes] = m_new[:, part]
                acc_ref[:, lanes] = alpha[:, part] * acc_ref[:, lanes] + pv[:, part]

    def steps(first, n_plain, n_masked, last):
        n = n_plain + n_masked
        first_keys = [None] * n_plain + [u * DIFF_KB for u in range(n_masked)]
        for u in range(n):
            more = not (last and u == n - 1)
            if more:
                nxt = produce(first + u + 1, (first_keys[u + 1] or 0) if u + 1 < n else 0)
            consume(u % 2, first + u, first_keys[u])
            if more:
                store(1 - u % 2, nxt)

    store(0, produce(0))

    def body(c, carry):
        steps(c * DIFF_UNROLL, DIFF_UNROLL, 0, False)
        return carry

    lax.fori_loop(0, n_below // DIFF_UNROLL, body, 0)
    for rem in range(0, DIFF_UNROLL, math.gcd(n_diag, DIFF_UNROLL)):
        @pl.when(n_below % DIFF_UNROLL == rem)
        def _(rem=rem):
            steps(n_below - rem, rem, n_diag, True)

    lv = lam_ref[...]
    lam = (jnp.exp(jnp.sum(lv[0:1] * lv[1:2], axis=1, keepdims=True))
           - jnp.exp(jnp.sum(lv[2:3] * lv[3:4], axis=1, keepdims=True)) + lam_init)
    l = acc_ref[C_VDIM:C_VDIM + 1, :]
    acc = acc_ref[0:C_VDIM, :]
    ot = acc[:, :tq] / l[:, :tq] - lam * (acc[:, tq:] / l[:, tq:])
    ms = jnp.mean(ot * ot, axis=0, keepdims=True)
    yt = ot * lax.rsqrt(ms + DIFF_EPS) * (g_ref[...] * (1.0 - lam_init))
    o_ref[...] = yt.T.astype(o_ref.dtype)


def _diff(proj, vt, lam_vecs, subln, *, B, S, lam_init, tq=1024):
    nq = S // tq
    nkb = S // DIFF_KB
    assert tq % DIFF_KB == 0 and DIFF_UNROLL % 2 == 0
    return pl.pallas_call(
        functools.partial(_diff_kernel, tq=tq, lam_init=lam_init),
        grid=(B, C_HEADS, nq),
        in_specs=[
            pl.BlockSpec((tq, C_VDIM), lambda b, h, i: (b * nq + i, COL_QC // C_VDIM + h)),
            pl.BlockSpec((S, C_VDIM), lambda b, h, i: (b, COL_KC // C_VDIM + h)),
            pl.BlockSpec((None, None, nkb, VT_ROWS, DIFF_KB), lambda b, h, i: (b, h, 0, 0, 0)),
            pl.BlockSpec((4, HEAD_DIM), lambda b, h, i: (0, 0)),
            pl.BlockSpec((C_VDIM, 1), lambda b, h, i: (0, 0)),
        ],
        out_specs=pl.BlockSpec((tq, C_VDIM), lambda b, h, i: (b * nq + i, h)),
        out_shape=jax.ShapeDtypeStruct((B * S, C_WIDTH), BF16),
        scratch_shapes=[pltpu.VMEM((2 * tq, C_VDIM), BF16),
                        pltpu.VMEM((2, DIFF_KB, 2 * tq), F32),
                        pltpu.VMEM((2, 1, 2 * tq), F32),
                        pltpu.VMEM((1, 2 * tq), F32),
                        pltpu.VMEM((VT_ROWS, 2 * tq), F32)],
        compiler_params=_cparams(("parallel", "parallel", "arbitrary")),
        name="diff",
    )(proj, proj, vt, lam_vecs, subln.reshape(C_VDIM, 1))


OUT_SUB = 256


def _out_kernel(oa_ref, ob_ref, oc_ref, w_ref, h_ref, g_ref, h1_ref, hn_ref, *, tm):
    subs = [slice(s * OUT_SUB, (s + 1) * OUT_SUB) for s in range(tm // OUT_SUB)]
    accs = []
    for rows in subs:
        acc = jnp.dot(oa_ref[rows, :], w_ref[0:A_WIDTH, :], preferred_element_type=F32)
        acc = acc + jnp.dot(ob_ref[rows, :], w_ref[A_WIDTH:A_WIDTH + B_WIDTH, :], preferred_element_type=F32)
        acc = acc + jnp.dot(oc_ref[rows, :], w_ref[A_WIDTH + B_WIDTH:, :], preferred_element_type=F32)
        accs.append(acc)
    for rows, acc in zip(subs, accs):
        h1 = h_ref[rows, :] + acc
        h1_ref[rows, :] = h1
        ms = jnp.mean(h1 * h1, axis=1, keepdims=True)
        hn = h1 * lax.rsqrt(ms + NORM_EPS) * g_ref[...]
        hn_ref[rows, :] = _pack_rows(hn)


def _out(oa, ob, oc, w, h, g, *, tm=512):
    T, D = h.shape
    return pl.pallas_call(
        functools.partial(_out_kernel, tm=tm),
        grid=(T // tm,),
        in_specs=[
            pl.BlockSpec((tm, A_WIDTH), lambda i: (i, 0)),
            pl.BlockSpec((tm, B_WIDTH), lambda i: (i, 0)),
            pl.BlockSpec((tm, C_WIDTH), lambda i: (i, 0)),
            pl.BlockSpec((D, D), lambda i: (0, 0)),
            pl.BlockSpec((tm, D), lambda i: (i, 0)),
            pl.BlockSpec((1, D), lambda i: (0, 0)),
        ],
        out_specs=[
            pl.BlockSpec((tm, D), lambda i: (i, 0)),
            pl.BlockSpec((tm, D_PACK), lambda i: (i, 0)),
        ],
        out_shape=[
            jax.ShapeDtypeStruct((T, D), F32),
            jax.ShapeDtypeStruct((T, D_PACK), U32),
        ],
        compiler_params=_cparams(("parallel",)),
        name="out",
    )(oa, ob, oc, w, h, g.reshape(1, D))


ROUTER_ROWS = 128
ROUTER_LO = 64
ROUTER_GROUP_ROW = 0
ROUTER_EXPERT_ROW = 8


def _router_kernel(x_ref, wt_ref, b_ref, e_ref, g_ref):
    x = _unpack_rows_bf16(x_ref[...])
    lg = lax.dot_general(wt_ref[...], x, (((1,), (1,)), ((), ())), preferred_element_type=F32)
    lg = lg[0:ROUTER_LO] + lg[ROUTER_LO:ROUTER_ROWS] + b_ref[...]
    grp = lg[ROUTER_GROUP_ROW:ROUTER_GROUP_ROW + N_GROUPS]
    ex = lg[ROUTER_EXPERT_ROW:ROUTER_EXPERT_ROW + N_EXPERTS]

    mg = jnp.max(grp, axis=0, keepdims=True)
    pg_sel = 1.0 / jnp.sum(jnp.exp(grp - mg), axis=0, keepdims=True)
    gi = lax.broadcasted_iota(jnp.int32, grp.shape, 0)
    g_sel = jnp.min(jnp.where(grp == mg, gi, N_GROUPS), axis=0, keepdims=True)

    er = lax.broadcasted_iota(jnp.int32, ex.shape, 0)
    group_of = lax.shift_right_logical(er, EXPERTS_PER_GROUP.bit_length() - 1)
    cand = jnp.where(group_of == g_sel, ex, NEG)
    v1 = jnp.max(cand, axis=0, keepdims=True)
    i1 = jnp.min(jnp.where(cand == v1, er, N_EXPERTS), axis=0, keepdims=True)
    rest = jnp.where(er == i1, NEG, cand)
    v2 = jnp.max(rest, axis=0, keepdims=True)
    i2 = jnp.min(jnp.where(rest == v2, er, N_EXPERTS), axis=0, keepdims=True)
    t = jnp.exp(v2 - v1)
    g1 = pg_sel / (1.0 + t)
    e_ref[...] = jnp.concatenate([i1, i2], axis=0)
    g_ref[...] = jnp.concatenate([g1, g1 * t], axis=0)


def _router(hn_packed, wt, b, *, tm=1024):
    T = hn_packed.shape[0]
    D = wt.shape[1]
    return pl.pallas_call(
        _router_kernel,
        grid=(T // tm,),
        in_specs=[
            pl.BlockSpec((tm, D_PACK), lambda i: (i, 0)),
            pl.BlockSpec((ROUTER_ROWS, D), lambda i: (0, 0)),
            pl.BlockSpec((ROUTER_LO, 1), lambda i: (0, 0)),
        ],
        out_specs=[pl.BlockSpec((TOP_K, tm), lambda i: (0, i)), pl.BlockSpec((TOP_K, tm), lambda i: (0, i))],
        out_shape=[jax.ShapeDtypeStruct((TOP_K, T), jnp.int32), jax.ShapeDtypeStruct((TOP_K, T), F32)],
        compiler_params=_cparams(("parallel",)),
        name="router",
    )(hn_packed, wt, b)


EXPERT_SUB = 256
EXPERT_VMEM = 56 * 1024 * 1024


def _expert_kernel(blk_ref, exp_ref, lo_ref, hi_ref, x_ref, g_ref, wg_ref, wu_ref, wd_ref, o_ref,
                   wg_bf, wu_bf, wd_bf, *, tm):
    p = pl.program_id(0)
    prev = jnp.maximum(p - 1, 0)
    blk = blk_ref[p]
    first = (p == 0) | (blk != blk_ref[prev])
    new_expert = (p == 0) | (exp_ref[p] != exp_ref[prev])
    lo = lo_ref[p]
    hi = hi_ref[p]

    @pl.when(new_expert)
    def _():
        wg_bf[...] = wg_ref[...].astype(BF16)
        wu_bf[...] = wu_ref[...].astype(BF16)
        wd_bf[...] = wd_ref[...].astype(BF16)

    @pl.when(first)
    def _():
        o_ref[...] = jnp.zeros_like(o_ref)

    for s in range(tm // EXPERT_SUB):
        row0 = blk * tm + s * EXPERT_SUB
        rows = slice(s * EXPERT_SUB, (s + 1) * EXPERT_SUB)

        @pl.when((hi > row0) & (lo < row0 + EXPERT_SUB))
        def _(row0=row0, rows=rows):
            x = _unpack_rows_bf16(x_ref[rows, :])
            a = jnp.dot(x, wg_bf[...], preferred_element_type=F32)
            u = jnp.dot(x, wu_bf[...], preferred_element_type=F32)
            act = (a / (1.0 + jnp.exp(-a))) * u
            y = jnp.dot(act.astype(BF16), wd_bf[...], preferred_element_type=F32)
            r = row0 + lax.broadcasted_iota(jnp.int32, (EXPERT_SUB, 1), 0)
            mine = (r >= lo) & (r < hi)
            o_ref[rows, :] = jnp.where(mine, _pack_rows(y * g_ref[rows, :]), o_ref[rows, :])


def _experts(seg_blk, seg_exp, seg_lo, seg_hi, xs, gates, wg, wu, wd, layer, *, tm):
    N = xs.shape[0]
    D = wg.shape[2]
    P = seg_blk.shape[0]
    grid_spec = pltpu.PrefetchScalarGridSpec(
        num_scalar_prefetch=4,
        grid=(P,),
        in_specs=[
            pl.BlockSpec((tm, D_PACK), lambda p, blk, ex, lo, hi: (blk[p], 0)),
            pl.BlockSpec((tm, 1), lambda p, blk, ex, lo, hi: (blk[p], 0)),
            pl.BlockSpec((None, None, D, D_FF), lambda p, blk, ex, lo, hi: (layer, ex[p], 0, 0)),
            pl.BlockSpec((None, None, D, D_FF), lambda p, blk, ex, lo, hi: (layer, ex[p], 0, 0)),
            pl.BlockSpec((None, None, D_FF, D), lambda p, blk, ex, lo, hi: (layer, ex[p], 0, 0)),
        ],
        out_specs=pl.BlockSpec((tm, D_PACK), lambda p, blk, ex, lo, hi: (blk[p], 0)),
        scratch_shapes=[pltpu.VMEM((D, D_FF), BF16), pltpu.VMEM((D, D_FF), BF16), pltpu.VMEM((D_FF, D), BF16)],
    )
    return pl.pallas_call(
        functools.partial(_expert_kernel, tm=tm),
        grid_spec=grid_spec,
        out_shape=jax.ShapeDtypeStruct((N, D_PACK), U32),
        compiler_params=_cparams(("arbitrary",), EXPERT_VMEM),
        name="experts",
    )(seg_blk, seg_exp, seg_lo, seg_hi, xs, gates, wg, wu, wd)


def _route(eidx_t, gates_t, T, tm):
    N = T * TOP_K
    flat_e = eidx_t.T.reshape(N)
    iota = jnp.arange(N, dtype=jnp.int32)
    sorted_e, order, g_sorted = lax.sort((flat_e, iota, gates_t.T.reshape(N)), num_keys=1, is_stable=True)
    tok = order // TOP_K
    _, pos = lax.sort((order, iota), num_keys=1)
    counts = jnp.bincount(flat_e, length=N_EXPERTS)
    starts = (jnp.cumsum(counts) - counts).astype(jnp.int32)
    nb = N // tm
    bounds = jnp.sort(jnp.concatenate([jnp.arange(nb, dtype=jnp.int32) * tm, starts]))
    ends = jnp.concatenate([bounds[1:], jnp.full((1,), N, bounds.dtype)])
    seg_blk = jnp.minimum(bounds // tm, nb - 1).astype(jnp.int32)
    seg_exp = sorted_e[jnp.minimum(bounds, N - 1)].astype(jnp.int32)
    pos_rows = pos.reshape(T, TOP_K).T.reshape(N)
    return tok, g_sorted, pos_rows, seg_blk, seg_exp, bounds, ends


def _group_major(a, axis):
    shape = a.shape
    a = a.reshape(shape[:axis] + (A_KV_HEADS, A_GROUP, HEAD_DIM) + shape[axis + 1:])
    return jnp.swapaxes(a, axis, axis + 1).reshape(shape)


def _permute_in_proj(w):
    bounds = np.cumsum([0, 768, 256, 256, 512, 768, 768, 768])
    qa, ka, va, ub, qc, kc, vc = [w[:, bounds[s]:bounds[s + 1]] for s in range(7)]
    return jnp.concatenate([_group_major(qa, 1), qc, ka, kc, ub, va, vc], axis=1)


def kernel(x, positions, attn_norm, w_in, sinks, branch_norm_a, w_pool, pool_scale, lambda_q1, lambda_k1,
           lambda_q2, lambda_k2, subln, w_out, ffn_norm, w_router_group, b_router_group, w_router_expert,
           b_router_expert, w_expert_gate, w_expert_up, w_expert_down, final_norm):
    B, S, D = x.shape
    T = B * S
    depth = w_in.shape[0]
    tm_e = 512

    half = HEAD_DIM // 2
    inv = ROPE_THETA ** (-jnp.arange(half, dtype=F32) / half)
    ang = positions.astype(F32).reshape(T, 1) * inv
    cos, sin = jnp.cos(ang), jnp.sin(ang)
    cos_t = jnp.concatenate([cos, cos, cos, cos], axis=1)
    sin_t = jnp.concatenate([-sin, sin, -sin, sin], axis=1)

    h = x.reshape(T, D)
    y_rows = None
    for l in range(depth):
        if l == 0:
            (hn,) = _norm(h, None, attn_norm[l], emit_h=False, emit_bf16=True)
        else:
            h, hn = _norm(h, y_rows, attn_norm[l], emit_h=True, emit_bf16=True)
        w_in_l = _permute_in_proj(w_in[l]).astype(BF16)
        proj, vat, vt = _proj(hn, w_in_l[:, :COL_VA], w_in_l[:, COL_VA:].T, cos_t, sin_t, B=B, S=S)

        sink_heads = sinks[l].reshape(A_KV_HEADS, A_GROUP).T.reshape(A_HEADS) * LOG2E
        sink_row = jnp.repeat(sink_heads, WINDOW).reshape(1, A_COLS)
        oa = _swa(proj, vat, sink_row, _group_major(branch_norm_a[l], 0), B=B, S=S)
        ob = _pool(proj, w_pool[l].astype(BF16), pool_scale[l], B=B, S=S)

        lam_init = 0.8 - 0.6 * math.exp(-0.3 * l)
        lam_vecs = jnp.stack([lambda_q1[l], lambda_k1[l], lambda_q2[l], lambda_k2[l]]).astype(F32)
        oc = _diff(proj, vt, lam_vecs, subln[l], B=B, S=S, lam_init=lam_init)

        w_out_l = jnp.concatenate([_group_major(w_out[l][:A_WIDTH], 0), w_out[l][A_WIDTH:]], axis=0)
        h1, hn2 = _out(oa, ob, oc, w_out_l.astype(BF16), h, ffn_norm[l])

        wr = jnp.zeros((ROUTER_LO, D), F32)
        wr = wr.at[ROUTER_GROUP_ROW:ROUTER_GROUP_ROW + N_GROUPS].set(w_router_group[l].T)
        wr = wr.at[ROUTER_EXPERT_ROW:ROUTER_EXPERT_ROW + N_EXPERTS].set(w_router_expert[l].T)
        wr_hi = wr.astype(BF16)
        wr_lo = (wr - wr_hi.astype(F32)).astype(BF16)
        br = jnp.zeros((ROUTER_LO, 1), F32)
        br = br.at[ROUTER_GROUP_ROW:ROUTER_GROUP_ROW + N_GROUPS, 0].set(b_router_group[l])
        br = br.at[ROUTER_EXPERT_ROW:ROUTER_EXPERT_ROW + N_EXPERTS, 0].set(b_router_expert[l])
        eidx_t, gates_t = _router(hn2, jnp.concatenate([wr_hi, wr_lo], axis=0), br)

        tok, g_sorted, pos_rows, seg_blk, seg_exp, seg_lo, seg_hi = _route(eidx_t, gates_t, T, tm_e)
        xs = _sc_gather_rows(hn2, tok)
        ys = _experts(seg_blk, seg_exp, seg_lo, seg_hi, xs, g_sorted.reshape(-1, 1),
                      w_expert_gate, w_expert_up, w_expert_down, l, tm=tm_e)
        y_rows = _sc_gather_rows(ys, pos_rows)
        h = h1

    (out,) = _norm(h, y_rows, final_norm, emit_h=False, emit_bf16=False)
    return out.reshape(B, S, D)
```

```python
import functools
import math

import jax
import jax.numpy as jnp
from jax import lax
from jax.experimental import pallas as pl
from jax.experimental.pallas import tpu as pltpu
from jax.experimental.pallas import tpu_sc as plsc
import numpy as np

F32 = jnp.float32
BF16 = jnp.bfloat16

D_MODEL = 2048
HEAD_DIM = 64
ROPE_THETA = 10000.0
A_HEADS = 12
A_KV_HEADS = 4
A_GROUP = 3
A_WIDTH = 768
WINDOW = 128
POOL_WINDOWS = (2, 4, 8, 16)
B_WIDTH = 512
B_GROUP_DIM = 128
C_VDIM = 128
C_WIDTH = 768
C_HEADS = 6
DIFF_EPS = 1e-5
IN_COLS = 4096
N_GROUPS = 4
EXPERTS_PER_GROUP = 8
N_EXPERTS = 32
TOP_K = 2
D_FF = 512
NORM_EPS = 1e-6
NEG = -1e30
LOG2E = math.log2(math.e)
Q_SCALE = HEAD_DIM ** -0.5 * LOG2E

COL_QA, COL_QC, COL_KA, COL_KC, COL_UB, COL_VA, COL_VC = 0, 768, 1536, 1792, 2560, 3072, 3328
PROJ_TN = 512
A_KV_WIDTH = A_KV_HEADS * HEAD_DIM
PROJ_SUB = 256
N_SCALED_TILES = COL_KA // PROJ_TN
N_ROPE_TILES = COL_UB // PROJ_TN
N_MAIN_TILES = COL_VA // PROJ_TN
DIFF_KB = 512
VT_ROWS = C_VDIM + 16

VMEM_LIMIT = 48 * 1024 * 1024
D_PACK = D_MODEL // 2
U32 = jnp.uint32


def _cparams(sem, vmem=VMEM_LIMIT):
    return pltpu.CompilerParams(dimension_semantics=sem, vmem_limit_bytes=vmem)


def _pack_rows(x):
    lo = pltpu.bitcast(x[:, :D_PACK].astype(BF16).astype(F32), U32) >> 16
    hi = pltpu.bitcast(x[:, D_PACK:].astype(BF16).astype(F32), U32)
    return hi | lo


def _unpack_rows(w):
    return pltpu.bitcast(w << 16, F32), pltpu.bitcast(w & U32(0xFFFF0000), F32)


def _unpack_rows_bf16(w):
    lo, hi = _unpack_rows(w)
    return jnp.concatenate([lo.astype(BF16), hi.astype(BF16)], axis=1)


SC_WINDOW = 32
SC_INDEX_LANES = 128


def _sc_gather_rows(x, idx):
    M = idx.shape[0]
    W = x.shape[1]
    idx2 = jnp.pad(idx.reshape(M // SC_WINDOW, SC_WINDOW), ((0, 0), (0, SC_INDEX_LANES - SC_WINDOW)))
    mesh = plsc.VectorSubcoreMesh(core_axis_name="core", subcore_axis_name="subcore")

    @pl.kernel(out_type=jax.ShapeDtypeStruct((M, W), x.dtype), mesh=mesh)
    def gather(x_hbm, i_hbm, o_hbm):
        def body(i_vmem, o_vmem):
            pltpu.sync_copy(x_hbm.at[i_vmem.at[0, pl.ds(0, SC_WINDOW)]], o_vmem)

        pltpu.emit_pipeline(
            body,
            grid=(M // SC_WINDOW,),
            in_specs=[pl.BlockSpec((1, SC_INDEX_LANES), index_map=lambda i: (i, 0))],
            out_specs=[pl.BlockSpec((SC_WINDOW, W), index_map=lambda i: (i, 0))],
            core_axis_name=("core", "subcore"),
            dimension_semantics=(pltpu.PARALLEL,),
        )(i_hbm, o_hbm)

    return gather(x, idx2)


def _norm_kernel(*refs, n_add, emit_h, emit_bf16, eps):
    h_ref = refs[0]
    add_refs = refs[1:1 + n_add]
    g_ref = refs[1 + n_add]
    outs = refs[2 + n_add:]
    h = h_ref[...]
    if n_add:
        lo = jnp.zeros((h.shape[0], D_PACK), F32)
        hi = jnp.zeros((h.shape[0], D_PACK), F32)
        for r in add_refs:
            a, b = _unpack_rows(r[...])
            lo, hi = lo + a, hi + b
        h = h + jnp.concatenate([lo, hi], axis=1)
    ms = jnp.mean(h * h, axis=-1, keepdims=True)
    y = h * lax.rsqrt(ms + eps) * g_ref[...]
    k = 0
    if emit_h:
        outs[k][...] = h
        k += 1
    outs[k][...] = y.astype(BF16 if emit_bf16 else F32)


def _norm(h, y_rows, g, *, emit_h, emit_bf16, tm=512):
    T, D = h.shape
    nt = T // tm
    row = pl.BlockSpec((tm, D), lambda i: (i, 0))
    n_add = 0 if y_rows is None else TOP_K
    add_specs = [pl.BlockSpec((tm, D_PACK), lambda i, k=k: (i + k * nt, 0)) for k in range(n_add)]
    out_shape, out_specs = [], []
    if emit_h:
        out_shape.append(jax.ShapeDtypeStruct((T, D), F32))
        out_specs.append(row)
    out_shape.append(jax.ShapeDtypeStruct((T, D), BF16 if emit_bf16 else F32))
    out_specs.append(row)
    return pl.pallas_call(
        functools.partial(_norm_kernel, n_add=n_add, emit_h=emit_h, emit_bf16=emit_bf16, eps=NORM_EPS),
        grid=(nt,),
        in_specs=[row] + add_specs + [pl.BlockSpec((1, D), lambda i: (0, 0))],
        out_specs=out_specs,
        out_shape=out_shape,
        compiler_params=_cparams(("parallel",)),
        name="norm",
    )(h, *([y_rows] * n_add), g.reshape(1, D))


def _proj_kernel(x_ref, w_ref, wt_ref, cos_ref, sin_ref, o_ref, vat_ref, vt_ref, *, tm):
    j = pl.program_id(1)
    subs = [slice(s * PROJ_SUB, (s + 1) * PROJ_SUB) for s in range(tm // PROJ_SUB)]

    def matmuls():
        return [jnp.dot(x_ref[rows, :], w_ref[...], preferred_element_type=F32) for rows in subs]

    @pl.when(j < N_ROPE_TILES)
    def _():
        accs = matmuls()
        scale = jnp.where(j < N_SCALED_TILES, Q_SCALE, 1.0).astype(F32)
        lane = lax.broadcasted_iota(jnp.int32, (PROJ_SUB, PROJ_TN), 1)
        first_half = (lane & (HEAD_DIM // 2)) == 0
        for rows, acc in zip(subs, accs):
            c = jnp.concatenate([cos_ref[rows, :] * scale] * (PROJ_TN // 128), axis=1)
            s = jnp.concatenate([sin_ref[rows, :] * scale] * (PROJ_TN // 128), axis=1)
            swapped = jnp.where(first_half,
                                pltpu.roll(acc, PROJ_TN - HEAD_DIM // 2, 1),
                                pltpu.roll(acc, HEAD_DIM // 2, 1))
            o_ref[rows, :] = (acc * c + swapped * s).astype(o_ref.dtype)

    @pl.when((j >= N_ROPE_TILES) & (j < N_MAIN_TILES))
    def _():
        for rows, acc in zip(subs, matmuls()):
            o_ref[rows, :] = acc.astype(o_ref.dtype)

    per_sub = PROJ_SUB // WINDOW
    per_kb = DIFF_KB // PROJ_SUB
    for t in range((IN_COLS - COL_VA) // PROJ_TN):
        @pl.when(j == N_MAIN_TILES + t)
        def _(t=t):
            accs = [lax.dot_general(wt_ref[...], x_ref[rows, :], (((1,), (1,)), ((), ())),
                                    preferred_element_type=F32) for rows in subs]
            for s, acc in enumerate(accs):
                cols = slice((s % per_kb) * PROJ_SUB, (s % per_kb + 1) * PROJ_SUB)
                for r0 in range(0, PROJ_TN, C_VDIM):
                    row = t * PROJ_TN + r0
                    if row == 0:
                        for b in range(per_sub):
                            vat_ref[s * per_sub + b] = acc[0:A_KV_WIDTH, b * WINDOW:(b + 1) * WINDOW].astype(
                                vat_ref.dtype)
                    elif row >= A_KV_WIDTH:
                        head = (row - A_KV_WIDTH) // C_VDIM
                        vt_ref[head, s // per_kb, 0:C_VDIM, cols] = acc[r0:r0 + C_VDIM, :].astype(vt_ref.dtype)

    @pl.when(j == N_MAIN_TILES)
    def _():
        row = lax.broadcasted_iota(jnp.int32, (VT_ROWS - C_VDIM, DIFF_KB), 0)
        extra = jnp.where(row == 0, 1.0, 0.0).astype(vt_ref.dtype)
        for head in range(C_HEADS):
            for kb in range(tm // DIFF_KB):
                vt_ref[head, kb, C_VDIM:VT_ROWS, :] = extra


def _proj(hn, w, wt, cos_t, sin_t, *, B, S, tm=2048):
    T, D = hn.shape
    spt = S // tm
    last_main = N_MAIN_TILES - 1
    return pl.pallas_call(
        functools.partial(_proj_kernel, tm=tm),
        grid=(T // tm, IN_COLS // PROJ_TN),
        in_specs=[
            pl.BlockSpec((tm, D), lambda i, j: (i, 0)),
            pl.BlockSpec((D, PROJ_TN), lambda i, j: (0, jnp.minimum(j, last_main))),
            pl.BlockSpec((PROJ_TN, D), lambda i, j: (jnp.maximum(j - N_MAIN_TILES, 0), 0)),
            pl.BlockSpec((tm, 128), lambda i, j: (i, 0)),
            pl.BlockSpec((tm, 128), lambda i, j: (i, 0)),
        ],
        out_specs=[
            pl.BlockSpec((tm, PROJ_TN), lambda i, j: (i, jnp.minimum(j, last_main))),
            pl.BlockSpec((tm // WINDOW, A_KV_WIDTH, WINDOW), lambda i, j: (i, 0, 0)),
            pl.BlockSpec((None, C_HEADS, tm // DIFF_KB, VT_ROWS, DIFF_KB),
                         lambda i, j: (i // spt, 0, i % spt, 0, 0)),
        ],
        out_shape=[
            jax.ShapeDtypeStruct((T, COL_VA), BF16),
            jax.ShapeDtypeStruct((T // WINDOW, A_KV_WIDTH, WINDOW), BF16),
            jax.ShapeDtypeStruct((B, C_HEADS, S // DIFF_KB, VT_ROWS, DIFF_KB), BF16),
        ],
        compiler_params=_cparams(("parallel", "arbitrary")),
        name="proj",
    )(hn, w, wt, cos_t, sin_t)


A_COLS = A_HEADS * WINDOW


def _swa_kernel(q_ref, kc_ref, kp_ref, vc_ref, vp_ref, sink_ref, g_ref, o_ref, *, tq):
    i = pl.program_id(1)
    nblk = tq // WINDOW
    kext = jnp.concatenate([kp_ref[...], kc_ref[...]], axis=0)
    lane_kv = lax.broadcasted_iota(jnp.int32, (WINDOW, A_KV_WIDTH), 1) // HEAD_DIM
    zero = jnp.zeros((WINDOW, A_KV_WIDTH), BF16)

    scores = []
    for b in range(nblk):
        parts = []
        for g in range(A_GROUP):
            qg = q_ref[b * WINDOW:(b + 1) * WINDOW, g * A_KV_WIDTH:(g + 1) * A_KV_WIDTH]
            parts += [jnp.where(lane_kv == j, qg, zero) for j in range(A_KV_HEADS)]
        qcat = jnp.concatenate(parts, axis=0)
        scores.append(lax.dot_general(kext[b * WINDOW:(b + 2) * WINDOW, :], qcat, (((1,), (1,)), ((), ())),
                                      preferred_element_type=F32))

    r = lax.broadcasted_iota(jnp.int32, (2 * WINDOW, A_COLS), 0)
    c = lax.broadcasted_iota(jnp.int32, (2 * WINDOW, A_COLS), 1) & (WINDOW - 1)
    band = (r > c) & (r <= c + WINDOW)
    sink = sink_ref[...]
    gain = g_ref[...]
    for b, st in enumerate(scores):
        mask = band
        if b == 0:
            mask = band & ((r >= WINDOW) | (i > 0))
        st = jnp.where(mask, st, NEG)
        m = jnp.maximum(jnp.max(st, axis=0, keepdims=True), sink)
        p = jnp.exp2(st - m)
        l = jnp.sum(p, axis=0, keepdims=True) + jnp.exp2(sink - m)
        vt = jnp.concatenate([vp_ref[0] if b == 0 else vc_ref[b - 1], vc_ref[b]], axis=1)
        pv = jnp.dot(vt, p.astype(BF16), preferred_element_type=F32)
        inv_l = 1.0 / l
        heads = []
        for g in range(A_GROUP):
            for j in range(A_KV_HEADS):
                cols = slice((g * A_KV_HEADS + j) * WINDOW, (g * A_KV_HEADS + j + 1) * WINDOW)
                heads.append(pv[j * HEAD_DIM:(j + 1) * HEAD_DIM, cols] * inv_l[:, cols])
        ot = jnp.concatenate(heads, axis=0)
        ms = jnp.mean(ot * ot, axis=0, keepdims=True)
        y = (ot * lax.rsqrt(ms + NORM_EPS)).T * gain
        o_ref[b * WINDOW:(b + 1) * WINDOW, :] = y.astype(o_ref.dtype)


def _swa(proj, vat, sink_row, g, *, B, S, tq=512):
    nq = S // tq
    rpb = tq // WINDOW

    def prev_blk(b, i):
        return jnp.maximum(b * (S // WINDOW) + i * rpb - 1, 0)

    return pl.pallas_call(
        functools.partial(_swa_kernel, tq=tq),
        grid=(B, nq),
        in_specs=[
            pl.BlockSpec((tq, A_WIDTH), lambda b, i: (b * nq + i, COL_QA // A_WIDTH)),
            pl.BlockSpec((tq, A_KV_WIDTH), lambda b, i: (b * nq + i, COL_KA // A_KV_WIDTH)),
            pl.BlockSpec((WINDOW, A_KV_WIDTH), lambda b, i: (prev_blk(b, i), COL_KA // A_KV_WIDTH)),
            pl.BlockSpec((rpb, A_KV_WIDTH, WINDOW), lambda b, i: (b * nq + i, 0, 0)),
            pl.BlockSpec((1, A_KV_WIDTH, WINDOW), lambda b, i: (prev_blk(b, i), 0, 0)),
            pl.BlockSpec((1, A_COLS), lambda b, i: (0, 0)),
            pl.BlockSpec((1, A_WIDTH), lambda b, i: (0, 0)),
        ],
        out_specs=pl.BlockSpec((tq, A_WIDTH), lambda b, i: (b * nq + i, 0)),
        out_shape=jax.ShapeDtypeStruct((B * S, A_WIDTH), BF16),
        compiler_params=_cparams(("parallel", "parallel")),
        name="swa",
    )(proj, proj, proj, vat, vat, sink_row, g.reshape(1, A_WIDTH))


POOL_HALO = 16


def _pool_kernel(u0_ref, u1_ref, h0_ref, h1_ref, w_ref, s_ref, o_ref, *, tq):
    i = pl.program_id(1)
    u = jnp.concatenate([u0_ref[...], u1_ref[...]], axis=1).astype(F32)
    halo = jnp.concatenate([h0_ref[...], h1_ref[...]], axis=1).astype(F32)
    halo = halo * (i > 0).astype(F32)
    ext = jnp.concatenate([halo, u], axis=0)
    t = i * tq + lax.broadcasted_iota(jnp.int32, (tq, 1), 0)
    for gi, w in enumerate(POOL_WINDOWS):
        sl = slice(gi * B_GROUP_DIM, (gi + 1) * B_GROUP_DIM)
        s = ext[:, sl]
        shift = 1
        while shift < w:
            s = s + pltpu.roll(s, shift, 0)
            shift *= 2
        cnt = jnp.minimum(t + 1, w).astype(F32)
        y = (s[POOL_HALO:, :] / cnt - u[:, sl]).astype(BF16)
        o = jnp.dot(y, w_ref[gi], preferred_element_type=F32) * s_ref[:, sl]
        o_ref[:, sl] = o.astype(o_ref.dtype)


def _pool(proj, w_pool, scale, *, B, S, tq=512):
    nq = S // tq
    c0 = COL_UB // 256

    def cur(c):
        return pl.BlockSpec((tq, 256), lambda b, i: (b * nq + i, c))

    def halo(c):
        return pl.BlockSpec(
            (POOL_HALO, 256),
            lambda b, i: (jnp.maximum((b * S + i * tq) // POOL_HALO - 1, 0), c))

    return pl.pallas_call(
        functools.partial(_pool_kernel, tq=tq),
        grid=(B, nq),
        in_specs=[cur(c0), cur(c0 + 1), halo(c0), halo(c0 + 1),
                  pl.BlockSpec((len(POOL_WINDOWS), B_GROUP_DIM, B_GROUP_DIM), lambda b, i: (0, 0, 0)),
                  pl.BlockSpec((1, B_WIDTH), lambda b, i: (0, 0))],
        out_specs=pl.BlockSpec((tq, B_WIDTH), lambda b, i: (b * nq + i, 0)),
        out_shape=jax.ShapeDtypeStruct((B * S, B_WIDTH), BF16),
        compiler_params=_cparams(("parallel", "parallel")),
        name="pool",
    )(proj, proj, proj, proj, w_pool, scale.reshape(1, B_WIDTH))


DIFF_UNROLL = 4


def _diff_kernel(q_ref, k_ref, vt_ref, lam_ref, g_ref, o_ref, qcat_ref, s_ref, bm_ref, m_ref, acc_ref,
                 *, tq, lam_init):
    qi = pl.program_id(2)
    n_diag = tq // DIFF_KB
    n_below = qi * n_diag
    q = q_ref[...]
    lane = lax.broadcasted_iota(jnp.int32, q.shape, 1)
    zero = jnp.zeros_like(q)
    qcat_ref[0:tq, :] = jnp.where(lane < HEAD_DIM, q, zero)
    qcat_ref[tq:2 * tq, :] = jnp.where(lane >= HEAD_DIM, q, zero)
    m_ref[...] = jnp.full(m_ref.shape, NEG, F32)
    acc_ref[...] = jnp.zeros(acc_ref.shape, F32)

    def produce(kidx, q0=0):
        off = pl.multiple_of(kidx * DIFF_KB, DIFF_KB)
        queries = qcat_ref[...] if q0 == 0 else jnp.concatenate(
            [qcat_ref[q0:tq, :], qcat_ref[tq + q0:2 * tq, :]], axis=0)
        st = lax.dot_general(k_ref[pl.ds(off, DIFF_KB), :], queries, (((1,), (1,)), ((), ())),
                             preferred_element_type=F32)
        return st, jnp.max(st, axis=0, keepdims=True)

    def store(buf, block):
        width = block[0].shape[1]
        s_ref[buf, :, 0:width], bm_ref[buf, :, 0:width] = block

    def consume(buf, kidx, first_key):
        q0 = first_key or 0
        nq_act = tq - q0
        st = s_ref[buf, :, 0:2 * nq_act]
        if first_key is not None:
            r = lax.broadcasted_iota(jnp.int32, st.shape, 0)
            c = lax.broadcasted_iota(jnp.int32, st.shape, 1)
            c = jnp.where(c >= nq_act, c - nq_act, c)
            st = jnp.where(r <= c, st, NEG)
            block_max = jnp.max(st, axis=0, keepdims=True)
        else:
            block_max = bm_ref[buf]
        maps = (slice(q0, tq), slice(tq + q0, 2 * tq))
        m = m_ref[...] if q0 == 0 else jnp.concatenate([m_ref[:, lanes] for lanes in maps], axis=1)
        m_new = jnp.maximum(m, block_max)
        p = jnp.exp2(st - m_new)
        pv = jnp.dot(vt_ref[kidx], p.astype(BF16), preferred_element_type=F32)
        alpha = jnp.exp2(m - m_new)
        if q0 == 0:
            m_ref[...] = m_new
            acc_ref[...] = alpha * acc_ref[...] + pv
        else:
            for c, lanes in enumerate(maps):
                part = slice(c * nq_act, (c + 1) * nq_act)
                m_ref[:, lanes] = m_new[:, part]
                acc_ref[:, lanes] = alpha[:, part] * acc_ref[:, lanes] + pv[:, part]

    def steps(first, n_plain, n_masked, last):
        n = n_plain + n_masked
        first_keys = [None] * n_plain + [u * DIFF_KB for u in range(n_masked)]
        for u in range(n):
            more = not (last and u == n - 1)
            if more:
                nxt = produce(first + u + 1, (first_keys[u + 1] or 0) if u + 1 < n else 0)
            consume(u % 2, first + u, first_keys[u])
            if more:
                store(1 - u % 2, nxt)

    store(0, produce(0))

    def body(c, carry):
        steps(c * DIFF_UNROLL, DIFF_UNROLL, 0, False)
        return carry

    lax.fori_loop(0, n_below // DIFF_UNROLL, body, 0)
    for rem in range(0, DIFF_UNROLL, math.gcd(n_diag, DIFF_UNROLL)):
        @pl.when(n_below % DIFF_UNROLL == rem)
        def _(rem=rem):
            steps(n_below - rem, rem, n_diag, True)

    lv = lam_ref[...]
    lam = (jnp.exp(jnp.sum(lv[0:1] * lv[1:2], axis=1, keepdims=True))
           - jnp.exp(jnp.sum(lv[2:3] * lv[3:4], axis=1, keepdims=True)) + lam_init)
    l = acc_ref[C_VDIM:C_VDIM + 1, :]
    acc = acc_ref[0:C_VDIM, :]
    ot = acc[:, :tq] / l[:, :tq] - lam * (acc[:, tq:] / l[:, tq:])
    ms = jnp.mean(ot * ot, axis=0, keepdims=True)
    yt = ot * lax.rsqrt(ms + DIFF_EPS) * (g_ref[...] * (1.0 - lam_init))
    o_ref[...] = yt.T.astype(o_ref.dtype)


def _diff(proj, vt, lam_vecs, subln, *, B, S, lam_init, tq=2048):
    nq = S // tq
    nkb = S // DIFF_KB
    assert tq % DIFF_KB == 0 and DIFF_UNROLL % 2 == 0
    return pl.pallas_call(
        functools.partial(_diff_kernel, tq=tq, lam_init=lam_init),
        grid=(B, C_HEADS, nq),
        in_specs=[
            pl.BlockSpec((tq, C_VDIM), lambda b, h, i: (b * nq + i, COL_QC // C_VDIM + h)),
            pl.BlockSpec((S, C_VDIM), lambda b, h, i: (b, COL_KC // C_VDIM + h)),
            pl.BlockSpec((None, None, nkb, VT_ROWS, DIFF_KB), lambda b, h, i: (b, h, 0, 0, 0)),
            pl.BlockSpec((4, HEAD_DIM), lambda b, h, i: (0, 0)),
            pl.BlockSpec((C_VDIM, 1), lambda b, h, i: (0, 0)),
        ],
        out_specs=pl.BlockSpec((tq, C_VDIM), lambda b, h, i: (b * nq + i, h)),
        out_shape=jax.ShapeDtypeStruct((B * S, C_WIDTH), BF16),
        scratch_shapes=[pltpu.VMEM((2 * tq, C_VDIM), BF16),
                        pltpu.VMEM((2, DIFF_KB, 2 * tq), F32),
                        pltpu.VMEM((2, 1, 2 * tq), F32),
                        pltpu.VMEM((1, 2 * tq), F32),
                        pltpu.VMEM((VT_ROWS, 2 * tq), F32)],
        compiler_params=_cparams(("parallel", "parallel", "arbitrary")),
        name="diff",
    )(proj, proj, vt, lam_vecs, subln.reshape(C_VDIM, 1))


OUT_SUB = 256


def _out_kernel(oa_ref, ob_ref, oc_ref, w_ref, h_ref, g_ref, h1_ref, hn_ref, *, tm):
    subs = [slice(s * OUT_SUB, (s + 1) * OUT_SUB) for s in range(tm // OUT_SUB)]
    accs = []
    for rows in subs:
        acc = jnp.dot(oa_ref[rows, :], w_ref[0:A_WIDTH, :], preferred_element_type=F32)
        acc = acc + jnp.dot(ob_ref[rows, :], w_ref[A_WIDTH:A_WIDTH + B_WIDTH, :], preferred_element_type=F32)
        acc = acc + jnp.dot(oc_ref[rows, :], w_ref[A_WIDTH + B_WIDTH:, :], preferred_element_type=F32)
        accs.append(acc)
    for rows, acc in zip(subs, accs):
        h1 = h_ref[rows, :] + acc
        h1_ref[rows, :] = h1
        ms = jnp.mean(h1 * h1, axis=1, keepdims=True)
        hn = h1 * lax.rsqrt(ms + NORM_EPS) * g_ref[...]
        hn_ref[rows, :] = _pack_rows(hn)


def _out(oa, ob, oc, w, h, g, *, tm=512):
    T, D = h.shape
    return pl.pallas_call(
        functools.partial(_out_kernel, tm=tm),
        grid=(T // tm,),
        in_specs=[
            pl.BlockSpec((tm, A_WIDTH), lambda i: (i, 0)),
            pl.BlockSpec((tm, B_WIDTH), lambda i: (i, 0)),
            pl.BlockSpec((tm, C_WIDTH), lambda i: (i, 0)),
            pl.BlockSpec((D, D), lambda i: (0, 0)),
            pl.BlockSpec((tm, D), lambda i: (i, 0)),
            pl.BlockSpec((1, D), lambda i: (0, 0)),
        ],
        out_specs=[
            pl.BlockSpec((tm, D), lambda i: (i, 0)),
            pl.BlockSpec((tm, D_PACK), lambda i: (i, 0)),
        ],
        out_shape=[
            jax.ShapeDtypeStruct((T, D), F32),
            jax.ShapeDtypeStruct((T, D_PACK), U32),
        ],
        compiler_params=_cparams(("parallel",)),
        name="out",
    )(oa, ob, oc, w, h, g.reshape(1, D))


ROUTER_ROWS = 128
ROUTER_LO = 64
ROUTER_GROUP_ROW = 0
ROUTER_EXPERT_ROW = 8


def _router_kernel(x_ref, wt_ref, b_ref, e_ref, g_ref):
    x = _unpack_rows_bf16(x_ref[...])
    lg = lax.dot_general(wt_ref[...], x, (((1,), (1,)), ((), ())), preferred_element_type=F32)
    lg = lg[0:ROUTER_LO] + lg[ROUTER_LO:ROUTER_ROWS] + b_ref[...]
    grp = lg[ROUTER_GROUP_ROW:ROUTER_GROUP_ROW + N_GROUPS]
    ex = lg[ROUTER_EXPERT_ROW:ROUTER_EXPERT_ROW + N_EXPERTS]

    mg = jnp.max(grp, axis=0, keepdims=True)
    pg_sel = 1.0 / jnp.sum(jnp.exp(grp - mg), axis=0, keepdims=True)
    gi = lax.broadcasted_iota(jnp.int32, grp.shape, 0)
    g_sel = jnp.min(jnp.where(grp == mg, gi, N_GROUPS), axis=0, keepdims=True)

    er = lax.broadcasted_iota(jnp.int32, ex.shape, 0)
    group_of = lax.shift_right_logical(er, EXPERTS_PER_GROUP.bit_length() - 1)
    cand = jnp.where(group_of == g_sel, ex, NEG)
    v1 = jnp.max(cand, axis=0, keepdims=True)
    i1 = jnp.min(jnp.where(cand == v1, er, N_EXPERTS), axis=0, keepdims=True)
    rest = jnp.where(er == i1, NEG, cand)
    v2 = jnp.max(rest, axis=0, keepdims=True)
    i2 = jnp.min(jnp.where(rest == v2, er, N_EXPERTS), axis=0, keepdims=True)
    t = jnp.exp(v2 - v1)
    g1 = pg_sel / (1.0 + t)
    e_ref[...] = jnp.concatenate([i1, i2], axis=0)
    g_ref[...] = jnp.concatenate([g1, g1 * t], axis=0)


def _router(hn_packed, wt, b, *, tm=1024):
    T = hn_packed.shape[0]
    D = wt.shape[1]
    return pl.pallas_call(
        _router_kernel,
        grid=(T // tm,),
        in_specs=[
            pl.BlockSpec((tm, D_PACK), lambda i: (i, 0)),
            pl.BlockSpec((ROUTER_ROWS, D), lambda i: (0, 0)),
            pl.BlockSpec((ROUTER_LO, 1), lambda i: (0, 0)),
        ],
        out_specs=[pl.BlockSpec((TOP_K, tm), lambda i: (0, i)), pl.BlockSpec((TOP_K, tm), lambda i: (0, i))],
        out_shape=[jax.ShapeDtypeStruct((TOP_K, T), jnp.int32), jax.ShapeDtypeStruct((TOP_K, T), F32)],
        compiler_params=_cparams(("parallel",)),
        name="router",
    )(hn_packed, wt, b)


EXPERT_SUB = 256
EXPERT_VMEM = 56 * 1024 * 1024


def _expert_kernel(blk_ref, exp_ref, lo_ref, hi_ref, x_ref, g_ref, wg_ref, wu_ref, wd_ref, o_ref,
                   wg_bf, wu_bf, wd_bf, *, tm):
    p = pl.program_id(0)
    prev = jnp.maximum(p - 1, 0)
    blk = blk_ref[p]
    first = (p == 0) | (blk != blk_ref[prev])
    new_expert = (p == 0) | (exp_ref[p] != exp_ref[prev])
    lo = lo_ref[p]
    hi = hi_ref[p]

    @pl.when(new_expert)
    def _():
        wg_bf[...] = wg_ref[...].astype(BF16)
        wu_bf[...] = wu_ref[...].astype(BF16)
        wd_bf[...] = wd_ref[...].astype(BF16)

    @pl.when(first)
    def _():
        o_ref[...] = jnp.zeros_like(o_ref)

    for s in range(tm // EXPERT_SUB):
        row0 = blk * tm + s * EXPERT_SUB
        rows = slice(s * EXPERT_SUB, (s + 1) * EXPERT_SUB)

        @pl.when((hi > row0) & (lo < row0 + EXPERT_SUB))
        def _(row0=row0, rows=rows):
            x = _unpack_rows_bf16(x_ref[rows, :])
            a = jnp.dot(x, wg_bf[...], preferred_element_type=F32)
            u = jnp.dot(x, wu_bf[...], preferred_element_type=F32)
            act = (a / (1.0 + jnp.exp(-a))) * u
            y = jnp.dot(act.astype(BF16), wd_bf[...], preferred_element_type=F32)
            r = row0 + lax.broadcasted_iota(jnp.int32, (EXPERT_SUB, 1), 0)
            mine = (r >= lo) & (r < hi)
            o_ref[rows, :] = jnp.where(mine, _pack_rows(y * g_ref[rows, :]), o_ref[rows, :])


def _experts(seg_blk, seg_exp, seg_lo, seg_hi, xs, gates, wg, wu, wd, layer, *, tm):
    N = xs.shape[0]
    D = wg.shape[2]
    P = seg_blk.shape[0]
    grid_spec = pltpu.PrefetchScalarGridSpec(
        num_scalar_prefetch=4,
        grid=(P,),
        in_specs=[
            pl.BlockSpec((tm, D_PACK), lambda p, blk, ex, lo, hi: (blk[p], 0)),
            pl.BlockSpec((tm, 1), lambda p, blk, ex, lo, hi: (blk[p], 0)),
            pl.BlockSpec((None, None, D, D_FF), lambda p, blk, ex, lo, hi: (layer, ex[p], 0, 0)),
            pl.BlockSpec((None, None, D, D_FF), lambda p, blk, ex, lo, hi: (layer, ex[p], 0, 0)),
            pl.BlockSpec((None, None, D_FF, D), lambda p, blk, ex, lo, hi: (layer, ex[p], 0, 0)),
        ],
        out_specs=pl.BlockSpec((tm, D_PACK), lambda p, blk, ex, lo, hi: (blk[p], 0)),
        scratch_shapes=[pltpu.VMEM((D, D_FF), BF16), pltpu.VMEM((D, D_FF), BF16), pltpu.VMEM((D_FF, D), BF16)],
    )
    return pl.pallas_call(
        functools.partial(_expert_kernel, tm=tm),
        grid_spec=grid_spec,
        out_shape=jax.ShapeDtypeStruct((N, D_PACK), U32),
        compiler_params=_cparams(("arbitrary",), EXPERT_VMEM),
        name="experts",
    )(seg_blk, seg_exp, seg_lo, seg_hi, xs, gates, wg, wu, wd)


def _route(eidx_t, gates_t, T, tm):
    N = T * TOP_K
    flat_e = eidx_t.T.reshape(N)
    iota = jnp.arange(N, dtype=jnp.int32)
    sorted_e, order, g_sorted = lax.sort((flat_e, iota, gates_t.T.reshape(N)), num_keys=1, is_stable=True)
    tok = order // TOP_K
    _, pos = lax.sort((order, iota), num_keys=1)
    counts = jnp.bincount(flat_e, length=N_EXPERTS)
    starts = (jnp.cumsum(counts) - counts).astype(jnp.int32)
    nb = N // tm
    bounds = jnp.sort(jnp.concatenate([jnp.arange(nb, dtype=jnp.int32) * tm, starts]))
    ends = jnp.concatenate([bounds[1:], jnp.full((1,), N, bounds.dtype)])
    seg_blk = jnp.minimum(bounds // tm, nb - 1).astype(jnp.int32)
    seg_exp = sorted_e[jnp.minimum(bounds, N - 1)].astype(jnp.int32)
    pos_rows = pos.reshape(T, TOP_K).T.reshape(N)
    return tok, g_sorted, pos_rows, seg_blk, seg_exp, bounds, ends


def _group_major(a, axis):
    shape = a.shape
    a = a.reshape(shape[:axis] + (A_KV_HEADS, A_GROUP, HEAD_DIM) + shape[axis + 1:])
    return jnp.swapaxes(a, axis, axis + 1).reshape(shape)


def _permute_in_proj(w):
    bounds = np.cumsum([0, 768, 256, 256, 512, 768, 768, 768])
    qa, ka, va, ub, qc, kc, vc = [w[:, bounds[s]:bounds[s + 1]] for s in range(7)]
    return jnp.concatenate([_group_major(qa, 1), qc, ka, kc, ub, va, vc], axis=1)


def kernel(x, positions, attn_norm, w_in, sinks, branch_norm_a, w_pool, pool_scale, lambda_q1, lambda_k1,
           lambda_q2, lambda_k2, subln, w_out, ffn_norm, w_router_group, b_router_group, w_router_expert,
           b_router_expert, w_expert_gate, w_expert_up, w_expert_down, final_norm):
    B, S, D = x.shape
    T = B * S
    depth = w_in.shape[0]
    tm_e = 512

    half = HEAD_DIM // 2
    inv = ROPE_THETA ** (-jnp.arange(half, dtype=F32) / half)
    ang = positions.astype(F32).reshape(T, 1) * inv
    cos, sin = jnp.cos(ang), jnp.sin(ang)
    cos_t = jnp.concatenate([cos, cos, cos, cos], axis=1)
    sin_t = jnp.concatenate([-sin, sin, -sin, sin], axis=1)

    h = x.reshape(T, D)
    y_rows = None
    for l in range(depth):
        if l == 0:
            (hn,) = _norm(h, None, attn_norm[l], emit_h=False, emit_bf16=True)
        else:
            h, hn = _norm(h, y_rows, attn_norm[l], emit_h=True, emit_bf16=True)
        w_in_l = _permute_in_proj(w_in[l]).astype(BF16)
        proj, vat, vt = _proj(hn, w_in_l[:, :COL_VA], w_in_l[:, COL_VA:].T, cos_t, sin_t, B=B, S=S)

        sink_heads = sinks[l].reshape(A_KV_HEADS, A_GROUP).T.reshape(A_HEADS) * LOG2E
        sink_row = jnp.repeat(sink_heads, WINDOW).reshape(1, A_COLS)
        oa = _swa(proj, vat, sink_row, _group_major(branch_norm_a[l], 0), B=B, S=S)
        ob = _pool(proj, w_pool[l].astype(BF16), pool_scale[l], B=B, S=S)

        lam_init = 0.8 - 0.6 * math.exp(-0.3 * l)
        lam_vecs = jnp.stack([lambda_q1[l], lambda_k1[l], lambda_q2[l], lambda_k2[l]]).astype(F32)
        oc = _diff(proj, vt, lam_vecs, subln[l], B=B, S=S, lam_init=lam_init)

        w_out_l = jnp.concatenate([_group_major(w_out[l][:A_WIDTH], 0), w_out[l][A_WIDTH:]], axis=0)
        h1, hn2 = _out(oa, ob, oc, w_out_l.astype(BF16), h, ffn_norm[l])

        wr = jnp.zeros((ROUTER_LO, D), F32)
        wr = wr.at[ROUTER_GROUP_ROW:ROUTER_GROUP_ROW + N_GROUPS].set(w_router_group[l].T)
        wr = wr.at[ROUTER_EXPERT_ROW:ROUTER_EXPERT_ROW + N_EXPERTS].set(w_router_expert[l].T)
        wr_hi = wr.astype(BF16)
        wr_lo = (wr - wr_hi.astype(F32)).astype(BF16)
        br = jnp.zeros((ROUTER_LO, 1), F32)
        br = br.at[ROUTER_GROUP_ROW:ROUTER_GROUP_ROW + N_GROUPS, 0].set(b_router_group[l])
        br = br.at[ROUTER_EXPERT_ROW:ROUTER_EXPERT_ROW + N_EXPERTS, 0].set(b_router_expert[l])
        eidx_t, gates_t = _router(hn2, jnp.concatenate([wr_hi, wr_lo], axis=0), br)

        tok, g_sorted, pos_rows, seg_blk, seg_exp, seg_lo, seg_hi = _route(eidx_t, gates_t, T, tm_e)
        xs = _sc_gather_rows(hn2, tok)
        ys = _experts(seg_blk, seg_exp, seg_lo, seg_hi, xs, g_sorted.reshape(-1, 1),
                      w_expert_gate, w_expert_up, w_expert_down, l, tm=tm_e)
        y_rows = _sc_gather_rows(ys, pos_rows)
        h = h1

    (out,) = _norm(h, y_rows, final_norm, emit_h=False, emit_bf16=False)
    return out.reshape(B, S, D)
```

```python
import functools
import math

import jax
import jax.numpy as jnp
from jax import lax
from jax.experimental import pallas as pl
from jax.experimental.pallas import tpu as pltpu
from jax.experimental.pallas import tpu_sc as plsc
import numpy as np

F32 = jnp.float32
BF16 = jnp.bfloat16

D_MODEL = 2048
HEAD_DIM = 64
ROPE_THETA = 10000.0
A_HEADS = 12
A_KV_HEADS = 4
A_GROUP = 3
A_WIDTH = 768
WINDOW = 128
POOL_WINDOWS = (2, 4, 8, 16)
B_WIDTH = 512
B_GROUP_DIM = 128
C_VDIM = 128
C_WIDTH = 768
C_HEADS = 6
DIFF_EPS = 1e-5
IN_COLS = 4096
N_GROUPS = 4
EXPERTS_PER_GROUP = 8
N_EXPERTS = 32
TOP_K = 2
D_FF = 512
NORM_EPS = 1e-6
NEG = -1e30
LOG2E = math.log2(math.e)
Q_SCALE = HEAD_DIM ** -0.5 * LOG2E

COL_QA, COL_QC, COL_KA, COL_KC, COL_UB, COL_VA, COL_VC = 0, 768, 1536, 1792, 2560, 3072, 3328
PROJ_TN = 512
A_KV_WIDTH = A_KV_HEADS * HEAD_DIM
PROJ_SUB = 256
N_SCALED_TILES = COL_KA // PROJ_TN
N_ROPE_TILES = COL_UB // PROJ_TN
N_MAIN_TILES = COL_VA // PROJ_TN
DIFF_KB = 512
VT_ROWS = C_VDIM + 16

VMEM_LIMIT = 48 * 1024 * 1024
D_PACK = D_MODEL // 2
U32 = jnp.uint32


def _cparams(sem, vmem=VMEM_LIMIT):
    return pltpu.CompilerParams(dimension_semantics=sem, vmem_limit_bytes=vmem)


def _pack_rows(x):
    lo = pltpu.bitcast(x[:, :D_PACK].astype(BF16).astype(F32), U32) >> 16
    hi = pltpu.bitcast(x[:, D_PACK:].astype(BF16).astype(F32), U32)
    return hi | lo


def _unpack_rows(w):
    return pltpu.bitcast(w << 16, F32), pltpu.bitcast(w & U32(0xFFFF0000), F32)


def _unpack_rows_bf16(w):
    lo, hi = _unpack_rows(w)
    return jnp.concatenate([lo.astype(BF16), hi.astype(BF16)], axis=1)


SC_WINDOW = 32
SC_INDEX_LANES = 128


def _sc_gather_rows(x, idx):
    M = idx.shape[0]
    W = x.shape[1]
    idx2 = jnp.pad(idx.reshape(M // SC_WINDOW, SC_WINDOW), ((0, 0), (0, SC_INDEX_LANES - SC_WINDOW)))
    mesh = plsc.VectorSubcoreMesh(core_axis_name="core", subcore_axis_name="subcore")

    @pl.kernel(out_type=jax.ShapeDtypeStruct((M, W), x.dtype), mesh=mesh)
    def gather(x_hbm, i_hbm, o_hbm):
        def body(i_vmem, o_vmem):
            pltpu.sync_copy(x_hbm.at[i_vmem.at[0, pl.ds(0, SC_WINDOW)]], o_vmem)

        pltpu.emit_pipeline(
            body,
            grid=(M // SC_WINDOW,),
            in_specs=[pl.BlockSpec((1, SC_INDEX_LANES), index_map=lambda i: (i, 0))],
            out_specs=[pl.BlockSpec((SC_WINDOW, W), index_map=lambda i: (i, 0))],
            core_axis_name=("core", "subcore"),
            dimension_semantics=(pltpu.PARALLEL,),
        )(i_hbm, o_hbm)

    return gather(x, idx2)


def _norm_kernel(*refs, n_add, emit_h, emit_bf16, eps):
    h_ref = refs[0]
    add_refs = refs[1:1 + n_add]
    g_ref = refs[1 + n_add]
    outs = refs[2 + n_add:]
    h = h_ref[...]
    if n_add:
        lo = jnp.zeros((h.shape[0], D_PACK), F32)
        hi = jnp.zeros((h.shape[0], D_PACK), F32)
        for r in add_refs:
            a, b = _unpack_rows(r[...])
            lo, hi = lo + a, hi + b
        h = h + jnp.concatenate([lo, hi], axis=1)
    ms = jnp.mean(h * h, axis=-1, keepdims=True)
    y = h * lax.rsqrt(ms + eps) * g_ref[...]
    k = 0
    if emit_h:
        outs[k][...] = h
        k += 1
    outs[k][...] = y.astype(BF16 if emit_bf16 else F32)


def _norm(h, y_rows, g, *, emit_h, emit_bf16, tm=512):
    T, D = h.shape
    nt = T // tm
    row = pl.BlockSpec((tm, D), lambda i: (i, 0))
    n_add = 0 if y_rows is None else TOP_K
    add_specs = [pl.BlockSpec((tm, D_PACK), lambda i, k=k: (i + k * nt, 0)) for k in range(n_add)]
    out_shape, out_specs = [], []
    if emit_h:
        out_shape.append(jax.ShapeDtypeStruct((T, D), F32))
        out_specs.append(row)
    out_shape.append(jax.ShapeDtypeStruct((T, D), BF16 if emit_bf16 else F32))
    out_specs.append(row)
    return pl.pallas_call(
        functools.partial(_norm_kernel, n_add=n_add, emit_h=emit_h, emit_bf16=emit_bf16, eps=NORM_EPS),
        grid=(nt,),
        in_specs=[row] + add_specs + [pl.BlockSpec((1, D), lambda i: (0, 0))],
        out_specs=out_specs,
        out_shape=out_shape,
        compiler_params=_cparams(("parallel",)),
        name="norm",
    )(h, *([y_rows] * n_add), g.reshape(1, D))


def _proj_kernel(x_ref, w_ref, wt_ref, cos_ref, sin_ref, o_ref, vat_ref, vt_ref, *, tm):
    j = pl.program_id(1)
    subs = [slice(s * PROJ_SUB, (s + 1) * PROJ_SUB) for s in range(tm // PROJ_SUB)]

    def matmuls():
        return [jnp.dot(x_ref[rows, :], w_ref[...], preferred_element_type=F32) for rows in subs]

    @pl.when(j < N_ROPE_TILES)
    def _():
        accs = matmuls()
        scale = jnp.where(j < N_SCALED_TILES, Q_SCALE, 1.0).astype(F32)
        lane = lax.broadcasted_iota(jnp.int32, (PROJ_SUB, PROJ_TN), 1)
        first_half = (lane & (HEAD_DIM // 2)) == 0
        for rows, acc in zip(subs, accs):
            c = jnp.concatenate([cos_ref[rows, :] * scale] * (PROJ_TN // 128), axis=1)
            s = jnp.concatenate([sin_ref[rows, :] * scale] * (PROJ_TN // 128), axis=1)
            swapped = jnp.where(first_half,
                                pltpu.roll(acc, PROJ_TN - HEAD_DIM // 2, 1),
                                pltpu.roll(acc, HEAD_DIM // 2, 1))
            o_ref[rows, :] = (acc * c + swapped * s).astype(o_ref.dtype)

    @pl.when((j >= N_ROPE_TILES) & (j < N_MAIN_TILES))
    def _():
        for rows, acc in zip(subs, matmuls()):
            o_ref[rows, :] = acc.astype(o_ref.dtype)

    per_sub = PROJ_SUB // WINDOW
    per_kb = DIFF_KB // PROJ_SUB
    for t in range((IN_COLS - COL_VA) // PROJ_TN):
        @pl.when(j == N_MAIN_TILES + t)
        def _(t=t):
            accs = [lax.dot_general(wt_ref[...], x_ref[rows, :], (((1,), (1,)), ((), ())),
                                    preferred_element_type=F32) for rows in subs]
            for s, acc in enumerate(accs):
                cols = slice((s % per_kb) * PROJ_SUB, (s % per_kb + 1) * PROJ_SUB)
                for r0 in range(0, PROJ_TN, C_VDIM):
                    row = t * PROJ_TN + r0
                    if row == 0:
                        for b in range(per_sub):
                            vat_ref[s * per_sub + b] = acc[0:A_KV_WIDTH, b * WINDOW:(b + 1) * WINDOW].astype(
                                vat_ref.dtype)
                    elif row >= A_KV_WIDTH:
                        head = (row - A_KV_WIDTH) // C_VDIM
                        vt_ref[head, s // per_kb, 0:C_VDIM, cols] = acc[r0:r0 + C_VDIM, :].astype(vt_ref.dtype)

    @pl.when(j == N_MAIN_TILES)
    def _():
        row = lax.broadcasted_iota(jnp.int32, (VT_ROWS - C_VDIM, DIFF_KB), 0)
        extra = jnp.where(row == 0, 1.0, 0.0).astype(vt_ref.dtype)
        for head in range(C_HEADS):
            for kb in range(tm // DIFF_KB):
                vt_ref[head, kb, C_VDIM:VT_ROWS, :] = extra


def _proj(hn, w, wt, cos_t, sin_t, *, B, S, tm=2048):
    T, D = hn.shape
    spt = S // tm
    last_main = N_MAIN_TILES - 1
    return pl.pallas_call(
        functools.partial(_proj_kernel, tm=tm),
        grid=(T // tm, IN_COLS // PROJ_TN),
        in_specs=[
            pl.BlockSpec((tm, D), lambda i, j: (i, 0)),
            pl.BlockSpec((D, PROJ_TN), lambda i, j: (0, jnp.minimum(j, last_main))),
            pl.BlockSpec((PROJ_TN, D), lambda i, j: (jnp.maximum(j - N_MAIN_TILES, 0), 0)),
            pl.BlockSpec((tm, 128), lambda i, j: (i, 0)),
            pl.BlockSpec((tm, 128), lambda i, j: (i, 0)),
        ],
        out_specs=[
            pl.BlockSpec((tm, PROJ_TN), lambda i, j: (i, jnp.minimum(j, last_main))),
            pl.BlockSpec((tm // WINDOW, A_KV_WIDTH, WINDOW), lambda i, j: (i, 0, 0)),
            pl.BlockSpec((None, C_HEADS, tm // DIFF_KB, VT_ROWS, DIFF_KB),
                         lambda i, j: (i // spt, 0, i % spt, 0, 0)),
        ],
        out_shape=[
            jax.ShapeDtypeStruct((T, COL_VA), BF16),
            jax.ShapeDtypeStruct((T // WINDOW, A_KV_WIDTH, WINDOW), BF16),
            jax.ShapeDtypeStruct((B, C_HEADS, S // DIFF_KB, VT_ROWS, DIFF_KB), BF16),
        ],
        compiler_params=_cparams(("parallel", "arbitrary")),
        name="proj",
    )(hn, w, wt, cos_t, sin_t)


A_COLS = A_HEADS * WINDOW


def _swa_kernel(q_ref, kc_ref, kp_ref, vc_ref, vp_ref, sink_ref, g_ref, o_ref, *, tq):
    i = pl.program_id(1)
    nblk = tq // WINDOW
    kext = jnp.concatenate([kp_ref[...], kc_ref[...]], axis=0)
    lane_kv = lax.broadcasted_iota(jnp.int32, (WINDOW, A_KV_WIDTH), 1) // HEAD_DIM
    zero = jnp.zeros((WINDOW, A_KV_WIDTH), BF16)

    scores = []
    for b in range(nblk):
        parts = []
        for g in range(A_GROUP):
            qg = q_ref[b * WINDOW:(b + 1) * WINDOW, g * A_KV_WIDTH:(g + 1) * A_KV_WIDTH]
            parts += [jnp.where(lane_kv == j, qg, zero) for j in range(A_KV_HEADS)]
        qcat = jnp.concatenate(parts, axis=0)
        scores.append(lax.dot_general(kext[b * WINDOW:(b + 2) * WINDOW, :], qcat, (((1,), (1,)), ((), ())),
                                      preferred_element_type=F32))

    r = lax.broadcasted_iota(jnp.int32, (2 * WINDOW, A_COLS), 0)
    c = lax.broadcasted_iota(jnp.int32, (2 * WINDOW, A_COLS), 1) & (WINDOW - 1)
    band = (r > c) & (r <= c + WINDOW)
    sink = sink_ref[...]
    gain = g_ref[...]
    for b, st in enumerate(scores):
        mask = band
        if b == 0:
            mask = band & ((r >= WINDOW) | (i > 0))
        st = jnp.where(mask, st, NEG)
        m = jnp.maximum(jnp.max(st, axis=0, keepdims=True), sink)
        p = jnp.exp2(st - m)
        l = jnp.sum(p, axis=0, keepdims=True) + jnp.exp2(sink - m)
        vt = jnp.concatenate([vp_ref[0] if b == 0 else vc_ref[b - 1], vc_ref[b]], axis=1)
        pv = jnp.dot(vt, p.astype(BF16), preferred_element_type=F32)
        inv_l = 1.0 / l
        heads = []
        for g in range(A_GROUP):
            for j in range(A_KV_HEADS):
                cols = slice((g * A_KV_HEADS + j) * WINDOW, (g * A_KV_HEADS + j + 1) * WINDOW)
                heads.append(pv[j * HEAD_DIM:(j + 1) * HEAD_DIM, cols] * inv_l[:, cols])
        ot = jnp.concatenate(heads, axis=0)
        ms = jnp.mean(ot * ot, axis=0, keepdims=True)
        y = (ot * lax.rsqrt(ms + NORM_EPS)).T * gain
        o_ref[b * WINDOW:(b + 1) * WINDOW, :] = y.astype(o_ref.dtype)


def _swa(proj, vat, sink_row, g, *, B, S, tq=512):
    nq = S // tq
    rpb = tq // WINDOW

    def prev_blk(b, i):
        return jnp.maximum(b * (S // WINDOW) + i * rpb - 1, 0)

    return pl.pallas_call(
        functools.partial(_swa_kernel, tq=tq),
        grid=(B, nq),
        in_specs=[
            pl.BlockSpec((tq, A_WIDTH), lambda b, i: (b * nq + i, COL_QA // A_WIDTH)),
            pl.BlockSpec((tq, A_KV_WIDTH), lambda b, i: (b * nq + i, COL_KA // A_KV_WIDTH)),
            pl.BlockSpec((WINDOW, A_KV_WIDTH), lambda b, i: (prev_blk(b, i), COL_KA // A_KV_WIDTH)),
            pl.BlockSpec((rpb, A_KV_WIDTH, WINDOW), lambda b, i: (b * nq + i, 0, 0)),
            pl.BlockSpec((1, A_KV_WIDTH, WINDOW), lambda b, i: (prev_blk(b, i), 0, 0)),
            pl.BlockSpec((1, A_COLS), lambda b, i: (0, 0)),
            pl.BlockSpec((1, A_WIDTH), lambda b, i: (0, 0)),
        ],
        out_specs=pl.BlockSpec((tq, A_WIDTH), lambda b, i: (b * nq + i, 0)),
        out_shape=jax.ShapeDtypeStruct((B * S, A_WIDTH), BF16),
        compiler_params=_cparams(("parallel", "parallel")),
        name="swa",
    )(proj, proj, proj, vat, vat, sink_row, g.reshape(1, A_WIDTH))


POOL_HALO = 16


def _pool_kernel(u0_ref, u1_ref, h0_ref, h1_ref, w_ref, s_ref, o_ref, *, tq):
    i = pl.program_id(1)
    u = jnp.concatenate([u0_ref[...], u1_ref[...]], axis=1).astype(F32)
    halo = jnp.concatenate([h0_ref[...], h1_ref[...]], axis=1).astype(F32)
    halo = halo * (i > 0).astype(F32)
    ext = jnp.concatenate([halo, u], axis=0)
    t = i * tq + lax.broadcasted_iota(jnp.int32, (tq, 1), 0)
    for gi, w in enumerate(POOL_WINDOWS):
        sl = slice(gi * B_GROUP_DIM, (gi + 1) * B_GROUP_DIM)
        s = ext[:, sl]
        shift = 1
        while shift < w:
            s = s + pltpu.roll(s, shift, 0)
            shift *= 2
        cnt = jnp.minimum(t + 1, w).astype(F32)
        y = (s[POOL_HALO:, :] / cnt - u[:, sl]).astype(BF16)
        o = jnp.dot(y, w_ref[gi], preferred_element_type=F32) * s_ref[:, sl]
        o_ref[:, sl] = o.astype(o_ref.dtype)


def _pool(proj, w_pool, scale, *, B, S, tq=512):
    nq = S // tq
    c0 = COL_UB // 256

    def cur(c):
        return pl.BlockSpec((tq, 256), lambda b, i: (b * nq + i, c))

    def halo(c):
        return pl.BlockSpec(
            (POOL_HALO, 256),
            lambda b, i: (jnp.maximum((b * S + i * tq) // POOL_HALO - 1, 0), c))

    return pl.pallas_call(
        functools.partial(_pool_kernel, tq=tq),
        grid=(B, nq),
        in_specs=[cur(c0), cur(c0 + 1), halo(c0), halo(c0 + 1),
                  pl.BlockSpec((len(POOL_WINDOWS), B_GROUP_DIM, B_GROUP_DIM), lambda b, i: (0, 0, 0)),
                  pl.BlockSpec((1, B_WIDTH), lambda b, i: (0, 0))],
        out_specs=pl.BlockSpec((tq, B_WIDTH), lambda b, i: (b * nq + i, 0)),
        out_shape=jax.ShapeDtypeStruct((B * S, B_WIDTH), BF16),
        compiler_params=_cparams(("parallel", "parallel")),
        name="pool",
    )(proj, proj, proj, proj, w_pool, scale.reshape(1, B_WIDTH))


DIFF_UNROLL = 4


def _diff_kernel(q_ref, k_ref, vt_ref, lam_ref, g_ref, o_ref, qcat_ref, s_ref, bm_ref, m_ref, acc_ref,
                 *, tq, lam_init):
    qi = pl.program_id(2)
    n_diag = tq // DIFF_KB
    n_below = qi * n_diag
    q = q_ref[...]
    lane = lax.broadcasted_iota(jnp.int32, q.shape, 1)
    zero = jnp.zeros_like(q)
    qcat_ref[0:tq, :] = jnp.where(lane < HEAD_DIM, q, zero)
    qcat_ref[tq:2 * tq, :] = jnp.where(lane >= HEAD_DIM, q, zero)
    m_ref[...] = jnp.full(m_ref.shape, NEG, F32)
    acc_ref[...] = jnp.zeros(acc_ref.shape, F32)

    def produce(kidx, q0=0):
        off = pl.multiple_of(kidx * DIFF_KB, DIFF_KB)
        queries = qcat_ref[...] if q0 == 0 else jnp.concatenate(
            [qcat_ref[q0:tq, :], qcat_ref[tq + q0:2 * tq, :]], axis=0)
        st = lax.dot_general(k_ref[pl.ds(off, DIFF_KB), :], queries, (((1,), (1,)), ((), ())),
                             preferred_element_type=F32)
        return st, jnp.max(st, axis=0, keepdims=True)

    def store(buf, block):
        width = block[0].shape[1]
        s_ref[buf, :, 0:width], bm_ref[buf, :, 0:width] = block

    def consume(buf, kidx, first_key):
        q0 = first_key or 0
        nq_act = tq - q0
        st = s_ref[buf, :, 0:2 * nq_act]
        if first_key is not None:
            r = lax.broadcasted_iota(jnp.int32, st.shape, 0)
            c = lax.broadcasted_iota(jnp.int32, st.shape, 1)
            c = jnp.where(c >= nq_act, c - nq_act, c)
            st = jnp.where(r <= c, st, NEG)
            block_max = jnp.max(st, axis=0, keepdims=True)
        else:
            block_max = bm_ref[buf]
        maps = (slice(q0, tq), slice(tq + q0, 2 * tq))
        m = m_ref[...] if q0 == 0 else jnp.concatenate([m_ref[:, lanes] for lanes in maps], axis=1)
        m_new = jnp.maximum(m, block_max)
        p = jnp.exp2(st - m_new)
        pv = jnp.dot(vt_ref[kidx], p.astype(BF16), preferred_element_type=F32)
        alpha = jnp.exp2(m - m_new)
        if q0 == 0:
            m_ref[...] = m_new
            acc_ref[...] = alpha * acc_ref[...] + pv
        else:
            for c, lanes in enumerate(maps):
                part = slice(c * nq_act, (c + 1) * nq_act)
                m_ref[:, lanes] = m_new[:, part]
                acc_ref[:, lanes] = alpha[:, part] * acc_ref[:, lanes] + pv[:, part]

    def steps(first, n_plain, n_masked, last):
        n = n_plain + n_masked
        first_keys = [None] * n_plain + [u * DIFF_KB for u in range(n_masked)]
        for u in range(n):
            more = not (last and u == n - 1)
            if more:
                nxt = produce(first + u + 1, (first_keys[u + 1] or 0) if u + 1 < n else 0)
            consume(u % 2, first + u, first_keys[u])
            if more:
                store(1 - u % 2, nxt)

    store(0, produce(0))

    def body(c, carry):
        steps(c * DIFF_UNROLL, DIFF_UNROLL, 0, False)
        return carry

    lax.fori_loop(0, n_below // DIFF_UNROLL, body, 0)
    for rem in range(0, DIFF_UNROLL, math.gcd(n_diag, DIFF_UNROLL)):
        @pl.when(n_below % DIFF_UNROLL == rem)
        def _(rem=rem):
            steps(n_below - rem, rem, n_diag, True)

    lv = lam_ref[...]
    lam = (jnp.exp(jnp.sum(lv[0:1] * lv[1:2], axis=1, keepdims=True))
           - jnp.exp(jnp.sum(lv[2:3] * lv[3:4], axis=1, keepdims=True)) + lam_init)
    l = acc_ref[C_VDIM:C_VDIM + 1, :]
    acc = acc_ref[0:C_VDIM, :]
    ot = acc[:, :tq] / l[:, :tq] - lam * (acc[:, tq:] / l[:, tq:])
    ms = jnp.mean(ot * ot, axis=0, keepdims=True)
    yt = ot * lax.rsqrt(ms + DIFF_EPS) * (g_ref[...] * (1.0 - lam_init))
    o_ref[...] = yt.T.astype(o_ref.dtype)


def _diff(proj, vt, lam_vecs, subln, *, B, S, lam_init, tq=2048):
    nq = S // tq
    nkb = S // DIFF_KB
    assert tq % DIFF_KB == 0 and DIFF_UNROLL % 2 == 0
    return pl.pallas_call(
        functools.partial(_diff_kernel, tq=tq, lam_init=lam_init),
        grid=(B, C_HEADS, nq),
        in_specs=[
            pl.BlockSpec((tq, C_VDIM), lambda b, h, i: (b * nq + i, COL_QC // C_VDIM + h)),
            pl.BlockSpec((S, C_VDIM), lambda b, h, i: (b, COL_KC // C_VDIM + h)),
            pl.BlockSpec((None, None, nkb, VT_ROWS, DIFF_KB), lambda b, h, i: (b, h, 0, 0, 0)),
            pl.BlockSpec((4, HEAD_DIM), lambda b, h, i: (0, 0)),
            pl.BlockSpec((C_VDIM, 1), lambda b, h, i: (0, 0)),
        ],
        out_specs=pl.BlockSpec((tq, C_VDIM), lambda b, h, i: (b * nq + i, h)),
        out_shape=jax.ShapeDtypeStruct((B * S, C_WIDTH), BF16),
        scratch_shapes=[pltpu.VMEM((2 * tq, C_VDIM), BF16),
                        pltpu.VMEM((2, DIFF_KB, 2 * tq), F32),
                        pltpu.VMEM((2, 1, 2 * tq), F32),
                        pltpu.VMEM((1, 2 * tq), F32),
                        pltpu.VMEM((VT_ROWS, 2 * tq), F32)],
        compiler_params=_cparams(("parallel", "parallel", "arbitrary")),
        name="diff",
    )(proj, proj, vt, lam_vecs, subln.reshape(C_VDIM, 1))


OUT_SUB = 256


def _out_kernel(oa_ref, ob_ref, oc_ref, w_ref, h_ref, g_ref, h1_ref, hn_ref, *, tm):
    subs = [slice(s * OUT_SUB, (s + 1) * OUT_SUB) for s in range(tm // OUT_SUB)]
    accs = []
    for rows in subs:
        acc = jnp.dot(oa_ref[rows, :], w_ref[0:A_WIDTH, :], preferred_element_type=F32)
        acc = acc + jnp.dot(ob_ref[rows, :], w_ref[A_WIDTH:A_WIDTH + B_WIDTH, :], preferred_element_type=F32)
        acc = acc + jnp.dot(oc_ref[rows, :], w_ref[A_WIDTH + B_WIDTH:, :], preferred_element_type=F32)
        accs.append(acc)
    for rows, acc in zip(subs, accs):
        h1 = h_ref[rows, :] + acc
        h1_ref[rows, :] = h1
        ms = jnp.mean(h1 * h1, axis=1, keepdims=True)
        hn = h1 * lax.rsqrt(ms + NORM_EPS) * g_ref[...]
        hn_ref[rows, :] = _pack_rows(hn)


def _out(oa, ob, oc, w, h, g, *, tm=512):
    T, D = h.shape
    return pl.pallas_call(
        functools.partial(_out_kernel, tm=tm),
        grid=(T // tm,),
        in_specs=[
            pl.BlockSpec((tm, A_WIDTH), lambda i: (i, 0)),
            pl.BlockSpec((tm, B_WIDTH), lambda i: (i, 0)),
            pl.BlockSpec((tm, C_WIDTH), lambda i: (i, 0)),
            pl.BlockSpec((D, D), lambda i: (0, 0)),
            pl.BlockSpec((tm, D), lambda i: (i, 0)),
            pl.BlockSpec((1, D), lambda i: (0, 0)),
        ],
        out_specs=[
            pl.BlockSpec((tm, D), lambda i: (i, 0)),
            pl.BlockSpec((tm, D_PACK), lambda i: (i, 0)),
        ],
        out_shape=[
            jax.ShapeDtypeStruct((T, D), F32),
            jax.ShapeDtypeStruct((T, D_PACK), U32),
        ],
        compiler_params=_cparams(("parallel",)),
        name="out",
    )(oa, ob, oc, w, h, g.reshape(1, D))


ROUTER_ROWS = 128
ROUTER_LO = 64
ROUTER_GROUP_ROW = 0
ROUTER_EXPERT_ROW = 8


def _router_kernel(x_ref, wt_ref, b_ref, e_ref, g_ref):
    x = _unpack_rows_bf16(x_ref[...])
    lg = lax.dot_general(wt_ref[...], x, (((1,), (1,)), ((), ())), preferred_element_type=F32)
    lg = lg[0:ROUTER_LO] + lg[ROUTER_LO:ROUTER_ROWS] + b_ref[...]
    grp = lg[ROUTER_GROUP_ROW:ROUTER_GROUP_ROW + N_GROUPS]
    ex = lg[ROUTER_EXPERT_ROW:ROUTER_EXPERT_ROW + N_EXPERTS]

    mg = jnp.max(grp, axis=0, keepdims=True)
    pg_sel = 1.0 / jnp.sum(jnp.exp(grp - mg), axis=0, keepdims=True)
    gi = lax.broadcasted_iota(jnp.int32, grp.shape, 0)
    g_sel = jnp.min(jnp.where(grp == mg, gi, N_GROUPS), axis=0, keepdims=True)

    er = lax.broadcasted_iota(jnp.int32, ex.shape, 0)
    group_of = lax.shift_right_logical(er, EXPERTS_PER_GROUP.bit_length() - 1)
    cand = jnp.where(group_of == g_sel, ex, NEG)
    v1 = jnp.max(cand, axis=0, keepdims=True)
    i1 = jnp.min(jnp.where(cand == v1, er, N_EXPERTS), axis=0, keepdims=True)
    rest = jnp.where(er == i1, NEG, cand)
    v2 = jnp.max(rest, axis=0, keepdims=True)
    i2 = jnp.min(jnp.where(rest == v2, er, N_EXPERTS), axis=0, keepdims=True)
    t = jnp.exp(v2 - v1)
    g1 = pg_sel / (1.0 + t)
    e_ref[...] = jnp.concatenate([i1, i2], axis=0)
    g_ref[...] = jnp.concatenate([g1, g1 * t], axis=0)


def _router(hn_packed, wt, b, *, tm=1024):
    T = hn_packed.shape[0]
    D = wt.shape[1]
    return pl.pallas_call(
        _router_kernel,
        grid=(T // tm,),
        in_specs=[
            pl.BlockSpec((tm, D_PACK), lambda i: (i, 0)),
            pl.BlockSpec((ROUTER_ROWS, D), lambda i: (0, 0)),
            pl.BlockSpec((ROUTER_LO, 1), lambda i: (0, 0)),
        ],
        out_specs=[pl.BlockSpec((TOP_K, tm), lambda i: (0, i)), pl.BlockSpec((TOP_K, tm), lambda i: (0, i))],
        out_shape=[jax.ShapeDtypeStruct((TOP_K, T), jnp.int32), jax.ShapeDtypeStruct((TOP_K, T), F32)],
        compiler_params=_cparams(("parallel",)),
        name="router",
    )(hn_packed, wt, b)


EXPERT_SUB = 256
EXPERT_VMEM = 56 * 1024 * 1024


def _expert_kernel(blk_ref, exp_ref, lo_ref, hi_ref, x_ref, g_ref, wg_ref, wu_ref, wd_ref, o_ref,
                   wg_bf, wu_bf, wd_bf, *, tm):
    p = pl.program_id(0)
    prev = jnp.maximum(p - 1, 0)
    blk = blk_ref[p]
    first = (p == 0) | (blk != blk_ref[prev])
    new_expert = (p == 0) | (exp_ref[p] != exp_ref[prev])
    lo = lo_ref[p]
    hi = hi_ref[p]

    @pl.when(new_expert)
    def _():
        wg_bf[...] = wg_ref[...].astype(BF16)
        wu_bf[...] = wu_ref[...].astype(BF16)
        wd_bf[...] = wd_ref[...].astype(BF16)

    @pl.when(first)
    def _():
        o_ref[...] = jnp.zeros_like(o_ref)

    for s in range(tm // EXPERT_SUB):
        row0 = blk * tm + s * EXPERT_SUB
        rows = slice(s * EXPERT_SUB, (s + 1) * EXPERT_SUB)

        @pl.when((hi > row0) & (lo < row0 + EXPERT_SUB))
        def _(row0=row0, rows=rows):
            x = _unpack_rows_bf16(x_ref[rows, :])
            a = jnp.dot(x, wg_bf[...], preferred_element_type=F32)
            u = jnp.dot(x, wu_bf[...], preferred_element_type=F32)
            act = (a / (1.0 + jnp.exp(-a))) * u
            y = jnp.dot(act.astype(BF16), wd_bf[...], preferred_element_type=F32)
            r = row0 + lax.broadcasted_iota(jnp.int32, (EXPERT_SUB, 1), 0)
            mine = (r >= lo) & (r < hi)
            o_ref[rows, :] = jnp.where(mine, _pack_rows(y * g_ref[rows, :]), o_ref[rows, :])


def _experts(seg_blk, seg_exp, seg_lo, seg_hi, xs, gates, wg, wu, wd, layer, *, tm):
    N = xs.shape[0]
    D = wg.shape[2]
    P = seg_blk.shape[0]
    grid_spec = pltpu.PrefetchScalarGridSpec(
        num_scalar_prefetch=4,
        grid=(P,),
        in_specs=[
            pl.BlockSpec((tm, D_PACK), lambda p, blk, ex, lo, hi: (blk[p], 0)),
            pl.BlockSpec((tm, 1), lambda p, blk, ex, lo, hi: (blk[p], 0)),
            pl.BlockSpec((None, None, D, D_FF), lambda p, blk, ex, lo, hi: (layer, ex[p], 0, 0)),
            pl.BlockSpec((None, None, D, D_FF), lambda p, blk, ex, lo, hi: (layer, ex[p], 0, 0)),
            pl.BlockSpec((None, None, D_FF, D), lambda p, blk, ex, lo, hi: (layer, ex[p], 0, 0)),
        ],
        out_specs=pl.BlockSpec((tm, D_PACK), lambda p, blk, ex, lo, hi: (blk[p], 0)),
        scratch_shapes=[pltpu.VMEM((D, D_FF), BF16), pltpu.VMEM((D, D_FF), BF16), pltpu.VMEM((D_FF, D), BF16)],
    )
    return pl.pallas_call(
        functools.partial(_expert_kernel, tm=tm),
        grid_spec=grid_spec,
        out_shape=jax.ShapeDtypeStruct((N, D_PACK), U32),
        compiler_params=_cparams(("arbitrary",), EXPERT_VMEM),
        name="experts",
    )(seg_blk, seg_exp, seg_lo, seg_hi, xs, gates, wg, wu, wd)


def _route(eidx_t, gates_t, T, tm):
    N = T * TOP_K
    flat_e = eidx_t.T.reshape(N)
    iota = jnp.arange(N, dtype=jnp.int32)
    sorted_e, order, g_sorted = lax.sort((flat_e, iota, gates_t.T.reshape(N)), num_keys=1, is_stable=True)
    tok = order // TOP_K
    _, pos = lax.sort((order, iota), num_keys=1)
    counts = jnp.bincount(flat_e, length=N_EXPERTS)
    starts = (jnp.cumsum(counts) - counts).astype(jnp.int32)
    nb = N // tm
    bounds = jnp.sort(jnp.concatenate([jnp.arange(nb, dtype=jnp.int32) * tm, starts]))
    ends = jnp.concatenate([bounds[1:], jnp.full((1,), N, bounds.dtype)])
    seg_blk = jnp.minimum(bounds // tm, nb - 1).astype(jnp.int32)
    seg_exp = sorted_e[jnp.minimum(bounds, N - 1)].astype(jnp.int32)
    pos_rows = pos.reshape(T, TOP_K).T.reshape(N)
    return tok, g_sorted, pos_rows, seg_blk, seg_exp, bounds, ends


def _group_major(a, axis):
    shape = a.shape
    a = a.reshape(shape[:axis] + (A_KV_HEADS, A_GROUP, HEAD_DIM) + shape[axis + 1:])
    return jnp.swapaxes(a, axis, axis + 1).reshape(shape)


def _permute_in_proj(w):
    bounds = np.cumsum([0, 768, 256, 256, 512, 768, 768, 768])
    qa, ka, va, ub, qc, kc, vc = [w[:, bounds[s]:bounds[s + 1]] for s in range(7)]
    return jnp.concatenate([_group_major(qa, 1), qc, ka, kc, ub, va, vc], axis=1)


def kernel(x, positions, attn_norm, w_in, sinks, branch_norm_a, w_pool, pool_scale, lambda_q1, lambda_k1,
           lambda_q2, lambda_k2, subln, w_out, ffn_norm, w_router_group, b_router_group, w_router_expert,
           b_router_expert, w_expert_gate, w_expert_up, w_expert_down, final_norm):
    B, S, D = x.shape
    T = B * S
    depth = w_in.shape[0]
    tm_e = 1024

    half = HEAD_DIM // 2
    inv = ROPE_THETA ** (-jnp.arange(half, dtype=F32) / half)
    ang = positions.astype(F32).reshape(T, 1) * inv
    cos, sin = jnp.cos(ang), jnp.sin(ang)
    cos_t = jnp.concatenate([cos, cos, cos, cos], axis=1)
    sin_t = jnp.concatenate([-sin, sin, -sin, sin], axis=1)

    h = x.reshape(T, D)
    y_rows = None
    for l in range(depth):
        if l == 0:
            (hn,) = _norm(h, None, attn_norm[l], emit_h=False, emit_bf16=True)
        else:
            h, hn = _norm(h, y_rows, attn_norm[l], emit_h=True, emit_bf16=True)
        w_in_l = _permute_in_proj(w_in[l]).astype(BF16)
        proj, vat, vt = _proj(hn, w_in_l[:, :COL_VA], w_in_l[:, COL_VA:].T, cos_t, sin_t, B=B, S=S)

        sink_heads = sinks[l].reshape(A_KV_HEADS, A_GROUP).T.reshape(A_HEADS) * LOG2E
        sink_row = jnp.repeat(sink_heads, WINDOW).reshape(1, A_COLS)
        oa = _swa(proj, vat, sink_row, _group_major(branch_norm_a[l], 0), B=B, S=S)
        ob = _pool(proj, w_pool[l].astype(BF16), pool_scale[l], B=B, S=S)

        lam_init = 0.8 - 0.6 * math.exp(-0.3 * l)
        lam_vecs = jnp.stack([lambda_q1[l], lambda_k1[l], lambda_q2[l], lambda_k2[l]]).astype(F32)
        oc = _diff(proj, vt, lam_vecs, subln[l], B=B, S=S, lam_init=lam_init)

        w_out_l = jnp.concatenate([_group_major(w_out[l][:A_WIDTH], 0), w_out[l][A_WIDTH:]], axis=0)
        h1, hn2 = _out(oa, ob, oc, w_out_l.astype(BF16), h, ffn_norm[l])

        wr = jnp.zeros((ROUTER_LO, D), F32)
        wr = wr.at[ROUTER_GROUP_ROW:ROUTER_GROUP_ROW + N_GROUPS].set(w_router_group[l].T)
        wr = wr.at[ROUTER_EXPERT_ROW:ROUTER_EXPERT_ROW + N_EXPERTS].set(w_router_expert[l].T)
        wr_hi = wr.astype(BF16)
        wr_lo = (wr - wr_hi.astype(F32)).astype(BF16)
        br = jnp.zeros((ROUTER_LO, 1), F32)
        br = br.at[ROUTER_GROUP_ROW:ROUTER_GROUP_ROW + N_GROUPS, 0].set(b_router_group[l])
        br = br.at[ROUTER_EXPERT_ROW:ROUTER_EXPERT_ROW + N_EXPERTS, 0].set(b_router_expert[l])
        eidx_t, gates_t = _router(hn2, jnp.concatenate([wr_hi, wr_lo], axis=0), br)

        tok, g_sorted, pos_rows, seg_blk, seg_exp, seg_lo, seg_hi = _route(eidx_t, gates_t, T, tm_e)
        xs = _sc_gather_rows(hn2, tok)
        ys = _experts(seg_blk, seg_exp, seg_lo, seg_hi, xs, g_sorted.reshape(-1, 1),
                      w_expert_gate, w_expert_up, w_expert_down, l, tm=tm_e)
        y_rows = _sc_gather_rows(ys, pos_rows)
        h = h1

    (out,) = _norm(h, y_rows, final_norm, emit_h=False, emit_bf16=False)
    return out.reshape(B, S, D)
```

```python
import functools
import math

import jax
import jax.numpy as jnp
from jax import lax
from jax.experimental import pallas as pl
from jax.experimental.pallas import tpu as pltpu
from jax.experimental.pallas import tpu_sc as plsc
import numpy as np

F32 = jnp.float32
BF16 = jnp.bfloat16

D_MODEL = 2048
HEAD_DIM = 64
ROPE_THETA = 10000.0
A_HEADS = 12
A_KV_HEADS = 4
A_GROUP = 3
A_WIDTH = 768
WINDOW = 128
POOL_WINDOWS = (2, 4, 8, 16)
B_WIDTH = 512
B_GROUP_DIM = 128
C_VDIM = 128
C_WIDTH = 768
C_HEADS = 6
DIFF_EPS = 1e-5
IN_COLS = 4096
N_GROUPS = 4
EXPERTS_PER_GROUP = 8
N_EXPERTS = 32
TOP_K = 2
D_FF = 512
NORM_EPS = 1e-6
NEG = -1e30
LOG2E = math.log2(math.e)
Q_SCALE = HEAD_DIM ** -0.5 * LOG2E

COL_QA, COL_QC, COL_KA, COL_KC, COL_UB, COL_VA, COL_VC = 0, 768, 1536, 1792, 2560, 3072, 3328
PROJ_TN = 512
A_KV_WIDTH = A_KV_HEADS * HEAD_DIM
PROJ_SUB = 256
N_SCALED_TILES = COL_KA // PROJ_TN
N_ROPE_TILES = COL_UB // PROJ_TN
N_MAIN_TILES = COL_VA // PROJ_TN
DIFF_KB = 512
VT_ROWS = C_VDIM + 16

VMEM_LIMIT = 48 * 1024 * 1024
D_PACK = D_MODEL // 2
U32 = jnp.uint32


def _cparams(sem, vmem=VMEM_LIMIT):
    return pltpu.CompilerParams(dimension_semantics=sem, vmem_limit_bytes=vmem)


def _pack_rows(x):
    lo = pltpu.bitcast(x[:, :D_PACK].astype(BF16).astype(F32), U32) >> 16
    hi = pltpu.bitcast(x[:, D_PACK:].astype(BF16).astype(F32), U32)
    return hi | lo


def _unpack_rows(w):
    return pltpu.bitcast(w << 16, F32), pltpu.bitcast(w & U32(0xFFFF0000), F32)


def _unpack_rows_bf16(w):
    lo, hi = _unpack_rows(w)
    return jnp.concatenate([lo.astype(BF16), hi.astype(BF16)], axis=1)


SC_WINDOW = 32
SC_INDEX_LANES = 128


def _sc_gather_rows(x, idx):
    M = idx.shape[0]
    W = x.shape[1]
    idx2 = jnp.pad(idx.reshape(M // SC_WINDOW, SC_WINDOW), ((0, 0), (0, SC_INDEX_LANES - SC_WINDOW)))
    mesh = plsc.VectorSubcoreMesh(core_axis_name="core", subcore_axis_name="subcore")

    @pl.kernel(out_type=jax.ShapeDtypeStruct((M, W), x.dtype), mesh=mesh)
    def gather(x_hbm, i_hbm, o_hbm):
        def body(i_vmem, o_vmem):
            pltpu.sync_copy(x_hbm.at[i_vmem.at[0, pl.ds(0, SC_WINDOW)]], o_vmem)

        pltpu.emit_pipeline(
            body,
            grid=(M // SC_WINDOW,),
            in_specs=[pl.BlockSpec((1, SC_INDEX_LANES), index_map=lambda i: (i, 0))],
            out_specs=[pl.BlockSpec((SC_WINDOW, W), index_map=lambda i: (i, 0))],
            core_axis_name=("core", "subcore"),
            dimension_semantics=(pltpu.PARALLEL,),
        )(i_hbm, o_hbm)

    return gather(x, idx2)


def _norm_kernel(*refs, n_add, emit_h, emit_bf16, eps):
    h_ref = refs[0]
    add_refs = refs[1:1 + n_add]
    g_ref = refs[1 + n_add]
    outs = refs[2 + n_add:]
    h = h_ref[...]
    if n_add:
        lo = jnp.zeros((h.shape[0], D_PACK), F32)
        hi = jnp.zeros((h.shape[0], D_PACK), F32)
        for r in add_refs:
            a, b = _unpack_rows(r[...])
            lo, hi = lo + a, hi + b
        h = h + jnp.concatenate([lo, hi], axis=1)
    ms = jnp.mean(h * h, axis=-1, keepdims=True)
    y = h * lax.rsqrt(ms + eps) * g_ref[...]
    k = 0
    if emit_h:
        outs[k][...] = h
        k += 1
    outs[k][...] = y.astype(BF16 if emit_bf16 else F32)


def _norm(h, y_rows, g, *, emit_h, emit_bf16, tm=512):
    T, D = h.shape
    nt = T // tm
    row = pl.BlockSpec((tm, D), lambda i: (i, 0))
    n_add = 0 if y_rows is None else TOP_K
    add_specs = [pl.BlockSpec((tm, D_PACK), lambda i, k=k: (i + k * nt, 0)) for k in range(n_add)]
    out_shape, out_specs = [], []
    if emit_h:
        out_shape.append(jax.ShapeDtypeStruct((T, D), F32))
        out_specs.append(row)
    out_shape.append(jax.ShapeDtypeStruct((T, D), BF16 if emit_bf16 else F32))
    out_specs.append(row)
    return pl.pallas_call(
        functools.partial(_norm_kernel, n_add=n_add, emit_h=emit_h, emit_bf16=emit_bf16, eps=NORM_EPS),
        grid=(nt,),
        in_specs=[row] + add_specs + [pl.BlockSpec((1, D), lambda i: (0, 0))],
        out_specs=out_specs,
        out_shape=out_shape,
        compiler_params=_cparams(("parallel",)),
        name="norm",
    )(h, *([y_rows] * n_add), g.reshape(1, D))


def _proj_kernel(x_ref, w_ref, wt_ref, cos_ref, sin_ref, o_ref, vat_ref, vt_ref, *, tm):
    j = pl.program_id(1)
    subs = [slice(s * PROJ_SUB, (s + 1) * PROJ_SUB) for s in range(tm // PROJ_SUB)]

    def matmuls():
        return [jnp.dot(x_ref[rows, :], w_ref[...], preferred_element_type=F32) for rows in subs]

    @pl.when(j < N_ROPE_TILES)
    def _():
        accs = matmuls()
        scale = jnp.where(j < N_SCALED_TILES, Q_SCALE, 1.0).astype(F32)
        lane = lax.broadcasted_iota(jnp.int32, (PROJ_SUB, PROJ_TN), 1)
        first_half = (lane & (HEAD_DIM // 2)) == 0
        for rows, acc in zip(subs, accs):
            c = jnp.concatenate([cos_ref[rows, :] * scale] * (PROJ_TN // 128), axis=1)
            s = jnp.concatenate([sin_ref[rows, :] * scale] * (PROJ_TN // 128), axis=1)
            swapped = jnp.where(first_half,
                                pltpu.roll(acc, PROJ_TN - HEAD_DIM // 2, 1),
                                pltpu.roll(acc, HEAD_DIM // 2, 1))
            o_ref[rows, :] = (acc * c + swapped * s).astype(o_ref.dtype)

    @pl.when((j >= N_ROPE_TILES) & (j < N_MAIN_TILES))
    def _():
        for rows, acc in zip(subs, matmuls()):
            o_ref[rows, :] = acc.astype(o_ref.dtype)

    per_sub = PROJ_SUB // WINDOW
    per_kb = DIFF_KB // PROJ_SUB
    for t in range((IN_COLS - COL_VA) // PROJ_TN):
        @pl.when(j == N_MAIN_TILES + t)
        def _(t=t):
            accs = [lax.dot_general(wt_ref[...], x_ref[rows, :], (((1,), (1,)), ((), ())),
                                    preferred_element_type=F32) for rows in subs]
            for s, acc in enumerate(accs):
                cols = slice((s % per_kb) * PROJ_SUB, (s % per_kb + 1) * PROJ_SUB)
                for r0 in range(0, PROJ_TN, C_VDIM):
                    row = t * PROJ_TN + r0
                    if row == 0:
                        for b in range(per_sub):
                            vat_ref[s * per_sub + b] = acc[0:A_KV_WIDTH, b * WINDOW:(b + 1) * WINDOW].astype(
                                vat_ref.dtype)
                    elif row >= A_KV_WIDTH:
                        head = (row - A_KV_WIDTH) // C_VDIM
                        vt_ref[head, s // per_kb, 0:C_VDIM, cols] = acc[r0:r0 + C_VDIM, :].astype(vt_ref.dtype)

    @pl.when(j == N_MAIN_TILES)
    def _():
        row = lax.broadcasted_iota(jnp.int32, (VT_ROWS - C_VDIM, DIFF_KB), 0)
        extra = jnp.where(row == 0, 1.0, 0.0).astype(vt_ref.dtype)
        for head in range(C_HEADS):
            for kb in range(tm // DIFF_KB):
                vt_ref[head, kb, C_VDIM:VT_ROWS, :] = extra


def _proj(hn, w, wt, cos_t, sin_t, *, B, S, tm=2048):
    T, D = hn.shape
    spt = S // tm
    last_main = N_MAIN_TILES - 1
    return pl.pallas_call(
        functools.partial(_proj_kernel, tm=tm),
        grid=(T // tm, IN_COLS // PROJ_TN),
        in_specs=[
            pl.BlockSpec((tm, D), lambda i, j: (i, 0)),
            pl.BlockSpec((D, PROJ_TN), lambda i, j: (0, jnp.minimum(j, last_main))),
            pl.BlockSpec((PROJ_TN, D), lambda i, j: (jnp.maximum(j - N_MAIN_TILES, 0), 0)),
            pl.BlockSpec((tm, 128), lambda i, j: (i, 0)),
            pl.BlockSpec((tm, 128), lambda i, j: (i, 0)),
        ],
        out_specs=[
            pl.BlockSpec((tm, PROJ_TN), lambda i, j: (i, jnp.minimum(j, last_main))),
            pl.BlockSpec((tm // WINDOW, A_KV_WIDTH, WINDOW), lambda i, j: (i, 0, 0)),
            pl.BlockSpec((None, C_HEADS, tm // DIFF_KB, VT_ROWS, DIFF_KB),
                         lambda i, j: (i // spt, 0, i % spt, 0, 0)),
        ],
        out_shape=[
            jax.ShapeDtypeStruct((T, COL_VA), BF16),
            jax.ShapeDtypeStruct((T // WINDOW, A_KV_WIDTH, WINDOW), BF16),
            jax.ShapeDtypeStruct((B, C_HEADS, S // DIFF_KB, VT_ROWS, DIFF_KB), BF16),
        ],
        compiler_params=_cparams(("parallel", "arbitrary")),
        name="proj",
    )(hn, w, wt, cos_t, sin_t)


A_COLS = A_HEADS * WINDOW


def _swa_kernel(q_ref, kc_ref, kp_ref, vc_ref, vp_ref, sink_ref, g_ref, o_ref, *, tq):
    i = pl.program_id(1)
    nblk = tq // WINDOW
    kext = jnp.concatenate([kp_ref[...], kc_ref[...]], axis=0)
    lane_kv = lax.broadcasted_iota(jnp.int32, (WINDOW, A_KV_WIDTH), 1) // HEAD_DIM
    zero = jnp.zeros((WINDOW, A_KV_WIDTH), BF16)

    scores = []
    for b in range(nblk):
        parts = []
        for g in range(A_GROUP):
            qg = q_ref[b * WINDOW:(b + 1) * WINDOW, g * A_KV_WIDTH:(g + 1) * A_KV_WIDTH]
            parts += [jnp.where(lane_kv == j, qg, zero) for j in range(A_KV_HEADS)]
        qcat = jnp.concatenate(parts, axis=0)
        scores.append(lax.dot_general(kext[b * WINDOW:(b + 2) * WINDOW, :], qcat, (((1,), (1,)), ((), ())),
                                      preferred_element_type=F32))

    r = lax.broadcasted_iota(jnp.int32, (2 * WINDOW, A_COLS), 0)
    c = lax.broadcasted_iota(jnp.int32, (2 * WINDOW, A_COLS), 1) & (WINDOW - 1)
    band = (r > c) & (r <= c + WINDOW)
    sink = sink_ref[...]
    gain = g_ref[...]
    for b, st in enumerate(scores):
        mask = band
        if b == 0:
            mask = band & ((r >= WINDOW) | (i > 0))
        st = jnp.where(mask, st, NEG)
        m = jnp.maximum(jnp.max(st, axis=0, keepdims=True), sink)
        p = jnp.exp2(st - m)
        l = jnp.sum(p, axis=0, keepdims=True) + jnp.exp2(sink - m)
        vt = jnp.concatenate([vp_ref[0] if b == 0 else vc_ref[b - 1], vc_ref[b]], axis=1)
        pv = jnp.dot(vt, p.astype(BF16), preferred_element_type=F32)
        inv_l = 1.0 / l
        heads = []
        for g in range(A_GROUP):
            for j in range(A_KV_HEADS):
                cols = slice((g * A_KV_HEADS + j) * WINDOW, (g * A_KV_HEADS + j + 1) * WINDOW)
                heads.append(pv[j * HEAD_DIM:(j + 1) * HEAD_DIM, cols] * inv_l[:, cols])
        ot = jnp.concatenate(heads, axis=0)
        ms = jnp.mean(ot * ot, axis=0, keepdims=True)
        y = (ot * lax.rsqrt(ms + NORM_EPS)).T * gain
        o_ref[b * WINDOW:(b + 1) * WINDOW, :] = y.astype(o_ref.dtype)


def _swa(proj, vat, sink_row, g, *, B, S, tq=1024):
    nq = S // tq
    rpb = tq // WINDOW

    def prev_blk(b, i):
        return jnp.maximum(b * (S // WINDOW) + i * rpb - 1, 0)

    return pl.pallas_call(
        functools.partial(_swa_kernel, tq=tq),
        grid=(B, nq),
        in_specs=[
            pl.BlockSpec((tq, A_WIDTH), lambda b, i: (b * nq + i, COL_QA // A_WIDTH)),
            pl.BlockSpec((tq, A_KV_WIDTH), lambda b, i: (b * nq + i, COL_KA // A_KV_WIDTH)),
            pl.BlockSpec((WINDOW, A_KV_WIDTH), lambda b, i: (prev_blk(b, i), COL_KA // A_KV_WIDTH)),
            pl.BlockSpec((rpb, A_KV_WIDTH, WINDOW), lambda b, i: (b * nq + i, 0, 0)),
            pl.BlockSpec((1, A_KV_WIDTH, WINDOW), lambda b, i: (prev_blk(b, i), 0, 0)),
            pl.BlockSpec((1, A_COLS), lambda b, i: (0, 0)),
            pl.BlockSpec((1, A_WIDTH), lambda b, i: (0, 0)),
        ],
        out_specs=pl.BlockSpec((tq, A_WIDTH), lambda b, i: (b * nq + i, 0)),
        out_shape=jax.ShapeDtypeStruct((B * S, A_WIDTH), BF16),
        compiler_params=_cparams(("parallel", "parallel")),
        name="swa",
    )(proj, proj, proj, vat, vat, sink_row, g.reshape(1, A_WIDTH))


POOL_HALO = 16


def _pool_kernel(u0_ref, u1_ref, h0_ref, h1_ref, w_ref, s_ref, o_ref, *, tq):
    i = pl.program_id(1)
    u = jnp.concatenate([u0_ref[...], u1_ref[...]], axis=1).astype(F32)
    halo = jnp.concatenate([h0_ref[...], h1_ref[...]], axis=1).astype(F32)
    halo = halo * (i > 0).astype(F32)
    ext = jnp.concatenate([halo, u], axis=0)
    t = i * tq + lax.broadcasted_iota(jnp.int32, (tq, 1), 0)
    for gi, w in enumerate(POOL_WINDOWS):
        sl = slice(gi * B_GROUP_DIM, (gi + 1) * B_GROUP_DIM)
        s = ext[:, sl]
        shift = 1
        while shift < w:
            s = s + pltpu.roll(s, shift, 0)
            shift *= 2
        cnt = jnp.minimum(t + 1, w).astype(F32)
        y = (s[POOL_HALO:, :] / cnt - u[:, sl]).astype(BF16)
        o = jnp.dot(y, w_ref[gi], preferred_element_type=F32) * s_ref[:, sl]
        o_ref[:, sl] = o.astype(o_ref.dtype)


def _pool(proj, w_pool, scale, *, B, S, tq=1024):
    nq = S // tq
    c0 = COL_UB // 256

    def cur(c):
        return pl.BlockSpec((tq, 256), lambda b, i: (b * nq + i, c))

    def halo(c):
        return pl.BlockSpec(
            (POOL_HALO, 256),
            lambda b, i: (jnp.maximum((b * S + i * tq) // POOL_HALO - 1, 0), c))

    return pl.pallas_call(
        functools.partial(_pool_kernel, tq=tq),
        grid=(B, nq),
        in_specs=[cur(c0), cur(c0 + 1), halo(c0), halo(c0 + 1),
                  pl.BlockSpec((len(POOL_WINDOWS), B_GROUP_DIM, B_GROUP_DIM), lambda b, i: (0, 0, 0)),
                  pl.BlockSpec((1, B_WIDTH), lambda b, i: (0, 0))],
        out_specs=pl.BlockSpec((tq, B_WIDTH), lambda b, i: (b * nq + i, 0)),
        out_shape=jax.ShapeDtypeStruct((B * S, B_WIDTH), BF16),
        compiler_params=_cparams(("parallel", "parallel")),
        name="pool",
    )(proj, proj, proj, proj, w_pool, scale.reshape(1, B_WIDTH))


DIFF_UNROLL = 4


def _diff_kernel(q_ref, k_ref, vt_ref, lam_ref, g_ref, o_ref, qcat_ref, s_ref, bm_ref, m_ref, acc_ref,
                 *, tq, lam_init):
    qi = pl.program_id(2)
    n_diag = tq // DIFF_KB
    n_below = qi * n_diag
    q = q_ref[...]
    lane = lax.broadcasted_iota(jnp.int32, q.shape, 1)
    zero = jnp.zeros_like(q)
    qcat_ref[0:tq, :] = jnp.where(lane < HEAD_DIM, q, zero)
    qcat_ref[tq:2 * tq, :] = jnp.where(lane >= HEAD_DIM, q, zero)
    m_ref[...] = jnp.full(m_ref.shape, NEG, F32)
    acc_ref[...] = jnp.zeros(acc_ref.shape, F32)

    def produce(kidx, q0=0):
        off = pl.multiple_of(kidx * DIFF_KB, DIFF_KB)
        queries = qcat_ref[...] if q0 == 0 else jnp.concatenate(
            [qcat_ref[q0:tq, :], qcat_ref[tq + q0:2 * tq, :]], axis=0)
        st = lax.dot_general(k_ref[pl.ds(off, DIFF_KB), :], queries, (((1,), (1,)), ((), ())),
                             preferred_element_type=F32)
        return st, jnp.max(st, axis=0, keepdims=True)

    def store(buf, block):
        width = block[0].shape[1]
        s_ref[buf, :, 0:width], bm_ref[buf, :, 0:width] = block

    def consume(buf, kidx, first_key):
        q0 = first_key or 0
        nq_act = tq - q0
        st = s_ref[buf, :, 0:2 * nq_act]
        if first_key is not None:
            r = lax.broadcasted_iota(jnp.int32, st.shape, 0)
            c = lax.broadcasted_iota(jnp.int32, st.shape, 1)
            c = jnp.where(c >= nq_act, c - nq_act, c)
            st = jnp.where(r <= c, st, NEG)
            block_max = jnp.max(st, axis=0, keepdims=True)
        else:
            block_max = bm_ref[buf]
        maps = (slice(q0, tq), slice(tq + q0, 2 * tq))
        m = m_ref[...] if q0 == 0 else jnp.concatenate([m_ref[:, lanes] for lanes in maps], axis=1)
        m_new = jnp.maximum(m, block_max)
        p = jnp.exp2(st - m_new)
        pv = jnp.dot(vt_ref[kidx], p.astype(BF16), preferred_element_type=F32)
        alpha = jnp.exp2(m - m_new)
        if q0 == 0:
            m_ref[...] = m_new
            acc_ref[...] = alpha * acc_ref[...] + pv
        else:
            for c, lanes in enumerate(maps):
                part = slice(c * nq_act, (c + 1) * nq_act)
                m_ref[:, lanes] = m_new[:, part]
                acc_ref[:, lanes] = alpha[:, part] * acc_ref[:, lanes] + pv[:, part]

    def steps(first, n_plain, n_masked, last):
        n = n_plain + n_masked
        first_keys = [None] * n_plain + [u * DIFF_KB for u in range(n_masked)]
        for u in range(n):
            more = not (last and u == n - 1)
            if more:
                nxt = produce(first + u + 1, (first_keys[u + 1] or 0) if u + 1 < n else 0)
            consume(u % 2, first + u, first_keys[u])
            if more:
                store(1 - u % 2, nxt)

    store(0, produce(0))

    def body(c, carry):
        steps(c * DIFF_UNROLL, DIFF_UNROLL, 0, False)
        return carry

    lax.fori_loop(0, n_below // DIFF_UNROLL, body, 0)
    for rem in range(0, DIFF_UNROLL, math.gcd(n_diag, DIFF_UNROLL)):
        @pl.when(n_below % DIFF_UNROLL == rem)
        def _(rem=rem):
            steps(n_below - rem, rem, n_diag, True)

    lv = lam_ref[...]
    lam = (jnp.exp(jnp.sum(lv[0:1] * lv[1:2], axis=1, keepdims=True))
           - jnp.exp(jnp.sum(lv[2:3] * lv[3:4], axis=1, keepdims=True)) + lam_init)
    l = acc_ref[C_VDIM:C_VDIM + 1, :]
    acc = acc_ref[0:C_VDIM, :]
    ot = acc[:, :tq] / l[:, :tq] - lam * (acc[:, tq:] / l[:, tq:])
    ms = jnp.mean(ot * ot, axis=0, keepdims=True)
    yt = ot * lax.rsqrt(ms + DIFF_EPS) * (g_ref[...] * (1.0 - lam_init))
    o_ref[...] = yt.T.astype(o_ref.dtype)


def _diff(proj, vt, lam_vecs, subln, *, B, S, lam_init, tq=2048):
    nq = S // tq
    nkb = S // DIFF_KB
    assert tq % DIFF_KB == 0 and DIFF_UNROLL % 2 == 0
    return pl.pallas_call(
        functools.partial(_diff_kernel, tq=tq, lam_init=lam_init),
        grid=(B, C_HEADS, nq),
        in_specs=[
            pl.BlockSpec((tq, C_VDIM), lambda b, h, i: (b * nq + i, COL_QC // C_VDIM + h)),
            pl.BlockSpec((S, C_VDIM), lambda b, h, i: (b, COL_KC // C_VDIM + h)),
            pl.BlockSpec((None, None, nkb, VT_ROWS, DIFF_KB), lambda b, h, i: (b, h, 0, 0, 0)),
            pl.BlockSpec((4, HEAD_DIM), lambda b, h, i: (0, 0)),
            pl.BlockSpec((C_VDIM, 1), lambda b, h, i: (0, 0)),
        ],
        out_specs=pl.BlockSpec((tq, C_VDIM), lambda b, h, i: (b * nq + i, h)),
        out_shape=jax.ShapeDtypeStruct((B * S, C_WIDTH), BF16),
        scratch_shapes=[pltpu.VMEM((2 * tq, C_VDIM), BF16),
                        pltpu.VMEM((2, DIFF_KB, 2 * tq), F32),
                        pltpu.VMEM((2, 1, 2 * tq), F32),
                        pltpu.VMEM((1, 2 * tq), F32),
                        pltpu.VMEM((VT_ROWS, 2 * tq), F32)],
        compiler_params=_cparams(("parallel", "parallel", "arbitrary")),
        name="diff",
    )(proj, proj, vt, lam_vecs, subln.reshape(C_VDIM, 1))


OUT_SUB = 256


def _out_kernel(oa_ref, ob_ref, oc_ref, w_ref, h_ref, g_ref, h1_ref, hn_ref, *, tm):
    subs = [slice(s * OUT_SUB, (s + 1) * OUT_SUB) for s in range(tm // OUT_SUB)]
    accs = []
    for rows in subs:
        acc = jnp.dot(oa_ref[rows, :], w_ref[0:A_WIDTH, :], preferred_element_type=F32)
        acc = acc + jnp.dot(ob_ref[rows, :], w_ref[A_WIDTH:A_WIDTH + B_WIDTH, :], preferred_element_type=F32)
        acc = acc + jnp.dot(oc_ref[rows, :], w_ref[A_WIDTH + B_WIDTH:, :], preferred_element_type=F32)
        accs.append(acc)
    for rows, acc in zip(subs, accs):
        h1 = h_ref[rows, :] + acc
        h1_ref[rows, :] = h1
        ms = jnp.mean(h1 * h1, axis=1, keepdims=True)
        hn = h1 * lax.rsqrt(ms + NORM_EPS) * g_ref[...]
        hn_ref[rows, :] = _pack_rows(hn)


def _out(oa, ob, oc, w, h, g, *, tm=512):
    T, D = h.shape
    return pl.pallas_call(
        functools.partial(_out_kernel, tm=tm),
        grid=(T // tm,),
        in_specs=[
            pl.BlockSpec((tm, A_WIDTH), lambda i: (i, 0)),
            pl.BlockSpec((tm, B_WIDTH), lambda i: (i, 0)),
            pl.BlockSpec((tm, C_WIDTH), lambda i: (i, 0)),
            pl.BlockSpec((D, D), lambda i: (0, 0)),
            pl.BlockSpec((tm, D), lambda i: (i, 0)),
            pl.BlockSpec((1, D), lambda i: (0, 0)),
        ],
        out_specs=[
            pl.BlockSpec((tm, D), lambda i: (i, 0)),
            pl.BlockSpec((tm, D_PACK), lambda i: (i, 0)),
        ],
        out_shape=[
            jax.ShapeDtypeStruct((T, D), F32),
            jax.ShapeDtypeStruct((T, D_PACK), U32),
        ],
        compiler_params=_cparams(("parallel",)),
        name="out",
    )(oa, ob, oc, w, h, g.reshape(1, D))


ROUTER_ROWS = 128
ROUTER_LO = 64
ROUTER_GROUP_ROW = 0
ROUTER_EXPERT_ROW = 8


def _router_kernel(x_ref, wt_ref, b_ref, e_ref, g_ref):
    x = _unpack_rows_bf16(x_ref[...])
    lg = lax.dot_general(wt_ref[...], x, (((1,), (1,)), ((), ())), preferred_element_type=F32)
    lg = lg[0:ROUTER_LO] + lg[ROUTER_LO:ROUTER_ROWS] + b_ref[...]
    grp = lg[ROUTER_GROUP_ROW:ROUTER_GROUP_ROW + N_GROUPS]
    ex = lg[ROUTER_EXPERT_ROW:ROUTER_EXPERT_ROW + N_EXPERTS]

    mg = jnp.max(grp, axis=0, keepdims=True)
    pg_sel = 1.0 / jnp.sum(jnp.exp(grp - mg), axis=0, keepdims=True)
    gi = lax.broadcasted_iota(jnp.int32, grp.shape, 0)
    g_sel = jnp.min(jnp.where(grp == mg, gi, N_GROUPS), axis=0, keepdims=True)

    er = lax.broadcasted_iota(jnp.int32, ex.shape, 0)
    group_of = lax.shift_right_logical(er, EXPERTS_PER_GROUP.bit_length() - 1)
    cand = jnp.where(group_of == g_sel, ex, NEG)
    v1 = jnp.max(cand, axis=0, keepdims=True)
    i1 = jnp.min(jnp.where(cand == v1, er, N_EXPERTS), axis=0, keepdims=True)
    rest = jnp.where(er == i1, NEG, cand)
    v2 = jnp.max(rest, axis=0, keepdims=True)
    i2 = jnp.min(jnp.where(rest == v2, er, N_EXPERTS), axis=0, keepdims=True)
    t = jnp.exp(v2 - v1)
    g1 = pg_sel / (1.0 + t)
    e_ref[...] = jnp.concatenate([i1, i2], axis=0)
    g_ref[...] = jnp.concatenate([g1, g1 * t], axis=0)


def _router(hn_packed, wt, b, *, tm=2048):
    T = hn_packed.shape[0]
    D = wt.shape[1]
    return pl.pallas_call(
        _router_kernel,
        grid=(T // tm,),
        in_specs=[
            pl.BlockSpec((tm, D_PACK), lambda i: (i, 0)),
            pl.BlockSpec((ROUTER_ROWS, D), lambda i: (0, 0)),
            pl.BlockSpec((ROUTER_LO, 1), lambda i: (0, 0)),
        ],
        out_specs=[pl.BlockSpec((TOP_K, tm), lambda i: (0, i)), pl.BlockSpec((TOP_K, tm), lambda i: (0, i))],
        out_shape=[jax.ShapeDtypeStruct((TOP_K, T), jnp.int32), jax.ShapeDtypeStruct((TOP_K, T), F32)],
        compiler_params=_cparams(("parallel",)),
        name="router",
    )(hn_packed, wt, b)


EXPERT_SUB = 256
EXPERT_VMEM = 56 * 1024 * 1024


def _expert_kernel(blk_ref, exp_ref, lo_ref, hi_ref, x_ref, g_ref, wg_ref, wu_ref, wd_ref, o_ref,
                   wg_bf, wu_bf, wd_bf, *, tm):
    p = pl.program_id(0)
    prev = jnp.maximum(p - 1, 0)
    blk = blk_ref[p]
    first = (p == 0) | (blk != blk_ref[prev])
    new_expert = (p == 0) | (exp_ref[p] != exp_ref[prev])
    lo = lo_ref[p]
    hi = hi_ref[p]

    @pl.when(new_expert)
    def _():
        wg_bf[...] = wg_ref[...].astype(BF16)
        wu_bf[...] = wu_ref[...].astype(BF16)
        wd_bf[...] = wd_ref[...].astype(BF16)

    @pl.when(first)
    def _():
        o_ref[...] = jnp.zeros_like(o_ref)

    for s in range(tm // EXPERT_SUB):
        row0 = blk * tm + s * EXPERT_SUB
        rows = slice(s * EXPERT_SUB, (s + 1) * EXPERT_SUB)

        @pl.when((hi > row0) & (lo < row0 + EXPERT_SUB))
        def _(row0=row0, rows=rows):
            x = _unpack_rows_bf16(x_ref[rows, :])
            a = jnp.dot(x, wg_bf[...], preferred_element_type=F32)
            u = jnp.dot(x, wu_bf[...], preferred_element_type=F32)
            act = (a / (1.0 + jnp.exp(-a))) * u
            y = jnp.dot(act.astype(BF16), wd_bf[...], preferred_element_type=F32)
            r = row0 + lax.broadcasted_iota(jnp.int32, (EXPERT_SUB, 1), 0)
            mine = (r >= lo) & (r < hi)
            o_ref[rows, :] = jnp.where(mine, _pack_rows(y * g_ref[rows, :]), o_ref[rows, :])


def _experts(seg_blk, seg_exp, seg_lo, seg_hi, xs, gates, wg, wu, wd, layer, *, tm):
    N = xs.shape[0]
    D = wg.shape[2]
    P = seg_blk.shape[0]
    grid_spec = pltpu.PrefetchScalarGridSpec(
        num_scalar_prefetch=4,
        grid=(P,),
        in_specs=[
            pl.BlockSpec((tm, D_PACK), lambda p, blk, ex, lo, hi: (blk[p], 0)),
            pl.BlockSpec((tm, 1), lambda p, blk, ex, lo, hi: (blk[p], 0)),
            pl.BlockSpec((None, None, D, D_FF), lambda p, blk, ex, lo, hi: (layer, ex[p], 0, 0)),
            pl.BlockSpec((None, None, D, D_FF), lambda p, blk, ex, lo, hi: (layer, ex[p], 0, 0)),
            pl.BlockSpec((None, None, D_FF, D), lambda p, blk, ex, lo, hi: (layer, ex[p], 0, 0)),
        ],
        out_specs=pl.BlockSpec((tm, D_PACK), lambda p, blk, ex, lo, hi: (blk[p], 0)),
        scratch_shapes=[pltpu.VMEM((D, D_FF), BF16), pltpu.VMEM((D, D_FF), BF16), pltpu.VMEM((D_FF, D), BF16)],
    )
    return pl.pallas_call(
        functools.partial(_expert_kernel, tm=tm),
        grid_spec=grid_spec,
        out_shape=jax.ShapeDtypeStruct((N, D_PACK), U32),
        compiler_params=_cparams(("arbitrary",), EXPERT_VMEM),
        name="experts",
    )(seg_blk, seg_exp, seg_lo, seg_hi, xs, gates, wg, wu, wd)


def _route(eidx_t, gates_t, T, tm):
    N = T * TOP_K
    flat_e = eidx_t.T.reshape(N)
    iota = jnp.arange(N, dtype=jnp.int32)
    sorted_e, order, g_sorted = lax.sort((flat_e, iota, gates_t.T.reshape(N)), num_keys=1, is_stable=True)
    tok = order // TOP_K
    _, pos = lax.sort((order, iota), num_keys=1)
    counts = jnp.bincount(flat_e, length=N_EXPERTS)
    starts = (jnp.cumsum(counts) - counts).astype(jnp.int32)
    nb = N // tm
    bounds = jnp.sort(jnp.concatenate([jnp.arange(nb, dtype=jnp.int32) * tm, starts]))
    ends = jnp.concatenate([bounds[1:], jnp.full((1,), N, bounds.dtype)])
    seg_blk = jnp.minimum(bounds // tm, nb - 1).astype(jnp.int32)
    seg_exp = sorted_e[jnp.minimum(bounds, N - 1)].astype(jnp.int32)
    pos_rows = pos.reshape(T, TOP_K).T.reshape(N)
    return tok, g_sorted, pos_rows, seg_blk, seg_exp, bounds, ends


def _group_major(a, axis):
    shape = a.shape
    a = a.reshape(shape[:axis] + (A_KV_HEADS, A_GROUP, HEAD_DIM) + shape[axis + 1:])
    return jnp.swapaxes(a, axis, axis + 1).reshape(shape)


def _permute_in_proj(w):
    bounds = np.cumsum([0, 768, 256, 256, 512, 768, 768, 768])
    qa, ka, va, ub, qc, kc, vc = [w[:, bounds[s]:bounds[s + 1]] for s in range(7)]
    return jnp.concatenate([_group_major(qa, 1), qc, ka, kc, ub, va, vc], axis=1)


def kernel(x, positions, attn_norm, w_in, sinks, branch_norm_a, w_pool, pool_scale, lambda_q1, lambda_k1,
           lambda_q2, lambda_k2, subln, w_out, ffn_norm, w_router_group, b_router_group, w_router_expert,
           b_router_expert, w_expert_gate, w_expert_up, w_expert_down, final_norm):
    B, S, D = x.shape
    T = B * S
    depth = w_in.shape[0]
    tm_e = 1024

    half = HEAD_DIM // 2
    inv = ROPE_THETA ** (-jnp.arange(half, dtype=F32) / half)
    ang = positions.astype(F32).reshape(T, 1) * inv
    cos, sin = jnp.cos(ang), jnp.sin(ang)
    cos_t = jnp.concatenate([cos, cos, cos, cos], axis=1)
    sin_t = jnp.concatenate([-sin, sin, -sin, sin], axis=1)

    h = x.reshape(T, D)
    y_rows = None
    for l in range(depth):
        if l == 0:
            (hn,) = _norm(h, None, attn_norm[l], emit_h=False, emit_bf16=True)
        else:
            h, hn = _norm(h, y_rows, attn_norm[l], emit_h=True, emit_bf16=True)
        w_in_l = _permute_in_proj(w_in[l]).astype(BF16)
        proj, vat, vt = _proj(hn, w_in_l[:, :COL_VA], w_in_l[:, COL_VA:].T, cos_t, sin_t, B=B, S=S)

        sink_heads = sinks[l].reshape(A_KV_HEADS, A_GROUP).T.reshape(A_HEADS) * LOG2E
        sink_row = jnp.repeat(sink_heads, WINDOW).reshape(1, A_COLS)
        oa = _swa(proj, vat, sink_row, _group_major(branch_norm_a[l], 0), B=B, S=S)
        ob = _pool(proj, w_pool[l].astype(BF16), pool_scale[l], B=B, S=S)

        lam_init = 0.8 - 0.6 * math.exp(-0.3 * l)
        lam_vecs = jnp.stack([lambda_q1[l], lambda_k1[l], lambda_q2[l], lambda_k2[l]]).astype(F32)
        oc = _diff(proj, vt, lam_vecs, subln[l], B=B, S=S, lam_init=lam_init)

        w_out_l = jnp.concatenate([_group_major(w_out[l][:A_WIDTH], 0), w_out[l][A_WIDTH:]], axis=0)
        h1, hn2 = _out(oa, ob, oc, w_out_l.astype(BF16), h, ffn_norm[l])

        wr = jnp.zeros((ROUTER_LO, D), F32)
        wr = wr.at[ROUTER_GROUP_ROW:ROUTER_GROUP_ROW + N_GROUPS].set(w_router_group[l].T)
        wr = wr.at[ROUTER_EXPERT_ROW:ROUTER_EXPERT_ROW + N_EXPERTS].set(w_router_expert[l].T)
        wr_hi = wr.astype(BF16)
        wr_lo = (wr - wr_hi.astype(F32)).astype(BF16)
        br = jnp.zeros((ROUTER_LO, 1), F32)
        br = br.at[ROUTER_GROUP_ROW:ROUTER_GROUP_ROW + N_GROUPS, 0].set(b_router_group[l])
        br = br.at[ROUTER_EXPERT_ROW:ROUTER_EXPERT_ROW + N_EXPERTS, 0].set(b_router_expert[l])
        eidx_t, gates_t = _router(hn2, jnp.concatenate([wr_hi, wr_lo], axis=0), br)

        tok, g_sorted, pos_rows, seg_blk, seg_exp, seg_lo, seg_hi = _route(eidx_t, gates_t, T, tm_e)
        xs = _sc_gather_rows(hn2, tok)
        ys = _experts(seg_blk, seg_exp, seg_lo, seg_hi, xs, g_sorted.reshape(-1, 1),
                      w_expert_gate, w_expert_up, w_expert_down, l, tm=tm_e)
        y_rows = _sc_gather_rows(ys, pos_rows)
        h = h1

    (out,) = _norm(h, y_rows, final_norm, emit_h=False, emit_bf16=False)
    return out.reshape(B, S, D)
```

```python
import functools
import math

import jax
import jax.numpy as jnp
from jax import lax
from jax.experimental import pallas as pl
from jax.experimental.pallas import tpu as pltpu
from jax.experimental.pallas import tpu_sc as plsc
import numpy as np

F32 = jnp.float32
BF16 = jnp.bfloat16

D_MODEL = 2048
HEAD_DIM = 64
ROPE_THETA = 10000.0
A_HEADS = 12
A_KV_HEADS = 4
A_GROUP = 3
A_WIDTH = 768
WINDOW = 128
POOL_WINDOWS = (2, 4, 8, 16)
B_WIDTH = 512
B_GROUP_DIM = 128
C_VDIM = 128
C_WIDTH = 768
C_HEADS = 6
DIFF_EPS = 1e-5
IN_COLS = 4096
N_GROUPS = 4
EXPERTS_PER_GROUP = 8
N_EXPERTS = 32
TOP_K = 2
D_FF = 512
NORM_EPS = 1e-6
NEG = -1e30
LOG2E = math.log2(math.e)
Q_SCALE = HEAD_DIM ** -0.5 * LOG2E

COL_QA, COL_QC, COL_KA, COL_KC, COL_UB, COL_VA, COL_VC = 0, 768, 1536, 1792, 2560, 3072, 3328
PROJ_TN = 512
A_KV_WIDTH = A_KV_HEADS * HEAD_DIM
PROJ_SUB = 256
N_SCALED_TILES = COL_KA // PROJ_TN
N_ROPE_TILES = COL_UB // PROJ_TN
N_MAIN_TILES = COL_VA // PROJ_TN
DIFF_KB = 512
VT_ROWS = C_VDIM + 16

VMEM_LIMIT = 48 * 1024 * 1024
D_PACK = D_MODEL // 2
U32 = jnp.uint32


def _cparams(sem, vmem=VMEM_LIMIT):
    return pltpu.CompilerParams(dimension_semantics=sem, vmem_limit_bytes=vmem)


def _pack_rows(x):
    lo = pltpu.bitcast(x[:, :D_PACK].astype(BF16).astype(F32), U32) >> 16
    hi = pltpu.bitcast(x[:, D_PACK:].astype(BF16).astype(F32), U32)
    return hi | lo


def _unpack_rows(w):
    return pltpu.bitcast(w << 16, F32), pltpu.bitcast(w & U32(0xFFFF0000), F32)


def _unpack_rows_bf16(w):
    lo, hi = _unpack_rows(w)
    return jnp.concatenate([lo.astype(BF16), hi.astype(BF16)], axis=1)


SC_WINDOW = 32
SC_INDEX_LANES = 128


def _sc_gather_rows(x, idx):
    M = idx.shape[0]
    W = x.shape[1]
    idx2 = jnp.pad(idx.reshape(M // SC_WINDOW, SC_WINDOW), ((0, 0), (0, SC_INDEX_LANES - SC_WINDOW)))
    mesh = plsc.VectorSubcoreMesh(core_axis_name="core", subcore_axis_name="subcore")

    @pl.kernel(out_type=jax.ShapeDtypeStruct((M, W), x.dtype), mesh=mesh)
    def gather(x_hbm, i_hbm, o_hbm):
        def body(i_vmem, o_vmem):
            pltpu.sync_copy(x_hbm.at[i_vmem.at[0, pl.ds(0, SC_WINDOW)]], o_vmem)

        pltpu.emit_pipeline(
            body,
            grid=(M // SC_WINDOW,),
            in_specs=[pl.BlockSpec((1, SC_INDEX_LANES), index_map=lambda i: (i, 0))],
            out_specs=[pl.BlockSpec((SC_WINDOW, W), index_map=lambda i: (i, 0))],
            core_axis_name=("core", "subcore"),
            dimension_semantics=(pltpu.PARALLEL,),
        )(i_hbm, o_hbm)

    return gather(x, idx2)


def _norm_kernel(*refs, n_add, emit_h, emit_bf16, eps):
    h_ref = refs[0]
    add_refs = refs[1:1 + n_add]
    g_ref = refs[1 + n_add]
    outs = refs[2 + n_add:]
    h = h_ref[...]
    if n_add:
        lo = jnp.zeros((h.shape[0], D_PACK), F32)
        hi = jnp.zeros((h.shape[0], D_PACK), F32)
        for r in add_refs:
            a, b = _unpack_rows(r[...])
            lo, hi = lo + a, hi + b
        h = h + jnp.concatenate([lo, hi], axis=1)
    ms = jnp.mean(h * h, axis=-1, keepdims=True)
    y = h * lax.rsqrt(ms + eps) * g_ref[...]
    k = 0
    if emit_h:
        outs[k][...] = h
        k += 1
    outs[k][...] = y.astype(BF16 if emit_bf16 else F32)


def _norm(h, y_rows, g, *, emit_h, emit_bf16, tm=512):
    T, D = h.shape
    nt = T // tm
    row = pl.BlockSpec((tm, D), lambda i: (i, 0))
    n_add = 0 if y_rows is None else TOP_K
    add_specs = [pl.BlockSpec((tm, D_PACK), lambda i, k=k: (i + k * nt, 0)) for k in range(n_add)]
    out_shape, out_specs = [], []
    if emit_h:
        out_shape.append(jax.ShapeDtypeStruct((T, D), F32))
        out_specs.append(row)
    out_shape.append(jax.ShapeDtypeStruct((T, D), BF16 if emit_bf16 else F32))
    out_specs.append(row)
    return pl.pallas_call(
        functools.partial(_norm_kernel, n_add=n_add, emit_h=emit_h, emit_bf16=emit_bf16, eps=NORM_EPS),
        grid=(nt,),
        in_specs=[row] + add_specs + [pl.BlockSpec((1, D), lambda i: (0, 0))],
        out_specs=out_specs,
        out_shape=out_shape,
        compiler_params=_cparams(("parallel",)),
        name="norm",
    )(h, *([y_rows] * n_add), g.reshape(1, D))


def _proj_kernel(x_ref, w_ref, wt_ref, cos_ref, sin_ref, o_ref, vat_ref, vt_ref, *, tm):
    j = pl.program_id(1)
    subs = [slice(s * PROJ_SUB, (s + 1) * PROJ_SUB) for s in range(tm // PROJ_SUB)]

    def matmuls():
        return [jnp.dot(x_ref[rows, :], w_ref[...], preferred_element_type=F32) for rows in subs]

    @pl.when(j < N_ROPE_TILES)
    def _():
        accs = matmuls()
        scale = jnp.where(j < N_SCALED_TILES, Q_SCALE, 1.0).astype(F32)
        lane = lax.broadcasted_iota(jnp.int32, (PROJ_SUB, PROJ_TN), 1)
        first_half = (lane & (HEAD_DIM // 2)) == 0
        for rows, acc in zip(subs, accs):
            c = jnp.concatenate([cos_ref[rows, :] * scale] * (PROJ_TN // 128), axis=1)
            s = jnp.concatenate([sin_ref[rows, :] * scale] * (PROJ_TN // 128), axis=1)
            swapped = jnp.where(first_half,
                                pltpu.roll(acc, PROJ_TN - HEAD_DIM // 2, 1),
                                pltpu.roll(acc, HEAD_DIM // 2, 1))
            o_ref[rows, :] = (acc * c + swapped * s).astype(o_ref.dtype)

    @pl.when((j >= N_ROPE_TILES) & (j < N_MAIN_TILES))
    def _():
        for rows, acc in zip(subs, matmuls()):
            o_ref[rows, :] = acc.astype(o_ref.dtype)

    per_sub = PROJ_SUB // WINDOW
    per_kb = DIFF_KB // PROJ_SUB
    for t in range((IN_COLS - COL_VA) // PROJ_TN):
        @pl.when(j == N_MAIN_TILES + t)
        def _(t=t):
            accs = [lax.dot_general(wt_ref[...], x_ref[rows, :], (((1,), (1,)), ((), ())),
                                    preferred_element_type=F32) for rows in subs]
            for s, acc in enumerate(accs):
                cols = slice((s % per_kb) * PROJ_SUB, (s % per_kb + 1) * PROJ_SUB)
                for r0 in range(0, PROJ_TN, C_VDIM):
                    row = t * PROJ_TN + r0
                    if row == 0:
                        for b in range(per_sub):
                            vat_ref[s * per_sub + b] = acc[0:A_KV_WIDTH, b * WINDOW:(b + 1) * WINDOW].astype(
                                vat_ref.dtype)
                    elif row >= A_KV_WIDTH:
                        head = (row - A_KV_WIDTH) // C_VDIM
                        vt_ref[head, s // per_kb, 0:C_VDIM, cols] = acc[r0:r0 + C_VDIM, :].astype(vt_ref.dtype)

    @pl.when(j == N_MAIN_TILES)
    def _():
        row = lax.broadcasted_iota(jnp.int32, (VT_ROWS - C_VDIM, DIFF_KB), 0)
        extra = jnp.where(row == 0, 1.0, 0.0).astype(vt_ref.dtype)
        for head in range(C_HEADS):
            for kb in range(tm // DIFF_KB):
                vt_ref[head, kb, C_VDIM:VT_ROWS, :] = extra


def _proj(hn, w, wt, cos_t, sin_t, *, B, S, tm=2048):
    T, D = hn.shape
    spt = S // tm
    last_main = N_MAIN_TILES - 1
    return pl.pallas_call(
        functools.partial(_proj_kernel, tm=tm),
        grid=(T // tm, IN_COLS // PROJ_TN),
        in_specs=[
            pl.BlockSpec((tm, D), lambda i, j: (i, 0)),
            pl.BlockSpec((D, PROJ_TN), lambda i, j: (0, jnp.minimum(j, last_main))),
            pl.BlockSpec((PROJ_TN, D), lambda i, j: (jnp.maximum(j - N_MAIN_TILES, 0), 0)),
            pl.BlockSpec((tm, 128), lambda i, j: (i, 0)),
            pl.BlockSpec((tm, 128), lambda i, j: (i, 0)),
        ],
        out_specs=[
            pl.BlockSpec((tm, PROJ_TN), lambda i, j: (i, jnp.minimum(j, last_main))),
            pl.BlockSpec((tm // WINDOW, A_KV_WIDTH, WINDOW), lambda i, j: (i, 0, 0)),
            pl.BlockSpec((None, C_HEADS, tm // DIFF_KB, VT_ROWS, DIFF_KB),
                         lambda i, j: (i // spt, 0, i % spt, 0, 0)),
        ],
        out_shape=[
            jax.ShapeDtypeStruct((T, COL_VA), BF16),
            jax.ShapeDtypeStruct((T // WINDOW, A_KV_WIDTH, WINDOW), BF16),
            jax.ShapeDtypeStruct((B, C_HEADS, S // DIFF_KB, VT_ROWS, DIFF_KB), BF16),
        ],
        compiler_params=_cparams(("parallel", "arbitrary")),
        name="proj",
    )(hn, w, wt, cos_t, sin_t)


A_COLS = A_HEADS * WINDOW


def _swa_kernel(q_ref, kc_ref, kp_ref, vc_ref, vp_ref, sink_ref, g_ref, o_ref, *, tq):
    i = pl.program_id(1)
    nblk = tq // WINDOW
    kext = jnp.concatenate([kp_ref[...], kc_ref[...]], axis=0)
    lane_kv = lax.broadcasted_iota(jnp.int32, (WINDOW, A_KV_WIDTH), 1) // HEAD_DIM
    zero = jnp.zeros((WINDOW, A_KV_WIDTH), BF16)

    scores = []
    for b in range(nblk):
        parts = []
        for g in range(A_GROUP):
            qg = q_ref[b * WINDOW:(b + 1) * WINDOW, g * A_KV_WIDTH:(g + 1) * A_KV_WIDTH]
            parts += [jnp.where(lane_kv == j, qg, zero) for j in range(A_KV_HEADS)]
        qcat = jnp.concatenate(parts, axis=0)
        scores.append(lax.dot_general(kext[b * WINDOW:(b + 2) * WINDOW, :], qcat, (((1,), (1,)), ((), ())),
                                      preferred_element_type=F32))

    r = lax.broadcasted_iota(jnp.int32, (2 * WINDOW, A_COLS), 0)
    c = lax.broadcasted_iota(jnp.int32, (2 * WINDOW, A_COLS), 1) & (WINDOW - 1)
    band = (r > c) & (r <= c + WINDOW)
    sink = sink_ref[...]
    gain = g_ref[...]
    for b, st in enumerate(scores):
        mask = band
        if b == 0:
            mask = band & ((r >= WINDOW) | (i > 0))
        st = jnp.where(mask, st, NEG)
        m = jnp.maximum(jnp.max(st, axis=0, keepdims=True), sink)
        p = jnp.exp2(st - m)
        l = jnp.sum(p, axis=0, keepdims=True) + jnp.exp2(sink - m)
        vt = jnp.concatenate([vp_ref[0] if b == 0 else vc_ref[b - 1], vc_ref[b]], axis=1)
        pv = jnp.dot(vt, p.astype(BF16), preferred_element_type=F32)
        inv_l = 1.0 / l
        heads = []
        for g in range(A_GROUP):
            for j in range(A_KV_HEADS):
                cols = slice((g * A_KV_HEADS + j) * WINDOW, (g * A_KV_HEADS + j + 1) * WINDOW)
                heads.append(pv[j * HEAD_DIM:(j + 1) * HEAD_DIM, cols] * inv_l[:, cols])
        ot = jnp.concatenate(heads, axis=0)
        ms = jnp.mean(ot * ot, axis=0, keepdims=True)
        y = (ot * lax.rsqrt(ms + NORM_EPS)).T * gain
        o_ref[b * WINDOW:(b + 1) * WINDOW, :] = y.astype(o_ref.dtype)


def _swa(proj, vat, sink_row, g, *, B, S, tq=1024):
    nq = S // tq
    rpb = tq // WINDOW

    def prev_blk(b, i):
        return jnp.maximum(b * (S // WINDOW) + i * rpb - 1, 0)

    return pl.pallas_call(
        functools.partial(_swa_kernel, tq=tq),
        grid=(B, nq),
        in_specs=[
            pl.BlockSpec((tq, A_WIDTH), lambda b, i: (b * nq + i, COL_QA // A_WIDTH)),
            pl.BlockSpec((tq, A_KV_WIDTH), lambda b, i: (b * nq + i, COL_KA // A_KV_WIDTH)),
            pl.BlockSpec((WINDOW, A_KV_WIDTH), lambda b, i: (prev_blk(b, i), COL_KA // A_KV_WIDTH)),
            pl.BlockSpec((rpb, A_KV_WIDTH, WINDOW), lambda b, i: (b * nq + i, 0, 0)),
            pl.BlockSpec((1, A_KV_WIDTH, WINDOW), lambda b, i: (prev_blk(b, i), 0, 0)),
            pl.BlockSpec((1, A_COLS), lambda b, i: (0, 0)),
            pl.BlockSpec((1, A_WIDTH), lambda b, i: (0, 0)),
        ],
        out_specs=pl.BlockSpec((tq, A_WIDTH), lambda b, i: (b * nq + i, 0)),
        out_shape=jax.ShapeDtypeStruct((B * S, A_WIDTH), BF16),
        compiler_params=_cparams(("parallel", "parallel")),
        name="swa",
    )(proj, proj, proj, vat, vat, sink_row, g.reshape(1, A_WIDTH))


POOL_HALO = 16


def _pool_kernel(u0_ref, u1_ref, h0_ref, h1_ref, w_ref, s_ref, o_ref, *, tq):
    i = pl.program_id(1)
    u = jnp.concatenate([u0_ref[...], u1_ref[...]], axis=1).astype(F32)
    halo = jnp.concatenate([h0_ref[...], h1_ref[...]], axis=1).astype(F32)
    halo = halo * (i > 0).astype(F32)
    ext = jnp.concatenate([halo, u], axis=0)
    t = i * tq + lax.broadcasted_iota(jnp.int32, (tq, 1), 0)
    for gi, w in enumerate(POOL_WINDOWS):
        sl = slice(gi * B_GROUP_DIM, (gi + 1) * B_GROUP_DIM)
        s = ext[:, sl]
        shift = 1
        while shift < w:
            s = s + pltpu.roll(s, shift, 0)
            shift *= 2
        cnt = jnp.minimum(t + 1, w).astype(F32)
        y = (s[POOL_HALO:, :] / cnt - u[:, sl]).astype(BF16)
        o = jnp.dot(y, w_ref[gi], preferred_element_type=F32) * s_ref[:, sl]
        o_ref[:, sl] = o.astype(o_ref.dtype)


def _pool(proj, w_pool, scale, *, B, S, tq=1024):
    nq = S // tq
    c0 = COL_UB // 256

    def cur(c):
        return pl.BlockSpec((tq, 256), lambda b, i: (b * nq + i, c))

    def halo(c):
        return pl.BlockSpec(
            (POOL_HALO, 256),
            lambda b, i: (jnp.maximum((b * S + i * tq) // POOL_HALO - 1, 0), c))

    return pl.pallas_call(
        functools.partial(_pool_kernel, tq=tq),
        grid=(B, nq),
        in_specs=[cur(c0), cur(c0 + 1), halo(c0), halo(c0 + 1),
                  pl.BlockSpec((len(POOL_WINDOWS), B_GROUP_DIM, B_GROUP_DIM), lambda b, i: (0, 0, 0)),
                  pl.BlockSpec((1, B_WIDTH), lambda b, i: (0, 0))],
        out_specs=pl.BlockSpec((tq, B_WIDTH), lambda b, i: (b * nq + i, 0)),
        out_shape=jax.ShapeDtypeStruct((B * S, B_WIDTH), BF16),
        compiler_params=_cparams(("parallel", "parallel")),
        name="pool",
    )(proj, proj, proj, proj, w_pool, scale.reshape(1, B_WIDTH))


DIFF_UNROLL = 4


def _diff_kernel(q_ref, k_ref, vt_ref, lam_ref, g_ref, o_ref, qcat_ref, s_ref, bm_ref, m_ref, acc_ref,
                 *, tq, lam_init):
    qi = pl.program_id(2)
    n_diag = tq // DIFF_KB
    n_below = qi * n_diag
    q = q_ref[...]
    lane = lax.broadcasted_iota(jnp.int32, q.shape, 1)
    zero = jnp.zeros_like(q)
    qcat_ref[0:tq, :] = jnp.where(lane < HEAD_DIM, q, zero)
    qcat_ref[tq:2 * tq, :] = jnp.where(lane >= HEAD_DIM, q, zero)
    m_ref[...] = jnp.full(m_ref.shape, NEG, F32)
    acc_ref[...] = jnp.zeros(acc_ref.shape, F32)

    def produce(kidx, q0=0):
        off = pl.multiple_of(kidx * DIFF_KB, DIFF_KB)
        queries = qcat_ref[...] if q0 == 0 else jnp.concatenate(
            [qcat_ref[q0:tq, :], qcat_ref[tq + q0:2 * tq, :]], axis=0)
        st = lax.dot_general(k_ref[pl.ds(off, DIFF_KB), :], queries, (((1,), (1,)), ((), ())),
                             preferred_element_type=F32)
        return st, jnp.max(st, axis=0, keepdims=True)

    def store(buf, block):
        width = block[0].shape[1]
        s_ref[buf, :, 0:width], bm_ref[buf, :, 0:width] = block

    def consume(buf, kidx, first_key):
        q0 = first_key or 0
        nq_act = tq - q0
        st = s_ref[buf, :, 0:2 * nq_act]
        if first_key is not None:
            r = lax.broadcasted_iota(jnp.int32, st.shape, 0)
            c = lax.broadcasted_iota(jnp.int32, st.shape, 1)
            c = jnp.where(c >= nq_act, c - nq_act, c)
            st = jnp.where(r <= c, st, NEG)
            block_max = jnp.max(st, axis=0, keepdims=True)
        else:
            block_max = bm_ref[buf]
        maps = (slice(q0, tq), slice(tq + q0, 2 * tq))
        m = m_ref[...] if q0 == 0 else jnp.concatenate([m_ref[:, lanes] for lanes in maps], axis=1)
        m_new = jnp.maximum(m, block_max)
        p = jnp.exp2(st - m_new)
        pv = jnp.dot(vt_ref[kidx], p.astype(BF16), preferred_element_type=F32)
        alpha = jnp.exp2(m - m_new)
        if q0 == 0:
            m_ref[...] = m_new
            acc_ref[...] = alpha * acc_ref[...] + pv
        else:
            for c, lanes in enumerate(maps):
                part = slice(c * nq_act, (c + 1) * nq_act)
                m_ref[:, lanes] = m_new[:, part]
                acc_ref[:, lanes] = alpha[:, part] * acc_ref[:, lanes] + pv[:, part]

    def steps(first, n_plain, n_masked, last):
        n = n_plain + n_masked
        first_keys = [None] * n_plain + [u * DIFF_KB for u in range(n_masked)]
        for u in range(n):
            more = not (last and u == n - 1)
            if more:
                nxt = produce(first + u + 1, (first_keys[u + 1] or 0) if u + 1 < n else 0)
            consume(u % 2, first + u, first_keys[u])
            if more:
                store(1 - u % 2, nxt)

    store(0, produce(0))

    def body(c, carry):
        steps(c * DIFF_UNROLL, DIFF_UNROLL, 0, False)
        return carry

    lax.fori_loop(0, n_below // DIFF_UNROLL, body, 0)
    for rem in range(0, DIFF_UNROLL, math.gcd(n_diag, DIFF_UNROLL)):
        @pl.when(n_below % DIFF_UNROLL == rem)
        def _(rem=rem):
            steps(n_below - rem, rem, n_diag, True)

    lv = lam_ref[...]
    lam = (jnp.exp(jnp.sum(lv[0:1] * lv[1:2], axis=1, keepdims=True))
           - jnp.exp(jnp.sum(lv[2:3] * lv[3:4], axis=1, keepdims=True)) + lam_init)
    l = acc_ref[C_VDIM:C_VDIM + 1, :]
    acc = acc_ref[0:C_VDIM, :]
    ot = acc[:, :tq] / l[:, :tq] - lam * (acc[:, tq:] / l[:, tq:])
    ms = jnp.mean(ot * ot, axis=0, keepdims=True)
    yt = ot * lax.rsqrt(ms + DIFF_EPS) * (g_ref[...] * (1.0 - lam_init))
    o_ref[...] = yt.T.astype(o_ref.dtype)


def _diff(proj, vt, lam_vecs, subln, *, B, S, lam_init, tq=2048):
    nq = S // tq
    nkb = S // DIFF_KB
    assert tq % DIFF_KB == 0 and DIFF_UNROLL % 2 == 0
    return pl.pallas_call(
        functools.partial(_diff_kernel, tq=tq, lam_init=lam_init),
        grid=(B, C_HEADS, nq),
        in_specs=[
            pl.BlockSpec((tq, C_VDIM), lambda b, h, i: (b * nq + i, COL_QC // C_VDIM + h)),
            pl.BlockSpec((S, C_VDIM), lambda b, h, i: (b, COL_KC // C_VDIM + h)),
            pl.BlockSpec((None, None, nkb, VT_ROWS, DIFF_KB), lambda b, h, i: (b, h, 0, 0, 0)),
            pl.BlockSpec((4, HEAD_DIM), lambda b, h, i: (0, 0)),
            pl.BlockSpec((C_VDIM, 1), lambda b, h, i: (0, 0)),
        ],
        out_specs=pl.BlockSpec((tq, C_VDIM), lambda b, h, i: (b * nq + i, h)),
        out_shape=jax.ShapeDtypeStruct((B * S, C_WIDTH), BF16),
        scratch_shapes=[pltpu.VMEM((2 * tq, C_VDIM), BF16),
                        pltpu.VMEM((2, DIFF_KB, 2 * tq), F32),
                        pltpu.VMEM((2, 1, 2 * tq), F32),
                        pltpu.VMEM((1, 2 * tq), F32),
                        pltpu.VMEM((VT_ROWS, 2 * tq), F32)],
        compiler_params=_cparams(("parallel", "parallel", "arbitrary")),
        name="diff",
    )(proj, proj, vt, lam_vecs, subln.reshape(C_VDIM, 1))


OUT_SUB = 256


def _out_kernel(oa_ref, ob_ref, oc_ref, w_ref, h_ref, g_ref, h1_ref, hn_ref, *, tm):
    subs = [slice(s * OUT_SUB, (s + 1) * OUT_SUB) for s in range(tm // OUT_SUB)]
    accs = []
    for rows in subs:
        acc = jnp.dot(oa_ref[rows, :], w_ref[0:A_WIDTH, :], preferred_element_type=F32)
        acc = acc + jnp.dot(ob_ref[rows, :], w_ref[A_WIDTH:A_WIDTH + B_WIDTH, :], preferred_element_type=F32)
        acc = acc + jnp.dot(oc_ref[rows, :], w_ref[A_WIDTH + B_WIDTH:, :], preferred_element_type=F32)
        accs.append(acc)
    for rows, acc in zip(subs, accs):
        h1 = h_ref[rows, :] + acc
        h1_ref[rows, :] = h1
        ms = jnp.mean(h1 * h1, axis=1, keepdims=True)
        hn = h1 * lax.rsqrt(ms + NORM_EPS) * g_ref[...]
        hn_ref[rows, :] = _pack_rows(hn)


def _out(oa, ob, oc, w, h, g, *, tm=512):
    T, D = h.shape
    return pl.pallas_call(
        functools.partial(_out_kernel, tm=tm),
        grid=(T // tm,),
        in_specs=[
            pl.BlockSpec((tm, A_WIDTH), lambda i: (i, 0)),
            pl.BlockSpec((tm, B_WIDTH), lambda i: (i, 0)),
            pl.BlockSpec((tm, C_WIDTH), lambda i: (i, 0)),
            pl.BlockSpec((D, D), lambda i: (0, 0)),
            pl.BlockSpec((tm, D), lambda i: (i, 0)),
            pl.BlockSpec((1, D), lambda i: (0, 0)),
        ],
        out_specs=[
            pl.BlockSpec((tm, D), lambda i: (i, 0)),
            pl.BlockSpec((tm, D_PACK), lambda i: (i, 0)),
        ],
        out_shape=[
            jax.ShapeDtypeStruct((T, D), F32),
            jax.ShapeDtypeStruct((T, D_PACK), U32),
        ],
        compiler_params=_cparams(("parallel",)),
        name="out",
    )(oa, ob, oc, w, h, g.reshape(1, D))


ROUTER_ROWS = 128
ROUTER_LO = 64
ROUTER_GROUP_ROW = 0
ROUTER_EXPERT_ROW = 8


def _router_kernel(x_ref, wt_ref, b_ref, e_ref, g_ref):
    x = _unpack_rows_bf16(x_ref[...])
    lg = lax.dot_general(wt_ref[...], x, (((1,), (1,)), ((), ())), preferred_element_type=F32)
    lg = lg[0:ROUTER_LO] + lg[ROUTER_LO:ROUTER_ROWS] + b_ref[...]
    grp = lg[ROUTER_GROUP_ROW:ROUTER_GROUP_ROW + N_GROUPS]
    ex = lg[ROUTER_EXPERT_ROW:ROUTER_EXPERT_ROW + N_EXPERTS]

    mg = jnp.max(grp, axis=0, keepdims=True)
    pg_sel = 1.0 / jnp.sum(jnp.exp(grp - mg), axis=0, keepdims=True)
    gi = lax.broadcasted_iota(jnp.int32, grp.shape, 0)
    g_sel = jnp.min(jnp.where(grp == mg, gi, N_GROUPS), axis=0, keepdims=True)

    er = lax.broadcasted_iota(jnp.int32, ex.shape, 0)
    group_of = lax.shift_right_logical(er, EXPERTS_PER_GROUP.bit_length() - 1)
    cand = jnp.where(group_of == g_sel, ex, NEG)
    v1 = jnp.max(cand, axis=0, keepdims=True)
    i1 = jnp.min(jnp.where(cand == v1, er, N_EXPERTS), axis=0, keepdims=True)
    rest = jnp.where(er == i1, NEG, cand)
    v2 = jnp.max(rest, axis=0, keepdims=True)
    i2 = jnp.min(jnp.where(rest == v2, er, N_EXPERTS), axis=0, keepdims=True)
    t = jnp.exp(v2 - v1)
    g1 = pg_sel / (1.0 + t)
    e_ref[...] = jnp.concatenate([i1, i2], axis=0)
    g_ref[...] = jnp.concatenate([g1, g1 * t], axis=0)


def _router(hn_packed, wt, b, *, tm=2048):
    T = hn_packed.shape[0]
    D = wt.shape[1]
    return pl.pallas_call(
        _router_kernel,
        grid=(T // tm,),
        in_specs=[
            pl.BlockSpec((tm, D_PACK), lambda i: (i, 0)),
            pl.BlockSpec((ROUTER_ROWS, D), lambda i: (0, 0)),
            pl.BlockSpec((ROUTER_LO, 1), lambda i: (0, 0)),
        ],
        out_specs=[pl.BlockSpec((TOP_K, tm), lambda i: (0, i)), pl.BlockSpec((TOP_K, tm), lambda i: (0, i))],
        out_shape=[jax.ShapeDtypeStruct((TOP_K, T), jnp.int32), jax.ShapeDtypeStruct((TOP_K, T), F32)],
        compiler_params=_cparams(("parallel",)),
        name="router",
    )(hn_packed, wt, b)


EXPERT_SUB = 256
EXPERT_VMEM = 56 * 1024 * 1024


def _expert_kernel(blk_ref, exp_ref, lo_ref, hi_ref, x_ref, g_ref, wg_ref, wu_ref, wd_ref, o_ref,
                   wg_bf, wu_bf, wd_bf, *, tm):
    p = pl.program_id(0)
    prev = jnp.maximum(p - 1, 0)
    blk = blk_ref[p]
    first = (p == 0) | (blk != blk_ref[prev])
    new_expert = (p == 0) | (exp_ref[p] != exp_ref[prev])
    lo = lo_ref[p]
    hi = hi_ref[p]

    @pl.when(new_expert)
    def _():
        wg_bf[...] = wg_ref[...].astype(BF16)
        wu_bf[...] = wu_ref[...].astype(BF16)
        wd_bf[...] = wd_ref[...].astype(BF16)

    @pl.when(first)
    def _():
        o_ref[...] = jnp.zeros_like(o_ref)

    for s in range(tm // EXPERT_SUB):
        row0 = blk * tm + s * EXPERT_SUB
        rows = slice(s * EXPERT_SUB, (s + 1) * EXPERT_SUB)

        @pl.when((hi > row0) & (lo < row0 + EXPERT_SUB))
        def _(row0=row0, rows=rows):
            x = _unpack_rows_bf16(x_ref[rows, :])
            a = jnp.dot(x, wg_bf[...], preferred_element_type=F32)
            u = jnp.dot(x, wu_bf[...], preferred_element_type=F32)
            act = (a / (1.0 + jnp.exp(-a))) * u
            y = jnp.dot(act.astype(BF16), wd_bf[...], preferred_element_type=F32)
            r = row0 + lax.broadcasted_iota(jnp.int32, (EXPERT_SUB, 1), 0)
            mine = (r >= lo) & (r < hi)
            o_ref[rows, :] = jnp.where(mine, _pack_rows(y * g_ref[rows, :]), o_ref[rows, :])


def _experts(seg_blk, seg_exp, seg_lo, seg_hi, xs, gates, wg, wu, wd, layer, *, tm):
    N = xs.shape[0]
    D = wg.shape[2]
    P = seg_blk.shape[0]
    grid_spec = pltpu.PrefetchScalarGridSpec(
        num_scalar_prefetch=4,
        grid=(P,),
        in_specs=[
            pl.BlockSpec((tm, D_PACK), lambda p, blk, ex, lo, hi: (blk[p], 0)),
            pl.BlockSpec((tm, 1), lambda p, blk, ex, lo, hi: (blk[p], 0)),
            pl.BlockSpec((None, None, D, D_FF), lambda p, blk, ex, lo, hi: (layer, ex[p], 0, 0)),
            pl.BlockSpec((None, None, D, D_FF), lambda p, blk, ex, lo, hi: (layer, ex[p], 0, 0)),
            pl.BlockSpec((None, None, D_FF, D), lambda p, blk, ex, lo, hi: (layer, ex[p], 0, 0)),
        ],
        out_specs=pl.BlockSpec((tm, D_PACK), lambda p, blk, ex, lo, hi: (blk[p], 0)),
        scratch_shapes=[pltpu.VMEM((D, D_FF), BF16), pltpu.VMEM((D, D_FF), BF16), pltpu.VMEM((D_FF, D), BF16)],
    )
    return pl.pallas_call(
        functools.partial(_expert_kernel, tm=tm),
        grid_spec=grid_spec,
        out_shape=jax.ShapeDtypeStruct((N, D_PACK), U32),
        compiler_params=_cparams(("arbitrary",), EXPERT_VMEM),
        name="experts",
    )(seg_blk, seg_exp, seg_lo, seg_hi, xs, gates, wg, wu, wd)


def _route(eidx_t, gates_t, T, tm):
    N = T * TOP_K
    flat_e = eidx_t.reshape(N)
    iota = jnp.arange(N, dtype=jnp.int32)
    sorted_e, order, g_sorted = lax.sort((flat_e, iota, gates_t.reshape(N)), num_keys=1, is_stable=True)
    tok = order % T
    _, pos = lax.sort((order, iota), num_keys=1)
    counts = jnp.bincount(flat_e, length=N_EXPERTS)
    starts = (jnp.cumsum(counts) - counts).astype(jnp.int32)
    nb = N // tm
    bounds = jnp.sort(jnp.concatenate([jnp.arange(nb, dtype=jnp.int32) * tm, starts]))
    ends = jnp.concatenate([bounds[1:], jnp.full((1,), N, bounds.dtype)])
    seg_blk = jnp.minimum(bounds // tm, nb - 1).astype(jnp.int32)
    seg_exp = sorted_e[jnp.minimum(bounds, N - 1)].astype(jnp.int32)
    return tok, g_sorted, pos, seg_blk, seg_exp, bounds, ends


def _group_major(a, axis):
    shape = a.shape
    a = a.reshape(shape[:axis] + (A_KV_HEADS, A_GROUP, HEAD_DIM) + shape[axis + 1:])
    return jnp.swapaxes(a, axis, axis + 1).reshape(shape)


def _permute_in_proj(w):
    bounds = np.cumsum([0, 768, 256, 256, 512, 768, 768, 768])
    qa, ka, va, ub, qc, kc, vc = [w[:, bounds[s]:bounds[s + 1]] for s in range(7)]
    return jnp.concatenate([_group_major(qa, 1), qc, ka, kc, ub, va, vc], axis=1)


def kernel(x, positions, attn_norm, w_in, sinks, branch_norm_a, w_pool, pool_scale, lambda_q1, lambda_k1,
           lambda_q2, lambda_k2, subln, w_out, ffn_norm, w_router_group, b_router_group, w_router_expert,
           b_router_expert, w_expert_gate, w_expert_up, w_expert_down, final_norm):
    B, S, D = x.shape
    T = B * S
    depth = w_in.shape[0]
    tm_e = 1024

    half = HEAD_DIM // 2
    inv = ROPE_THETA ** (-jnp.arange(half, dtype=F32) / half)
    ang = positions.astype(F32).reshape(T, 1) * inv
    cos, sin = jnp.cos(ang), jnp.sin(ang)
    cos_t = jnp.concatenate([cos, cos, cos, cos], axis=1)
    sin_t = jnp.concatenate([-sin, sin, -sin, sin], axis=1)

    h = x.reshape(T, D)
    y_rows = None
    for l in range(depth):
        if l == 0:
            (hn,) = _norm(h, None, attn_norm[l], emit_h=False, emit_bf16=True)
        else:
            h, hn = _norm(h, y_rows, attn_norm[l], emit_h=True, emit_bf16=True)
        w_in_l = _permute_in_proj(w_in[l]).astype(BF16)
        proj, vat, vt = _proj(hn, w_in_l[:, :COL_VA], w_in_l[:, COL_VA:].T, cos_t, sin_t, B=B, S=S)

        sink_heads = sinks[l].reshape(A_KV_HEADS, A_GROUP).T.reshape(A_HEADS) * LOG2E
        sink_row = jnp.repeat(sink_heads, WINDOW).reshape(1, A_COLS)
        oa = _swa(proj, vat, sink_row, _group_major(branch_norm_a[l], 0), B=B, S=S)
        ob = _pool(proj, w_pool[l].astype(BF16), pool_scale[l], B=B, S=S)

        lam_init = 0.8 - 0.6 * math.exp(-0.3 * l)
        lam_vecs = jnp.stack([lambda_q1[l], lambda_k1[l], lambda_q2[l], lambda_k2[l]]).astype(F32)
        oc = _diff(proj, vt, lam_vecs, subln[l], B=B, S=S, lam_init=lam_init)

        w_out_l = jnp.concatenate([_group_major(w_out[l][:A_WIDTH], 0), w_out[l][A_WIDTH:]], axis=0)
        h1, hn2 = _out(oa, ob, oc, w_out_l.astype(BF16), h, ffn_norm[l])

        wr = jnp.zeros((ROUTER_LO, D), F32)
        wr = wr.at[ROUTER_GROUP_ROW:ROUTER_GROUP_ROW + N_GROUPS].set(w_router_group[l].T)
        wr = wr.at[ROUTER_EXPERT_ROW:ROUTER_EXPERT_ROW + N_EXPERTS].set(w_router_expert[l].T)
        wr_hi = wr.astype(BF16)
        wr_lo = (wr - wr_hi.astype(F32)).astype(BF16)
        br = jnp.zeros((ROUTER_LO, 1), F32)
        br = br.at[ROUTER_GROUP_ROW:ROUTER_GROUP_ROW + N_GROUPS, 0].set(b_router_group[l])
        br = br.at[ROUTER_EXPERT_ROW:ROUTER_EXPERT_ROW + N_EXPERTS, 0].set(b_router_expert[l])
        eidx_t, gates_t = _router(hn2, jnp.concatenate([wr_hi, wr_lo], axis=0), br)

        tok, g_sorted, pos_rows, seg_blk, seg_exp, seg_lo, seg_hi = _route(eidx_t, gates_t, T, tm_e)
        xs = _sc_gather_rows(hn2, tok)
        ys = _experts(seg_blk, seg_exp, seg_lo, seg_hi, xs, g_sorted.reshape(-1, 1),
                      w_expert_gate, w_expert_up, w_expert_down, l, tm=tm_e)
        y_rows = _sc_gather_rows(ys, pos_rows)
        h = h1

    (out,) = _norm(h, y_rows, final_norm, emit_h=False, emit_bf16=False)
    return out.reshape(B, S, D)
```
